```python
import math
import jax
import jax.numpy as jnp
from jax import lax
import numpy as np

D_MODEL = 1024
BATCH = 1
SEQ = 16384
DEPTH = 4

GRID_W = 64
CTX_LEN = 256

ATT_HEADS = 8
ATT_KV_HEADS = 2
ATT_Q_PER_KV = ATT_HEADS // ATT_KV_HEADS
HEAD_DIM = 64
WINDOW = 128
ATT_BLOCK = 128
ROPE_PAIRS = HEAD_DIM // 4
ROPE_BASE = 10000.0
BRANCH_W = 512
N_BRANCHES = 4
POOL_WINDOWS = (2, 4, 8, 16)
POOL_GROUPS = 4
POOL_GW = BRANCH_W // POOL_GROUPS
HY_ORDER = 2
HY_EMB = 33
HY_BANDS = (HY_EMB - 1) // 2
HY_HIDDEN = 64
HY_TARGET = 1e-2
HY_FAST_PCT = 0.3
HY_SLOW_PCT = 1.5
HY_MOD_SHIFT = 0.05
HY_FILT_OUT = HY_ORDER * 2 * BRANCH_W
Q_COLS = ATT_HEADS * HEAD_DIM
KV_COLS = ATT_KV_HEADS * HEAD_DIM
POOL_COLS = BRANCH_W
HY_COLS = (HY_ORDER + 1) * BRANCH_W
SC_COLS = 3 * BRANCH_W
GATE_COLS = N_BRANCHES * D_MODEL
IN_COLS = Q_COLS + 2 * KV_COLS + POOL_COLS + HY_COLS + SC_COLS + GATE_COLS
N_GROUPS = 4
EXPERTS_PER_GROUP = 8
N_EXPERTS = N_GROUPS * EXPERTS_PER_GROUP
TOP_K = 2
EXPERT_HIDDEN = 512
EXPERT_BLOCK = 128
DEEPNORM_ALPHA = (2 * DEPTH) ** 0.25
DEEPNORM_BETA = (8 * DEPTH) ** -0.25
LN_EPS = 1e-6
NEG_INF = -1e30

kernel_name = "hybrid_gated_branch_diffusion_trunk"


def layer_norm(x):
    xf = x.astype(jnp.float32)
    mu = jnp.mean(xf, axis=-1, keepdims=True)
    var = jnp.mean(jnp.square(xf - mu), axis=-1, keepdims=True)
    return ((xf - mu) * lax.rsqrt(var + LN_EPS)).astype(x.dtype)


def layer_norm_affine(x, g, b):
    return layer_norm(x) * g + b


def modulate(x, shift, scale):
    return layer_norm(x) * (1.0 + scale) + shift


def split_cols(p):
    sizes = (Q_COLS, KV_COLS, KV_COLS, POOL_COLS, HY_COLS, SC_COLS, GATE_COLS)
    idx = [int(v) for v in np.cumsum(sizes)[:-1]]
    return jnp.split(p, idx, axis=-1)


def conv3_centred(u, w):
    up = jnp.pad(u, ((0, 0), (1, 1), (0, 0)))
    return up[:, :-2] * w[0] + up[:, 1:-1] * w[1] + up[:, 2:] * w[2]


def axial_rope_tables(rows):
    r, col = jnp.meshgrid(jnp.arange(rows), jnp.arange(GRID_W), indexing="ij")
    r = r.reshape(-1).astype(jnp.float32)
    col = col.reshape(-1).astype(jnp.float32)
    inv = ROPE_BASE ** (-jnp.arange(ROPE_PAIRS, dtype=jnp.float32) / ROPE_PAIRS)
    ang = jnp.stack([r[:, None] * inv, col[:, None] * inv], axis=1)
    return jnp.cos(ang), jnp.sin(ang)


def apply_rope(x, cos, sin):
    b, l, h, d = x.shape
    xr = x.astype(jnp.float32).reshape(b, l, h, 2, 2, ROPE_PAIRS)
    x1, x2 = xr[..., 0, :], xr[..., 1, :]
    cs, sn = cos[None, :, None], sin[None, :, None]
    out = jnp.stack([x1 * cs - x2 * sn, x2 * cs + x1 * sn], axis=-2)
    return out.reshape(b, l, h, d).astype(x.dtype)


def window_attention(q, k, v, k_ctx, v_ctx, sink):
    b, l, h, hd = q.shape
    nb = l // ATT_BLOCK
    qb = q.reshape(b, nb, ATT_BLOCK, ATT_KV_HEADS, ATT_Q_PER_KV, hd)

    def band(t):
        tp = jnp.pad(t, ((0, 0), (ATT_BLOCK, ATT_BLOCK), (0, 0), (0, 0)))
        tp = tp.reshape(b, nb + 2, ATT_BLOCK, ATT_KV_HEADS, hd)
        return jnp.concatenate([tp[:, :-2], tp[:, 1:-1], tp[:, 2:]], axis=2)

    kb, vb = band(k), band(v)
    scale = hd ** -0.5
    s_loc = jnp.einsum("bnqkgd,bnskd->bnkgqs", qb, kb, preferred_element_type=jnp.float32) * scale
    s_ctx = jnp.einsum("bnqkgd,bckd->bnkgqc", qb, k_ctx, preferred_element_type=jnp.float32) * scale
    qi = jnp.arange(nb)[:, None, None] * ATT_BLOCK + jnp.arange(ATT_BLOCK)[None, :, None]
    kj = (jnp.arange(nb)[:, None, None] - 1) * ATT_BLOCK + jnp.arange(3 * ATT_BLOCK)[None, None, :]
    mask = (jnp.abs(qi - kj) <= WINDOW) & (kj >= 0) & (kj < l)
    s_loc = jnp.where(mask[None, :, None, None], s_loc, NEG_INF)
    sink_col = jnp.broadcast_to(
        sink.astype(jnp.float32).reshape(ATT_KV_HEADS, ATT_Q_PER_KV)[None, None, :, :, None, None],
        s_loc.shape[:-1] + (1,))
    p = jax.nn.softmax(jnp.concatenate([sink_col, s_ctx, s_loc], axis=-1), axis=-1)
    c = k_ctx.shape[1]
    o = (jnp.einsum("bnkgqc,bckd->bnqkgd", p[..., 1:1 + c].astype(v.dtype), v_ctx)
         + jnp.einsum("bnkgqs,bnskd->bnqkgd", p[..., 1 + c:].astype(v.dtype), vb))
    return o.reshape(b, l, h * hd)


def context_attention(q, k, v, sink):
    b, c, h, hd = q.shape
    qg = q.reshape(b, c, ATT_KV_HEADS, ATT_Q_PER_KV, hd)
    s = jnp.einsum("bqkgd,bskd->bkgqs", qg, k, preferred_element_type=jnp.float32) * (hd ** -0.5)
    sink_col = jnp.broadcast_to(
        sink.astype(jnp.float32).reshape(ATT_KV_HEADS, ATT_Q_PER_KV)[None, :, :, None, None],
        s.shape[:-1] + (1,))
    p = jax.nn.softmax(jnp.concatenate([sink_col, s], axis=-1), axis=-1)
    o = jnp.einsum("bkgqs,bskd->bqkgd", p[..., 1:].astype(v.dtype), v)
    return o.reshape(b, c, h * hd)


def multiscale_pool(u, w_grp, scale):
    b, l, _ = u.shape
    ug = u.astype(jnp.float32).reshape(b, l, POOL_GROUPS, POOL_GW)
    csum = jnp.pad(jnp.cumsum(ug, axis=1), ((0, 0), (1, 0), (0, 0), (0, 0)))
    t = jnp.arange(l)
    outs = []
    for gi, w in enumerate(POOL_WINDOWS):
        left = w // 2
        right = w - left - 1
        lo = jnp.maximum(t - left, 0)
        hi = jnp.minimum(t + right + 1, l)
        cg = csum[:, :, gi]
        mean = (cg[:, hi] - cg[:, lo]) / (hi - lo).astype(jnp.float32)[None, :, None]
        outs.append(mean - ug[:, :, gi])
    dpool = jnp.stack(outs, axis=2).astype(u.dtype)
    y = jnp.einsum("blgc,gce->blge", dpool, w_grp)
    return y.reshape(b, l, BRANCH_W) * scale


def hyena_filter_spectra(l, w1, b1, f1, w2, b2, f2, w3):
    t01 = jnp.linspace(0.0, 1.0, l, dtype=jnp.float32)
    fr = jnp.linspace(1e-4, HY_BANDS - 1, HY_BANDS, dtype=jnp.float32)
    ang = (2.0 * math.pi / l) * jnp.arange(l, dtype=jnp.float32)[:, None] * fr[None, :]
    feat = jnp.concatenate([t01[:, None], jnp.cos(ang), -jnp.sin(ang)], axis=-1).astype(w1.dtype)
    hdn = jnp.sin(f1 * (feat @ w1 + b1))
    hdn = jnp.sin(f2 * (hdn @ w2 + b2))
    raw = (hdn @ w3).astype(jnp.float32).reshape(l, HY_ORDER, 2, BRANCH_W)
    deltas = jnp.abs(jnp.linspace(math.log(HY_TARGET) / HY_SLOW_PCT, math.log(HY_TARGET) / HY_FAST_PCT,
                                  BRANCH_W, dtype=jnp.float32))
    window = jnp.exp(-t01[:, None] * deltas[None, :]) + HY_MOD_SHIFT
    filt = raw * window[:, None, None, :]
    fwd, bwd = filt[:, :, 0], filt[:, :, 1]
    kern = jnp.concatenate([fwd, jnp.zeros((1, HY_ORDER, BRANCH_W), jnp.float32), bwd[1:][::-1]], axis=0)
    kern = kern * lax.rsqrt(jnp.sum(jnp.square(kern), axis=0, keepdims=True) + LN_EPS)
    return jnp.fft.rfft(kern, axis=0)


def fft_long_conv(z, kern_f):
    l = z.shape[1]
    zf = jnp.fft.rfft(z.astype(jnp.float32), n=2 * l, axis=1)
    y = jnp.fft.irfft(zf * kern_f[None], n=2 * l, axis=1)[:, :l]
    return y.astype(z.dtype)


def hyena_mixer(u, conv_w, conv_b, kern_f, bias):
    uc = conv3_centred(u, conv_w) + conv_b
    x1, x2, v = jnp.split(uc, 3, axis=-1)
    z = v
    for n, gate in enumerate((x1, x2)):
        z = gate * (fft_long_conv(z, kern_f[:, n]) + bias[n] * z)
    return z


def short_conv_mixer(u, conv_w):
    bg, cg, xin = jnp.split(u, 3, axis=-1)
    return bg * conv3_centred(cg * xin, conv_w)


def mix_and_merge(attn_o, pool_in, hy_in, sc_in, gate_logits, lw):
    b, l, _ = attn_o.shape
    y_pool = multiscale_pool(pool_in, lw["pool_w"], lw["pool_scale"])
    kern_f = hyena_filter_spectra(l, lw["hy_w1"], lw["hy_b1"], lw["hy_freq1"], lw["hy_w2"], lw["hy_b2"],
                                  lw["hy_freq2"], lw["hy_w3"])
    y_hy = hyena_mixer(hy_in, lw["hy_conv_w"], lw["hy_conv_b"], kern_f, lw["hy_bias"])
    y_sc = short_conv_mixer(sc_in, lw["sc_conv_w"])
    ys = jnp.stack([attn_o, y_pool.astype(attn_o.dtype), y_hy, y_sc], axis=0)
    br = jnp.einsum("nblc,ncd->nbld", ys, lw["w_branch"])
    gates = jax.nn.sigmoid(gate_logits.reshape(b, l, N_BRANCHES, D_MODEL))
    merged = jnp.einsum("blnd,nbld->bld", gates, br)
    return merged @ lw["w_out"]


def hierarchical_moe(h, rg_w, rg_b, re_w, re_b, w1, w3, w2):
    n, d = h.shape
    lg = (h @ rg_w).astype(jnp.float32) + rg_b.astype(jnp.float32)
    pg = jax.nn.softmax(lg, axis=-1)
    gsel = jnp.argmax(lg, axis=-1).astype(jnp.int32)
    gate_g = jnp.take_along_axis(pg, gsel[:, None], axis=1)[:, 0]
    le = ((h @ re_w).astype(jnp.float32) + re_b.astype(jnp.float32)).reshape(n, N_GROUPS, EXPERTS_PER_GROUP)
    le = jnp.take_along_axis(le, gsel[:, None, None], axis=1)[:, 0]
    top_v, top_i = lax.top_k(le, TOP_K)
    wts = (gate_g[:, None] * jax.nn.softmax(top_v, axis=-1)).reshape(-1)
    eid = (gsel[:, None] * EXPERTS_PER_GROUP + top_i.astype(jnp.int32)).reshape(-1)
    tok = jnp.repeat(jnp.arange(n, dtype=jnp.int32), TOP_K)
    a = n * TOP_K
    order = jnp.argsort(eid)
    e_s, t_s, w_s = eid[order], tok[order], wts[order]
    counts = jnp.bincount(eid, length=N_EXPERTS)
    starts = jnp.cumsum(counts) - counts
    padded = (counts + EXPERT_BLOCK - 1) // EXPERT_BLOCK * EXPERT_BLOCK
    pend = jnp.cumsum(padded)
    pstart = pend - padded
    dest = pstart[e_s] + jnp.arange(a, dtype=jnp.int32) - starts[e_s]
    n_blocks = -(-a // EXPERT_BLOCK) + N_EXPERTS
    slots = n_blocks * EXPERT_BLOCK
    buf_tok = jnp.full((slots,), n, jnp.int32).at[dest].set(t_s)
    buf_w = jnp.zeros((slots,), jnp.float32).at[dest].set(w_s)
    blk_e = jnp.minimum(jnp.searchsorted(pend, jnp.arange(n_blocks, dtype=jnp.int32) * EXPERT_BLOCK, side="right"),
                        N_EXPERTS - 1).astype(jnp.int32)
    xb = jnp.take(h, buf_tok, axis=0, mode="fill", fill_value=0).reshape(n_blocks, EXPERT_BLOCK, d)

    def expert_block(args):
        xblk, e = args
        return (jax.nn.silu(xblk @ w1[e]) * (xblk @ w3[e])) @ w2[e]

    yb = lax.map(expert_block, (xb, blk_e)).reshape(slots, d)
    return jnp.zeros((n, d), h.dtype).at[buf_tok].add(yb * buf_w[:, None].astype(h.dtype), mode="drop")


def setup_inputs(seed: int = 0) -> dict:
    key = jax.random.key(seed)
    ks = iter(jax.random.split(key, 48))

    def nrm(shape, scale):
        return jax.random.normal(next(ks), shape, jnp.float32) * scale

    L = DEPTH
    D = D_MODEL
    return {
        "x": nrm((BATCH, SEQ, D), 1.0),
        "c": nrm((BATCH, D), 1.0),
        "ctx": nrm((BATCH, CTX_LEN, D), 1.0),
        "c_ctx": nrm((D,), 1.0),
        "ada_w": nrm((L, D, 6 * D), D ** -0.5),
        "ada_b": nrm((L, 6 * D), 0.02),
        "w_in": nrm((L, D, IN_COLS), D ** -0.5),
        "attn_sink": nrm((L, ATT_HEADS), 0.5),
        "pool_w": nrm((L, POOL_GROUPS, POOL_GW, POOL_GW), POOL_GW ** -0.5),
        "pool_scale": 1.0 + nrm((L, BRANCH_W), 0.02),
        "hy_conv_w": nrm((L, 3, HY_COLS), 3 ** -0.5),
        "hy_conv_b": nrm((L, HY_COLS), 0.02),
        "hy_w1": nrm((L, HY_EMB, HY_HIDDEN), HY_EMB ** -0.5),
        "hy_b1": nrm((L, HY_HIDDEN), 0.02),
        "hy_freq1": 1.0 + nrm((L, HY_HIDDEN), 0.02),
        "hy_w2": nrm((L, HY_HIDDEN, HY_HIDDEN), HY_HIDDEN ** -0.5),
        "hy_b2": nrm((L, HY_HIDDEN), 0.02),
        "hy_freq2": 1.0 + nrm((L, HY_HIDDEN), 0.02),
        "hy_w3": nrm((L, HY_HIDDEN, HY_FILT_OUT), HY_HIDDEN ** -0.5),
        "hy_bias": nrm((L, HY_ORDER, BRANCH_W), 1.0),
        "sc_conv_w": nrm((L, 3, BRANCH_W), 3 ** -0.5),
        "w_branch": nrm((L, N_BRANCHES, BRANCH_W, D), BRANCH_W ** -0.5),
        "w_out": nrm((L, D, D), DEEPNORM_BETA * D ** -0.5),
        "ln1_g": 1.0 + nrm((L, D), 0.02),
        "ln1_b": nrm((L, D), 0.02),
        "ln2_g": 1.0 + nrm((L, D), 0.02),
        "ln2_b": nrm((L, D), 0.02),
        "rg_w": nrm((L, D, N_GROUPS), D ** -0.5),
        "rg_b": nrm((L, N_GROUPS), 0.01),
        "re_w": nrm((L, D, N_EXPERTS), D ** -0.5),
        "re_b": nrm((L, N_EXPERTS), 0.01),
        "ex_w1": nrm((L, N_EXPERTS, D, EXPERT_HIDDEN), D ** -0.5),
        "ex_w3": nrm((L, N_EXPERTS, D, EXPERT_HIDDEN), D ** -0.5),
        "ex_w2": nrm((L, N_EXPERTS, EXPERT_HIDDEN, D), DEEPNORM_BETA * EXPERT_HIDDEN ** -0.5),
    }


def reference(x, c, ctx, c_ctx, ada_w, ada_b, w_in, attn_sink, pool_w, pool_scale, hy_conv_w, hy_conv_b,
              hy_w1, hy_b1, hy_freq1, hy_w2, hy_b2, hy_freq2, hy_w3, hy_bias, sc_conv_w, w_branch, w_out,
              ln1_g, ln1_b, ln2_g, ln2_b, rg_w, rg_b, re_w, re_b, ex_w1, ex_w3, ex_w2):
    b, l, d = x.shape
    cl = ctx.shape[1]
    rows = l // GRID_W
    cos, sin = axial_rope_tables(rows)
    xc = ctx
    silu_c = jax.nn.silu(c)
    silu_cc = jax.nn.silu(c_ctx)
    for i in range(DEPTH):
        last = i == DEPTH - 1
        lw = {"pool_w": pool_w[i], "pool_scale": pool_scale[i], "hy_conv_w": hy_conv_w[i],
              "hy_conv_b": hy_conv_b[i], "hy_w1": hy_w1[i], "hy_b1": hy_b1[i], "hy_freq1": hy_freq1[i],
              "hy_w2": hy_w2[i], "hy_b2": hy_b2[i], "hy_freq2": hy_freq2[i], "hy_w3": hy_w3[i],
              "hy_bias": hy_bias[i], "sc_conv_w": sc_conv_w[i], "w_branch": w_branch[i], "w_out": w_out[i]}
        mod_l = (silu_c @ ada_w[i] + ada_b[i])[:, None, :]
        mod_c = silu_cc @ ada_w[i] + ada_b[i]
        sh1_l, sc1_l, g1_l, sh2_l, sc2_l, g2_l = jnp.split(mod_l, 6, axis=-1)
        sh1_c, sc1_c, g1_c, sh2_c, sc2_c, g2_c = jnp.split(mod_c, 6, axis=-1)

        hc_in = modulate(xc, sh1_c, sc1_c)
        if last:
            kv_c = hc_in @ w_in[i][:, Q_COLS:Q_COLS + 2 * KV_COLS]
            kc, vc = jnp.split(kv_c, 2, axis=-1)
        else:
            qc, kc, vc, pool_c, hy_c, sc_c, gate_c = split_cols(hc_in @ w_in[i])
        kc = kc.reshape(b, cl, ATT_KV_HEADS, HEAD_DIM)
        vc = vc.reshape(b, cl, ATT_KV_HEADS, HEAD_DIM)

        ql, kl, vl, pool_l, hy_l, sc_l, gate_l = split_cols(modulate(x, sh1_l, sc1_l) @ w_in[i])
        ql = apply_rope(ql.reshape(b, l, ATT_HEADS, HEAD_DIM), cos, sin)
        kl = apply_rope(kl.reshape(b, l, ATT_KV_HEADS, HEAD_DIM), cos, sin)
        vl = vl.reshape(b, l, ATT_KV_HEADS, HEAD_DIM)
        att_l = window_attention(ql, kl, vl, kc, vc, attn_sink[i])
        y_l = mix_and_merge(att_l, pool_l, hy_l, sc_l, gate_l, lw)
        x = layer_norm_affine(DEEPNORM_ALPHA * x + g1_l * y_l, ln1_g[i], ln1_b[i])
        if not last:
            att_c = context_attention(qc.reshape(b, cl, ATT_HEADS, HEAD_DIM), kc, vc, attn_sink[i])
            y_c = mix_and_merge(att_c, pool_c, hy_c, sc_c, gate_c, lw)
            xc = layer_norm_affine(DEEPNORM_ALPHA * xc + g1_c * y_c, ln1_g[i], ln1_b[i])

        m_l = modulate(x, sh2_l, sc2_l).reshape(b * l, d)
        if last:
            f_l = hierarchical_moe(m_l, rg_w[i], rg_b[i], re_w[i], re_b[i], ex_w1[i], ex_w3[i], ex_w2[i])
        else:
            m_c = modulate(xc, sh2_c, sc2_c).reshape(b * cl, d)
            f_all = hierarchical_moe(jnp.concatenate([m_c, m_l], axis=0), rg_w[i], rg_b[i], re_w[i], re_b[i],
                                     ex_w1[i], ex_w3[i], ex_w2[i])
            f_c, f_l = f_all[:b * cl], f_all[b * cl:]
            xc = layer_norm_affine(DEEPNORM_ALPHA * xc + g2_c * f_c.reshape(b, cl, d), ln2_g[i], ln2_b[i])
        x = layer_norm_affine(DEEPNORM_ALPHA * x + g2_l * f_l.reshape(b, l, d), ln2_g[i], ln2_b[i])
    return x
```

```python
import functools
import math

import jax
import jax.numpy as jnp
from jax import lax
from jax.experimental import pallas as pl
from jax.experimental.pallas import tpu as pltpu

F32 = jnp.float32
BF16 = jnp.bfloat16
I32 = jnp.int32
SDS = jax.ShapeDtypeStruct
HIGHEST = lax.Precision.HIGHEST

D = 1024
GRID_W = 64
HEADS = 8
KV_HEADS = 2
HEAD_DIM = 64
WINDOW = 128
ATT_BLK = 128
ROPE_PAIRS = 16
ROPE_BASE = 10000.0
BW = 512
POOL_WINDOWS = (2, 4, 8, 16)
POOL_GW = 128
HY_EMB = 33
HY_BANDS = 16
HY_HIDDEN = 64
HY_TARGET = 1e-2
HY_FAST_PCT = 0.3
HY_SLOW_PCT = 1.5
HY_MOD_SHIFT = 0.05
Q_COLS = HEADS * HEAD_DIM
KV_COLS = KV_HEADS * HEAD_DIM
IN_COLS = Q_COLS + 2 * KV_COLS + BW + 3 * BW + 3 * BW + 4 * D
N_GROUPS = 4
EPG = 8
N_EXPERTS = 32
EXPERT_HIDDEN = 512
LN_EPS = 1e-6
NEG_INF = -1e30

LANE = 128
SUB = 8
BSUB = 16
TM = 256
TM1 = 512
PADR = 256
HALO = 16
FFT_N1 = 512
FFT_K1 = 264
FFT_KB = 8
FFT_TN = 2048
MOE_PIECE = 8
MOE_MB = 256
MOE_S = 2 * TM1 + 256


def _cparams(sem, vmem_mb):
    return pltpu.CompilerParams(dimension_semantics=sem, vmem_limit_bytes=vmem_mb * 2 ** 20)


def _layer_norm(x):
    mu = jnp.mean(x, axis=-1, keepdims=True)
    xc = x - mu
    var = jnp.mean(xc * xc, axis=-1, keepdims=True)
    return xc * lax.rsqrt(var + LN_EPS)


def _mod_body(s_ref, w_ref, b_ref, o_ref):
    s = s_ref[...]
    s = s * (1.0 / (1.0 + jnp.exp(-s)))
    o_ref[...] = jnp.dot(s.astype(BF16), w_ref[...].astype(BF16), preferred_element_type=F32) + b_ref[...]


def _mod_all(s8, ada_w, ada_b3):
    depth = ada_w.shape[0]
    tn = 1024
    return pl.pallas_call(
        _mod_body,
        out_shape=SDS((depth, SUB, 6 * D), F32),
        grid=(depth, 6 * D // tn),
        in_specs=[pl.BlockSpec((SUB, D), lambda l, j: (0, 0)),
                  pl.BlockSpec((None, D, tn), lambda l, j: (l, 0, j)),
                  pl.BlockSpec((None, 1, tn), lambda l, j: (l, 0, j))],
        out_specs=pl.BlockSpec((None, SUB, tn), lambda l, j: (l, 0, j)),
        compiler_params=_cparams(("parallel", "parallel"), 32),
        name="ada_mod",
    )(s8, ada_w, ada_b3)


def _inproj_body(x_ref, sh_ref, sc_ref, cos_ref, sin_ref, w_ref,
                 q_ref, k_ref, v_ref, pool_ref, hy_ref, scv_ref, gate_ref):
    h = _layer_norm(x_ref[...]) * (1.0 + sc_ref[...]) + sh_ref[...]
    hb = h.astype(BF16)

    def mm(a, b):
        return jnp.dot(hb, w_ref[:, a:b], preferred_element_type=F32)

    cs = cos_ref[...]
    sn = sin_ref[...]
    lane = lax.broadcasted_iota(I32, cs.shape, 1)
    first = (lane & ROPE_PAIRS) == 0

    def rope(t):
        sw = jnp.where(first, pltpu.roll(t, LANE - ROPE_PAIRS, 1), pltpu.roll(t, ROPE_PAIRS, 1))
        return t * cs + sw * sn

    for j in range(Q_COLS // LANE):
        q_ref[:, j * LANE:(j + 1) * LANE] = (rope(mm(j * LANE, (j + 1) * LANE)) * (HEAD_DIM ** -0.5)).astype(BF16)
    o = Q_COLS
    k_ref[...] = rope(mm(o, o + KV_COLS)).astype(BF16)
    o += KV_COLS
    v_ref[...] = mm(o, o + KV_COLS).astype(BF16)
    o += KV_COLS
    pool_ref[...] = mm(o, o + BW).astype(BF16)
    o += BW
    for j in range(3):
        hy_ref[:, j * BW:(j + 1) * BW] = mm(o + j * BW, o + (j + 1) * BW).astype(BF16)
    o += 3 * BW
    for j in range(3):
        scv_ref[:, j * BW:(j + 1) * BW] = mm(o + j * BW, o + (j + 1) * BW).astype(BF16)
    o += 3 * BW
    for j in range(4 * D // 1024):
        gate_ref[:, j * 1024:(j + 1) * 1024] = mm(o + j * 1024, o + (j + 1) * 1024).astype(BF16)


def _inproj(x, sh, sc, cos_t, sin_t, w_in_bf, layer, n_lat_tiles):
    nt = x.shape[0]
    sel = lambda i: (jnp.where(i >= n_lat_tiles, 1, 0), 0, 0)
    row = lambda c: pl.BlockSpec((TM1, c), lambda i: (i, 0))
    outs = [SDS((nt, Q_COLS), BF16), SDS((nt, KV_COLS), BF16), SDS((nt, KV_COLS), BF16), SDS((nt, BW), BF16),
            SDS((nt, 3 * BW), BF16), SDS((nt, 3 * BW), BF16), SDS((nt, 4 * D), BF16)]
    return pl.pallas_call(
        _inproj_body,
        out_shape=outs,
        grid=(nt // TM1,),
        in_specs=[row(D),
                  pl.BlockSpec((None, 1, D), sel), pl.BlockSpec((None, 1, D), sel),
                  row(LANE), row(LANE),
                  pl.BlockSpec((None, D, IN_COLS), lambda i: (layer, 0, 0), pipeline_mode=pl.Buffered(1))],
        out_specs=[row(Q_COLS), row(KV_COLS), row(KV_COLS), row(BW), row(3 * BW), row(3 * BW), row(4 * D)],
        compiler_params=_cparams(("parallel",), 58),
        name="inproj",
    )(x, sh, sc, cos_t, sin_t, w_in_bf)


def _attn_body(sink_ref, q_ref, kp_ref, kc_ref, kn_ref, vp_ref, vc_ref, vn_ref, kx_ref, vx_ref, o_ref, *, nlb):
    i = pl.program_id(0)
    is_lat = i < nlb
    lat_i = jnp.where(is_lat, 1, 0)
    prev_i = jnp.where(jnp.logical_and(is_lat, i > 0), 1, 0)
    next_i = jnp.where(jnp.logical_and(is_lat, i < nlb - 1), 1, 0)
    ncx = kx_ref.shape[0]
    nk = ncx + 3 * ATT_BLK

    r = lax.broadcasted_iota(I32, (ATT_BLK, nk), 0)
    c = lax.broadcasted_iota(I32, (ATT_BLK, nk), 1) - ncx
    band = jnp.logical_and(c >= r, c <= r + 2 * WINDOW)
    grp = jnp.where(c < ATT_BLK, prev_i, jnp.where(c < 2 * ATT_BLK, lat_i, next_i))
    mask = jnp.logical_or(c < 0, jnp.logical_and(band, grp > 0))

    kcat = jnp.concatenate([kx_ref[...], kp_ref[...], kc_ref[...], kn_ref[...]], axis=0).astype(F32)
    vcat = jnp.concatenate([vx_ref[...], vp_ref[...], vc_ref[...], vn_ref[...]], axis=0).astype(F32)
    lo = lax.broadcasted_iota(I32, kcat.shape, 1) < HEAD_DIM
    kroll = pltpu.roll(kcat, HEAD_DIM, 1)
    vroll = pltpu.roll(vcat, HEAD_DIM, 1)
    zero = jnp.zeros_like(kcat)
    kvar = [[jnp.where(lo, kcat, zero), jnp.where(lo, zero, kroll)],
            [jnp.where(lo, kroll, zero), jnp.where(lo, zero, kcat)]]
    vvar = [[jnp.where(lo, vcat, zero), jnp.where(lo, zero, vroll)],
            [jnp.where(lo, vroll, zero), jnp.where(lo, zero, vcat)]]
    kvar = [[a.astype(BF16) for a in row] for row in kvar]
    vvar = [[a.astype(BF16) for a in row] for row in vvar]

    for p in range(HEADS // 2):
        qp = q_ref[:, p * LANE:(p + 1) * LANE]
        kh = (2 * p) // (HEADS // KV_HEADS)
        acc = jnp.zeros((ATT_BLK, LANE), F32)
        for rr in range(2):
            sink = sink_ref[2 * p + rr]
            s = lax.dot_general(qp, kvar[kh][rr], (((1,), (1,)), ((), ())), preferred_element_type=F32)
            s = jnp.where(mask, s, NEG_INF)
            m = jnp.maximum(jnp.max(s, axis=1, keepdims=True), sink)
            e = jnp.exp(s - m)
            den = jnp.sum(e, axis=1, keepdims=True) + jnp.exp(sink - m)
            o = jnp.dot(e.astype(BF16), vvar[kh][rr], preferred_element_type=F32)
            acc = acc + o * (1.0 / den)
        o_ref[:, p * LANE:(p + 1) * LANE] = acc.astype(BF16)


def _attention(sink, q, k, v, n_lat):
    nt = q.shape[0]
    nlb = n_lat // ATT_BLK
    cxb = n_lat // TM
    cl = lambda i: jnp.clip(i, 0, nlb - 1)
    kspec = lambda f: pl.BlockSpec((ATT_BLK, KV_COLS), lambda i: (f(i), 0))
    prev = lambda i: jnp.clip(i - 1, 0, nlb - 1)
    nxt = lambda i: jnp.clip(i + 1, 0, nlb - 1)
    cx = pl.BlockSpec((TM, KV_COLS), lambda i: (cxb, 0))
    return pl.pallas_call(
        functools.partial(_attn_body, nlb=nlb),
        out_shape=SDS((nt, Q_COLS), BF16),
        grid=(nt // ATT_BLK,),
        in_specs=[pl.BlockSpec(memory_space=pltpu.SMEM),
                  pl.BlockSpec((ATT_BLK, Q_COLS), lambda i: (i, 0)),
                  kspec(prev), kspec(cl), kspec(nxt), kspec(prev), kspec(cl), kspec(nxt), cx, cx],
        out_specs=pl.BlockSpec((ATT_BLK, Q_COLS), lambda i: (i, 0)),
        compiler_params=_cparams(("parallel",), 32),
        name="attention",
    )(sink, q, k, k, k, v, v, v, k, v)


def _local_body(pm_ref, pa_ref, pb_ref, sm_ref, sa_ref, sb_ref, hm_ref, ha_ref, hb_ref,
                pw_ref, ps_ref, hw_ref, hbias_ref, sw_ref,
                ypool_ref, ysc_ref, x1_ref, x2_ref, v_ref,
                pext, sext, hext, *, nlat, n_lat):
    i = pl.program_id(0)
    is_lat = i < nlat
    prev_ok = jnp.logical_and(is_lat, i > 0)
    next_ok = jnp.logical_and(is_lat, i < nlat - 1)
    pf = jnp.where(prev_ok, 1.0, 0.0)
    nf = jnp.where(next_ok, 1.0, 0.0)

    def fill(ext, a_ref, m_ref, b_ref, cols=None):
        if cols is None:
            a, m, b = a_ref[...], m_ref[...], b_ref[...]
        else:
            a, m, b = a_ref[:, cols], m_ref[:, cols], b_ref[:, cols]
        ext[0:HALO, :] = a.astype(F32) * pf
        ext[HALO:HALO + TM, :] = m.astype(F32)
        ext[HALO + TM:HALO + TM + HALO, :] = b.astype(F32) * nf

    def sh(ext, d, c0=None, c1=None):
        if c0 is None:
            return ext[pl.ds(HALO + d, TM), :]
        return ext[pl.ds(HALO + d, TM), c0:c1]

    fill(pext, pa_ref, pm_ref, pb_ref)
    r = lax.broadcasted_iota(I32, (TM, 1), 0)
    t = jnp.where(is_lat, i * TM, 0) + r
    ln = jnp.where(is_lat, n_lat, TM)
    for g, w in enumerate(POOL_WINDOWS):
        left = w // 2
        right = w - left - 1
        c0, c1 = g * POOL_GW, (g + 1) * POOL_GW
        acc = sh(pext, -left, c0, c1)
        for dd in range(-left + 1, right + 1):
            acc = acc + sh(pext, dd, c0, c1)
        cnt = (jnp.minimum(t + right + 1, ln) - jnp.maximum(t - left, 0)).astype(F32)
        dpool = acc / cnt - sh(pext, 0, c0, c1)
        y = jnp.dot(dpool.astype(BF16), pw_ref[g], preferred_element_type=F32)
        ypool_ref[:, c0:c1] = (y * ps_ref[:, c0:c1]).astype(BF16)

    sext[0:HALO, :] = (sa_ref[:, BW:2 * BW].astype(F32) * sa_ref[:, 2 * BW:3 * BW].astype(F32)) * pf
    sext[HALO:HALO + TM, :] = sm_ref[:, BW:2 * BW].astype(F32) * sm_ref[:, 2 * BW:3 * BW].astype(F32)
    sext[HALO + TM:HALO + TM + HALO, :] = (sb_ref[:, BW:2 * BW].astype(F32) * sb_ref[:, 2 * BW:3 * BW].astype(F32)) * nf
    conv = sh(sext, -1) * sw_ref[0:1, :] + sh(sext, 0) * sw_ref[1:2, :] + sh(sext, 1) * sw_ref[2:3, :]
    ysc_ref[...] = (sm_ref[:, 0:BW].astype(F32) * conv).astype(BF16)

    fill(hext, ha_ref, hm_ref, hb_ref)
    outs = (x1_ref, x2_ref, v_ref)
    for j in range(3):
        c0, c1 = j * BW, (j + 1) * BW
        uc = (sh(hext, -1, c0, c1) * hw_ref[0:1, c0:c1] + sh(hext, 0, c0, c1) * hw_ref[1:2, c0:c1]
              + sh(hext, 1, c0, c1) * hw_ref[2:3, c0:c1] + hbias_ref[:, c0:c1])
        outs[j][...] = uc.astype(BF16)


def _local_ops(pool_in, sc_in, hy_in, pool_w_bf, pool_scale, hy_conv_w, hy_conv_b, sc_conv_w, layer, n_lat):
    nt = pool_in.shape[0]
    nlat = n_lat // TM
    nhb = nt // HALO
    per = TM // HALO
    main = lambda c: pl.BlockSpec((TM, c), lambda i: (i, 0))
    before = lambda c: pl.BlockSpec((HALO, c), lambda i: (jnp.maximum(i * per - 1, 0), 0))
    after = lambda c: pl.BlockSpec((HALO, c), lambda i: (jnp.minimum((i + 1) * per, nhb - 1), 0))
    lsel = lambda *blk: pl.BlockSpec((None,) + blk, lambda i: (layer,) + (0,) * len(blk))
    out = SDS((nt, BW), BF16)
    return pl.pallas_call(
        functools.partial(_local_body, nlat=nlat, n_lat=n_lat),
        out_shape=[out] * 5,
        grid=(nt // TM,),
        in_specs=[main(BW), before(BW), after(BW),
                  main(3 * BW), before(3 * BW), after(3 * BW),
                  main(3 * BW), before(3 * BW), after(3 * BW),
                  lsel(4, POOL_GW, POOL_GW), lsel(1, BW), lsel(3, 3 * BW), lsel(1, 3 * BW), lsel(3, BW)],
        out_specs=[main(BW)] * 5,
        scratch_shapes=[pltpu.VMEM((TM + 2 * HALO, BW), F32), pltpu.VMEM((TM + 2 * HALO, BW), F32),
                        pltpu.VMEM((TM + 2 * HALO, 3 * BW), F32)],
        compiler_params=_cparams(("parallel",), 32),
        name="local_ops",
    )(pool_in, pool_in, pool_in, sc_in, sc_in, sc_in, hy_in, hy_in, hy_in,
      pool_w_bf, pool_scale, hy_conv_w, hy_conv_b, sc_conv_w)


def _dft_tables(n2):
    n = FFT_N1 * n2
    k1 = jnp.arange(FFT_K1, dtype=I32)
    n1 = jnp.arange(FFT_N1 // 2, dtype=I32)
    ok = (k1 <= FFT_N1 // 2)
    ang = (2.0 * math.pi / FFT_N1) * ((k1[:, None] * n1[None, :]) % FFT_N1).astype(F32)
    c1 = jnp.where(ok[:, None], jnp.cos(ang), 0.0)
    s1 = jnp.where(ok[:, None], jnp.sin(ang), 0.0)
    f1 = jnp.concatenate([c1, -s1], axis=0).astype(BF16)
    wt = jnp.where((k1 == 0) | (k1 == FFT_N1 // 2), 1.0, 2.0) / n
    f1i = jnp.concatenate([(c1 * wt[:, None]).T, (-s1 * wt[:, None]).T], axis=1).astype(BF16)
    if n2 == 1:
        return f1, f1i, None, None
    k2 = jnp.arange(n2, dtype=I32)
    m2 = jnp.arange(n2, dtype=I32)
    kk = k1[:, None, None] + FFT_N1 * k2[None, :, None]
    th = (2.0 * math.pi / n) * ((kk * m2[None, None, :]) % n).astype(F32)
    okb = ok[:, None, None]
    cc = jnp.where(okb, jnp.cos(th), 0.0)
    ss = jnp.where(okb, jnp.sin(th), 0.0)
    g = jnp.concatenate([jnp.concatenate([cc, ss], axis=2), jnp.concatenate([-ss, cc], axis=2)], axis=1)
    return f1, f1i, g.astype(BF16), jnp.swapaxes(g, 1, 2).astype(BF16)


def _filter_body(feat_ref, w1_ref, b1_ref, f1_ref, w2_ref, b2_ref, f2_ref, w3_ref, dl_ref, filt_ref, ss_ref):
    i = pl.program_id(0)
    feat = feat_ref[...]
    h = jnp.sin(f1_ref[...] * (jnp.dot(feat, w1_ref[...], precision=HIGHEST, preferred_element_type=F32) + b1_ref[...]))
    h = jnp.sin(f2_ref[...] * (jnp.dot(h, w2_ref[...], precision=HIGHEST, preferred_element_type=F32) + b2_ref[...]))
    raw = jnp.dot(h.astype(BF16), w3_ref[...], preferred_element_type=F32)
    win = jnp.exp(-feat[:, 0:1] * dl_ref[...]) + HY_MOD_SHIFT
    row = lax.broadcasted_iota(I32, (TM, 1), 0) + i * TM

    @pl.when(i == 0)
    def _():
        ss_ref[...] = jnp.zeros_like(ss_ref)

    for j in range(4):
        f = raw[:, j * BW:(j + 1) * BW] * win
        if j >= 2:
            f = jnp.where(row == 0, 0.0, f)
        filt_ref[:, j * BW:(j + 1) * BW] = f.astype(BF16)
        ss_ref[:, j * BW:(j + 1) * BW] += jnp.sum(f * f, axis=0, keepdims=True)


def _hy_filters(feat, w1p, b1, fr1, w2, b2, fr2, w3p_bf, deltas, layer):
    l = feat.shape[0]
    lsel = lambda *blk: pl.BlockSpec((None,) + blk, lambda i: (layer,) + (0,) * len(blk))
    return pl.pallas_call(
        _filter_body,
        out_shape=[SDS((l, 4 * BW), BF16), SDS((1, 4 * BW), F32)],
        grid=(l // TM,),
        in_specs=[pl.BlockSpec((TM, LANE), lambda i: (i, 0)),
                  lsel(LANE, HY_HIDDEN), lsel(1, HY_HIDDEN), lsel(1, HY_HIDDEN),
                  lsel(HY_HIDDEN, HY_HIDDEN), lsel(1, HY_HIDDEN), lsel(1, HY_HIDDEN),
                  lsel(HY_HIDDEN, 4 * BW), pl.BlockSpec((1, BW), lambda i: (0, 0))],
        out_specs=[pl.BlockSpec((TM, 4 * BW), lambda i: (i, 0)), pl.BlockSpec((1, 4 * BW), lambda i: (0, 0))],
        compiler_params=_cparams(("arbitrary",), 32),
        name="hy_filters",
    )(feat, w1p, b1, fr1, w2, b2, fr2, w3p_bf, deltas)


def _dft_major_body(f_ref, x_ref, o_ref):
    o_ref[...] = jnp.dot(f_ref[...], x_ref[...], preferred_element_type=F32).astype(BF16)


def _dft_major(f1, x2d, row_blk, ncols):
    tn = min(FFT_TN, ncols)
    return pl.pallas_call(
        _dft_major_body,
        out_shape=SDS((2 * FFT_K1, ncols), BF16),
        grid=(ncols // tn,),
        in_specs=[pl.BlockSpec((2 * FFT_K1, FFT_N1 // 2), lambda j: (0, 0)),
                  pl.BlockSpec((FFT_N1 // 2, tn), lambda j: (row_blk, j))],
        out_specs=pl.BlockSpec((2 * FFT_K1, tn), lambda j: (0, j)),
        compiler_params=_cparams(("parallel",), 32),
        name="dft_major",
    )(f1, x2d)


def _filter_minor_body(a_ref, g_ref, ss_ref, h_ref, *, n2):
    ssv = ss_ref[...]
    scale = lax.rsqrt(ssv[:, 0:2 * BW] + ssv[:, 2 * BW:4 * BW] + LN_EPS)
    for j in range(a_ref.shape[1]):
        if n2 == 1:
            s = jnp.concatenate([a_ref[0, j], a_ref[1, j]], axis=0).astype(F32)
        else:
            a = jnp.concatenate([a_ref[0, j], a_ref[1, j]], axis=0)
            s = jnp.dot(g_ref[j], a, preferred_element_type=F32)
        sf, sb = s[:, 0:2 * BW], s[:, 2 * BW:4 * BW]
        hr = (sf[0:n2] + sb[0:n2]) * scale
        hi = (sf[n2:2 * n2] - sb[n2:2 * n2]) * scale
        h_ref[j] = jnp.concatenate([hr, hi], axis=0).astype(BF16)


def _conv_minor_body(a_ref, h_ref, g_ref, gt_ref, d_ref, *, n2):
    for j in range(a_ref.shape[1]):
        a = jnp.concatenate([a_ref[0, j], a_ref[1, j]], axis=0)
        if n2 == 1:
            x = a.astype(F32)
        else:
            x = jnp.dot(g_ref[j], a, preferred_element_type=F32)
        h = h_ref[j].astype(F32)
        xr, xi, hr, hi = x[0:n2], x[n2:2 * n2], h[0:n2], h[n2:2 * n2]
        y = jnp.concatenate([xr * hr - xi * hi, xr * hi + xi * hr], axis=0)
        if n2 == 1:
            dv = y
        else:
            dv = jnp.dot(gt_ref[j], y.astype(BF16), preferred_element_type=F32)
        d_ref[0, j] = dv[0:n2].astype(BF16)
        d_ref[1, j] = dv[n2:2 * n2].astype(BF16)


def _filter_minor(af, g, ss, n2):
    a4 = af.reshape(2, FFT_K1, n2, 4 * BW)
    kb = FFT_KB
    gspec = (pl.BlockSpec((kb, 2 * n2, 2 * n2), lambda i: (i, 0, 0)) if n2 > 1
             else pl.BlockSpec((SUB, LANE), lambda i: (0, 0)))
    return pl.pallas_call(
        functools.partial(_filter_minor_body, n2=n2),
        out_shape=SDS((FFT_K1, 2 * n2, 2 * BW), BF16),
        grid=(FFT_K1 // kb,),
        in_specs=[pl.BlockSpec((2, kb, n2, 4 * BW), lambda i: (0, i, 0, 0)), gspec,
                  pl.BlockSpec((1, 4 * BW), lambda i: (0, 0))],
        out_specs=pl.BlockSpec((kb, 2 * n2, 2 * BW), lambda i: (i, 0, 0)),
        compiler_params=_cparams(("parallel",), 40),
        name="hy_filter_minor",
    )(a4, g if n2 > 1 else jnp.zeros((SUB, LANE), BF16), ss)


def _conv_minor(a, hspec, g, gt, order, n2):
    a4 = a.reshape(2, FFT_K1, n2, BW)
    kb = FFT_KB
    dummy = jnp.zeros((SUB, LANE), BF16)
    gspec = (pl.BlockSpec((kb, 2 * n2, 2 * n2), lambda i: (i, 0, 0)) if n2 > 1
             else pl.BlockSpec((SUB, LANE), lambda i: (0, 0)))
    d4 = pl.pallas_call(
        functools.partial(_conv_minor_body, n2=n2),
        out_shape=SDS((2, FFT_K1, n2, BW), BF16),
        grid=(FFT_K1 // kb,),
        in_specs=[pl.BlockSpec((2, kb, n2, BW), lambda i: (0, i, 0, 0)),
                  pl.BlockSpec((kb, 2 * n2, BW), lambda i: (i, 0, order)), gspec, gspec],
        out_specs=pl.BlockSpec((2, kb, n2, BW), lambda i: (0, i, 0, 0)),
        compiler_params=_cparams(("parallel",), 32),
        name="hy_conv_minor",
    )(a4, hspec, g if n2 > 1 else dummy, gt if n2 > 1 else dummy)
    return d4.reshape(2 * FFT_K1, n2 * BW)


def _idft_gate_body(f_ref, d_ref, gate_ref, z_ref, b_ref, o_ref):
    y = jnp.dot(f_ref[...], d_ref[...], preferred_element_type=F32)
    z = z_ref[...].astype(F32)
    o_ref[...] = (gate_ref[...].astype(F32) * (y + b_ref[...] * z)).astype(BF16)


def _idft_gate(f1i, d, gate2d, gate_blk, z2d, z_blk, bias_t, ncols):
    tn = min(FFT_TN, ncols)
    return pl.pallas_call(
        _idft_gate_body,
        out_shape=SDS((FFT_N1 // 2, ncols), BF16),
        grid=(ncols // tn,),
        in_specs=[pl.BlockSpec((FFT_N1 // 2, 2 * FFT_K1), lambda j: (0, 0)),
                  pl.BlockSpec((2 * FFT_K1, tn), lambda j: (0, j)),
                  pl.BlockSpec((FFT_N1 // 2, tn), lambda j: (gate_blk, j)),
                  pl.BlockSpec((FFT_N1 // 2, tn), lambda j: (z_blk, j)),
                  pl.BlockSpec((1, tn), lambda j: (0, 0))],
        out_specs=pl.BlockSpec((FFT_N1 // 2, tn), lambda j: (0, j)),
        compiler_params=_cparams(("parallel",), 32),
        name="idft_gate",
    )(f1i, d, gate2d, z2d, bias_t)


def _hyena(x1, x2, v, feat, tabs, hyw, layer, seq_len, row0):
    n2 = seq_len // (FFT_N1 // 2)
    f1, f1i, g, gt = tabs
    w1p, b1, fr1, w2, b2, fr2, w3p_bf, deltas, hy_bias = hyw
    nt = x1.shape[0]
    filt, ss = _hy_filters(feat, w1p, b1, fr1, w2, b2, fr2, w3p_bf, deltas, layer)
    af = _dft_major(f1, filt.reshape(FFT_N1 // 2, n2 * 4 * BW), 0, n2 * 4 * BW)
    hspec = _filter_minor(af, g, ss, n2)
    view = lambda a: a.reshape(nt // n2, n2 * BW)
    blk = row0 // seq_len if n2 == 1 else 0
    ncols = n2 * BW
    tn = min(FFT_TN, ncols)
    z, zv, zblk = v, view(v), blk
    gates = (x1, x2)
    for o in range(2):
        a = _dft_major(f1, zv, zblk, ncols)
        d = _conv_minor(a, hspec, g, gt, o, n2)
        bias_t = jnp.tile(hy_bias[layer, o][None, :], (1, tn // BW))
        z2 = _idft_gate(f1i, d, view(gates[o]), blk, zv, zblk, bias_t, ncols)
        zv, zblk = z2, 0
    return zv.reshape(seq_len, BW)


def _merge_body(att_ref, yp_ref, yhl_ref, yhc_ref, ys_ref, gate_ref, x_ref, wb_ref, wo_ref,
                g1_ref, lg_ref, lb_ref, sh2_ref, sc2_ref, rw_ref, rb_ref,
                x1_ref, m_ref, rt_ref, cnt_ref, *, nlat, nvalid, alpha):
    i = pl.program_id(0)
    yh = jnp.where(i >= nlat, yhc_ref[...], yhl_ref[...])
    ys = (att_ref[...], yp_ref[...], yh, ys_ref[...])
    merged = jnp.zeros((TM, D), F32)
    for n in range(4):
        br = jnp.dot(ys[n], wb_ref[n], preferred_element_type=F32)
        gl = gate_ref[:, n * D:(n + 1) * D].astype(F32)
        merged = merged + br / (1.0 + jnp.exp(-gl))
    y = jnp.dot(merged.astype(BF16), wo_ref[...], preferred_element_type=F32)
    x1 =_layer_norm(alpha * x_ref[...] + g1_ref[...] * y) * lg_ref[...] + lb_ref[...]
    x1_ref[...] = x1
    m = _layer_norm(x1) * (1.0 + sc2_ref[...]) + sh2_ref[...]
    m_ref[...] = m.astype(BF16)

    logits = jnp.dot(m, rw_ref[...], precision=HIGHEST, preferred_element_type=F32) + rb_ref[...]
    lt = logits.T
    le = lt[0:N_EXPERTS]
    lgp = lt[N_EXPERTS:N_EXPERTS + SUB]
    big = 1 << 20
    gi = lax.broadcasted_iota(I32, lgp.shape, 0)
    gmax = jnp.max(lgp, axis=0, keepdims=True)
    gsel = jnp.min(jnp.where(lgp == gmax, gi, big), axis=0, keepdims=True)
    gate_g = 1.0 / jnp.sum(jnp.exp(lgp - gmax), axis=0, keepdims=True)
    ei = lax.broadcasted_iota(I32, le.shape, 0)
    lem = jnp.where(lax.shift_right_logical(ei, 3) == gsel, le, -3.0e38)
    v1 = jnp.max(lem, axis=0, keepdims=True)
    i1 = jnp.min(jnp.where(lem == v1, ei, big), axis=0, keepdims=True)
    lem2 = jnp.where(ei == i1, -3.0e38, lem)
    v2 = jnp.max(lem2, axis=0, keepdims=True)
    i2 = jnp.min(jnp.where(lem2 == v2, ei, big), axis=0, keepdims=True)
    e2 = jnp.exp(v2 - v1)
    wa = gate_g / (1.0 + e2)
    wb = gate_g * e2 / (1.0 + e2)
    valid = i < nvalid
    i1 = jnp.where(valid, i1, -1)
    i2 = jnp.where(valid, i2, -1)
    ri = lax.broadcasted_iota(I32, (SUB, TM), 0)
    rt = jnp.where(ri == 0, i1.astype(F32), jnp.where(ri == 1, i2.astype(F32),
                   jnp.where(ri == 2, wa, jnp.where(ri == 3, wb, 0.0))))
    rt_ref[...] = rt
    ci = lax.broadcasted_iota(I32, (LANE, TM), 0)
    oh = jnp.logical_or(ci == i1, ci == i2).astype(BF16)
    cnt_ref[0] = lax.dot_general(jnp.ones((SUB, TM), BF16), oh, (((1,), (1,)), ((), ())),
                                 preferred_element_type=F32)


def _merge(att, ypool, yhl, yhc, ysc, gate, x, wb_bf, wo_bf, g1, ln_g, ln_b, sh2, sc2, rw, rb, layer, n_lat, alpha):
    nt = x.shape[0]
    nlat = n_lat // TM
    nvalid = nlat + 1
    ntl = nt // TM
    sel = lambda i: (jnp.where(i >= nlat, 1, 0), 0, 0)
    row = lambda c: pl.BlockSpec((TM, c), lambda i: (i, 0))
    lsel = lambda *blk, **kw: pl.BlockSpec((None,) + blk, lambda i: (layer,) + (0,) * len(blk), **kw)
    msel = pl.BlockSpec((None, 1, D), sel)
    return pl.pallas_call(
        functools.partial(_merge_body, nlat=nlat, nvalid=nvalid, alpha=alpha),
        out_shape=[SDS((nt, D), F32), SDS((nt, D), BF16), SDS((SUB, nt), F32), SDS((ntl, SUB, LANE), F32)],
        grid=(ntl,),
        in_specs=[row(BW), row(BW),
                  pl.BlockSpec((TM, BW), lambda i: (jnp.minimum(i, nlat - 1), 0)),
                  pl.BlockSpec((TM, BW), lambda i: (0, 0)),
                  row(BW), row(4 * D), row(D),
                  lsel(4, BW, D, pipeline_mode=pl.Buffered(1)), lsel(D, D, pipeline_mode=pl.Buffered(1)),
                  msel, lsel(1, D), lsel(1, D), msel, msel,
                  lsel(D, LANE), lsel(1, LANE)],
        out_specs=[row(D), row(D), pl.BlockSpec((SUB, TM), lambda i: (0, i)),
                   pl.BlockSpec((1, SUB, LANE), lambda i: (i, 0, 0))],
        compiler_params=_cparams(("parallel",), 48),
        name="merge_router",
    )(att, ypool, yhl, yhc, ysc, gate, x, wb_bf, wo_bf, g1, ln_g, ln_b, sh2, sc2, rw, rb)


def _piece_loops(np_ref, so_ref, do_ref, j, fn):
    for e in range(N_EXPERTS):
        n = np_ref[j * N_EXPERTS + e]
        so = so_ref[j * N_EXPERTS + e]
        do = do_ref[j * N_EXPERTS + e]

        def body(p, carry, so=so, do=do):
            fn(pl.multiple_of(so + p * MOE_PIECE, MOE_PIECE), pl.multiple_of(do + p * MOE_PIECE, MOE_PIECE))
            return carry

        lax.fori_loop(0, n, body, 0)


def _dispatch_body(np_ref, so_ref, do_ref, nv_ref, m_ref, rt_ref, u_ref, hs_hbm, pos_ref, hs_vmem, sem, *,
                   ntiles, nblk):
    j = pl.program_id(0)
    real = j < ntiles
    rt = rt_ref[...]
    e0 = jnp.where(real, rt[0:1].astype(I32), -1)
    e1 = jnp.where(real, rt[1:2].astype(I32), -1)
    ei = lax.broadcasted_iota(I32, (N_EXPERTS, TM1), 0)
    oh0 = (ei == e0).astype(F32)
    oh1 = (ei == e1).astype(F32)
    c0 = jnp.dot(oh0.astype(BF16), u_ref[...], preferred_element_type=F32)
    c1 = jnp.dot(oh1.astype(BF16), u_ref[...], preferred_element_type=F32)
    n0 = jnp.sum(oh0, axis=1, keepdims=True)
    ecol = lax.broadcasted_iota(I32, (N_EXPERTS, 1), 0)
    toff = jnp.zeros((N_EXPERTS, 1), F32)
    for e in range(N_EXPERTS):
        toff = jnp.where(ecol == e, so_ref[j * N_EXPERTS + e].astype(F32), toff)
    pos0 = jnp.sum(oh0 * (toff + c0), axis=0, keepdims=True)
    pos1 = jnp.sum(oh1 * (toff + n0 + c1), axis=0, keepdims=True)
    pos0 = jnp.where(e0 >= 0, pos0, -1.0)
    pos1 = jnp.where(e1 >= 0, pos1, -1.0)
    ri = lax.broadcasted_iota(I32, (SUB, TM1), 0)
    pos_ref[...] = jnp.where(ri == 0, pos0, jnp.where(ri == 1, pos1, 0.0))
    si = lax.broadcasted_iota(I32, (MOE_S, TM1), 0)
    perm = jnp.logical_or(si == pos0.astype(I32), si == pos1.astype(I32)).astype(BF16)
    hs_vmem[...] = jnp.dot(perm, m_ref[...], preferred_element_type=F32)

    def copy(so, do):
        return pltpu.make_async_copy(hs_vmem.at[pl.ds(so, MOE_PIECE)], hs_hbm.at[pl.ds(do, MOE_PIECE)], sem)

    _piece_loops(np_ref, so_ref, do_ref, j, lambda so, do: copy(so, do).start())
    _piece_loops(np_ref, so_ref, do_ref, j, lambda so, do: copy(so, do).wait())

    @pl.when(j == ntiles)
    def _():
        def blk_copy(b):
            return pltpu.make_async_copy(hs_vmem.at[pl.ds(0, MOE_MB)],
                                         hs_hbm.at[pl.ds(pl.multiple_of(b * MOE_MB, MOE_MB), MOE_MB)], sem)

        def start(b, carry):
            blk_copy(b).start()
            return carry

        def wait(b, carry):
            blk_copy(b).wait()
            return carry

        lax.fori_loop(nv_ref[0], nblk, start, 0)
        lax.fori_loop(nv_ref[0], nblk, wait, 0)


def _dispatch(npieces, soff, doff, nvb, m_bf, rt, upper, rmax):
    nt = m_bf.shape[0]
    ntiles = nt // TM1
    last = ntiles - 1
    return pl.pallas_call(
        functools.partial(_dispatch_body, ntiles=ntiles, nblk=rmax // MOE_MB),
        out_shape=[SDS((rmax, D), F32), SDS((SUB, (ntiles + 1) * TM1), F32)],
        grid_spec=pltpu.PrefetchScalarGridSpec(
            num_scalar_prefetch=4,
            grid=(ntiles + 1,),
            in_specs=[pl.BlockSpec((TM1, D), lambda j, *_: (jnp.minimum(j, last), 0)),
                      pl.BlockSpec((SUB, TM1), lambda j, *_: (0, jnp.minimum(j, last))),
                      pl.BlockSpec((TM1, TM1), lambda j, *_: (0, 0))],
            out_specs=[pl.BlockSpec(memory_space=pl.ANY),
                       pl.BlockSpec((SUB, TM1), lambda j, *_: (0, j))],
            scratch_shapes=[pltpu.VMEM((MOE_S, D), F32), pltpu.SemaphoreType.DMA(())]),
        compiler_params=_cparams(("arbitrary",), 40),
        name="moe_dispatch",
    )(npieces, soff, doff, nvb, m_bf, rt, upper)


def _expert_body(be_ref, nv_ref, x_ref, w1_ref, w3_ref, w2_ref, o_ref):
    b = pl.program_id(0)

    @pl.when(b < nv_ref[0])
    def _():
        xb = x_ref[...].astype(BF16)
        h1 = jnp.dot(xb, w1_ref[...].astype(BF16), preferred_element_type=F32)
        h3 = jnp.dot(xb, w3_ref[...].astype(BF16), preferred_element_type=F32)
        hh = (h1 / (1.0 + jnp.exp(-h1))) * h3
        o_ref[...] = jnp.dot(hh.astype(BF16), w2_ref[...].astype(BF16), preferred_element_type=F32)

    @pl.when(b >= nv_ref[0])
    def _():
        o_ref[...] = jnp.zeros_like(o_ref)


def _experts(blk_e, nvb, hs, ex_w1, ex_w3, ex_w2, layer):
    rmax = hs.shape[0]
    nb = rmax // MOE_MB
    bi = lambda b, be, nv: jnp.minimum(b, nv[0] - 1)
    return pl.pallas_call(
        _expert_body,
        out_shape=SDS((rmax, D), F32),
        grid_spec=pltpu.PrefetchScalarGridSpec(
            num_scalar_prefetch=2,
            grid=(nb,),
            in_specs=[pl.BlockSpec((MOE_MB, D), lambda b, be, nv: (bi(b, be, nv), 0)),
                      pl.BlockSpec((None, None, D, EXPERT_HIDDEN), lambda b, be, nv: (layer, be[bi(b, be, nv)], 0, 0)),
                      pl.BlockSpec((None, None, D, EXPERT_HIDDEN), lambda b, be, nv: (layer, be[bi(b, be, nv)], 0, 0)),
                      pl.BlockSpec((None, None, EXPERT_HIDDEN, D), lambda b, be, nv: (layer, be[bi(b, be, nv)], 0, 0))],
            out_specs=pl.BlockSpec((MOE_MB, D), lambda b, be, nv: (b, 0))),
        compiler_params=_cparams(("arbitrary",), 48),
        name="moe_experts",
    )(blk_e, nvb, hs, ex_w1, ex_w3, ex_w2)


def _combine_body(np_ref, so_ref, do_ref, ys_hbm, pos_ref, rt_ref, x1_ref, g2_ref, lg_ref, lb_ref,
                  o_ref, ys_vmem, sem, *, alpha):
    j = pl.program_id(0)

    def copy(so, do):
        return pltpu.make_async_copy(ys_hbm.at[pl.ds(do, MOE_PIECE)], ys_vmem.at[pl.ds(so, MOE_PIECE)], sem)

    _piece_loops(np_ref, so_ref, do_ref, j, lambda so, do: copy(so, do).start())
    z = jnp.concatenate([pos_ref[...], rt_ref[...], jnp.zeros((LANE - 2 * SUB, TM1), F32)], axis=0)
    zt = z.T
    p0 = zt[:, 0:1].astype(I32)
    p1 = zt[:, 1:2].astype(I32)
    w0 = zt[:, SUB + 2:SUB + 3]
    w1 = zt[:, SUB + 3:SUB + 4]
    si = lax.broadcasted_iota(I32, (TM1, MOE_S), 1)
    wm = (jnp.where(si == p0, w0, 0.0) + jnp.where(si == p1, w1, 0.0)).astype(BF16)
    _piece_loops(np_ref, so_ref, do_ref, j, lambda so, do: copy(so, do).wait())
    last = j * N_EXPERTS + N_EXPERTS - 1
    total = so_ref[last] + np_ref[last] * MOE_PIECE
    srow = lax.broadcasted_iota(I32, (MOE_S, 1), 0)
    ys = jnp.where(srow < total, ys_vmem[...], 0.0).astype(BF16)
    f = jnp.dot(wm, ys, preferred_element_type=F32)
    o_ref[...] = _layer_norm(alpha * x1_ref[...] + g2_ref[...] * f) * lg_ref[...] + lb_ref[...]


def _combine(npieces, soff, doff, ys, pos, rt, x1, g2, ln_g, ln_b, layer, n_lat, n_out_tiles, n_out_rows, alpha):
    nlat = n_lat // TM1
    sel = lambda j, *_: (jnp.where(j >= nlat, 1, 0), 0, 0)
    lsel = lambda *blk: pl.BlockSpec((None,) + blk, lambda j, *_: (layer,) + (0,) * len(blk))
    return pl.pallas_call(
        functools.partial(_combine_body, alpha=alpha),
        out_shape=SDS((n_out_rows, D), F32),
        grid_spec=pltpu.PrefetchScalarGridSpec(
            num_scalar_prefetch=3,
            grid=(n_out_tiles,),
            in_specs=[pl.BlockSpec(memory_space=pl.ANY),
                      pl.BlockSpec((SUB, TM1), lambda j, *_: (0, j)),
                      pl.BlockSpec((SUB, TM1), lambda j, *_: (0, j)),
                      pl.BlockSpec((TM1, D), lambda j, *_: (j, 0)),
                      pl.BlockSpec((None, 1, D), sel), lsel(1, D), lsel(1, D)],
            out_specs=pl.BlockSpec((TM1, D), lambda j, *_: (j, 0)),
            scratch_shapes=[pltpu.VMEM((MOE_S, D), F32), pltpu.SemaphoreType.DMA(())]),
        compiler_params=_cparams(("arbitrary",), 40),
        name="moe_combine",
    )(npieces, soff, doff, ys, pos, rt, x1, g2, ln_g, ln_b)


def _moe_plan(cnt, ntiles):
    c = cnt[:, 0, :N_EXPERTS].astype(I32).reshape(ntiles, TM1 // TM, N_EXPERTS).sum(axis=1)
    pad8 = (c + MOE_PIECE - 1) // MOE_PIECE * MOE_PIECE
    toff = jnp.cumsum(pad8, axis=1) - pad8
    tot = pad8.sum(axis=0)
    totb = (tot + MOE_MB - 1) // MOE_MB * MOE_MB
    ends = jnp.cumsum(totb)
    base = ends - totb
    dest = base[None, :] + jnp.cumsum(pad8, axis=0) - pad8
    npieces = jnp.concatenate([pad8 // MOE_PIECE, ((totb - tot) // MOE_PIECE)[None, :]], axis=0)
    soff = jnp.concatenate([toff, jnp.zeros((1, N_EXPERTS), I32)], axis=0)
    doff = jnp.concatenate([dest, (base + tot)[None, :]], axis=0)
    return npieces.reshape(-1), soff.reshape(-1), doff.reshape(-1), ends


def _rope_tables(n_lat, nt):
    t = jnp.arange(n_lat)
    pos = jnp.stack([(t // GRID_W).astype(F32), (t % GRID_W).astype(F32)], axis=1)
    inv = ROPE_BASE ** (-jnp.arange(ROPE_PAIRS, dtype=F32) / ROPE_PAIRS)
    ang = pos[:, :, None] * inv[None, None, :]
    cos = jnp.repeat(jnp.cos(ang), 2, axis=1).reshape(n_lat, 2, 2, ROPE_PAIRS)
    sin = jnp.sin(ang)
    sin = jnp.stack([-sin, sin], axis=2)
    cos = jnp.tile(cos.reshape(n_lat, HEAD_DIM), (1, LANE // HEAD_DIM))
    sin = jnp.tile(sin.reshape(n_lat, HEAD_DIM), (1, LANE // HEAD_DIM))
    cos = jnp.concatenate([cos, jnp.ones((nt - n_lat, LANE), F32)], axis=0)
    sin = jnp.concatenate([sin, jnp.zeros((nt - n_lat, LANE), F32)], axis=0)
    return cos, sin


def _hy_features(l):
    t01 = jnp.linspace(0.0, 1.0, l, dtype=F32)
    fr = jnp.linspace(1e-4, HY_BANDS - 1, HY_BANDS, dtype=F32)
    ang = (2.0 * math.pi / l) * jnp.arange(l, dtype=F32)[:, None] * fr[None, :]
    feat = jnp.concatenate([t01[:, None], jnp.cos(ang), -jnp.sin(ang)], axis=-1)
    return jnp.pad(feat, ((0, 0), (0, LANE - HY_EMB)))


def kernel(x, c, ctx, c_ctx, ada_w, ada_b, w_in, attn_sink, pool_w, pool_scale, hy_conv_w, hy_conv_b, hy_w1, hy_b1, hy_freq1, hy_w2, hy_b2, hy_freq2, hy_w3, hy_bias, sc_conv_w, w_branch, w_out, ln1_g, ln1_b, ln2_g, ln2_b, rg_w, rg_b, re_w, re_b, ex_w1, ex_w3, ex_w2):
    depth = w_in.shape[0]
    assert x.shape[0] == 1 and ctx.shape[0] == 1 and x.shape[2] == D
    n_lat = x.shape[1]
    n_ctx = ctx.shape[1]
    assert n_ctx == TM and n_lat % TM1 == 0 and n_lat >= 2 * TM1
    nt = n_lat + n_ctx + PADR
    ntiles1 = nt // TM1
    alpha = (2 * depth) ** 0.25

    cos_t, sin_t = _rope_tables(n_lat, nt)
    w_in_bf = w_in.astype(BF16)
    wb_bf = w_branch.astype(BF16)
    wo_bf = w_out.astype(BF16)
    pool_w_bf = pool_w.astype(BF16)
    r3 = lambda a: a.reshape(depth, 1, a.shape[-1])
    tabs_l = _dft_tables(n_lat // (FFT_N1 // 2))
    tabs_c = _dft_tables(n_ctx // (FFT_N1 // 2))
    feat_l, feat_c = _hy_features(n_lat), _hy_features(n_ctx)
    deltas = jnp.abs(jnp.linspace(math.log(HY_TARGET) / HY_SLOW_PCT, math.log(HY_TARGET) / HY_FAST_PCT,
                                  BW, dtype=F32))[None, :]
    w1p = jnp.pad(hy_w1, ((0, 0), (0, LANE - HY_EMB), (0, 0)))
    w3p_bf = hy_w3.reshape(depth, HY_HIDDEN, 2, 2, BW).transpose(0, 1, 3, 2, 4).reshape(depth, HY_HIDDEN, 4 * BW).astype(BF16)
    hyw = (w1p, r3(hy_b1), r3(hy_freq1), hy_w2, r3(hy_b2), r3(hy_freq2), w3p_bf, deltas, hy_bias)
    rw = jnp.concatenate([re_w, rg_w, jnp.zeros((depth, D, LANE - N_EXPERTS - N_GROUPS), F32)], axis=2)
    rb = jnp.concatenate([re_b, rg_b, jnp.full((depth, LANE - N_EXPERTS - N_GROUPS), NEG_INF, F32)], axis=1)
    rb = rb.reshape(depth, 1, LANE)
    upper = (jnp.arange(TM1)[:, None] < jnp.arange(TM1)[None, :]).astype(BF16)
    rmax = -(-(2 * (n_lat + n_ctx) + ntiles1 * N_EXPERTS * (MOE_PIECE - 1) + N_EXPERTS * (MOE_MB - 1)) // MOE_MB) * MOE_MB
    nblk = rmax // MOE_MB

    s8 = jnp.concatenate([c, c_ctx[None, :], jnp.zeros((SUB - 2, D), F32)], axis=0)
    mod = _mod_all(s8, ada_w, ada_b.reshape(depth, 1, 6 * D))

    xs = jnp.concatenate([x[0], ctx[0], jnp.zeros((PADR, D), F32)], axis=0)
    for i in range(depth):
        last = i == depth - 1
        mp = lambda j: mod[i, 0:2, j * D:(j + 1) * D].reshape(2, 1, D)
        sh1, sc1, g1, sh2, sc2, g2 = (mp(j) for j in range(6))

        q, k, v, pool_in, hy_in, sc_in, gate = _inproj(xs, sh1, sc1, cos_t, sin_t, w_in_bf, i, n_lat // TM1)
        att = _attention(attn_sink[i], q, k, v, n_lat)
        ypool, ysc, hx1, hx2, hv = _local_ops(pool_in, sc_in, hy_in, pool_w_bf, pool_scale.reshape(depth, 1, BW),
                                              hy_conv_w, hy_conv_b.reshape(depth, 1, 3 * BW), sc_conv_w, i, n_lat)
        yhl = _hyena(hx1, hx2, hv, feat_l, tabs_l, hyw, i, n_lat, 0)
        if last:
            yhc = jnp.zeros((n_ctx, BW), BF16)
        else:
            yhc = _hyena(hx1, hx2, hv, feat_c, tabs_c, hyw, i, n_ctx, n_lat)
        x1, m_bf, rt, cnt = _merge(att, ypool, yhl, yhc, ysc, gate, xs, wb_bf, wo_bf, g1, r3(ln1_g), r3(ln1_b),
                                   sh2, sc2, rw, rb, i, n_lat, alpha)

        npieces, soff, doff, ends = _moe_plan(cnt, ntiles1)
        nvb = (ends[-1] // MOE_MB).astype(I32).reshape(1)
        blk_e = jnp.minimum(jnp.searchsorted(ends, jnp.arange(nblk, dtype=I32) * MOE_MB, side="right"),
                            N_EXPERTS - 1).astype(I32)
        hs, pos = _dispatch(npieces, soff, doff, nvb, m_bf, rt, upper, rmax)
        ys = _experts(blk_e, nvb, hs, ex_w1, ex_w3, ex_w2, i)
        if last:
            xs = _combine(npieces, soff, doff, ys, pos, rt, x1, g2, r3(ln2_g), r3(ln2_b), i, n_lat,
                          n_lat // TM1, n_lat, alpha)
        else:
            xs = _combine(npieces, soff, doff, ys, pos, rt, x1, g2, r3(ln2_g), r3(ln2_b), i, n_lat, ntiles1, nt,
                          alpha)
    return xs[None]
```

```python
import functools
import math

import jax
import jax.numpy as jnp
from jax import lax
from jax.experimental import pallas as pl
from jax.experimental.pallas import tpu as pltpu

F32 = jnp.float32
BF16 = jnp.bfloat16
I32 = jnp.int32
SDS = jax.ShapeDtypeStruct
HIGHEST = lax.Precision.HIGHEST

D = 1024
GRID_W = 64
HEADS = 8
KV_HEADS = 2
HEAD_DIM = 64
WINDOW = 128
ATT_BLK = 128
ROPE_PAIRS = 16
ROPE_BASE = 10000.0
BW = 512
POOL_WINDOWS = (2, 4, 8, 16)
POOL_GW = 128
HY_EMB = 33
HY_BANDS = 16
HY_HIDDEN = 64
HY_TARGET = 1e-2
HY_FAST_PCT = 0.3
HY_SLOW_PCT = 1.5
HY_MOD_SHIFT = 0.05
Q_COLS = HEADS * HEAD_DIM
KV_COLS = KV_HEADS * HEAD_DIM
IN_COLS = Q_COLS + 2 * KV_COLS + BW + 3 * BW + 3 * BW + 4 * D
N_GROUPS = 4
EPG = 8
N_EXPERTS = 32
EXPERT_HIDDEN = 512
LN_EPS = 1e-6
NEG_INF = -1e30

LANE = 128
SUB = 8
BSUB = 16
TM = 256
TM1 = 512
PADR = 256
HALO = 16
FFT_N1 = 512
FFT_K1 = 264
FFT_KB = 8
FFT_TN = 2048
MOE_PIECE = 8
MOE_MB = 256
MOE_S = 2 * TM1 + 256


def _cparams(sem, vmem_mb):
    return pltpu.CompilerParams(dimension_semantics=sem, vmem_limit_bytes=vmem_mb * 2 ** 20)


def _layer_norm(x):
    mu = jnp.mean(x, axis=-1, keepdims=True)
    xc = x - mu
    var = jnp.mean(xc * xc, axis=-1, keepdims=True)
    return xc * lax.rsqrt(var + LN_EPS)


def _mod_body(s_ref, w_ref, b_ref, o_ref):
    s = s_ref[...]
    s = s * (1.0 / (1.0 + jnp.exp(-s)))
    o_ref[...] = jnp.dot(s.astype(BF16), w_ref[...].astype(BF16), preferred_element_type=F32) + b_ref[...]


def _mod_all(s8, ada_w, ada_b3):
    depth = ada_w.shape[0]
    tn = 1024
    return pl.pallas_call(
        _mod_body,
        out_shape=SDS((depth, SUB, 6 * D), F32),
        grid=(depth, 6 * D // tn),
        in_specs=[pl.BlockSpec((SUB, D), lambda l, j: (0, 0)),
                  pl.BlockSpec((None, D, tn), lambda l, j: (l, 0, j)),
                  pl.BlockSpec((None, 1, tn), lambda l, j: (l, 0, j))],
        out_specs=pl.BlockSpec((None, SUB, tn), lambda l, j: (l, 0, j)),
        compiler_params=_cparams(("parallel", "parallel"), 32),
        name="ada_mod",
    )(s8, ada_w, ada_b3)


def _inproj_body(x_ref, sh_ref, sc_ref, cos_ref, sin_ref, w_ref,
                 q_ref, k_ref, v_ref, pool_ref, hy_ref, scv_ref, gate_ref):
    h = _layer_norm(x_ref[...]) * (1.0 + sc_ref[...]) + sh_ref[...]
    hb = h.astype(BF16)

    def mm(a, b):
        return jnp.dot(hb, w_ref[:, a:b], preferred_element_type=F32)

    cs = cos_ref[...]
    sn = sin_ref[...]
    lane = lax.broadcasted_iota(I32, cs.shape, 1)
    first = (lane & ROPE_PAIRS) == 0

    def rope(t):
        sw = jnp.where(first, pltpu.roll(t, LANE - ROPE_PAIRS, 1), pltpu.roll(t, ROPE_PAIRS, 1))
        return t * cs + sw * sn

    for j in range(Q_COLS // LANE):
        q_ref[:, j * LANE:(j + 1) * LANE] = (rope(mm(j * LANE, (j + 1) * LANE)) * (HEAD_DIM ** -0.5)).astype(BF16)
    o = Q_COLS
    k_ref[...] = rope(mm(o, o + KV_COLS)).astype(BF16)
    o += KV_COLS
    v_ref[...] = mm(o, o + KV_COLS).astype(BF16)
    o += KV_COLS
    pool_ref[...] = mm(o, o + BW).astype(BF16)
    o += BW
    for j in range(3):
        hy_ref[:, j * BW:(j + 1) * BW] = mm(o + j * BW, o + (j + 1) * BW).astype(BF16)
    o += 3 * BW
    for j in range(3):
        scv_ref[:, j * BW:(j + 1) * BW] = mm(o + j * BW, o + (j + 1) * BW).astype(BF16)
    o += 3 * BW
    for j in range(4 * D // 1024):
        gate_ref[:, j * 1024:(j + 1) * 1024] = mm(o + j * 1024, o + (j + 1) * 1024).astype(BF16)


def _inproj(x, sh, sc, cos_t, sin_t, w_in_bf, layer, n_lat_tiles):
    nt = x.shape[0]
    sel = lambda i: (jnp.where(i >= n_lat_tiles, 1, 0), 0, 0)
    row = lambda c: pl.BlockSpec((TM1, c), lambda i: (i, 0))
    outs = [SDS((nt, Q_COLS), BF16), SDS((nt, KV_COLS), BF16), SDS((nt, KV_COLS), BF16), SDS((nt, BW), BF16),
            SDS((nt, 3 * BW), BF16), SDS((nt, 3 * BW), BF16), SDS((nt, 4 * D), BF16)]
    return pl.pallas_call(
        _inproj_body,
        out_shape=outs,
        grid=(nt // TM1,),
        in_specs=[row(D),
                  pl.BlockSpec((None, 1, D), sel), pl.BlockSpec((None, 1, D), sel),
                  row(LANE), row(LANE),
                  pl.BlockSpec((None, D, IN_COLS), lambda i: (layer, 0, 0), pipeline_mode=pl.Buffered(1))],
        out_specs=[row(Q_COLS), row(KV_COLS), row(KV_COLS), row(BW), row(3 * BW), row(3 * BW), row(4 * D)],
        compiler_params=_cparams(("parallel",), 58),
        name="inproj",
    )(x, sh, sc, cos_t, sin_t, w_in_bf)


def _attn_body(sink_ref, q_ref, kp_ref, kc_ref, kn_ref, vp_ref, vc_ref, vn_ref, kx_ref, vx_ref, o_ref, *, nlb):
    i = pl.program_id(0)
    is_lat = i < nlb
    lat_i = jnp.where(is_lat, 1, 0)
    prev_i = jnp.where(jnp.logical_and(is_lat, i > 0), 1, 0)
    next_i = jnp.where(jnp.logical_and(is_lat, i < nlb - 1), 1, 0)
    ncx = kx_ref.shape[0]
    nk = ncx + 3 * ATT_BLK

    r = lax.broadcasted_iota(I32, (ATT_BLK, nk), 0)
    c = lax.broadcasted_iota(I32, (ATT_BLK, nk), 1) - ncx
    band = jnp.logical_and(c >= r, c <= r + 2 * WINDOW)
    grp = jnp.where(c < ATT_BLK, prev_i, jnp.where(c < 2 * ATT_BLK, lat_i, next_i))
    mask = jnp.logical_or(c < 0, jnp.logical_and(band, grp > 0))

    kcat = jnp.concatenate([kx_ref[...], kp_ref[...], kc_ref[...], kn_ref[...]], axis=0).astype(F32)
    vcat = jnp.concatenate([vx_ref[...], vp_ref[...], vc_ref[...], vn_ref[...]], axis=0).astype(F32)
    lo = lax.broadcasted_iota(I32, kcat.shape, 1) < HEAD_DIM
    kroll = pltpu.roll(kcat, HEAD_DIM, 1)
    vroll = pltpu.roll(vcat, HEAD_DIM, 1)
    zero = jnp.zeros_like(kcat)
    kvar = [[jnp.where(lo, kcat, zero), jnp.where(lo, zero, kroll)],
            [jnp.where(lo, kroll, zero), jnp.where(lo, zero, kcat)]]
    vvar = [[jnp.where(lo, vcat, zero), jnp.where(lo, zero, vroll)],
            [jnp.where(lo, vroll, zero), jnp.where(lo, zero, vcat)]]
    kvar = [[a.astype(BF16) for a in row] for row in kvar]
    vvar = [[a.astype(BF16) for a in row] for row in vvar]

    for p in range(HEADS // 2):
        qp = q_ref[:, p * LANE:(p + 1) * LANE]
        kh = (2 * p) // (HEADS // KV_HEADS)
        acc = jnp.zeros((ATT_BLK, LANE), F32)
        for rr in range(2):
            sink = sink_ref[2 * p + rr]
            s = lax.dot_general(qp, kvar[kh][rr], (((1,), (1,)), ((), ())), preferred_element_type=F32)
            s = jnp.where(mask, s, NEG_INF)
            m = jnp.maximum(jnp.max(s, axis=1, keepdims=True), sink)
            e = jnp.exp(s - m)
            den = jnp.sum(e, axis=1, keepdims=True) + jnp.exp(sink - m)
            o = jnp.dot(e.astype(BF16), vvar[kh][rr], preferred_element_type=F32)
            acc = acc + o * (1.0 / den)
        o_ref[:, p * LANE:(p + 1) * LANE] = acc.astype(BF16)


def _attention(sink, q, k, v, n_lat):
    nt = q.shape[0]
    nlb = n_lat // ATT_BLK
    cxb = n_lat // TM
    cl = lambda i: jnp.clip(i, 0, nlb - 1)
    kspec = lambda f: pl.BlockSpec((ATT_BLK, KV_COLS), lambda i: (f(i), 0))
    prev = lambda i: jnp.clip(i - 1, 0, nlb - 1)
    nxt = lambda i: jnp.clip(i + 1, 0, nlb - 1)
    cx = pl.BlockSpec((TM, KV_COLS), lambda i: (cxb, 0))
    return pl.pallas_call(
        functools.partial(_attn_body, nlb=nlb),
        out_shape=SDS((nt, Q_COLS), BF16),
        grid=(nt // ATT_BLK,),
        in_specs=[pl.BlockSpec(memory_space=pltpu.SMEM),
                  pl.BlockSpec((ATT_BLK, Q_COLS), lambda i: (i, 0)),
                  kspec(prev), kspec(cl), kspec(nxt), kspec(prev), kspec(cl), kspec(nxt), cx, cx],
        out_specs=pl.BlockSpec((ATT_BLK, Q_COLS), lambda i: (i, 0)),
        compiler_params=_cparams(("parallel",), 32),
        name="attention",
    )(sink, q, k, k, k, v, v, v, k, v)


def _local_body(pm_ref, pa_ref, pb_ref, sm_ref, sa_ref, sb_ref, hm_ref, ha_ref, hb_ref,
                pw_ref, ps_ref, hw_ref, hbias_ref, sw_ref,
                ypool_ref, ysc_ref, x1_ref, x2_ref, v_ref,
                pext, sext, hext, *, nlat, n_lat):
    i = pl.program_id(0)
    is_lat = i < nlat
    prev_ok = jnp.logical_and(is_lat, i > 0)
    next_ok = jnp.logical_and(is_lat, i < nlat - 1)
    pf = jnp.where(prev_ok, 1.0, 0.0)
    nf = jnp.where(next_ok, 1.0, 0.0)

    def fill(ext, a_ref, m_ref, b_ref, cols=None):
        if cols is None:
            a, m, b = a_ref[...], m_ref[...], b_ref[...]
        else:
            a, m, b = a_ref[:, cols], m_ref[:, cols], b_ref[:, cols]
        ext[0:HALO, :] = a.astype(F32) * pf
        ext[HALO:HALO + TM, :] = m.astype(F32)
        ext[HALO + TM:HALO + TM + HALO, :] = b.astype(F32) * nf

    def sh(ext, d, c0=None, c1=None):
        if c0 is None:
            return ext[pl.ds(HALO + d, TM), :]
        return ext[pl.ds(HALO + d, TM), c0:c1]

    fill(pext, pa_ref, pm_ref, pb_ref)
    r = lax.broadcasted_iota(I32, (TM, 1), 0)
    t = jnp.where(is_lat, i * TM, 0) + r
    ln = jnp.where(is_lat, n_lat, TM)
    for g, w in enumerate(POOL_WINDOWS):
        left = w // 2
        right = w - left - 1
        c0, c1 = g * POOL_GW, (g + 1) * POOL_GW
        acc = sh(pext, -left, c0, c1)
        for dd in range(-left + 1, right + 1):
            acc = acc + sh(pext, dd, c0, c1)
        cnt = (jnp.minimum(t + right + 1, ln) - jnp.maximum(t - left, 0)).astype(F32)
        dpool = acc / cnt - sh(pext, 0, c0, c1)
        y = jnp.dot(dpool.astype(BF16), pw_ref[g], preferred_element_type=F32)
        ypool_ref[:, c0:c1] = (y * ps_ref[:, c0:c1]).astype(BF16)

    sext[0:HALO, :] = (sa_ref[:, BW:2 * BW].astype(F32) * sa_ref[:, 2 * BW:3 * BW].astype(F32)) * pf
    sext[HALO:HALO + TM, :] = sm_ref[:, BW:2 * BW].astype(F32) * sm_ref[:, 2 * BW:3 * BW].astype(F32)
    sext[HALO + TM:HALO + TM + HALO, :] = (sb_ref[:, BW:2 * BW].astype(F32) * sb_ref[:, 2 * BW:3 * BW].astype(F32)) * nf
    conv = sh(sext, -1) * sw_ref[0:1, :] + sh(sext, 0) * sw_ref[1:2, :] + sh(sext, 1) * sw_ref[2:3, :]
    ysc_ref[...] = (sm_ref[:, 0:BW].astype(F32) * conv).astype(BF16)

    fill(hext, ha_ref, hm_ref, hb_ref)
    outs = (x1_ref, x2_ref, v_ref)
    for j in range(3):
        c0, c1 = j * BW, (j + 1) * BW
        uc = (sh(hext, -1, c0, c1) * hw_ref[0:1, c0:c1] + sh(hext, 0, c0, c1) * hw_ref[1:2, c0:c1]
              + sh(hext, 1, c0, c1) * hw_ref[2:3, c0:c1] + hbias_ref[:, c0:c1])
        outs[j][...] = uc.astype(BF16)


def _local_ops(pool_in, sc_in, hy_in, pool_w_bf, pool_scale, hy_conv_w, hy_conv_b, sc_conv_w, layer, n_lat):
    nt = pool_in.shape[0]
    nlat = n_lat // TM
    nhb = nt // HALO
    per = TM // HALO
    main = lambda c: pl.BlockSpec((TM, c), lambda i: (i, 0))
    before = lambda c: pl.BlockSpec((HALO, c), lambda i: (jnp.maximum(i * per - 1, 0), 0))
    after = lambda c: pl.BlockSpec((HALO, c), lambda i: (jnp.minimum((i + 1) * per, nhb - 1), 0))
    lsel = lambda *blk: pl.BlockSpec((None,) + blk, lambda i: (layer,) + (0,) * len(blk))
    out = SDS((nt, BW), BF16)
    return pl.pallas_call(
        functools.partial(_local_body, nlat=nlat, n_lat=n_lat),
        out_shape=[out] * 5,
        grid=(nt // TM,),
        in_specs=[main(BW), before(BW), after(BW),
                  main(3 * BW), before(3 * BW), after(3 * BW),
                  main(3 * BW), before(3 * BW), after(3 * BW),
                  lsel(4, POOL_GW, POOL_GW), lsel(1, BW), lsel(3, 3 * BW), lsel(1, 3 * BW), lsel(3, BW)],
        out_specs=[main(BW)] * 5,
        scratch_shapes=[pltpu.VMEM((TM + 2 * HALO, BW), F32), pltpu.VMEM((TM + 2 * HALO, BW), F32),
                        pltpu.VMEM((TM + 2 * HALO, 3 * BW), F32)],
        compiler_params=_cparams(("parallel",), 32),
        name="local_ops",
    )(pool_in, pool_in, pool_in, sc_in, sc_in, sc_in, hy_in, hy_in, hy_in,
      pool_w_bf, pool_scale, hy_conv_w, hy_conv_b, sc_conv_w)


def _dft_tables(n2):
    n = FFT_N1 * n2
    k1 = jnp.arange(FFT_K1, dtype=I32)
    n1 = jnp.arange(FFT_N1 // 2, dtype=I32)
    ok = (k1 <= FFT_N1 // 2)
    ang = (2.0 * math.pi / FFT_N1) * ((k1[:, None] * n1[None, :]) % FFT_N1).astype(F32)
    c1 = jnp.where(ok[:, None], jnp.cos(ang), 0.0)
    s1 = jnp.where(ok[:, None], jnp.sin(ang), 0.0)
    f1 = jnp.concatenate([c1, -s1], axis=0).astype(BF16)
    wt = jnp.where((k1 == 0) | (k1 == FFT_N1 // 2), 1.0, 2.0) / n
    f1i = jnp.concatenate([(c1 * wt[:, None]).T, (-s1 * wt[:, None]).T], axis=1).astype(BF16)
    if n2 == 1:
        return f1, f1i, None, None
    k2 = jnp.arange(n2, dtype=I32)
    m2 = jnp.arange(n2, dtype=I32)
    kk = k1[:, None, None] + FFT_N1 * k2[None, :, None]
    th = (2.0 * math.pi / n) * ((kk * m2[None, None, :]) % n).astype(F32)
    okb = ok[:, None, None]
    cc = jnp.where(okb, jnp.cos(th), 0.0)
    ss = jnp.where(okb, jnp.sin(th), 0.0)
    g = jnp.concatenate([jnp.concatenate([cc, ss], axis=2), jnp.concatenate([-ss, cc], axis=2)], axis=1)
    return f1, f1i, g.astype(BF16), jnp.swapaxes(g, 1, 2).astype(BF16)


def _filter_body(feat_ref, w1_ref, b1_ref, f1_ref, w2_ref, b2_ref, f2_ref, w3_ref, dl_ref, dft_ref,
                 af_ref, ss_ref, filt):
    i = pl.program_id(0)
    feat = feat_ref[...]
    h = jnp.sin(f1_ref[...] * (jnp.dot(feat, w1_ref[...], precision=HIGHEST, preferred_element_type=F32) + b1_ref[...]))
    h = jnp.sin(f2_ref[...] * (jnp.dot(h, w2_ref[...], precision=HIGHEST, preferred_element_type=F32) + b2_ref[...]))
    raw = jnp.dot(h.astype(BF16), w3_ref[...], preferred_element_type=F32)
    win = jnp.exp(-feat[:, 0:1] * dl_ref[...]) + HY_MOD_SHIFT
    row = lax.broadcasted_iota(I32, (TM, 1), 0) + i * TM

    @pl.when(i == 0)
    def _():
        ss_ref[...] = jnp.zeros_like(ss_ref)

    for j in range(4):
        f = raw[:, j * BW:(j + 1) * BW] * win
        if j >= 2:
            f = jnp.where(row == 0, 0.0, f)
        filt[:, j * BW:(j + 1) * BW] = f.astype(BF16)
        ss_ref[:, j * BW:(j + 1) * BW] += jnp.sum(f * f, axis=0, keepdims=True)
    af_ref[...] = jnp.dot(dft_ref[...], filt[...], preferred_element_type=F32).astype(BF16)


def _hy_filters(feat_perm, w1p, b1, fr1, w2, b2, fr2, w3p_bf, deltas, f1, layer):
    l = feat_perm.shape[0]
    n2 = l // TM
    lsel = lambda *blk: pl.BlockSpec((None,) + blk, lambda i: (layer,) + (0,) * len(blk))
    return pl.pallas_call(
        _filter_body,
        out_shape=[SDS((2 * FFT_K1, n2 * 4 * BW), BF16), SDS((1, 4 * BW), F32)],
        grid=(n2,),
        in_specs=[pl.BlockSpec((TM, LANE), lambda i: (i, 0)),
                  lsel(LANE, HY_HIDDEN), lsel(1, HY_HIDDEN), lsel(1, HY_HIDDEN),
                  lsel(HY_HIDDEN, HY_HIDDEN), lsel(1, HY_HIDDEN), lsel(1, HY_HIDDEN),
                  lsel(HY_HIDDEN, 4 * BW), pl.BlockSpec((1, BW), lambda i: (0, 0)),
                  pl.BlockSpec((2 * FFT_K1, FFT_N1 // 2), lambda i: (0, 0))],
        out_specs=[pl.BlockSpec((2 * FFT_K1, 4 * BW), lambda i: (0, i)), pl.BlockSpec((1, 4 * BW), lambda i: (0, 0))],
        scratch_shapes=[pltpu.VMEM((TM, 4 * BW), BF16)],
        compiler_params=_cparams(("arbitrary",), 32),
        name="hy_filters",
    )(feat_perm, w1p, b1, fr1, w2, b2, fr2, w3p_bf, deltas, f1)


def _dft_major_body(f_ref, x_ref, o_ref):
    o_ref[...] = jnp.dot(f_ref[...], x_ref[...], preferred_element_type=F32).astype(BF16)


def _dft_major(f1, x2d, row_blk, ncols):
    tn = min(FFT_TN, ncols)
    return pl.pallas_call(
        _dft_major_body,
        out_shape=SDS((2 * FFT_K1, ncols), BF16),
        grid=(ncols // tn,),
        in_specs=[pl.BlockSpec((2 * FFT_K1, FFT_N1 // 2), lambda j: (0, 0)),
                  pl.BlockSpec((FFT_N1 // 2, tn), lambda j: (row_blk, j))],
        out_specs=pl.BlockSpec((2 * FFT_K1, tn), lambda j: (0, j)),
        compiler_params=_cparams(("parallel",), 32),
        name="dft_major",
    )(f1, x2d)


def _filter_minor_body(a_ref, g_ref, ss_ref, h_ref, *, n2):
    ssv = ss_ref[...]
    scale = lax.rsqrt(ssv[:, 0:2 * BW] + ssv[:, 2 * BW:4 * BW] + LN_EPS)
    for j in range(a_ref.shape[1]):
        if n2 == 1:
            s = jnp.concatenate([a_ref[0, j], a_ref[1, j]], axis=0).astype(F32)
        else:
            a = jnp.concatenate([a_ref[0, j], a_ref[1, j]], axis=0)
            s = jnp.dot(g_ref[j], a, preferred_element_type=F32)
        sf, sb = s[:, 0:2 * BW], s[:, 2 * BW:4 * BW]
        hr = (sf[0:n2] + sb[0:n2]) * scale
        hi = (sf[n2:2 * n2] - sb[n2:2 * n2]) * scale
        h_ref[j] = jnp.concatenate([hr, hi], axis=0).astype(BF16)


def _conv_minor_body(a_ref, h_ref, g_ref, gt_ref, d_ref, *, n2):
    for j in range(a_ref.shape[1]):
        a = jnp.concatenate([a_ref[0, j], a_ref[1, j]], axis=0)
        if n2 == 1:
            x = a.astype(F32)
        else:
            x = jnp.dot(g_ref[j], a, preferred_element_type=F32)
        h = h_ref[j].astype(F32)
        xr, xi, hr, hi = x[0:n2], x[n2:2 * n2], h[0:n2], h[n2:2 * n2]
        y = jnp.concatenate([xr * hr - xi * hi, xr * hi + xi * hr], axis=0)
        if n2 == 1:
            dv = y
        else:
            dv = jnp.dot(gt_ref[j], y.astype(BF16), preferred_element_type=F32)
        d_ref[0, j] = dv[0:n2].astype(BF16)
        d_ref[1, j] = dv[n2:2 * n2].astype(BF16)


def _filter_minor(af, g, ss, n2):
    a4 = af.reshape(2, FFT_K1, n2, 4 * BW)
    kb = FFT_KB
    gspec = (pl.BlockSpec((kb, 2 * n2, 2 * n2), lambda i: (i, 0, 0)) if n2 > 1
             else pl.BlockSpec((SUB, LANE), lambda i: (0, 0)))
    return pl.pallas_call(
        functools.partial(_filter_minor_body, n2=n2),
        out_shape=SDS((FFT_K1, 2 * n2, 2 * BW), BF16),
        grid=(FFT_K1 // kb,),
        in_specs=[pl.BlockSpec((2, kb, n2, 4 * BW), lambda i: (0, i, 0, 0)), gspec,
                  pl.BlockSpec((1, 4 * BW), lambda i: (0, 0))],
        out_specs=pl.BlockSpec((kb, 2 * n2, 2 * BW), lambda i: (i, 0, 0)),
        compiler_params=_cparams(("parallel",), 40),
        name="hy_filter_minor",
    )(a4, g if n2 > 1 else jnp.zeros((SUB, LANE), BF16), ss)


def _conv_minor(a, hspec, g, gt, order, n2):
    a4 = a.reshape(2, FFT_K1, n2, BW)
    kb = FFT_KB
    dummy = jnp.zeros((SUB, LANE), BF16)
    gspec = (pl.BlockSpec((kb, 2 * n2, 2 * n2), lambda i: (i, 0, 0)) if n2 > 1
             else pl.BlockSpec((SUB, LANE), lambda i: (0, 0)))
    d4 = pl.pallas_call(
        functools.partial(_conv_minor_body, n2=n2),
        out_shape=SDS((2, FFT_K1, n2, BW), BF16),
        grid=(FFT_K1 // kb,),
        in_specs=[pl.BlockSpec((2, kb, n2, BW), lambda i: (0, i, 0, 0)),
                  pl.BlockSpec((kb, 2 * n2, BW), lambda i: (i, 0, order)), gspec, gspec],
        out_specs=pl.BlockSpec((2, kb, n2, BW), lambda i: (0, i, 0, 0)),
        compiler_params=_cparams(("parallel",), 32),
        name="hy_conv_minor",
    )(a4, hspec, g if n2 > 1 else dummy, gt if n2 > 1 else dummy)
    return d4.reshape(2 * FFT_K1, n2 * BW)


def _idft_gate_body(f_ref, d_ref, gate_ref, z_ref, b_ref, o_ref):
    y = jnp.dot(f_ref[...], d_ref[...], preferred_element_type=F32)
    z = z_ref[...].astype(F32)
    o_ref[...] = (gate_ref[...].astype(F32) * (y + b_ref[...] * z)).astype(BF16)


def _idft_gate(f1i, d, gate2d, gate_blk, z2d, z_blk, bias_t, ncols):
    tn = min(FFT_TN, ncols)
    return pl.pallas_call(
        _idft_gate_body,
        out_shape=SDS((FFT_N1 // 2, ncols), BF16),
        grid=(ncols // tn,),
        in_specs=[pl.BlockSpec((FFT_N1 // 2, 2 * FFT_K1), lambda j: (0, 0)),
                  pl.BlockSpec((2 * FFT_K1, tn), lambda j: (0, j)),
                  pl.BlockSpec((FFT_N1 // 2, tn), lambda j: (gate_blk, j)),
                  pl.BlockSpec((FFT_N1 // 2, tn), lambda j: (z_blk, j)),
                  pl.BlockSpec((1, tn), lambda j: (0, 0))],
        out_specs=pl.BlockSpec((FFT_N1 // 2, tn), lambda j: (0, j)),
        compiler_params=_cparams(("parallel",), 32),
        name="idft_gate",
    )(f1i, d, gate2d, z2d, bias_t)


def _hyena(x1, x2, v, feat, tabs, hyw, layer, seq_len, row0):
    n2 = seq_len // (FFT_N1 // 2)
    f1, f1i, g, gt = tabs
    w1p, b1, fr1, w2, b2, fr2, w3p_bf, deltas, hy_bias = hyw
    nt = x1.shape[0]
    af, ss = _hy_filters(feat, w1p, b1, fr1, w2, b2, fr2, w3p_bf, deltas, f1, layer)
    hspec = _filter_minor(af, g, ss, n2)
    view = lambda a: a.reshape(nt // n2, n2 * BW)
    blk = row0 // seq_len if n2 == 1 else 0
    ncols = n2 * BW
    tn = min(FFT_TN, ncols)
    z, zv, zblk = v, view(v), blk
    gates = (x1, x2)
    for o in range(2):
        a = _dft_major(f1, zv, zblk, ncols)
        d = _conv_minor(a, hspec, g, gt, o, n2)
        bias_t = jnp.tile(hy_bias[layer, o][None, :], (1, tn // BW))
        z2 = _idft_gate(f1i, d, view(gates[o]), blk, zv, zblk, bias_t, ncols)
        zv, zblk = z2, 0
    return zv.reshape(seq_len, BW)


def _merge_body(att_ref, yp_ref, yhl_ref, yhc_ref, ys_ref, gate_ref, x_ref, wb_ref, wo_ref,
                g1_ref, lg_ref, lb_ref, sh2_ref, sc2_ref, rw_ref, rb_ref,
                x1_ref, m_ref, rt_ref, cnt_ref, *, nlat, nvalid, alpha):
    i = pl.program_id(0)
    yh = jnp.where(i >= nlat, yhc_ref[...], yhl_ref[...])
    ys = (att_ref[...], yp_ref[...], yh, ys_ref[...])
    merged = jnp.zeros((TM, D), F32)
    for n in range(4):
        br = jnp.dot(ys[n], wb_ref[n], preferred_element_type=F32)
        gl = gate_ref[:, n * D:(n + 1) * D].astype(F32)
        merged = merged + br / (1.0 + jnp.exp(-gl))
    y = jnp.dot(merged.astype(BF16), wo_ref[...], preferred_element_type=F32)
    x1 =_layer_norm(alpha * x_ref[...] + g1_ref[...] * y) * lg_ref[...] + lb_ref[...]
    x1_ref[...] = x1
    m = _layer_norm(x1) * (1.0 + sc2_ref[...]) + sh2_ref[...]
    m_ref[...] = m.astype(BF16)

    logits = jnp.dot(m, rw_ref[...], precision=HIGHEST, preferred_element_type=F32) + rb_ref[...]
    lt = logits.T
    le = lt[0:N_EXPERTS]
    lgp = lt[N_EXPERTS:N_EXPERTS + SUB]
    big = 1 << 20
    gi = lax.broadcasted_iota(I32, lgp.shape, 0)
    gmax = jnp.max(lgp, axis=0, keepdims=True)
    gsel = jnp.min(jnp.where(lgp == gmax, gi, big), axis=0, keepdims=True)
    gate_g = 1.0 / jnp.sum(jnp.exp(lgp - gmax), axis=0, keepdims=True)
    ei = lax.broadcasted_iota(I32, le.shape, 0)
    lem = jnp.where(lax.shift_right_logical(ei, 3) == gsel, le, -3.0e38)
    v1 = jnp.max(lem, axis=0, keepdims=True)
    i1 = jnp.min(jnp.where(lem == v1, ei, big), axis=0, keepdims=True)
    lem2 = jnp.where(ei == i1, -3.0e38, lem)
    v2 = jnp.max(lem2, axis=0, keepdims=True)
    i2 = jnp.min(jnp.where(lem2 == v2, ei, big), axis=0, keepdims=True)
    e2 = jnp.exp(v2 - v1)
    wa = gate_g / (1.0 + e2)
    wb = gate_g * e2 / (1.0 + e2)
    valid = i < nvalid
    i1 = jnp.where(valid, i1, -1)
    i2 = jnp.where(valid, i2, -1)
    ri = lax.broadcasted_iota(I32, (SUB, TM), 0)
    rt = jnp.where(ri == 0, i1.astype(F32), jnp.where(ri == 1, i2.astype(F32),
                   jnp.where(ri == 2, wa, jnp.where(ri == 3, wb, 0.0))))
    rt_ref[...] = rt
    ci = lax.broadcasted_iota(I32, (LANE, TM), 0)
    oh = jnp.logical_or(ci == i1, ci == i2).astype(BF16)
    cnt_ref[0] = lax.dot_general(jnp.ones((SUB, TM), BF16), oh, (((1,), (1,)), ((), ())),
                                 preferred_element_type=F32)


def _merge(att, ypool, yhl, yhc, ysc, gate, x, wb_bf, wo_bf, g1, ln_g, ln_b, sh2, sc2, rw, rb, layer, n_lat, alpha):
    nt = x.shape[0]
    nlat = n_lat // TM
    nvalid = nlat + 1
    ntl = nt // TM
    sel = lambda i: (jnp.where(i >= nlat, 1, 0), 0, 0)
    row = lambda c: pl.BlockSpec((TM, c), lambda i: (i, 0))
    lsel = lambda *blk, **kw: pl.BlockSpec((None,) + blk, lambda i: (layer,) + (0,) * len(blk), **kw)
    msel = pl.BlockSpec((None, 1, D), sel)
    return pl.pallas_call(
        functools.partial(_merge_body, nlat=nlat, nvalid=nvalid, alpha=alpha),
        out_shape=[SDS((nt, D), F32), SDS((nt, D), BF16), SDS((SUB, nt), F32), SDS((ntl, SUB, LANE), F32)],
        grid=(ntl,),
        in_specs=[row(BW), row(BW),
                  pl.BlockSpec((TM, BW), lambda i: (jnp.minimum(i, nlat - 1), 0)),
                  pl.BlockSpec((TM, BW), lambda i: (0, 0)),
                  row(BW), row(4 * D), row(D),
                  lsel(4, BW, D, pipeline_mode=pl.Buffered(1)), lsel(D, D, pipeline_mode=pl.Buffered(1)),
                  msel, lsel(1, D), lsel(1, D), msel, msel,
                  lsel(D, LANE), lsel(1, LANE)],
        out_specs=[row(D), row(D), pl.BlockSpec((SUB, TM), lambda i: (0, i)),
                   pl.BlockSpec((1, SUB, LANE), lambda i: (i, 0, 0))],
        compiler_params=_cparams(("parallel",), 48),
        name="merge_router",
    )(att, ypool, yhl, yhc, ysc, gate, x, wb_bf, wo_bf, g1, ln_g, ln_b, sh2, sc2, rw, rb)


def _piece_loops(np_ref, so_ref, do_ref, j, fn):
    for e in range(N_EXPERTS):
        n = np_ref[j * N_EXPERTS + e]
        so = so_ref[j * N_EXPERTS + e]
        do = do_ref[j * N_EXPERTS + e]

        def body(p, carry, so=so, do=do):
            fn(pl.multiple_of(so + p * MOE_PIECE, MOE_PIECE), pl.multiple_of(do + p * MOE_PIECE, MOE_PIECE))
            return carry

        lax.fori_loop(0, n, body, 0)


def _dispatch_body(np_ref, so_ref, do_ref, nv_ref, m_ref, rt_ref, u_ref, hs_hbm, pos_ref, hs_vmem, sem, *,
                   ntiles, nblk):
    j = pl.program_id(0)
    real = j < ntiles
    rt = rt_ref[...]
    e0 = jnp.where(real, rt[0:1].astype(I32), -1)
    e1 = jnp.where(real, rt[1:2].astype(I32), -1)
    ei = lax.broadcasted_iota(I32, (N_EXPERTS, TM1), 0)
    oh0 = (ei == e0).astype(F32)
    oh1 = (ei == e1).astype(F32)
    c0 = jnp.dot(oh0.astype(BF16), u_ref[...], preferred_element_type=F32)
    c1 = jnp.dot(oh1.astype(BF16), u_ref[...], preferred_element_type=F32)
    n0 = jnp.sum(oh0, axis=1, keepdims=True)
    ecol = lax.broadcasted_iota(I32, (N_EXPERTS, 1), 0)
    toff = jnp.zeros((N_EXPERTS, 1), F32)
    for e in range(N_EXPERTS):
        toff = jnp.where(ecol == e, so_ref[j * N_EXPERTS + e].astype(F32), toff)
    pos0 = jnp.sum(oh0 * (toff + c0), axis=0, keepdims=True)
    pos1 = jnp.sum(oh1 * (toff + n0 + c1), axis=0, keepdims=True)
    pos0 = jnp.where(e0 >= 0, pos0, -1.0)
    pos1 = jnp.where(e1 >= 0, pos1, -1.0)
    ri = lax.broadcasted_iota(I32, (SUB, TM1), 0)
    pos_ref[...] = jnp.where(ri == 0, pos0, jnp.where(ri == 1, pos1, 0.0))
    si = lax.broadcasted_iota(I32, (MOE_S, TM1), 0)
    perm = jnp.logical_or(si == pos0.astype(I32), si == pos1.astype(I32)).astype(BF16)
    hs_vmem[...] = jnp.dot(perm, m_ref[...], preferred_element_type=F32)

    def copy(so, do):
        return pltpu.make_async_copy(hs_vmem.at[pl.ds(so, MOE_PIECE)], hs_hbm.at[pl.ds(do, MOE_PIECE)], sem)

    _piece_loops(np_ref, so_ref, do_ref, j, lambda so, do: copy(so, do).start())
    _piece_loops(np_ref, so_ref, do_ref, j, lambda so, do: copy(so, do).wait())

    @pl.when(j == ntiles)
    def _():
        def blk_copy(b):
            return pltpu.make_async_copy(hs_vmem.at[pl.ds(0, MOE_MB)],
                                         hs_hbm.at[pl.ds(pl.multiple_of(b * MOE_MB, MOE_MB), MOE_MB)], sem)

        def start(b, carry):
            blk_copy(b).start()
            return carry

        def wait(b, carry):
            blk_copy(b).wait()
            return carry

        lax.fori_loop(nv_ref[0], nblk, start, 0)
        lax.fori_loop(nv_ref[0], nblk, wait, 0)


def _dispatch(npieces, soff, doff, nvb, m_bf, rt, upper, rmax):
    nt = m_bf.shape[0]
    ntiles = nt // TM1
    last = ntiles - 1
    return pl.pallas_call(
        functools.partial(_dispatch_body, ntiles=ntiles, nblk=rmax // MOE_MB),
        out_shape=[SDS((rmax, D), F32), SDS((SUB, (ntiles + 1) * TM1), F32)],
        grid_spec=pltpu.PrefetchScalarGridSpec(
            num_scalar_prefetch=4,
            grid=(ntiles + 1,),
            in_specs=[pl.BlockSpec((TM1, D), lambda j, *_: (jnp.minimum(j, last), 0)),
                      pl.BlockSpec((SUB, TM1), lambda j, *_: (0, jnp.minimum(j, last))),
                      pl.BlockSpec((TM1, TM1), lambda j, *_: (0, 0))],
            out_specs=[pl.BlockSpec(memory_space=pl.ANY),
                       pl.BlockSpec((SUB, TM1), lambda j, *_: (0, j))],
            scratch_shapes=[pltpu.VMEM((MOE_S, D), F32), pltpu.SemaphoreType.DMA(())]),
        compiler_params=_cparams(("arbitrary",), 40),
        name="moe_dispatch",
    )(npieces, soff, doff, nvb, m_bf, rt, upper)


def _expert_body(be_ref, nv_ref, x_ref, w1_ref, w3_ref, w2_ref, o_ref):
    b = pl.program_id(0)

    @pl.when(b < nv_ref[0])
    def _():
        xb = x_ref[...].astype(BF16)
        h1 = jnp.dot(xb, w1_ref[...].astype(BF16), preferred_element_type=F32)
        h3 = jnp.dot(xb, w3_ref[...].astype(BF16), preferred_element_type=F32)
        hh = (h1 / (1.0 + jnp.exp(-h1))) * h3
        o_ref[...] = jnp.dot(hh.astype(BF16), w2_ref[...].astype(BF16), preferred_element_type=F32)

    @pl.when(b >= nv_ref[0])
    def _():
        o_ref[...] = jnp.zeros_like(o_ref)


def _experts(blk_e, nvb, hs, ex_w1, ex_w3, ex_w2, layer):
    rmax = hs.shape[0]
    nb = rmax // MOE_MB
    bi = lambda b, be, nv: jnp.minimum(b, nv[0] - 1)
    return pl.pallas_call(
        _expert_body,
        out_shape=SDS((rmax, D), F32),
        grid_spec=pltpu.PrefetchScalarGridSpec(
            num_scalar_prefetch=2,
            grid=(nb,),
            in_specs=[pl.BlockSpec((MOE_MB, D), lambda b, be, nv: (bi(b, be, nv), 0)),
                      pl.BlockSpec((None, None, D, EXPERT_HIDDEN), lambda b, be, nv: (layer, be[bi(b, be, nv)], 0, 0)),
                      pl.BlockSpec((None, None, D, EXPERT_HIDDEN), lambda b, be, nv: (layer, be[bi(b, be, nv)], 0, 0)),
                      pl.BlockSpec((None, None, EXPERT_HIDDEN, D), lambda b, be, nv: (layer, be[bi(b, be, nv)], 0, 0))],
            out_specs=pl.BlockSpec((MOE_MB, D), lambda b, be, nv: (b, 0))),
        compiler_params=_cparams(("arbitrary",), 48),
        name="moe_experts",
    )(blk_e, nvb, hs, ex_w1, ex_w3, ex_w2)


def _combine_body(np_ref, so_ref, do_ref, ys_hbm, pos_ref, rt_ref, x1_ref, g2_ref, lg_ref, lb_ref,
                  o_ref, ys_vmem, sem, *, alpha):
    j = pl.program_id(0)

    def copy(so, do):
        return pltpu.make_async_copy(ys_hbm.at[pl.ds(do, MOE_PIECE)], ys_vmem.at[pl.ds(so, MOE_PIECE)], sem)

    _piece_loops(np_ref, so_ref, do_ref, j, lambda so, do: copy(so, do).start())
    z = jnp.concatenate([pos_ref[...], rt_ref[...], jnp.zeros((LANE - 2 * SUB, TM1), F32)], axis=0)
    zt = z.T
    p0 = zt[:, 0:1].astype(I32)
    p1 = zt[:, 1:2].astype(I32)
    w0 = zt[:, SUB + 2:SUB + 3]
    w1 = zt[:, SUB + 3:SUB + 4]
    si = lax.broadcasted_iota(I32, (TM1, MOE_S), 1)
    wm = (jnp.where(si == p0, w0, 0.0) + jnp.where(si == p1, w1, 0.0)).astype(BF16)
    _piece_loops(np_ref, so_ref, do_ref, j, lambda so, do: copy(so, do).wait())
    last = j * N_EXPERTS + N_EXPERTS - 1
    total = so_ref[last] + np_ref[last] * MOE_PIECE
    srow = lax.broadcasted_iota(I32, (MOE_S, 1), 0)
    ys = jnp.where(srow < total, ys_vmem[...], 0.0).astype(BF16)
    f = jnp.dot(wm, ys, preferred_element_type=F32)
    o_ref[...] = _layer_norm(alpha * x1_ref[...] + g2_ref[...] * f) * lg_ref[...] + lb_ref[...]


def _combine(npieces, soff, doff, ys, pos, rt, x1, g2, ln_g, ln_b, layer, n_lat, n_out_tiles, n_out_rows, alpha):
    nlat = n_lat // TM1
    sel = lambda j, *_: (jnp.where(j >= nlat, 1, 0), 0, 0)
    lsel = lambda *blk: pl.BlockSpec((None,) + blk, lambda j, *_: (layer,) + (0,) * len(blk))
    return pl.pallas_call(
        functools.partial(_combine_body, alpha=alpha),
        out_shape=SDS((n_out_rows, D), F32),
        grid_spec=pltpu.PrefetchScalarGridSpec(
            num_scalar_prefetch=3,
            grid=(n_out_tiles,),
            in_specs=[pl.BlockSpec(memory_space=pl.ANY),
                      pl.BlockSpec((SUB, TM1), lambda j, *_: (0, j)),
                      pl.BlockSpec((SUB, TM1), lambda j, *_: (0, j)),
                      pl.BlockSpec((TM1, D), lambda j, *_: (j, 0)),
                      pl.BlockSpec((None, 1, D), sel), lsel(1, D), lsel(1, D)],
            out_specs=pl.BlockSpec((TM1, D), lambda j, *_: (j, 0)),
            scratch_shapes=[pltpu.VMEM((MOE_S, D), F32), pltpu.SemaphoreType.DMA(())]),
        compiler_params=_cparams(("arbitrary",), 40),
        name="moe_combine",
    )(npieces, soff, doff, ys, pos, rt, x1, g2, ln_g, ln_b)


def _moe_plan(cnt, ntiles):
    c = cnt[:, 0, :N_EXPERTS].astype(I32).reshape(ntiles, TM1 // TM, N_EXPERTS).sum(axis=1)
    pad8 = (c + MOE_PIECE - 1) // MOE_PIECE * MOE_PIECE
    toff = jnp.cumsum(pad8, axis=1) - pad8
    tot = pad8.sum(axis=0)
    totb = (tot + MOE_MB - 1) // MOE_MB * MOE_MB
    ends = jnp.cumsum(totb)
    base = ends - totb
    dest = base[None, :] + jnp.cumsum(pad8, axis=0) - pad8
    npieces = jnp.concatenate([pad8 // MOE_PIECE, ((totb - tot) // MOE_PIECE)[None, :]], axis=0)
    soff = jnp.concatenate([toff, jnp.zeros((1, N_EXPERTS), I32)], axis=0)
    doff = jnp.concatenate([dest, (base + tot)[None, :]], axis=0)
    return npieces.reshape(-1), soff.reshape(-1), doff.reshape(-1), ends


def _rope_tables(n_lat, nt):
    t = jnp.arange(n_lat)
    pos = jnp.stack([(t // GRID_W).astype(F32), (t % GRID_W).astype(F32)], axis=1)
    inv = ROPE_BASE ** (-jnp.arange(ROPE_PAIRS, dtype=F32) / ROPE_PAIRS)
    ang = pos[:, :, None] * inv[None, None, :]
    cos = jnp.repeat(jnp.cos(ang), 2, axis=1).reshape(n_lat, 2, 2, ROPE_PAIRS)
    sin = jnp.sin(ang)
    sin = jnp.stack([-sin, sin], axis=2)
    cos = jnp.tile(cos.reshape(n_lat, HEAD_DIM), (1, LANE // HEAD_DIM))
    sin = jnp.tile(sin.reshape(n_lat, HEAD_DIM), (1, LANE // HEAD_DIM))
    cos = jnp.concatenate([cos, jnp.ones((nt - n_lat, LANE), F32)], axis=0)
    sin = jnp.concatenate([sin, jnp.zeros((nt - n_lat, LANE), F32)], axis=0)
    return cos, sin


def _hy_features(l):
    t01 = jnp.linspace(0.0, 1.0, l, dtype=F32)
    fr = jnp.linspace(1e-4, HY_BANDS - 1, HY_BANDS, dtype=F32)
    ang = (2.0 * math.pi / l) * jnp.arange(l, dtype=F32)[:, None] * fr[None, :]
    feat = jnp.concatenate([t01[:, None], jnp.cos(ang), -jnp.sin(ang)], axis=-1)
    feat = jnp.pad(feat, ((0, 0), (0, LANE - HY_EMB)))
    return feat.reshape(FFT_N1 // 2, l // (FFT_N1 // 2), LANE).transpose(1, 0, 2).reshape(l, LANE)


def kernel(x, c, ctx, c_ctx, ada_w, ada_b, w_in, attn_sink, pool_w, pool_scale, hy_conv_w, hy_conv_b, hy_w1, hy_b1, hy_freq1, hy_w2, hy_b2, hy_freq2, hy_w3, hy_bias, sc_conv_w, w_branch, w_out, ln1_g, ln1_b, ln2_g, ln2_b, rg_w, rg_b, re_w, re_b, ex_w1, ex_w3, ex_w2):
    depth = w_in.shape[0]
    assert x.shape[0] == 1 and ctx.shape[0] == 1 and x.shape[2] == D
    n_lat = x.shape[1]
    n_ctx = ctx.shape[1]
    assert n_ctx == TM and n_lat % TM1 == 0 and n_lat >= 2 * TM1
    nt = n_lat + n_ctx + PADR
    ntiles1 = nt // TM1
    alpha = (2 * depth) ** 0.25

    cos_t, sin_t = _rope_tables(n_lat, nt)
    w_in_bf = w_in.astype(BF16)
    wb_bf = w_branch.astype(BF16)
    wo_bf = w_out.astype(BF16)
    pool_w_bf = pool_w.astype(BF16)
    r3 = lambda a: a.reshape(depth, 1, a.shape[-1])
    tabs_l = _dft_tables(n_lat // (FFT_N1 // 2))
    tabs_c = _dft_tables(n_ctx // (FFT_N1 // 2))
    feat_l, feat_c = _hy_features(n_lat), _hy_features(n_ctx)
    deltas = jnp.abs(jnp.linspace(math.log(HY_TARGET) / HY_SLOW_PCT, math.log(HY_TARGET) / HY_FAST_PCT,
                                  BW, dtype=F32))[None, :]
    w1p = jnp.pad(hy_w1, ((0, 0), (0, LANE - HY_EMB), (0, 0)))
    w3p_bf = hy_w3.reshape(depth, HY_HIDDEN, 2, 2, BW).transpose(0, 1, 3, 2, 4).reshape(depth, HY_HIDDEN, 4 * BW).astype(BF16)
    hyw = (w1p, r3(hy_b1), r3(hy_freq1), hy_w2, r3(hy_b2), r3(hy_freq2), w3p_bf, deltas, hy_bias)
    rw = jnp.concatenate([re_w, rg_w, jnp.zeros((depth, D, LANE - N_EXPERTS - N_GROUPS), F32)], axis=2)
    rb = jnp.concatenate([re_b, rg_b, jnp.full((depth, LANE - N_EXPERTS - N_GROUPS), NEG_INF, F32)], axis=1)
    rb = rb.reshape(depth, 1, LANE)
    upper = (jnp.arange(TM1)[:, None] < jnp.arange(TM1)[None, :]).astype(BF16)
    rmax = -(-(2 * (n_lat + n_ctx) + ntiles1 * N_EXPERTS * (MOE_PIECE - 1) + N_EXPERTS * (MOE_MB - 1)) // MOE_MB) * MOE_MB
    nblk = rmax // MOE_MB

    s8 = jnp.concatenate([c, c_ctx[None, :], jnp.zeros((SUB - 2, D), F32)], axis=0)
    mod = _mod_all(s8, ada_w, ada_b.reshape(depth, 1, 6 * D))

    xs = jnp.concatenate([x[0], ctx[0], jnp.zeros((PADR, D), F32)], axis=0)
    for i in range(depth):
        last = i == depth - 1
        mp = lambda j: mod[i, 0:2, j * D:(j + 1) * D].reshape(2, 1, D)
        sh1, sc1, g1, sh2, sc2, g2 = (mp(j) for j in range(6))

        q, k, v, pool_in, hy_in, sc_in, gate = _inproj(xs, sh1, sc1, cos_t, sin_t, w_in_bf, i, n_lat // TM1)
        att = _attention(attn_sink[i], q, k, v, n_lat)
        ypool, ysc, hx1, hx2, hv = _local_ops(pool_in, sc_in, hy_in, pool_w_bf, pool_scale.reshape(depth, 1, BW),
                                              hy_conv_w, hy_conv_b.reshape(depth, 1, 3 * BW), sc_conv_w, i, n_lat)
        yhl = _hyena(hx1, hx2, hv, feat_l, tabs_l, hyw, i, n_lat, 0)
        if last:
            yhc = jnp.zeros((n_ctx, BW), BF16)
        else:
            yhc = _hyena(hx1, hx2, hv, feat_c, tabs_c, hyw, i, n_ctx, n_lat)
        x1, m_bf, rt, cnt = _merge(att, ypool, yhl, yhc, ysc, gate, xs, wb_bf, wo_bf, g1, r3(ln1_g), r3(ln1_b),
                                   sh2, sc2, rw, rb, i, n_lat, alpha)

        npieces, soff, doff, ends = _moe_plan(cnt, ntiles1)
        nvb = (ends[-1] // MOE_MB).astype(I32).reshape(1)
        blk_start = jnp.arange(nblk, dtype=I32) * MOE_MB
        blk_e = jnp.minimum(jnp.sum((ends[None, :] <= blk_start[:, None]).astype(I32), axis=1), N_EXPERTS - 1)
        hs, pos = _dispatch(npieces, soff, doff, nvb, m_bf, rt, upper, rmax)
        ys = _experts(blk_e, nvb, hs, ex_w1, ex_w3, ex_w2, i)
        if last:
            xs = _combine(npieces, soff, doff, ys, pos, rt, x1, g2, r3(ln2_g), r3(ln2_b), i, n_lat,
                          n_lat // TM1, n_lat, alpha)
        else:
            xs = _combine(npieces, soff, doff, ys, pos, rt, x1, g2, r3(ln2_g), r3(ln2_b), i, n_lat, ntiles1, nt,
                          alpha)
    return xs[None]
```

```python
import functools
import math

import jax
import jax.numpy as jnp
from jax import lax
from jax.experimental import pallas as pl
from jax.experimental.pallas import tpu as pltpu

F32 = jnp.float32
BF16 = jnp.bfloat16
I32 = jnp.int32
U32 = jnp.uint32
SDS = jax.ShapeDtypeStruct
HIGHEST = lax.Precision.HIGHEST

D = 1024
GRID_W = 64
HEADS = 8
KV_HEADS = 2
HEAD_DIM = 64
WINDOW = 128
ATT_BLK = 128
ROPE_PAIRS = 16
ROPE_BASE = 10000.0
BW = 512
POOL_WINDOWS = (2, 4, 8, 16)
POOL_GW = 128
HY_EMB = 33
HY_BANDS = 16
HY_HIDDEN = 64
HY_TARGET = 1e-2
HY_FAST_PCT = 0.3
HY_SLOW_PCT = 1.5
HY_MOD_SHIFT = 0.05
Q_COLS = HEADS * HEAD_DIM
KV_COLS = KV_HEADS * HEAD_DIM
IN_COLS = Q_COLS + 2 * KV_COLS + BW + 3 * BW + 3 * BW + 4 * D
N_GROUPS = 4
EPG = 8
N_EXPERTS = 32
EXPERT_HIDDEN = 512
LN_EPS = 1e-6
NEG_INF = -1e30

LANE = 128
SUB = 8
BSUB = 16
TM = 256
TM1 = 512
PADR = 256
HALO = 16
FFT_N1 = 512
FFT_K1 = 264
FFT_KB = 8
FFT_TN = 2048
FFT_NB = 8
MOE_PIECE = 8
MOE_MB = 256
MOE_S = 2 * TM1 + 256


def _cparams(sem, vmem_mb):
    return pltpu.CompilerParams(dimension_semantics=sem, vmem_limit_bytes=vmem_mb * 2 ** 20)


def _layer_norm(x):
    mu = jnp.mean(x, axis=-1, keepdims=True)
    xc = x - mu
    var = jnp.mean(xc * xc, axis=-1, keepdims=True)
    return xc * lax.rsqrt(var + LN_EPS)


def _mod_body(s_ref, w_ref, b_ref, o_ref):
    s = s_ref[...]
    s = s * (1.0 / (1.0 + jnp.exp(-s)))
    o_ref[...] = jnp.dot(s.astype(BF16), w_ref[...].astype(BF16), preferred_element_type=F32) + b_ref[...]


def _mod_all(s8, ada_w, ada_b3):
    depth = ada_w.shape[0]
    tn = 1024
    return pl.pallas_call(
        _mod_body,
        out_shape=SDS((depth, SUB, 6 * D), F32),
        grid=(depth, 6 * D // tn),
        in_specs=[pl.BlockSpec((SUB, D), lambda l, j: (0, 0)),
                  pl.BlockSpec((None, D, tn), lambda l, j: (l, 0, j)),
                  pl.BlockSpec((None, 1, tn), lambda l, j: (l, 0, j))],
        out_specs=pl.BlockSpec((None, SUB, tn), lambda l, j: (l, 0, j)),
        compiler_params=_cparams(("parallel", "parallel"), 32),
        name="ada_mod",
    )(s8, ada_w, ada_b3)


def _inproj_body(x_ref, sh_ref, sc_ref, cos_ref, sin_ref, w_ref,
                 q_ref, k_ref, v_ref, pool_ref, hy_ref, scv_ref, gate_ref):
    h = _layer_norm(x_ref[...]) * (1.0 + sc_ref[...]) + sh_ref[...]
    hb = h.astype(BF16)

    def mm(a, b):
        return jnp.dot(hb, w_ref[:, a:b], preferred_element_type=F32)

    cs = cos_ref[...]
    sn = sin_ref[...]
    lane = lax.broadcasted_iota(I32, cs.shape, 1)
    first = (lane & ROPE_PAIRS) == 0

    def rope(t):
        sw = jnp.where(first, pltpu.roll(t, LANE - ROPE_PAIRS, 1), pltpu.roll(t, ROPE_PAIRS, 1))
        return t * cs + sw * sn

    for j in range(Q_COLS // LANE):
        q_ref[:, j * LANE:(j + 1) * LANE] = (rope(mm(j * LANE, (j + 1) * LANE)) * (HEAD_DIM ** -0.5)).astype(BF16)
    o = Q_COLS
    k_ref[...] = rope(mm(o, o + KV_COLS)).astype(BF16)
    o += KV_COLS
    v_ref[...] = mm(o, o + KV_COLS).astype(BF16)
    o += KV_COLS
    pool_ref[...] = mm(o, o + BW).astype(BF16)
    o += BW
    for j in range(3):
        hy_ref[:, j * BW:(j + 1) * BW] = mm(o + j * BW, o + (j + 1) * BW).astype(BF16)
    o += 3 * BW
    for j in range(3):
        scv_ref[:, j * BW:(j + 1) * BW] = mm(o + j * BW, o + (j + 1) * BW).astype(BF16)
    o += 3 * BW
    for j in range(4 * D // 1024):
        gate_ref[:, j * 1024:(j + 1) * 1024] = mm(o + j * 1024, o + (j + 1) * 1024).astype(BF16)


def _inproj(x, sh, sc, cos_t, sin_t, w_in_bf, layer, n_lat_tiles):
    nt = x.shape[0]
    sel = lambda i: (jnp.where(i >= n_lat_tiles, 1, 0), 0, 0)
    row = lambda c: pl.BlockSpec((TM1, c), lambda i: (i, 0))
    outs = [SDS((nt, Q_COLS), BF16), SDS((nt, KV_COLS), BF16), SDS((nt, KV_COLS), BF16), SDS((nt, BW), BF16),
            SDS((nt, 3 * BW), BF16), SDS((nt, 3 * BW), BF16), SDS((nt, 4 * D), BF16)]
    return pl.pallas_call(
        _inproj_body,
        out_shape=outs,
        grid=(nt // TM1,),
        in_specs=[row(D),
                  pl.BlockSpec((None, 1, D), sel), pl.BlockSpec((None, 1, D), sel),
                  row(LANE), row(LANE),
                  pl.BlockSpec((None, D, IN_COLS), lambda i: (layer, 0, 0), pipeline_mode=pl.Buffered(1))],
        out_specs=[row(Q_COLS), row(KV_COLS), row(KV_COLS), row(BW), row(3 * BW), row(3 * BW), row(4 * D)],
        compiler_params=_cparams(("parallel",), 58),
        name="inproj",
    )(x, sh, sc, cos_t, sin_t, w_in_bf)


def _attn_body(sink_ref, q_ref, kp_ref, kc_ref, kn_ref, vp_ref, vc_ref, vn_ref, kx_ref, vx_ref, o_ref, *, nlb):
    i = pl.program_id(0)
    is_lat = i < nlb
    lat_i = jnp.where(is_lat, 1, 0)
    prev_i = jnp.where(jnp.logical_and(is_lat, i > 0), 1, 0)
    next_i = jnp.where(jnp.logical_and(is_lat, i < nlb - 1), 1, 0)
    ncx = kx_ref.shape[0]
    nk = ncx + 3 * ATT_BLK

    r = lax.broadcasted_iota(I32, (ATT_BLK, nk), 0)
    c = lax.broadcasted_iota(I32, (ATT_BLK, nk), 1) - ncx
    band = jnp.logical_and(c >= r, c <= r + 2 * WINDOW)
    grp = jnp.where(c < ATT_BLK, prev_i, jnp.where(c < 2 * ATT_BLK, lat_i, next_i))
    mask = jnp.logical_or(c < 0, jnp.logical_and(band, grp > 0))

    kcat = jnp.concatenate([kx_ref[...], kp_ref[...], kc_ref[...], kn_ref[...]], axis=0).astype(F32)
    vcat = jnp.concatenate([vx_ref[...], vp_ref[...], vc_ref[...], vn_ref[...]], axis=0).astype(F32)
    lo = lax.broadcasted_iota(I32, kcat.shape, 1) < HEAD_DIM
    kroll = pltpu.roll(kcat, HEAD_DIM, 1)
    vroll = pltpu.roll(vcat, HEAD_DIM, 1)
    zero = jnp.zeros_like(kcat)
    kvar = [[jnp.where(lo, kcat, zero), jnp.where(lo, zero, kroll)],
            [jnp.where(lo, kroll, zero), jnp.where(lo, zero, kcat)]]
    vvar = [[jnp.where(lo, vcat, zero), jnp.where(lo, zero, vroll)],
            [jnp.where(lo, vroll, zero), jnp.where(lo, zero, vcat)]]
    kvar = [[a.astype(BF16) for a in row] for row in kvar]
    vvar = [[a.astype(BF16) for a in row] for row in vvar]

    for p in range(HEADS // 2):
        qp = q_ref[:, p * LANE:(p + 1) * LANE]
        kh = (2 * p) // (HEADS // KV_HEADS)
        acc = jnp.zeros((ATT_BLK, LANE), F32)
        for rr in range(2):
            sink = sink_ref[2 * p + rr]
            s = lax.dot_general(qp, kvar[kh][rr], (((1,), (1,)), ((), ())), preferred_element_type=F32)
            s = jnp.where(mask, s, NEG_INF)
            m = jnp.maximum(jnp.max(s, axis=1, keepdims=True), sink)
            e = jnp.exp(s - m)
            den = jnp.sum(e, axis=1, keepdims=True) + jnp.exp(sink - m)
            o = jnp.dot(e.astype(BF16), vvar[kh][rr], preferred_element_type=F32)
            acc = acc + o * (1.0 / den)
        o_ref[:, p * LANE:(p + 1) * LANE] = acc.astype(BF16)


def _attention(sink, q, k, v, n_lat):
    nt = q.shape[0]
    nlb = n_lat // ATT_BLK
    cxb = n_lat // TM
    cl = lambda i: jnp.clip(i, 0, nlb - 1)
    kspec = lambda f: pl.BlockSpec((ATT_BLK, KV_COLS), lambda i: (f(i), 0))
    prev = lambda i: jnp.clip(i - 1, 0, nlb - 1)
    nxt = lambda i: jnp.clip(i + 1, 0, nlb - 1)
    cx = pl.BlockSpec((TM, KV_COLS), lambda i: (cxb, 0))
    return pl.pallas_call(
        functools.partial(_attn_body, nlb=nlb),
        out_shape=SDS((nt, Q_COLS), BF16),
        grid=(nt // ATT_BLK,),
        in_specs=[pl.BlockSpec(memory_space=pltpu.SMEM),
                  pl.BlockSpec((ATT_BLK, Q_COLS), lambda i: (i, 0)),
                  kspec(prev), kspec(cl), kspec(nxt), kspec(prev), kspec(cl), kspec(nxt), cx, cx],
        out_specs=pl.BlockSpec((ATT_BLK, Q_COLS), lambda i: (i, 0)),
        compiler_params=_cparams(("parallel",), 32),
        name="attention",
    )(sink, q, k, k, k, v, v, v, k, v)


def _pack2(a, b):
    hi = lax.bitcast_convert_type(a.astype(BF16).astype(F32), U32)
    lo = lax.bitcast_convert_type(b.astype(BF16).astype(F32), U32)
    return hi | (lo >> 16)


def _unpack2(w):
    return (lax.bitcast_convert_type(w & jnp.uint32(0xFFFF0000), F32),
            lax.bitcast_convert_type(w << 16, F32))


def _local_body(pm_ref, pa_ref, pb_ref, sm_ref, sa_ref, sb_ref, hm_ref, ha_ref, hb_ref,
                pw_ref, ps_ref, hw_ref, hbias_ref, sw_ref,
                ypool_ref, ysc_ref, x1p_ref, x2p_ref, vp_ref, x1c_ref, x2c_ref, vc_ref,
                pext, sext, hext, *, nlat, n_lat, n_ctx):
    i = pl.program_id(0)
    is_lat = i < nlat
    pf = jnp.where(jnp.logical_and(is_lat, i > 0), 1.0, 0.0)
    nf = jnp.where(jnp.logical_and(is_lat, i < nlat - 1), 1.0, 0.0)
    r = lax.broadcasted_iota(I32, (TM1, 1), 0)
    keep = jnp.where(jnp.logical_or(is_lat, r < n_ctx), 1.0, 0.0)

    def fill(ext, a_ref, m_ref, b_ref):
        ext[0:HALO, :] = a_ref[...].astype(F32) * pf
        ext[HALO:HALO + TM1, :] = m_ref[...].astype(F32) * keep
        ext[HALO + TM1:HALO + TM1 + HALO, :] = b_ref[...].astype(F32) * nf

    def sh(ext, d, c0=None, c1=None):
        if c0 is None:
            return ext[pl.ds(HALO + d, TM1), :]
        return ext[pl.ds(HALO + d, TM1), c0:c1]

    fill(pext, pa_ref, pm_ref, pb_ref)
    t = jnp.where(is_lat, i * TM1, 0) + r
    ln = jnp.where(is_lat, n_lat, n_ctx)
    for g, w in enumerate(POOL_WINDOWS):
        left = w // 2
        right = w - left - 1
        c0, c1 = g * POOL_GW, (g + 1) * POOL_GW
        acc = sh(pext, -left, c0, c1)
        for dd in range(-left + 1, right + 1):
            acc = acc + sh(pext, dd, c0, c1)
        cnt = jnp.maximum(jnp.minimum(t + right + 1, ln) - jnp.maximum(t - left, 0), 1).astype(F32)
        dpool = acc / cnt - sh(pext, 0, c0, c1)
        y = jnp.dot(dpool.astype(BF16), pw_ref[g], preferred_element_type=F32)
        ypool_ref[:, c0:c1] = (y * ps_ref[:, c0:c1]).astype(BF16)

    sext[0:HALO, :] = (sa_ref[:, BW:2 * BW].astype(F32) * sa_ref[:, 2 * BW:3 * BW].astype(F32)) * pf
    sext[HALO:HALO + TM1, :] = sm_ref[:, BW:2 * BW].astype(F32) * sm_ref[:, 2 * BW:3 * BW].astype(F32) * keep
    sext[HALO + TM1:HALO + TM1 + HALO, :] = (sb_ref[:, BW:2 * BW].astype(F32) * sb_ref[:, 2 * BW:3 * BW].astype(F32)) * nf
    conv = sh(sext, -1) * sw_ref[0:1, :] + sh(sext, 0) * sw_ref[1:2, :] + sh(sext, 1) * sw_ref[2:3, :]
    ysc_ref[...] = (sm_ref[:, 0:BW].astype(F32) * conv).astype(BF16)

    fill(hext, ha_ref, hm_ref, hb_ref)
    nat = (x1c_ref, x2c_ref, vc_ref)
    pk = (x1p_ref, x2p_ref, vp_ref)
    n2c = x1p_ref.shape[0]
    for j in range(3):
        c0, c1 = j * BW, (j + 1) * BW
        uc = (sh(hext, -1, c0, c1) * hw_ref[0:1, c0:c1] + sh(hext, 0, c0, c1) * hw_ref[1:2, c0:c1]
              + sh(hext, 1, c0, c1) * hw_ref[2:3, c0:c1] + hbias_ref[:, c0:c1])
        nat[j][...] = uc.astype(BF16)
        word = _pack2(uc[:, 0:BW // 2], uc[:, BW // 2:BW])
        for a in range(TM1 // n2c):
            pk[j][:, a, :] = word[a * n2c:(a + 1) * n2c]


def _local_ops(pool_in, sc_in, hy_in, pool_w_bf, pool_scale, hy_conv_w, hy_conv_b, sc_conv_w, layer, n_lat, n_ctx):
    nt = pool_in.shape[0]
    nlat = n_lat // TM1
    n2c = n_lat // (FFT_N1 // 2)
    per_tile = TM1 // n2c
    nhb = nt // HALO
    per = TM1 // HALO
    main = lambda c: pl.BlockSpec((TM1, c), lambda i: (i, 0))
    before = lambda c: pl.BlockSpec((HALO, c), lambda i: (jnp.maximum(i * per - 1, 0), 0))
    after = lambda c: pl.BlockSpec((HALO, c), lambda i: (jnp.minimum((i + 1) * per, nhb - 1), 0))
    lsel = lambda *blk: pl.BlockSpec((None,) + blk, lambda i: (layer,) + (0,) * len(blk))
    out = SDS((nt, BW), BF16)
    packed = SDS((n2c, (nlat + 1) * per_tile, BW // 2), U32)
    ctx_out = SDS((TM1, BW), BF16)
    pspec = pl.BlockSpec((n2c, per_tile, BW // 2), lambda i: (0, i, 0))
    cspec = pl.BlockSpec((TM1, BW), lambda i: (0, 0))
    return pl.pallas_call(
        functools.partial(_local_body, nlat=nlat, n_lat=n_lat, n_ctx=n_ctx),
        out_shape=[out, out, packed, packed, packed, ctx_out, ctx_out, ctx_out],
        grid=(nlat + 1,),
        in_specs=[main(BW), before(BW), after(BW),
                  main(3 * BW), before(3 * BW), after(3 * BW),
                  main(3 * BW), before(3 * BW), after(3 * BW),
                  lsel(4, POOL_GW, POOL_GW), lsel(1, BW), lsel(3, 3 * BW), lsel(1, 3 * BW), lsel(3, BW)],
        out_specs=[main(BW), main(BW), pspec, pspec, pspec, cspec, cspec, cspec],
        scratch_shapes=[pltpu.VMEM((TM1 + 2 * HALO, BW), F32), pltpu.VMEM((TM1 + 2 * HALO, BW), F32),
                        pltpu.VMEM((TM1 + 2 * HALO, 3 * BW), F32)],
        compiler_params=_cparams(("arbitrary",), 48),
        name="local_ops",
    )(pool_in, pool_in, pool_in, sc_in, sc_in, sc_in, hy_in, hy_in, hy_in,
      pool_w_bf, pool_scale, hy_conv_w, hy_conv_b, sc_conv_w)


def _dft_tables(n2):
    n = FFT_N1 * n2
    k1 = jnp.arange(FFT_K1, dtype=I32)
    n1 = jnp.arange(FFT_N1 // 2, dtype=I32)
    ok = (k1 <= FFT_N1 // 2)
    ang = (2.0 * math.pi / FFT_N1) * ((k1[:, None] * n1[None, :]) % FFT_N1).astype(F32)
    c1 = jnp.where(ok[:, None], jnp.cos(ang), 0.0)
    s1 = jnp.where(ok[:, None], jnp.sin(ang), 0.0)
    f1 = jnp.concatenate([c1, -s1], axis=0).astype(BF16)
    wt = jnp.where((k1 == 0) | (k1 == FFT_N1 // 2), 1.0, 2.0) / n
    f1i = jnp.concatenate([(c1 * wt[:, None]).T, (-s1 * wt[:, None]).T], axis=1).astype(BF16)
    if n2 == 1:
        return f1, f1i, None, None
    k2 = jnp.arange(n2, dtype=I32)
    m2 = jnp.arange(n2, dtype=I32)
    kk = k1[:, None, None] + FFT_N1 * k2[None, :, None]
    th = (2.0 * math.pi / n) * ((kk * m2[None, None, :]) % n).astype(F32)
    okb = ok[:, None, None]
    cc = jnp.where(okb, jnp.cos(th), 0.0)
    ss = jnp.where(okb, jnp.sin(th), 0.0)
    g = jnp.concatenate([jnp.concatenate([cc, ss], axis=2), jnp.concatenate([-ss, cc], axis=2)], axis=1)
    return f1, f1i, g.astype(BF16), jnp.swapaxes(g, 1, 2).astype(BF16)


def _filter_body(feat_ref, w1_ref, b1_ref, f1_ref, w2_ref, b2_ref, f2_ref, w3_ref, dl_ref, dft_ref,
                 af_ref, ss_ref, filt):
    i = pl.program_id(0)
    feat = feat_ref[...]
    h = jnp.sin(f1_ref[...] * (jnp.dot(feat, w1_ref[...], precision=HIGHEST, preferred_element_type=F32) + b1_ref[...]))
    h = jnp.sin(f2_ref[...] * (jnp.dot(h, w2_ref[...], precision=HIGHEST, preferred_element_type=F32) + b2_ref[...]))
    raw = jnp.dot(h.astype(BF16), w3_ref[...], preferred_element_type=F32)
    win = jnp.exp(-feat[:, 0:1] * dl_ref[...]) + HY_MOD_SHIFT
    row = lax.broadcasted_iota(I32, (TM, 1), 0) + i * TM

    @pl.when(i == 0)
    def _():
        ss_ref[...] = jnp.zeros_like(ss_ref)

    for j in range(4):
        f = raw[:, j * BW:(j + 1) * BW] * win
        if j >= 2:
            f = jnp.where(row == 0, 0.0, f)
        filt[:, j * BW:(j + 1) * BW] = f.astype(BF16)
        ss_ref[:, j * BW:(j + 1) * BW] += jnp.sum(f * f, axis=0, keepdims=True)
    af_ref[...] = jnp.dot(dft_ref[...], filt[...], preferred_element_type=F32).astype(BF16)


def _hy_filters(feat_perm, w1p, b1, fr1, w2, b2, fr2, w3p_bf, deltas, f1, layer):
    l = feat_perm.shape[0]
    n2 = l // TM
    lsel = lambda *blk: pl.BlockSpec((None,) + blk, lambda i: (layer,) + (0,) * len(blk))
    return pl.pallas_call(
        _filter_body,
        out_shape=[SDS((2 * FFT_K1, n2 * 4 * BW), BF16), SDS((1, 4 * BW), F32)],
        grid=(n2,),
        in_specs=[pl.BlockSpec((TM, LANE), lambda i: (i, 0)),
                  lsel(LANE, HY_HIDDEN), lsel(1, HY_HIDDEN), lsel(1, HY_HIDDEN),
                  lsel(HY_HIDDEN, HY_HIDDEN), lsel(1, HY_HIDDEN), lsel(1, HY_HIDDEN),
                  lsel(HY_HIDDEN, 4 * BW), pl.BlockSpec((1, BW), lambda i: (0, 0)),
                  pl.BlockSpec((2 * FFT_K1, FFT_N1 // 2), lambda i: (0, 0))],
        out_specs=[pl.BlockSpec((2 * FFT_K1, 4 * BW), lambda i: (0, i)), pl.BlockSpec((1, 4 * BW), lambda i: (0, 0))],
        scratch_shapes=[pltpu.VMEM((TM, 4 * BW), BF16)],
        compiler_params=_cparams(("arbitrary",), 32),
        name="hy_filters",
    )(feat_perm, w1p, b1, fr1, w2, b2, fr2, w3p_bf, deltas, f1)


def _dft_major_body(f_ref, x_ref, o_ref):
    o_ref[...] = jnp.dot(f_ref[...], x_ref[...], preferred_element_type=F32).astype(BF16)


def _dft_major(f1, x2d, row_blk, ncols):
    tn = min(FFT_TN, ncols)
    return pl.pallas_call(
        _dft_major_body,
        out_shape=SDS((2 * FFT_K1, ncols), BF16),
        grid=(ncols // tn,),
        in_specs=[pl.BlockSpec((2 * FFT_K1, FFT_N1 // 2), lambda j: (0, 0)),
                  pl.BlockSpec((FFT_N1 // 2, tn), lambda j: (row_blk, j))],
        out_specs=pl.BlockSpec((2 * FFT_K1, tn), lambda j: (0, j)),
        compiler_params=_cparams(("parallel",), 32),
        name="dft_major",
    )(f1, x2d)


def _filter_minor_body(a_ref, g_ref, ss_ref, h_ref, *, n2):
    ssv = ss_ref[...]
    scale = lax.rsqrt(ssv[:, 0:2 * BW] + ssv[:, 2 * BW:4 * BW] + LN_EPS)
    for j in range(a_ref.shape[1]):
        if n2 == 1:
            s = jnp.concatenate([a_ref[0, j], a_ref[1, j]], axis=0).astype(F32)
        else:
            a = jnp.concatenate([a_ref[0, j], a_ref[1, j]], axis=0)
            s = jnp.dot(g_ref[j], a, preferred_element_type=F32)
        sf, sb = s[:, 0:2 * BW], s[:, 2 * BW:4 * BW]
        hr = (sf[0:n2] + sb[0:n2]) * scale
        hi = (sf[n2:2 * n2] - sb[n2:2 * n2]) * scale
        h_ref[j] = jnp.concatenate([hr, hi], axis=0).astype(BF16)


def _conv_minor_body(a_ref, h_ref, g_ref, gt_ref, d_ref, *, n2):
    for j in range(a_ref.shape[1]):
        a = jnp.concatenate([a_ref[0, j], a_ref[1, j]], axis=0)
        if n2 == 1:
            x = a.astype(F32)
        else:
            x = jnp.dot(g_ref[j], a, preferred_element_type=F32)
        h = h_ref[j].astype(F32)
        xr, xi, hr, hi = x[0:n2], x[n2:2 * n2], h[0:n2], h[n2:2 * n2]
        y = jnp.concatenate([xr * hr - xi * hi, xr * hi + xi * hr], axis=0)
        if n2 == 1:
            dv = y
        else:
            dv = jnp.dot(gt_ref[j], y.astype(BF16), preferred_element_type=F32)
        d_ref[0, j] = dv[0:n2].astype(BF16)
        d_ref[1, j] = dv[n2:2 * n2].astype(BF16)


def _filter_minor(af, g, ss, n2):
    a4 = af.reshape(2, FFT_K1, n2, 4 * BW)
    kb = FFT_KB
    gspec = (pl.BlockSpec((kb, 2 * n2, 2 * n2), lambda i: (i, 0, 0)) if n2 > 1
             else pl.BlockSpec((SUB, LANE), lambda i: (0, 0)))
    return pl.pallas_call(
        functools.partial(_filter_minor_body, n2=n2),
        out_shape=SDS((FFT_K1, 2 * n2, 2 * BW), BF16),
        grid=(FFT_K1 // kb,),
        in_specs=[pl.BlockSpec((2, kb, n2, 4 * BW), lambda i: (0, i, 0, 0)), gspec,
                  pl.BlockSpec((1, 4 * BW), lambda i: (0, 0))],
        out_specs=pl.BlockSpec((kb, 2 * n2, 2 * BW), lambda i: (i, 0, 0)),
        compiler_params=_cparams(("parallel",), 40),
        name="hy_filter_minor",
    )(a4, g if n2 > 1 else jnp.zeros((SUB, LANE), BF16), ss)


def _conv_minor(a, hspec, g, gt, order, n2):
    a4 = a.reshape(2, FFT_K1, n2, BW)
    kb = FFT_KB
    dummy = jnp.zeros((SUB, LANE), BF16)
    gspec = (pl.BlockSpec((kb, 2 * n2, 2 * n2), lambda i: (i, 0, 0)) if n2 > 1
             else pl.BlockSpec((SUB, LANE), lambda i: (0, 0)))
    d4 = pl.pallas_call(
        functools.partial(_conv_minor_body, n2=n2),
        out_shape=SDS((2, FFT_K1, n2, BW), BF16),
        grid=(FFT_K1 // kb,),
        in_specs=[pl.BlockSpec((2, kb, n2, BW), lambda i: (0, i, 0, 0)),
                  pl.BlockSpec((kb, 2 * n2, BW), lambda i: (i, 0, order)), gspec, gspec],
        out_specs=pl.BlockSpec((2, kb, n2, BW), lambda i: (0, i, 0, 0)),
        compiler_params=_cparams(("parallel",), 32),
        name="hy_conv_minor",
    )(a4, hspec, g if n2 > 1 else dummy, gt if n2 > 1 else dummy)
    return d4.reshape(2 * FFT_K1, n2 * BW)


def _idft_gate_body(f_ref, d_ref, gate_ref, z_ref, b_ref, o_ref):
    y = jnp.dot(f_ref[...], d_ref[...], preferred_element_type=F32)
    z = z_ref[...].astype(F32)
    o_ref[...] = (gate_ref[...].astype(F32) * (y + b_ref[...] * z)).astype(BF16)


def _idft_gate(f1i, d, gate2d, gate_blk, z2d, z_blk, bias_t, ncols):
    tn = min(FFT_TN, ncols)
    return pl.pallas_call(
        _idft_gate_body,
        out_shape=SDS((FFT_N1 // 2, ncols), BF16),
        grid=(ncols // tn,),
        in_specs=[pl.BlockSpec((FFT_N1 // 2, 2 * FFT_K1), lambda j: (0, 0)),
                  pl.BlockSpec((2 * FFT_K1, tn), lambda j: (0, j)),
                  pl.BlockSpec((FFT_N1 // 2, tn), lambda j: (gate_blk, j)),
                  pl.BlockSpec((FFT_N1 // 2, tn), lambda j: (z_blk, j)),
                  pl.BlockSpec((1, tn), lambda j: (0, 0))],
        out_specs=pl.BlockSpec((FFT_N1 // 2, tn), lambda j: (0, j)),
        compiler_params=_cparams(("parallel",), 32),
        name="idft_gate",
    )(f1i, d, gate2d, z2d, bias_t)


def _hyena(x1, x2, v, feat, tabs, hyw, layer, seq_len, row0):
    n2 = seq_len // (FFT_N1 // 2)
    f1, f1i, g, gt = tabs
    w1p, b1, fr1, w2, b2, fr2, w3p_bf, deltas, hy_bias = hyw
    nt = x1.shape[0]
    af, ss = _hy_filters(feat, w1p, b1, fr1, w2, b2, fr2, w3p_bf, deltas, f1, layer)
    hspec = _filter_minor(af, g, ss, n2)
    view = lambda a: a.reshape(nt // n2, n2 * BW)
    blk = row0 // seq_len if n2 == 1 else 0
    ncols = n2 * BW
    tn = min(FFT_TN, ncols)
    z, zv, zblk = v, view(v), blk
    gates = (x1, x2)
    for o in range(2):
        a = _dft_major(f1, zv, zblk, ncols)
        d = _conv_minor(a, hspec, g, gt, o, n2)
        bias_t = jnp.tile(hy_bias[layer, o][None, :], (1, tn // BW))
        z2 = _idft_gate(f1i, d, view(gates[o]), blk, zv, zblk, bias_t, ncols)
        zv, zblk = z2, 0
    return zv.reshape(seq_len, BW)


def _filter_p_body(feat_ref, w1_ref, b1_ref, f1_ref, w2_ref, b2_ref, f2_ref, w3_ref, dl_ref, dft_ref,
                   af_ref, ss_ref, h2):
    nb = pl.program_id(0)
    cb = pl.program_id(1)
    nbk = af_ref.shape[1]

    @pl.when(jnp.logical_and(nb == 0, cb == 0))
    def _():
        ss_ref[...] = jnp.zeros_like(ss_ref)

    @pl.when(cb == 0)
    def _():
        feat = feat_ref[...]
        h = jnp.sin(f1_ref[...] * (jnp.dot(feat, w1_ref[...], precision=HIGHEST, preferred_element_type=F32) + b1_ref[...]))
        h = jnp.sin(f2_ref[...] * (jnp.dot(h, w2_ref[...], precision=HIGHEST, preferred_element_type=F32) + b2_ref[...]))
        h2[...] = h.astype(BF16)

    row = lax.broadcasted_iota(I32, (TM, 1), 0)
    lag0_bwd = jnp.logical_and(jnp.logical_and(nb == 0, cb >= 2), row == 0)
    ssum = jnp.zeros((1, BW), F32)
    for j in range(nbk):
        raw = jnp.dot(h2[j * TM:(j + 1) * TM, :], w3_ref[...], preferred_element_type=F32)
        win = jnp.exp(-feat_ref[j * TM:(j + 1) * TM, 0:1] * dl_ref[...]) + HY_MOD_SHIFT
        f = raw * win
        if j == 0:
            f = jnp.where(lag0_bwd, 0.0, f)
        ssum = ssum + jnp.sum(f * f, axis=0, keepdims=True)
        rr = jnp.dot(dft_ref[...], f.astype(BF16), preferred_element_type=F32)
        af_ref[:, j, :] = _pack2(rr[0:FFT_K1], rr[FFT_K1:2 * FFT_K1])
    ss_ref[cb] = ss_ref[cb] + ssum


def _hy_filters_p(feat_perm, w1p, b1, fr1, w2, b2, fr2, w3p_bf, deltas, f1, layer):
    l = feat_perm.shape[0]
    n2 = l // TM
    nbk = min(FFT_NB, n2)
    lsel = lambda *blk: pl.BlockSpec((None,) + blk, lambda nb, cb: (layer,) + (0,) * len(blk))
    return pl.pallas_call(
        _filter_p_body,
        out_shape=[SDS((FFT_K1, n2, 4 * BW), U32), SDS((4, 1, BW), F32)],
        grid=(n2 // nbk, 4),
        in_specs=[pl.BlockSpec((nbk * TM, LANE), lambda nb, cb: (nb, 0)),
                  lsel(LANE, HY_HIDDEN), lsel(1, HY_HIDDEN), lsel(1, HY_HIDDEN),
                  lsel(HY_HIDDEN, HY_HIDDEN), lsel(1, HY_HIDDEN), lsel(1, HY_HIDDEN),
                  pl.BlockSpec((None, HY_HIDDEN, BW), lambda nb, cb: (layer, 0, cb)),
                  pl.BlockSpec((1, BW), lambda nb, cb: (0, 0)),
                  pl.BlockSpec((2 * FFT_K1, FFT_N1 // 2), lambda nb, cb: (0, 0))],
        out_specs=[pl.BlockSpec((FFT_K1, nbk, BW), lambda nb, cb: (0, nb, cb)),
                   pl.BlockSpec((4, 1, BW), lambda nb, cb: (0, 0, 0))],
        scratch_shapes=[pltpu.VMEM((nbk * TM, HY_HIDDEN), BF16)],
        compiler_params=_cparams(("arbitrary", "arbitrary"), 40),
        name="hy_filters_p",
    )(feat_perm, w1p, b1, fr1, w2, b2, fr2, w3p_bf, deltas, f1)


def _filter_minor_p_body(a_ref, g_ref, ss_ref, h_ref, *, n2):
    ssv = ss_ref[...]
    scale = lax.rsqrt(ssv[:, 0:2 * BW] + ssv[:, 2 * BW:4 * BW] + LN_EPS)
    for j in range(a_ref.shape[0]):
        re, im = _unpack2(a_ref[j])
        a = jnp.concatenate([re, im], axis=0).astype(BF16)
        s = jnp.dot(g_ref[j], a, preferred_element_type=F32)
        sf, sb = s[:, 0:2 * BW], s[:, 2 * BW:4 * BW]
        hr = (sf[0:n2] + sb[0:n2]) * scale
        hi = (sf[n2:2 * n2] - sb[n2:2 * n2]) * scale
        h_ref[j] = jnp.concatenate([hr, hi], axis=0).astype(BF16)


def _filter_minor_p(af_p, g, ss, n2):
    kb = FFT_KB
    return pl.pallas_call(
        functools.partial(_filter_minor_p_body, n2=n2),
        out_shape=SDS((FFT_K1, 2 * n2, 2 * BW), BF16),
        grid=(FFT_K1 // kb,),
        in_specs=[pl.BlockSpec((kb, n2, 4 * BW), lambda i: (i, 0, 0)),
                  pl.BlockSpec((kb, 2 * n2, 2 * n2), lambda i: (i, 0, 0)),
                  pl.BlockSpec((1, 4 * BW), lambda i: (0, 0))],
        out_specs=pl.BlockSpec((kb, 2 * n2, 2 * BW), lambda i: (i, 0, 0)),
        compiler_params=_cparams(("parallel",), 40),
        name="hy_filter_minor_p",
    )(af_p, g, ss)


def _unpack_slab(ref_slab):
    hi, lo = _unpack2(ref_slab)
    return jnp.concatenate([hi, lo], axis=1)


def _dft_major_p_body(f_ref, x_ref, a_ref):
    for j in range(x_ref.shape[0]):
        x = _unpack_slab(x_ref[j]).astype(BF16)
        rr = jnp.dot(f_ref[...], x, preferred_element_type=F32)
        a_ref[:, j, :] = _pack2(rr[0:FFT_K1], rr[FFT_K1:2 * FFT_K1])


def _dft_major_p(f1, xp, n2):
    nbk = min(FFT_NB, n2)
    return pl.pallas_call(
        _dft_major_p_body,
        out_shape=SDS((FFT_K1, n2, BW), U32),
        grid=(n2 // nbk,),
        in_specs=[pl.BlockSpec((2 * FFT_K1, FFT_N1 // 2), lambda i: (0, 0)),
                  pl.BlockSpec((nbk, FFT_N1 // 2, BW // 2), lambda i: (i, 0, 0))],
        out_specs=pl.BlockSpec((FFT_K1, nbk, BW), lambda i: (0, i, 0)),
        compiler_params=_cparams(("parallel",), 32),
        name="dft_major_p",
    )(f1, xp)


def _conv_minor_p_body(a_ref, h_ref, g_ref, gt_ref, d_ref, *, n2):
    for j in range(a_ref.shape[0]):
        re, im = _unpack2(a_ref[j])
        a = jnp.concatenate([re, im], axis=0).astype(BF16)
        x = jnp.dot(g_ref[j], a, preferred_element_type=F32)
        h = h_ref[j].astype(F32)
        xr, xi, hr, hi = x[0:n2], x[n2:2 * n2], h[0:n2], h[n2:2 * n2]
        y = jnp.concatenate([xr * hr - xi * hi, xr * hi + xi * hr], axis=0)
        dv = jnp.dot(gt_ref[j], y.astype(BF16), preferred_element_type=F32)
        d_ref[j] = _pack2(dv[0:n2], dv[n2:2 * n2])


def _conv_minor_p(a_p, hspec, g, gt, order, n2):
    kb = FFT_KB
    gspec = pl.BlockSpec((kb, 2 * n2, 2 * n2), lambda i: (i, 0, 0))
    return pl.pallas_call(
        functools.partial(_conv_minor_p_body, n2=n2),
        out_shape=SDS((FFT_K1, n2, BW), U32),
        grid=(FFT_K1 // kb,),
        in_specs=[pl.BlockSpec((kb, n2, BW), lambda i: (i, 0, 0)),
                  pl.BlockSpec((kb, 2 * n2, BW), lambda i: (i, 0, order)), gspec, gspec],
        out_specs=pl.BlockSpec((kb, n2, BW), lambda i: (i, 0, 0)),
        compiler_params=_cparams(("parallel",), 32),
        name="hy_conv_minor_p",
    )(a_p, hspec, g, gt)


def _idft_gate_p_body(f_ref, d_ref, gate_ref, z_ref, b_ref, o_ref):
    for j in range(o_ref.shape[0]):
        re, im = _unpack2(d_ref[:, j, :])
        d = jnp.concatenate([re, im], axis=0).astype(BF16)
        y = jnp.dot(f_ref[...], d, preferred_element_type=F32)
        out = _unpack_slab(gate_ref[j]) * (y + b_ref[...] * _unpack_slab(z_ref[j]))
        o_ref[j] = _pack2(out[:, 0:BW // 2], out[:, BW // 2:BW])


def _idft_gate_p(f1i, d_p, gate_p, z_p, bias, n2):
    nbk = min(FFT_NB, n2)
    slab = pl.BlockSpec((nbk, FFT_N1 // 2, BW // 2), lambda i: (i, 0, 0))
    return pl.pallas_call(
        _idft_gate_p_body,
        out_shape=SDS((n2, FFT_N1 // 2, BW // 2), U32),
        grid=(n2 // nbk,),
        in_specs=[pl.BlockSpec((FFT_N1 // 2, 2 * FFT_K1), lambda i: (0, 0)),
                  pl.BlockSpec((FFT_K1, nbk, BW), lambda i: (0, i, 0)),
                  slab, slab, pl.BlockSpec((1, BW), lambda i: (0, 0))],
        out_specs=slab,
        compiler_params=_cparams(("parallel",), 32),
        name="idft_gate_p",
    )(f1i, d_p, gate_p, z_p, bias)


def _hyena_latent(x1p, x2p, vp, feat_perm, tabs, hyw, layer, seq_len):
    n2 = seq_len // (FFT_N1 // 2)
    f1, f1i, g, gt = tabs
    w1p, b1, fr1, w2, b2, fr2, w3p_bf, deltas, hy_bias = hyw
    af_p, ss = _hy_filters_p(feat_perm, w1p, b1, fr1, w2, b2, fr2, w3p_bf, deltas, f1, layer)
    hspec = _filter_minor_p(af_p, g, ss.reshape(1, 4 * BW), n2)
    z = vp
    for o, gate in enumerate((x1p, x2p)):
        a_p = _dft_major_p(f1, z, n2)
        d_p = _conv_minor_p(a_p, hspec, g, gt, o, n2)
        z = _idft_gate_p(f1i, d_p, gate, z, hy_bias[layer, o][None, :], n2)
    return z


def _merge_body(att_ref, yp_ref, yhp_ref, yhc_ref, ys_ref, gate_ref, x_ref, wb_ref, wo_ref,
                g1_ref, lg_ref, lb_ref, sh2_ref, sc2_ref, rw_ref, rb_ref,
                x1_ref, m_ref, rt_ref, cnt_ref, *, nlat, n_ctx, alpha):
    i = pl.program_id(0)
    n2c = yhp_ref.shape[0]
    pieces = []
    for a in range(TM1 // n2c):
        hi, lo = _unpack2(yhp_ref[:, a, :])
        pieces.append(jnp.concatenate([hi, lo], axis=1).astype(BF16))
    yh_lat = jnp.concatenate(pieces, axis=0)
    yh_ctx = jnp.concatenate([yhc_ref[...], jnp.zeros((TM1 - n_ctx, BW), BF16)], axis=0)
    yh = jnp.where(i >= nlat, yh_ctx, yh_lat)
    ys = (att_ref[...], yp_ref[...], yh, ys_ref[...])
    merged = jnp.zeros((TM1, D), F32)
    for n in range(4):
        br = jnp.dot(ys[n], wb_ref[n], preferred_element_type=F32)
        gl = gate_ref[:, n * D:(n + 1) * D].astype(F32)
        merged = merged + br / (1.0 + jnp.exp(-gl))
    y = jnp.dot(merged.astype(BF16), wo_ref[...], preferred_element_type=F32)
    x1 =_layer_norm(alpha * x_ref[...] + g1_ref[...] * y) * lg_ref[...] + lb_ref[...]
    x1_ref[...] = x1
    m = _layer_norm(x1) * (1.0 + sc2_ref[...]) + sh2_ref[...]
    m_ref[...] = m.astype(BF16)

    logits = jnp.dot(m, rw_ref[...], precision=HIGHEST, preferred_element_type=F32) + rb_ref[...]
    lt = logits.T
    le = lt[0:N_EXPERTS]
    lgp = lt[N_EXPERTS:N_EXPERTS + SUB]
    big = 1 << 20
    gi = lax.broadcasted_iota(I32, lgp.shape, 0)
    gmax = jnp.max(lgp, axis=0, keepdims=True)
    gsel = jnp.min(jnp.where(lgp == gmax, gi, big), axis=0, keepdims=True)
    gate_g = 1.0 / jnp.sum(jnp.exp(lgp - gmax), axis=0, keepdims=True)
    ei = lax.broadcasted_iota(I32, le.shape, 0)
    lem = jnp.where(lax.shift_right_logical(ei, 3) == gsel, le, -3.0e38)
    v1 = jnp.max(lem, axis=0, keepdims=True)
    i1 = jnp.min(jnp.where(lem == v1, ei, big), axis=0, keepdims=True)
    lem2 = jnp.where(ei == i1, -3.0e38, lem)
    v2 = jnp.max(lem2, axis=0, keepdims=True)
    i2 = jnp.min(jnp.where(lem2 == v2, ei, big), axis=0, keepdims=True)
    e2 = jnp.exp(v2 - v1)
    wa = gate_g / (1.0 + e2)
    wb = gate_g * e2 / (1.0 + e2)
    tok = lax.broadcasted_iota(I32, (1, TM1), 1)
    valid = jnp.logical_or(i < nlat, tok < n_ctx)
    i1 = jnp.where(valid, i1, -1)
    i2 = jnp.where(valid, i2, -1)
    ri = lax.broadcasted_iota(I32, (SUB, TM1), 0)
    rt = jnp.where(ri == 0, i1.astype(F32), jnp.where(ri == 1, i2.astype(F32),
                   jnp.where(ri == 2, wa, jnp.where(ri == 3, wb, 0.0))))
    rt_ref[...] = rt
    ci = lax.broadcasted_iota(I32, (LANE, TM1), 0)
    oh = jnp.logical_or(ci == i1, ci == i2).astype(BF16)
    cnt_ref[0] = lax.dot_general(jnp.ones((SUB, TM1), BF16), oh, (((1,), (1,)), ((), ())),
                                 preferred_element_type=F32)


def _merge(att, ypool, yhp, yhc, ysc, gate, x, wb_bf, wo_bf, g1, ln_g, ln_b, sh2, sc2, rw, rb, layer, n_lat, n_ctx,
           alpha):
    nt = x.shape[0]
    nlat = n_lat // TM1
    ntl = nt // TM1
    n2c = yhp.shape[0]
    sel = lambda i: (jnp.where(i >= nlat, 1, 0), 0, 0)
    row = lambda c: pl.BlockSpec((TM1, c), lambda i: (i, 0))
    lsel = lambda *blk, **kw: pl.BlockSpec((None,) + blk, lambda i: (layer,) + (0,) * len(blk), **kw)
    msel = pl.BlockSpec((None, 1, D), sel)
    return pl.pallas_call(
        functools.partial(_merge_body, nlat=nlat, n_ctx=n_ctx, alpha=alpha),
        out_shape=[SDS((nt, D), F32), SDS((nt, D), BF16), SDS((SUB, nt), F32), SDS((ntl, SUB, LANE), F32)],
        grid=(ntl,),
        in_specs=[row(BW), row(BW),
                  pl.BlockSpec((n2c, TM1 // n2c, BW // 2), lambda i: (0, jnp.minimum(i, nlat - 1), 0)),
                  pl.BlockSpec((n_ctx, BW), lambda i: (0, 0)),
                  row(BW), row(4 * D), row(D),
                  lsel(4, BW, D, pipeline_mode=pl.Buffered(1)), lsel(D, D, pipeline_mode=pl.Buffered(1)),
                  msel, lsel(1, D), lsel(1, D), msel, msel,
                  lsel(D, LANE), lsel(1, LANE)],
        out_specs=[row(D), row(D), pl.BlockSpec((SUB, TM1), lambda i: (0, i)),
                   pl.BlockSpec((1, SUB, LANE), lambda i: (i, 0, 0))],
        compiler_params=_cparams(("parallel",), 56),
        name="merge_router",
    )(att, ypool, yhp, yhc, ysc, gate, x, wb_bf, wo_bf, g1, ln_g, ln_b, sh2, sc2, rw, rb)


def _piece_loops(np_ref, so_ref, do_ref, j, fn):
    for e in range(N_EXPERTS):
        n = np_ref[j * N_EXPERTS + e]
        so = so_ref[j * N_EXPERTS + e]
        do = do_ref[j * N_EXPERTS + e]

        def body(p, carry, so=so, do=do):
            fn(pl.multiple_of(so + p * MOE_PIECE, MOE_PIECE), pl.multiple_of(do + p * MOE_PIECE, MOE_PIECE))
            return carry

        lax.fori_loop(0, n, body, 0)


def _dispatch_body(np_ref, so_ref, do_ref, nv_ref, m_ref, rt_ref, u_ref, hs_hbm, pos_ref, hs_vmem, sem, *,
                   ntiles, nblk):
    j = pl.program_id(0)
    real = j < ntiles
    rt = rt_ref[...]
    e0 = jnp.where(real, rt[0:1].astype(I32), -1)
    e1 = jnp.where(real, rt[1:2].astype(I32), -1)
    ei = lax.broadcasted_iota(I32, (N_EXPERTS, TM1), 0)
    oh0 = (ei == e0).astype(F32)
    oh1 = (ei == e1).astype(F32)
    c0 = jnp.dot(oh0.astype(BF16), u_ref[...], preferred_element_type=F32)
    c1 = jnp.dot(oh1.astype(BF16), u_ref[...], preferred_element_type=F32)
    n0 = jnp.sum(oh0, axis=1, keepdims=True)
    ecol = lax.broadcasted_iota(I32, (N_EXPERTS, 1), 0)
    toff = jnp.zeros((N_EXPERTS, 1), F32)
    for e in range(N_EXPERTS):
        toff = jnp.where(ecol == e, so_ref[j * N_EXPERTS + e].astype(F32), toff)
    pos0 = jnp.sum(oh0 * (toff + c0), axis=0, keepdims=True)
    pos1 = jnp.sum(oh1 * (toff + n0 + c1), axis=0, keepdims=True)
    pos0 = jnp.where(e0 >= 0, pos0, -1.0)
    pos1 = jnp.where(e1 >= 0, pos1, -1.0)
    ri = lax.broadcasted_iota(I32, (SUB, TM1), 0)
    pos_ref[...] = jnp.where(ri == 0, pos0, jnp.where(ri == 1, pos1, 0.0))
    si = lax.broadcasted_iota(I32, (MOE_S, TM1), 0)
    perm = jnp.logical_or(si == pos0.astype(I32), si == pos1.astype(I32)).astype(BF16)
    hs_vmem[...] = jnp.dot(perm, m_ref[...], preferred_element_type=F32)

    def copy(so, do):
        return pltpu.make_async_copy(hs_vmem.at[pl.ds(so, MOE_PIECE)], hs_hbm.at[pl.ds(do, MOE_PIECE)], sem)

    _piece_loops(np_ref, so_ref, do_ref, j, lambda so, do: copy(so, do).start())
    _piece_loops(np_ref, so_ref, do_ref, j, lambda so, do: copy(so, do).wait())

    @pl.when(j == ntiles)
    def _():
        def blk_copy(b):
            return pltpu.make_async_copy(hs_vmem.at[pl.ds(0, MOE_MB)],
                                         hs_hbm.at[pl.ds(pl.multiple_of(b * MOE_MB, MOE_MB), MOE_MB)], sem)

        def start(b, carry):
            blk_copy(b).start()
            return carry

        def wait(b, carry):
            blk_copy(b).wait()
            return carry

        lax.fori_loop(nv_ref[0], nblk, start, 0)
        lax.fori_loop(nv_ref[0], nblk, wait, 0)


def _dispatch(npieces, soff, doff, nvb, m_bf, rt, upper, rmax):
    nt = m_bf.shape[0]
    ntiles = nt // TM1
    last = ntiles - 1
    return pl.pallas_call(
        functools.partial(_dispatch_body, ntiles=ntiles, nblk=rmax // MOE_MB),
        out_shape=[SDS((rmax, D), F32), SDS((SUB, (ntiles + 1) * TM1), F32)],
        grid_spec=pltpu.PrefetchScalarGridSpec(
            num_scalar_prefetch=4,
            grid=(ntiles + 1,),
            in_specs=[pl.BlockSpec((TM1, D), lambda j, *_: (jnp.minimum(j, last), 0)),
                      pl.BlockSpec((SUB, TM1), lambda j, *_: (0, jnp.minimum(j, last))),
                      pl.BlockSpec((TM1, TM1), lambda j, *_: (0, 0))],
            out_specs=[pl.BlockSpec(memory_space=pl.ANY),
                       pl.BlockSpec((SUB, TM1), lambda j, *_: (0, j))],
            scratch_shapes=[pltpu.VMEM((MOE_S, D), F32), pltpu.SemaphoreType.DMA(())]),
        compiler_params=_cparams(("arbitrary",), 40),
        name="moe_dispatch",
    )(npieces, soff, doff, nvb, m_bf, rt, upper)


def _expert_body(be_ref, nv_ref, x_ref, w1_ref, w3_ref, w2_ref, o_ref):
    b = pl.program_id(0)

    @pl.when(b < nv_ref[0])
    def _():
        xb = x_ref[...].astype(BF16)
        h1 = jnp.dot(xb, w1_ref[...].astype(BF16), preferred_element_type=F32)
        h3 = jnp.dot(xb, w3_ref[...].astype(BF16), preferred_element_type=F32)
        hh = (h1 / (1.0 + jnp.exp(-h1))) * h3
        o_ref[...] = jnp.dot(hh.astype(BF16), w2_ref[...].astype(BF16), preferred_element_type=F32)

    @pl.when(b >= nv_ref[0])
    def _():
        o_ref[...] = jnp.zeros_like(o_ref)


def _experts(blk_e, nvb, hs, ex_w1, ex_w3, ex_w2, layer):
    rmax = hs.shape[0]
    nb = rmax // MOE_MB
    bi = lambda b, be, nv: jnp.maximum(jnp.minimum(b, nv[0] - 1), 0)
    return pl.pallas_call(
        _expert_body,
        out_shape=SDS((rmax, D), F32),
        grid_spec=pltpu.PrefetchScalarGridSpec(
            num_scalar_prefetch=2,
            grid=(nb,),
            in_specs=[pl.BlockSpec((MOE_MB, D), lambda b, be, nv: (bi(b, be, nv), 0)),
                      pl.BlockSpec((None, None, D, EXPERT_HIDDEN), lambda b, be, nv: (layer, be[bi(b, be, nv)], 0, 0)),
                      pl.BlockSpec((None, None, D, EXPERT_HIDDEN), lambda b, be, nv: (layer, be[bi(b, be, nv)], 0, 0)),
                      pl.BlockSpec((None, None, EXPERT_HIDDEN, D), lambda b, be, nv: (layer, be[bi(b, be, nv)], 0, 0))],
            out_specs=pl.BlockSpec((MOE_MB, D), lambda b, be, nv: (b, 0))),
        compiler_params=_cparams(("arbitrary",), 48),
        name="moe_experts",
    )(blk_e, nvb, hs, ex_w1, ex_w3, ex_w2)


def _combine_body(np_ref, so_ref, do_ref, ys_hbm, pos_ref, rt_ref, x1_ref, g2_ref, lg_ref, lb_ref,
                  o_ref, ys_vmem, sem, *, alpha):
    j = pl.program_id(0)

    def copy(so, do):
        return pltpu.make_async_copy(ys_hbm.at[pl.ds(do, MOE_PIECE)], ys_vmem.at[pl.ds(so, MOE_PIECE)], sem)

    _piece_loops(np_ref, so_ref, do_ref, j, lambda so, do: copy(so, do).start())
    z = jnp.concatenate([pos_ref[...], rt_ref[...], jnp.zeros((LANE - 2 * SUB, TM1), F32)], axis=0)
    zt = z.T
    p0 = zt[:, 0:1].astype(I32)
    p1 = zt[:, 1:2].astype(I32)
    w0 = zt[:, SUB + 2:SUB + 3]
    w1 = zt[:, SUB + 3:SUB + 4]
    si = lax.broadcasted_iota(I32, (TM1, MOE_S), 1)
    wm = (jnp.where(si == p0, w0, 0.0) + jnp.where(si == p1, w1, 0.0)).astype(BF16)
    _piece_loops(np_ref, so_ref, do_ref, j, lambda so, do: copy(so, do).wait())
    last = j * N_EXPERTS + N_EXPERTS - 1
    total = so_ref[last] + np_ref[last] * MOE_PIECE
    srow = lax.broadcasted_iota(I32, (MOE_S, 1), 0)
    ys = jnp.where(srow < total, ys_vmem[...], 0.0).astype(BF16)
    f = jnp.dot(wm, ys, preferred_element_type=F32)
    o_ref[...] = _layer_norm(alpha * x1_ref[...] + g2_ref[...] * f) * lg_ref[...] + lb_ref[...]


def _combine(npieces, soff, doff, ys, pos, rt, x1, g2, ln_g, ln_b, layer, n_lat, n_out_tiles, n_out_rows, alpha):
    nlat = n_lat // TM1
    sel = lambda j, *_: (jnp.where(j >= nlat, 1, 0), 0, 0)
    lsel = lambda *blk: pl.BlockSpec((None,) + blk, lambda j, *_: (layer,) + (0,) * len(blk))
    return pl.pallas_call(
        functools.partial(_combine_body, alpha=alpha),
        out_shape=SDS((n_out_rows, D), F32),
        grid_spec=pltpu.PrefetchScalarGridSpec(
            num_scalar_prefetch=3,
            grid=(n_out_tiles,),
            in_specs=[pl.BlockSpec(memory_space=pl.ANY),
                      pl.BlockSpec((SUB, TM1), lambda j, *_: (0, j)),
                      pl.BlockSpec((SUB, TM1), lambda j, *_: (0, j)),
                      pl.BlockSpec((TM1, D), lambda j, *_: (j, 0)),
                      pl.BlockSpec((None, 1, D), sel), lsel(1, D), lsel(1, D)],
            out_specs=pl.BlockSpec((TM1, D), lambda j, *_: (j, 0)),
            scratch_shapes=[pltpu.VMEM((MOE_S, D), F32), pltpu.SemaphoreType.DMA(())]),
        compiler_params=_cparams(("arbitrary",), 40),
        name="moe_combine",
    )(npieces, soff, doff, ys, pos, rt, x1, g2, ln_g, ln_b)


def _moe_plan(cnt, ntiles):
    c = cnt[:, 0, :N_EXPERTS].astype(I32)
    pad8 = (c + MOE_PIECE - 1) // MOE_PIECE * MOE_PIECE
    toff = jnp.cumsum(pad8, axis=1) - pad8
    tot = pad8.sum(axis=0)
    totb = (tot + MOE_MB - 1) // MOE_MB * MOE_MB
    ends = jnp.cumsum(totb)
    base = ends - totb
    dest = base[None, :] + jnp.cumsum(pad8, axis=0) - pad8
    npieces = jnp.concatenate([pad8 // MOE_PIECE, ((totb - tot) // MOE_PIECE)[None, :]], axis=0)
    soff = jnp.concatenate([toff, jnp.zeros((1, N_EXPERTS), I32)], axis=0)
    doff = jnp.concatenate([dest, (base + tot)[None, :]], axis=0)
    return npieces.reshape(-1), soff.reshape(-1), doff.reshape(-1), ends


def _rope_tables(n_lat, nt):
    t = jnp.arange(n_lat)
    pos = jnp.stack([(t // GRID_W).astype(F32), (t % GRID_W).astype(F32)], axis=1)
    inv = ROPE_BASE ** (-jnp.arange(ROPE_PAIRS, dtype=F32) / ROPE_PAIRS)
    ang = pos[:, :, None] * inv[None, None, :]
    cos = jnp.repeat(jnp.cos(ang), 2, axis=1).reshape(n_lat, 2, 2, ROPE_PAIRS)
    sin = jnp.sin(ang)
    sin = jnp.stack([-sin, sin], axis=2)
    cos = jnp.tile(cos.reshape(n_lat, HEAD_DIM), (1, LANE // HEAD_DIM))
    sin = jnp.tile(sin.reshape(n_lat, HEAD_DIM), (1, LANE // HEAD_DIM))
    cos = jnp.concatenate([cos, jnp.ones((nt - n_lat, LANE), F32)], axis=0)
    sin = jnp.concatenate([sin, jnp.zeros((nt - n_lat, LANE), F32)], axis=0)
    return cos, sin


def _hy_features(l):
    t01 = jnp.linspace(0.0, 1.0, l, dtype=F32)
    fr = jnp.linspace(1e-4, HY_BANDS - 1, HY_BANDS, dtype=F32)
    ang = (2.0 * math.pi / l) * jnp.arange(l, dtype=F32)[:, None] * fr[None, :]
    feat = jnp.concatenate([t01[:, None], jnp.cos(ang), -jnp.sin(ang)], axis=-1)
    feat = jnp.pad(feat, ((0, 0), (0, LANE - HY_EMB)))
    return feat.reshape(FFT_N1 // 2, l // (FFT_N1 // 2), LANE).transpose(1, 0, 2).reshape(l, LANE)


def kernel(x, c, ctx, c_ctx, ada_w, ada_b, w_in, attn_sink, pool_w, pool_scale, hy_conv_w, hy_conv_b, hy_w1, hy_b1, hy_freq1, hy_w2, hy_b2, hy_freq2, hy_w3, hy_bias, sc_conv_w, w_branch, w_out, ln1_g, ln1_b, ln2_g, ln2_b, rg_w, rg_b, re_w, re_b, ex_w1, ex_w3, ex_w2):
    depth = w_in.shape[0]
    assert x.shape[0] == 1 and ctx.shape[0] == 1 and x.shape[2] == D
    n_lat = x.shape[1]
    n_ctx = ctx.shape[1]
    assert n_ctx == TM and n_lat % TM1 == 0 and n_lat >= 2 * TM1
    nt = n_lat + n_ctx + PADR
    ntiles1 = nt // TM1
    alpha = (2 * depth) ** 0.25

    cos_t, sin_t = _rope_tables(n_lat, nt)
    w_in_bf = w_in.astype(BF16)
    wb_bf = w_branch.astype(BF16)
    wo_bf = w_out.astype(BF16)
    pool_w_bf = pool_w.astype(BF16)
    r3 = lambda a: a.reshape(depth, 1, a.shape[-1])
    tabs_l = _dft_tables(n_lat // (FFT_N1 // 2))
    tabs_c = _dft_tables(n_ctx // (FFT_N1 // 2))
    feat_l, feat_c = _hy_features(n_lat), _hy_features(n_ctx)
    deltas = jnp.abs(jnp.linspace(math.log(HY_TARGET) / HY_SLOW_PCT, math.log(HY_TARGET) / HY_FAST_PCT,
                                  BW, dtype=F32))[None, :]
    w1p = jnp.pad(hy_w1, ((0, 0), (0, LANE - HY_EMB), (0, 0)))
    w3p_bf = hy_w3.reshape(depth, HY_HIDDEN, 2, 2, BW).transpose(0, 1, 3, 2, 4).reshape(depth, HY_HIDDEN, 4 * BW).astype(BF16)
    hyw = (w1p, r3(hy_b1), r3(hy_freq1), hy_w2, r3(hy_b2), r3(hy_freq2), w3p_bf, deltas, hy_bias)
    rw = jnp.concatenate([re_w, rg_w, jnp.zeros((depth, D, LANE - N_EXPERTS - N_GROUPS), F32)], axis=2)
    rb = jnp.concatenate([re_b, rg_b, jnp.full((depth, LANE - N_EXPERTS - N_GROUPS), NEG_INF, F32)], axis=1)
    rb = rb.reshape(depth, 1, LANE)
    upper = (jnp.arange(TM1)[:, None] < jnp.arange(TM1)[None, :]).astype(BF16)
    rmax = -(-(2 * (n_lat + n_ctx) + ntiles1 * N_EXPERTS * (MOE_PIECE - 1) + N_EXPERTS * (MOE_MB - 1)) // MOE_MB) * MOE_MB
    nblk = rmax // MOE_MB

    s8 = jnp.concatenate([c, c_ctx[None, :], jnp.zeros((SUB - 2, D), F32)], axis=0)
    mod = _mod_all(s8, ada_w, ada_b.reshape(depth, 1, 6 * D))

    xs = jnp.concatenate([x[0], ctx[0], jnp.zeros((PADR, D), F32)], axis=0)
    for i in range(depth):
        last = i == depth - 1
        mp = lambda j: mod[i, 0:2, j * D:(j + 1) * D].reshape(2, 1, D)
        sh1, sc1, g1, sh2, sc2, g2 = (mp(j) for j in range(6))

        q, k, v, pool_in, hy_in, sc_in, gate = _inproj(xs, sh1, sc1, cos_t, sin_t, w_in_bf, i, n_lat // TM1)
        att = _attention(attn_sink[i], q, k, v, n_lat)
        ypool, ysc, x1p, x2p, vp, x1c, x2c, vc = _local_ops(
            pool_in, sc_in, hy_in, pool_w_bf, pool_scale.reshape(depth, 1, BW),
            hy_conv_w, hy_conv_b.reshape(depth, 1, 3 * BW), sc_conv_w, i, n_lat, n_ctx)
        yhp = _hyena_latent(x1p, x2p, vp, feat_l, tabs_l, hyw, i, n_lat)
        if last:
            yhc = jnp.zeros((n_ctx, BW), BF16)
        else:
            yhc = _hyena(x1c, x2c, vc, feat_c, tabs_c, hyw, i, n_ctx, 0)
        x1, m_bf, rt, cnt = _merge(att, ypool, yhp, yhc, ysc, gate, xs, wb_bf, wo_bf, g1, r3(ln1_g), r3(ln1_b),
                                   sh2, sc2, rw, rb, i, n_lat, n_ctx, alpha)

        npieces, soff, doff, ends = _moe_plan(cnt, ntiles1)
        nvb = (ends[-1] // MOE_MB).astype(I32).reshape(1)
        blk_start = jnp.arange(nblk, dtype=I32) * MOE_MB
        blk_e = jnp.minimum(jnp.sum((ends[None, :] <= blk_start[:, None]).astype(I32), axis=1), N_EXPERTS - 1)
        hs, pos = _dispatch(npieces, soff, doff, nvb, m_bf, rt, upper, rmax)
        ys = _experts(blk_e, nvb, hs, ex_w1, ex_w3, ex_w2, i)
        if last:
            xs = _combine(npieces, soff, doff, ys, pos, rt, x1, g2, r3(ln2_g), r3(ln2_b), i, n_lat,
                          n_lat // TM1, n_lat, alpha)
        else:
            xs = _combine(npieces, soff, doff, ys, pos, rt, x1, g2, r3(ln2_g), r3(ln2_b), i, n_lat, ntiles1, nt,
                          alpha)
    return xs[None]
```

```python
import functools
import math

import jax
import jax.numpy as jnp
from jax import lax
from jax.experimental import pallas as pl
from jax.experimental.pallas import tpu as pltpu

F32 = jnp.float32
BF16 = jnp.bfloat16
I32 = jnp.int32
U32 = jnp.uint32
SDS = jax.ShapeDtypeStruct
HIGHEST = lax.Precision.HIGHEST

D = 1024
GRID_W = 64
HEADS = 8
KV_HEADS = 2
HEAD_DIM = 64
WINDOW = 128
ATT_BLK = 128
ROPE_PAIRS = 16
ROPE_BASE = 10000.0
BW = 512
POOL_WINDOWS = (2, 4, 8, 16)
POOL_GW = 128
HY_EMB = 33
HY_BANDS = 16
HY_HIDDEN = 64
HY_TARGET = 1e-2
HY_FAST_PCT = 0.3
HY_SLOW_PCT = 1.5
HY_MOD_SHIFT = 0.05
Q_COLS = HEADS * HEAD_DIM
KV_COLS = KV_HEADS * HEAD_DIM
IN_COLS = Q_COLS + 2 * KV_COLS + BW + 3 * BW + 3 * BW + 4 * D
N_GROUPS = 4
EPG = 8
N_EXPERTS = 32
EXPERT_HIDDEN = 512
LN_EPS = 1e-6
NEG_INF = -1e30

LANE = 128
SUB = 8
BSUB = 16
TM = 256
TM1 = 512
PADR = 256
HALO = 16
FFT_N1 = 512
FFT_K1 = 264
FFT_KB = 8
FFT_TN = 2048
FFT_NB = 8
MOE_PIECE = 8
MOE_MB = 256
MOE_S = 2 * TM1 + 256


def _cparams(sem, vmem_mb):
    return pltpu.CompilerParams(dimension_semantics=sem, vmem_limit_bytes=vmem_mb * 2 ** 20)


def _layer_norm(x):
    mu = jnp.mean(x, axis=-1, keepdims=True)
    xc = x - mu
    var = jnp.mean(xc * xc, axis=-1, keepdims=True)
    return xc * lax.rsqrt(var + LN_EPS)


def _mod_body(s_ref, w_ref, b_ref, o_ref):
    s = s_ref[...]
    s = s * (1.0 / (1.0 + jnp.exp(-s)))
    o_ref[...] = jnp.dot(s.astype(BF16), w_ref[...].astype(BF16), preferred_element_type=F32) + b_ref[...]


def _mod_all(s8, ada_w, ada_b3):
    depth = ada_w.shape[0]
    tn = 1024
    return pl.pallas_call(
        _mod_body,
        out_shape=SDS((depth, SUB, 6 * D), F32),
        grid=(depth, 6 * D // tn),
        in_specs=[pl.BlockSpec((SUB, D), lambda l, j: (0, 0)),
                  pl.BlockSpec((None, D, tn), lambda l, j: (l, 0, j)),
                  pl.BlockSpec((None, 1, tn), lambda l, j: (l, 0, j))],
        out_specs=pl.BlockSpec((None, SUB, tn), lambda l, j: (l, 0, j)),
        compiler_params=_cparams(("parallel", "parallel"), 32),
        name="ada_mod",
    )(s8, ada_w, ada_b3)


def _inproj_body(x_ref, sh_ref, sc_ref, cos_ref, sin_ref, w_ref,
                 q_ref, k_ref, v_ref, pool_ref, hy_ref, scv_ref, gate_ref):
    h = _layer_norm(x_ref[...]) * (1.0 + sc_ref[...]) + sh_ref[...]
    hb = h.astype(BF16)

    def mm(a, b):
        return jnp.dot(hb, w_ref[:, a:b], preferred_element_type=F32)

    cs = cos_ref[...]
    sn = sin_ref[...]
    lane = lax.broadcasted_iota(I32, cs.shape, 1)
    first = (lane & ROPE_PAIRS) == 0

    def rope(t):
        sw = jnp.where(first, pltpu.roll(t, LANE - ROPE_PAIRS, 1), pltpu.roll(t, ROPE_PAIRS, 1))
        return t * cs + sw * sn

    for j in range(Q_COLS // LANE):
        q_ref[:, j * LANE:(j + 1) * LANE] = (rope(mm(j * LANE, (j + 1) * LANE)) * (HEAD_DIM ** -0.5)).astype(BF16)
    o = Q_COLS
    k_ref[...] = rope(mm(o, o + KV_COLS)).astype(BF16)
    o += KV_COLS
    v_ref[...] = mm(o, o + KV_COLS).astype(BF16)
    o += KV_COLS
    pool_ref[...] = mm(o, o + BW).astype(BF16)
    o += BW
    for j in range(3):
        hy_ref[:, j * BW:(j + 1) * BW] = mm(o + j * BW, o + (j + 1) * BW).astype(BF16)
    o += 3 * BW
    for j in range(3):
        scv_ref[:, j * BW:(j + 1) * BW] = mm(o + j * BW, o + (j + 1) * BW).astype(BF16)
    o += 3 * BW
    for j in range(4 * D // 1024):
        gate_ref[:, j * 1024:(j + 1) * 1024] = mm(o + j * 1024, o + (j + 1) * 1024).astype(BF16)


def _inproj(x, sh, sc, cos_t, sin_t, w_in_bf, layer, n_lat_tiles):
    nt = x.shape[0]
    sel = lambda i: (jnp.where(i >= n_lat_tiles, 1, 0), 0, 0)
    row = lambda c: pl.BlockSpec((TM1, c), lambda i: (i, 0))
    outs = [SDS((nt, Q_COLS), BF16), SDS((nt, KV_COLS), BF16), SDS((nt, KV_COLS), BF16), SDS((nt, BW), BF16),
            SDS((nt, 3 * BW), BF16), SDS((nt, 3 * BW), BF16), SDS((nt, 4 * D), BF16)]
    return pl.pallas_call(
        _inproj_body,
        out_shape=outs,
        grid=(nt // TM1,),
        in_specs=[row(D),
                  pl.BlockSpec((None, 1, D), sel), pl.BlockSpec((None, 1, D), sel),
                  row(LANE), row(LANE),
                  pl.BlockSpec((None, D, IN_COLS), lambda i: (layer, 0, 0), pipeline_mode=pl.Buffered(1))],
        out_specs=[row(Q_COLS), row(KV_COLS), row(KV_COLS), row(BW), row(3 * BW), row(3 * BW), row(4 * D)],
        compiler_params=_cparams(("parallel",), 58),
        name="inproj",
    )(x, sh, sc, cos_t, sin_t, w_in_bf)


def _attn_body(sink_ref, q_ref, kp_ref, kc_ref, kn_ref, vp_ref, vc_ref, vn_ref, kx_ref, vx_ref, o_ref, *, nlb):
    i = pl.program_id(0)
    is_lat = i < nlb
    lat_i = jnp.where(is_lat, 1, 0)
    prev_i = jnp.where(jnp.logical_and(is_lat, i > 0), 1, 0)
    next_i = jnp.where(jnp.logical_and(is_lat, i < nlb - 1), 1, 0)
    ncx = kx_ref.shape[0]
    nk = ncx + 3 * ATT_BLK

    r = lax.broadcasted_iota(I32, (ATT_BLK, nk), 0)
    c = lax.broadcasted_iota(I32, (ATT_BLK, nk), 1) - ncx
    band = jnp.logical_and(c >= r, c <= r + 2 * WINDOW)
    grp = jnp.where(c < ATT_BLK, prev_i, jnp.where(c < 2 * ATT_BLK, lat_i, next_i))
    mask = jnp.logical_or(c < 0, jnp.logical_and(band, grp > 0))

    kcat = jnp.concatenate([kx_ref[...], kp_ref[...], kc_ref[...], kn_ref[...]], axis=0).astype(F32)
    vcat = jnp.concatenate([vx_ref[...], vp_ref[...], vc_ref[...], vn_ref[...]], axis=0).astype(F32)
    lo = lax.broadcasted_iota(I32, kcat.shape, 1) < HEAD_DIM
    kroll = pltpu.roll(kcat, HEAD_DIM, 1)
    vroll = pltpu.roll(vcat, HEAD_DIM, 1)
    zero = jnp.zeros_like(kcat)
    kvar = [[jnp.where(lo, kcat, zero), jnp.where(lo, zero, kroll)],
            [jnp.where(lo, kroll, zero), jnp.where(lo, zero, kcat)]]
    vvar = [[jnp.where(lo, vcat, zero), jnp.where(lo, zero, vroll)],
            [jnp.where(lo, vroll, zero), jnp.where(lo, zero, vcat)]]
    kvar = [[a.astype(BF16) for a in row] for row in kvar]
    vvar = [[a.astype(BF16) for a in row] for row in vvar]

    for p in range(HEADS // 2):
        qp = q_ref[:, p * LANE:(p + 1) * LANE]
        kh = (2 * p) // (HEADS // KV_HEADS)
        acc = jnp.zeros((ATT_BLK, LANE), F32)
        for rr in range(2):
            sink = sink_ref[2 * p + rr]
            s = lax.dot_general(qp, kvar[kh][rr], (((1,), (1,)), ((), ())), preferred_element_type=F32)
            s = jnp.where(mask, s, NEG_INF)
            m = jnp.maximum(jnp.max(s, axis=1, keepdims=True), sink)
            e = jnp.exp(s - m)
            den = jnp.sum(e, axis=1, keepdims=True) + jnp.exp(sink - m)
            o = jnp.dot(e.astype(BF16), vvar[kh][rr], preferred_element_type=F32)
            acc = acc + o * (1.0 / den)
        o_ref[:, p * LANE:(p + 1) * LANE] = acc.astype(BF16)


def _attention(sink, q, k, v, n_lat):
    nt = q.shape[0]
    nlb = n_lat // ATT_BLK
    cxb = n_lat // TM
    cl = lambda i: jnp.clip(i, 0, nlb - 1)
    kspec = lambda f: pl.BlockSpec((ATT_BLK, KV_COLS), lambda i: (f(i), 0))
    prev = lambda i: jnp.clip(i - 1, 0, nlb - 1)
    nxt = lambda i: jnp.clip(i + 1, 0, nlb - 1)
    cx = pl.BlockSpec((TM, KV_COLS), lambda i: (cxb, 0))
    return pl.pallas_call(
        functools.partial(_attn_body, nlb=nlb),
        out_shape=SDS((nt, Q_COLS), BF16),
        grid=(nt // ATT_BLK,),
        in_specs=[pl.BlockSpec(memory_space=pltpu.SMEM),
                  pl.BlockSpec((ATT_BLK, Q_COLS), lambda i: (i, 0)),
                  kspec(prev), kspec(cl), kspec(nxt), kspec(prev), kspec(cl), kspec(nxt), cx, cx],
        out_specs=pl.BlockSpec((ATT_BLK, Q_COLS), lambda i: (i, 0)),
        compiler_params=_cparams(("parallel",), 32),
        name="attention",
    )(sink, q, k, k, k, v, v, v, k, v)


def _pack2(a, b):
    hi = lax.bitcast_convert_type(a.astype(BF16).astype(F32), U32)
    lo = lax.bitcast_convert_type(b.astype(BF16).astype(F32), U32)
    return hi | (lo >> 16)


def _unpack2(w):
    return (lax.bitcast_convert_type(w & jnp.uint32(0xFFFF0000), F32),
            lax.bitcast_convert_type(w << 16, F32))


def _local_body(pm_ref, pa_ref, pb_ref, sm_ref, sa_ref, sb_ref, hm_ref, ha_ref, hb_ref,
                pw_ref, ps_ref, hw_ref, hbias_ref, sw_ref,
                ypool_ref, ysc_ref, x1p_ref, x2p_ref, vp_ref, x1c_ref, x2c_ref, vc_ref,
                pext, sext, hext, *, nlat, n_lat, n_ctx):
    i = pl.program_id(0)
    is_lat = i < nlat
    pf = jnp.where(jnp.logical_and(is_lat, i > 0), 1.0, 0.0)
    nf = jnp.where(jnp.logical_and(is_lat, i < nlat - 1), 1.0, 0.0)
    r = lax.broadcasted_iota(I32, (TM1, 1), 0)
    keep = jnp.where(jnp.logical_or(is_lat, r < n_ctx), 1.0, 0.0)

    def fill(ext, a_ref, m_ref, b_ref):
        ext[0:HALO, :] = a_ref[...].astype(F32) * pf
        ext[HALO:HALO + TM1, :] = m_ref[...].astype(F32) * keep
        ext[HALO + TM1:HALO + TM1 + HALO, :] = b_ref[...].astype(F32) * nf

    def sh(ext, d, c0=None, c1=None):
        if c0 is None:
            return ext[pl.ds(HALO + d, TM1), :]
        return ext[pl.ds(HALO + d, TM1), c0:c1]

    fill(pext, pa_ref, pm_ref, pb_ref)
    t = jnp.where(is_lat, i * TM1, 0) + r
    ln = jnp.where(is_lat, n_lat, n_ctx)
    for g, w in enumerate(POOL_WINDOWS):
        left = w // 2
        right = w - left - 1
        c0, c1 = g * POOL_GW, (g + 1) * POOL_GW
        acc = sh(pext, -left, c0, c1)
        for dd in range(-left + 1, right + 1):
            acc = acc + sh(pext, dd, c0, c1)
        cnt = jnp.maximum(jnp.minimum(t + right + 1, ln) - jnp.maximum(t - left, 0), 1).astype(F32)
        dpool = acc / cnt - sh(pext, 0, c0, c1)
        y = jnp.dot(dpool.astype(BF16), pw_ref[g], preferred_element_type=F32)
        ypool_ref[:, c0:c1] = (y * ps_ref[:, c0:c1]).astype(BF16)

    sext[0:HALO, :] = (sa_ref[:, BW:2 * BW].astype(F32) * sa_ref[:, 2 * BW:3 * BW].astype(F32)) * pf
    sext[HALO:HALO + TM1, :] = sm_ref[:, BW:2 * BW].astype(F32) * sm_ref[:, 2 * BW:3 * BW].astype(F32) * keep
    sext[HALO + TM1:HALO + TM1 + HALO, :] = (sb_ref[:, BW:2 * BW].astype(F32) * sb_ref[:, 2 * BW:3 * BW].astype(F32)) * nf
    conv = sh(sext, -1) * sw_ref[0:1, :] + sh(sext, 0) * sw_ref[1:2, :] + sh(sext, 1) * sw_ref[2:3, :]
    ysc_ref[...] = (sm_ref[:, 0:BW].astype(F32) * conv).astype(BF16)

    fill(hext, ha_ref, hm_ref, hb_ref)
    nat = (x1c_ref, x2c_ref, vc_ref)
    pk = (x1p_ref, x2p_ref, vp_ref)
    n2c = x1p_ref.shape[0]
    for j in range(3):
        c0, c1 = j * BW, (j + 1) * BW
        uc = (sh(hext, -1, c0, c1) * hw_ref[0:1, c0:c1] + sh(hext, 0, c0, c1) * hw_ref[1:2, c0:c1]
              + sh(hext, 1, c0, c1) * hw_ref[2:3, c0:c1] + hbias_ref[:, c0:c1])
        nat[j][...] = uc.astype(BF16)
        word = _pack2(uc[:, 0:BW // 2], uc[:, BW // 2:BW])
        for a in range(TM1 // n2c):
            pk[j][:, a, :] = word[a * n2c:(a + 1) * n2c]


def _local_ops(pool_in, sc_in, hy_in, pool_w_bf, pool_scale, hy_conv_w, hy_conv_b, sc_conv_w, layer, n_lat, n_ctx):
    nt = pool_in.shape[0]
    nlat = n_lat // TM1
    n2c = n_lat // (FFT_N1 // 2)
    per_tile = TM1 // n2c
    nhb = nt // HALO
    per = TM1 // HALO
    main = lambda c: pl.BlockSpec((TM1, c), lambda i: (i, 0))
    before = lambda c: pl.BlockSpec((HALO, c), lambda i: (jnp.maximum(i * per - 1, 0), 0))
    after = lambda c: pl.BlockSpec((HALO, c), lambda i: (jnp.minimum((i + 1) * per, nhb - 1), 0))
    lsel = lambda *blk: pl.BlockSpec((None,) + blk, lambda i: (layer,) + (0,) * len(blk))
    out = SDS((nt, BW), BF16)
    packed = SDS((n2c, (nlat + 1) * per_tile, BW // 2), U32)
    ctx_out = SDS((TM1, BW), BF16)
    pspec = pl.BlockSpec((n2c, per_tile, BW // 2), lambda i: (0, i, 0))
    cspec = pl.BlockSpec((TM1, BW), lambda i: (0, 0))
    return pl.pallas_call(
        functools.partial(_local_body, nlat=nlat, n_lat=n_lat, n_ctx=n_ctx),
        out_shape=[out, out, packed, packed, packed, ctx_out, ctx_out, ctx_out],
        grid=(nlat + 1,),
        in_specs=[main(BW), before(BW), after(BW),
                  main(3 * BW), before(3 * BW), after(3 * BW),
                  main(3 * BW), before(3 * BW), after(3 * BW),
                  lsel(4, POOL_GW, POOL_GW), lsel(1, BW), lsel(3, 3 * BW), lsel(1, 3 * BW), lsel(3, BW)],
        out_specs=[main(BW), main(BW), pspec, pspec, pspec, cspec, cspec, cspec],
        scratch_shapes=[pltpu.VMEM((TM1 + 2 * HALO, BW), F32), pltpu.VMEM((TM1 + 2 * HALO, BW), F32),
                        pltpu.VMEM((TM1 + 2 * HALO, 3 * BW), F32)],
        compiler_params=_cparams(("arbitrary",), 48),
        name="local_ops",
    )(pool_in, pool_in, pool_in, sc_in, sc_in, sc_in, hy_in, hy_in, hy_in,
      pool_w_bf, pool_scale, hy_conv_w, hy_conv_b, sc_conv_w)


def _dft_tables(n2):
    n = FFT_N1 * n2
    k1 = jnp.arange(FFT_K1, dtype=I32)
    n1 = jnp.arange(FFT_N1 // 2, dtype=I32)
    ok = (k1 <= FFT_N1 // 2)
    ang = (2.0 * math.pi / FFT_N1) * ((k1[:, None] * n1[None, :]) % FFT_N1).astype(F32)
    c1 = jnp.where(ok[:, None], jnp.cos(ang), 0.0)
    s1 = jnp.where(ok[:, None], jnp.sin(ang), 0.0)
    f1 = jnp.concatenate([c1, -s1], axis=0).astype(BF16)
    wt = jnp.where((k1 == 0) | (k1 == FFT_N1 // 2), 1.0, 2.0) / n
    f1i = jnp.concatenate([(c1 * wt[:, None]).T, (-s1 * wt[:, None]).T], axis=1).astype(BF16)
    if n2 == 1:
        return f1, f1i, None, None
    k2 = jnp.arange(n2, dtype=I32)
    m2 = jnp.arange(n2, dtype=I32)
    kk = k1[:, None, None] + FFT_N1 * k2[None, :, None]
    th = (2.0 * math.pi / n) * ((kk * m2[None, None, :]) % n).astype(F32)
    okb = ok[:, None, None]
    cc = jnp.where(okb, jnp.cos(th), 0.0)
    ss = jnp.where(okb, jnp.sin(th), 0.0)
    g = jnp.concatenate([jnp.concatenate([cc, ss], axis=2), jnp.concatenate([-ss, cc], axis=2)], axis=1)
    return f1, f1i, g.astype(BF16), jnp.swapaxes(g, 1, 2).astype(BF16)


def _filter_body(feat_ref, w1_ref, b1_ref, f1_ref, w2_ref, b2_ref, f2_ref, w3_ref, dl_ref, dft_ref,
                 af_ref, ss_ref, filt):
    i = pl.program_id(0)
    feat = feat_ref[...]
    h = jnp.sin(f1_ref[...] * (jnp.dot(feat, w1_ref[...], precision=HIGHEST, preferred_element_type=F32) + b1_ref[...]))
    h = jnp.sin(f2_ref[...] * (jnp.dot(h, w2_ref[...], precision=HIGHEST, preferred_element_type=F32) + b2_ref[...]))
    raw = jnp.dot(h.astype(BF16), w3_ref[...], preferred_element_type=F32)
    win = jnp.exp(-feat[:, 0:1] * dl_ref[...]) + HY_MOD_SHIFT
    row = lax.broadcasted_iota(I32, (TM, 1), 0) + i * TM

    @pl.when(i == 0)
    def _():
        ss_ref[...] = jnp.zeros_like(ss_ref)

    for j in range(4):
        f = raw[:, j * BW:(j + 1) * BW] * win
        if j >= 2:
            f = jnp.where(row == 0, 0.0, f)
        filt[:, j * BW:(j + 1) * BW] = f.astype(BF16)
        ss_ref[:, j * BW:(j + 1) * BW] += jnp.sum(f * f, axis=0, keepdims=True)
    af_ref[...] = jnp.dot(dft_ref[...], filt[...], preferred_element_type=F32).astype(BF16)


def _hy_filters(feat_perm, w1p, b1, fr1, w2, b2, fr2, w3p_bf, deltas, f1, layer):
    l = feat_perm.shape[0]
    n2 = l // TM
    lsel = lambda *blk: pl.BlockSpec((None,) + blk, lambda i: (layer,) + (0,) * len(blk))
    return pl.pallas_call(
        _filter_body,
        out_shape=[SDS((2 * FFT_K1, n2 * 4 * BW), BF16), SDS((1, 4 * BW), F32)],
        grid=(n2,),
        in_specs=[pl.BlockSpec((TM, LANE), lambda i: (i, 0)),
                  lsel(LANE, HY_HIDDEN), lsel(1, HY_HIDDEN), lsel(1, HY_HIDDEN),
                  lsel(HY_HIDDEN, HY_HIDDEN), lsel(1, HY_HIDDEN), lsel(1, HY_HIDDEN),
                  lsel(HY_HIDDEN, 4 * BW), pl.BlockSpec((1, BW), lambda i: (0, 0)),
                  pl.BlockSpec((2 * FFT_K1, FFT_N1 // 2), lambda i: (0, 0))],
        out_specs=[pl.BlockSpec((2 * FFT_K1, 4 * BW), lambda i: (0, i)), pl.BlockSpec((1, 4 * BW), lambda i: (0, 0))],
        scratch_shapes=[pltpu.VMEM((TM, 4 * BW), BF16)],
        compiler_params=_cparams(("arbitrary",), 32),
        name="hy_filters",
    )(feat_perm, w1p, b1, fr1, w2, b2, fr2, w3p_bf, deltas, f1)


def _dft_major_body(f_ref, x_ref, o_ref):
    o_ref[...] = jnp.dot(f_ref[...], x_ref[...], preferred_element_type=F32).astype(BF16)


def _dft_major(f1, x2d, row_blk, ncols):
    tn = min(FFT_TN, ncols)
    return pl.pallas_call(
        _dft_major_body,
        out_shape=SDS((2 * FFT_K1, ncols), BF16),
        grid=(ncols // tn,),
        in_specs=[pl.BlockSpec((2 * FFT_K1, FFT_N1 // 2), lambda j: (0, 0)),
                  pl.BlockSpec((FFT_N1 // 2, tn), lambda j: (row_blk, j))],
        out_specs=pl.BlockSpec((2 * FFT_K1, tn), lambda j: (0, j)),
        compiler_params=_cparams(("parallel",), 32),
        name="dft_major",
    )(f1, x2d)


def _filter_minor_body(a_ref, g_ref, ss_ref, h_ref, *, n2):
    ssv = ss_ref[...]
    scale = lax.rsqrt(ssv[:, 0:2 * BW] + ssv[:, 2 * BW:4 * BW] + LN_EPS)
    for j in range(a_ref.shape[1]):
        if n2 == 1:
            s = jnp.concatenate([a_ref[0, j], a_ref[1, j]], axis=0).astype(F32)
        else:
            a = jnp.concatenate([a_ref[0, j], a_ref[1, j]], axis=0)
            s = jnp.dot(g_ref[j], a, preferred_element_type=F32)
        sf, sb = s[:, 0:2 * BW], s[:, 2 * BW:4 * BW]
        hr = (sf[0:n2] + sb[0:n2]) * scale
        hi = (sf[n2:2 * n2] - sb[n2:2 * n2]) * scale
        h_ref[j] = jnp.concatenate([hr, hi], axis=0).astype(BF16)


def _conv_minor_body(a_ref, h_ref, g_ref, gt_ref, d_ref, *, n2):
    for j in range(a_ref.shape[1]):
        a = jnp.concatenate([a_ref[0, j], a_ref[1, j]], axis=0)
        if n2 == 1:
            x = a.astype(F32)
        else:
            x = jnp.dot(g_ref[j], a, preferred_element_type=F32)
        h = h_ref[j].astype(F32)
        xr, xi, hr, hi = x[0:n2], x[n2:2 * n2], h[0:n2], h[n2:2 * n2]
        y = jnp.concatenate([xr * hr - xi * hi, xr * hi + xi * hr], axis=0)
        if n2 == 1:
            dv = y
        else:
            dv = jnp.dot(gt_ref[j], y.astype(BF16), preferred_element_type=F32)
        d_ref[0, j] = dv[0:n2].astype(BF16)
        d_ref[1, j] = dv[n2:2 * n2].astype(BF16)


def _filter_minor(af, g, ss, n2):
    a4 = af.reshape(2, FFT_K1, n2, 4 * BW)
    kb = FFT_KB
    gspec = (pl.BlockSpec((kb, 2 * n2, 2 * n2), lambda i: (i, 0, 0)) if n2 > 1
             else pl.BlockSpec((SUB, LANE), lambda i: (0, 0)))
    return pl.pallas_call(
        functools.partial(_filter_minor_body, n2=n2),
        out_shape=SDS((FFT_K1, 2 * n2, 2 * BW), BF16),
        grid=(FFT_K1 // kb,),
        in_specs=[pl.BlockSpec((2, kb, n2, 4 * BW), lambda i: (0, i, 0, 0)), gspec,
                  pl.BlockSpec((1, 4 * BW), lambda i: (0, 0))],
        out_specs=pl.BlockSpec((kb, 2 * n2, 2 * BW), lambda i: (i, 0, 0)),
        compiler_params=_cparams(("parallel",), 40),
        name="hy_filter_minor",
    )(a4, g if n2 > 1 else jnp.zeros((SUB, LANE), BF16), ss)


def _conv_minor(a, hspec, g, gt, order, n2):
    a4 = a.reshape(2, FFT_K1, n2, BW)
    kb = FFT_KB
    dummy = jnp.zeros((SUB, LANE), BF16)
    gspec = (pl.BlockSpec((kb, 2 * n2, 2 * n2), lambda i: (i, 0, 0)) if n2 > 1
             else pl.BlockSpec((SUB, LANE), lambda i: (0, 0)))
    d4 = pl.pallas_call(
        functools.partial(_conv_minor_body, n2=n2),
        out_shape=SDS((2, FFT_K1, n2, BW), BF16),
        grid=(FFT_K1 // kb,),
        in_specs=[pl.BlockSpec((2, kb, n2, BW), lambda i: (0, i, 0, 0)),
                  pl.BlockSpec((kb, 2 * n2, BW), lambda i: (i, 0, order)), gspec, gspec],
        out_specs=pl.BlockSpec((2, kb, n2, BW), lambda i: (0, i, 0, 0)),
        compiler_params=_cparams(("parallel",), 32),
        name="hy_conv_minor",
    )(a4, hspec, g if n2 > 1 else dummy, gt if n2 > 1 else dummy)
    return d4.reshape(2 * FFT_K1, n2 * BW)


def _idft_gate_body(f_ref, d_ref, gate_ref, z_ref, b_ref, o_ref):
    y = jnp.dot(f_ref[...], d_ref[...], preferred_element_type=F32)
    z = z_ref[...].astype(F32)
    o_ref[...] = (gate_ref[...].astype(F32) * (y + b_ref[...] * z)).astype(BF16)


def _idft_gate(f1i, d, gate2d, gate_blk, z2d, z_blk, bias_t, ncols):
    tn = min(FFT_TN, ncols)
    return pl.pallas_call(
        _idft_gate_body,
        out_shape=SDS((FFT_N1 // 2, ncols), BF16),
        grid=(ncols // tn,),
        in_specs=[pl.BlockSpec((FFT_N1 // 2, 2 * FFT_K1), lambda j: (0, 0)),
                  pl.BlockSpec((2 * FFT_K1, tn), lambda j: (0, j)),
                  pl.BlockSpec((FFT_N1 // 2, tn), lambda j: (gate_blk, j)),
                  pl.BlockSpec((FFT_N1 // 2, tn), lambda j: (z_blk, j)),
                  pl.BlockSpec((1, tn), lambda j: (0, 0))],
        out_specs=pl.BlockSpec((FFT_N1 // 2, tn), lambda j: (0, j)),
        compiler_params=_cparams(("parallel",), 32),
        name="idft_gate",
    )(f1i, d, gate2d, z2d, bias_t)


def _hyena(x1, x2, v, feat, tabs, hyw, layer, seq_len, row0):
    n2 = seq_len // (FFT_N1 // 2)
    f1, f1i, g, gt = tabs
    w1p, b1, fr1, w2, b2, fr2, w3p_bf, deltas, hy_bias = hyw
    nt = x1.shape[0]
    af, ss = _hy_filters(feat, w1p, b1, fr1, w2, b2, fr2, w3p_bf, deltas, f1, layer)
    hspec = _filter_minor(af, g, ss, n2)
    view = lambda a: a.reshape(nt // n2, n2 * BW)
    blk = row0 // seq_len if n2 == 1 else 0
    ncols = n2 * BW
    tn = min(FFT_TN, ncols)
    z, zv, zblk = v, view(v), blk
    gates = (x1, x2)
    for o in range(2):
        a = _dft_major(f1, zv, zblk, ncols)
        d = _conv_minor(a, hspec, g, gt, o, n2)
        bias_t = jnp.tile(hy_bias[layer, o][None, :], (1, tn // BW))
        z2 = _idft_gate(f1i, d, view(gates[o]), blk, zv, zblk, bias_t, ncols)
        zv, zblk = z2, 0
    return zv.reshape(seq_len, BW)


def _store_spectrum_slab(ref, j, word):
    for q in range(word.shape[1] // LANE):
        ref[q, pl.ds(j, FFT_K1, stride=FFT_NB), :] = word[:, q * LANE:(q + 1) * LANE]


def _load_spectrum_slab(ref, j):
    return jnp.concatenate([ref[q, pl.ds(j, FFT_K1, stride=FFT_NB), :] for q in range(ref.shape[0])], axis=1)


def _load_spectrum_k1(ref, j):
    nq, nblk = ref.shape[0], ref.shape[1]
    return jnp.concatenate([jnp.concatenate([ref[q, b, j] for q in range(nq)], axis=1) for b in range(nblk)], axis=0)


def _store_spectrum_k1(ref, j, word):
    nq, nblk = ref.shape[0], ref.shape[1]
    for b in range(nblk):
        for q in range(nq):
            ref[q, b, j] = word[b * FFT_NB:(b + 1) * FFT_NB, q * LANE:(q + 1) * LANE]


def _filter_p_body(feat_ref, w1_ref, b1_ref, f1_ref, w2_ref, b2_ref, f2_ref, w3_ref, dl_ref, dft_ref,
                   af_ref, ss_ref, h2):
    nb = pl.program_id(0)
    cb = pl.program_id(1)
    nbk = FFT_NB

    @pl.when(jnp.logical_and(nb == 0, cb == 0))
    def _():
        ss_ref[...] = jnp.zeros_like(ss_ref)

    @pl.when(cb == 0)
    def _():
        feat = feat_ref[...]
        h = jnp.sin(f1_ref[...] * (jnp.dot(feat, w1_ref[...], precision=HIGHEST, preferred_element_type=F32) + b1_ref[...]))
        h = jnp.sin(f2_ref[...] * (jnp.dot(h, w2_ref[...], precision=HIGHEST, preferred_element_type=F32) + b2_ref[...]))
        h2[...] = h.astype(BF16)

    row = lax.broadcasted_iota(I32, (TM, 1), 0)
    lag0_bwd = jnp.logical_and(jnp.logical_and(nb == 0, cb >= 2), row == 0)
    ssum = jnp.zeros((1, BW), F32)
    for j in range(nbk):
        raw = jnp.dot(h2[j * TM:(j + 1) * TM, :], w3_ref[...], preferred_element_type=F32)
        win = jnp.exp(-feat_ref[j * TM:(j + 1) * TM, 0:1] * dl_ref[...]) + HY_MOD_SHIFT
        f = raw * win
        if j == 0:
            f = jnp.where(lag0_bwd, 0.0, f)
        ssum = ssum + jnp.sum(f * f, axis=0, keepdims=True)
        rr = jnp.dot(dft_ref[...], f.astype(BF16), preferred_element_type=F32)
        _store_spectrum_slab(af_ref, j, _pack2(rr[0:FFT_K1], rr[FFT_K1:2 * FFT_K1]))
    ss_ref[cb] = ss_ref[cb] + ssum


def _hy_filters_p(feat_perm, w1p, b1, fr1, w2, b2, fr2, w3p_bf, deltas, f1, layer):
    l = feat_perm.shape[0]
    n2 = l // TM
    nbk = FFT_NB
    lsel = lambda *blk: pl.BlockSpec((None,) + blk, lambda nb, cb: (layer,) + (0,) * len(blk))
    return pl.pallas_call(
        _filter_p_body,
        out_shape=[SDS((4 * BW // LANE, n2 // nbk, FFT_K1 * nbk, LANE), U32), SDS((4, 1, BW), F32)],
        grid=(n2 // nbk, 4),
        in_specs=[pl.BlockSpec((nbk * TM, LANE), lambda nb, cb: (nb, 0)),
                  lsel(LANE, HY_HIDDEN), lsel(1, HY_HIDDEN), lsel(1, HY_HIDDEN),
                  lsel(HY_HIDDEN, HY_HIDDEN), lsel(1, HY_HIDDEN), lsel(1, HY_HIDDEN),
                  pl.BlockSpec((None, HY_HIDDEN, BW), lambda nb, cb: (layer, 0, cb)),
                  pl.BlockSpec((1, BW), lambda nb, cb: (0, 0)),
                  pl.BlockSpec((2 * FFT_K1, FFT_N1 // 2), lambda nb, cb: (0, 0))],
        out_specs=[pl.BlockSpec((BW // LANE, None, FFT_K1 * nbk, LANE), lambda nb, cb: (cb, nb, 0, 0)),
                   pl.BlockSpec((4, 1, BW), lambda nb, cb: (0, 0, 0))],
        scratch_shapes=[pltpu.VMEM((nbk * TM, HY_HIDDEN), BF16)],
        compiler_params=_cparams(("arbitrary", "arbitrary"), 40),
        name="hy_filters_p",
    )(feat_perm, w1p, b1, fr1, w2, b2, fr2, w3p_bf, deltas, f1)


def _filter_minor_p_body(a_ref, g_ref, ss_ref, h_ref, *, n2):
    ssv = ss_ref[...]
    scale = lax.rsqrt(ssv[:, 0:2 * BW] + ssv[:, 2 * BW:4 * BW] + LN_EPS)
    for j in range(a_ref.shape[2]):
        re, im = _unpack2(_load_spectrum_k1(a_ref, j))
        a = jnp.concatenate([re, im], axis=0).astype(BF16)
        s = jnp.dot(g_ref[j], a, preferred_element_type=F32)
        sf, sb = s[:, 0:2 * BW], s[:, 2 * BW:4 * BW]
        hr = (sf[0:n2] + sb[0:n2]) * scale
        hi = (sf[n2:2 * n2] - sb[n2:2 * n2]) * scale
        h_ref[j] = jnp.concatenate([hr, hi], axis=0).astype(BF16)


def _filter_minor_p(af_p, g, ss, n2):
    kb = FFT_KB
    nq, nblk = af_p.shape[0], af_p.shape[1]
    af_p = af_p.reshape(nq, nblk, FFT_K1, FFT_NB, LANE)
    return pl.pallas_call(
        functools.partial(_filter_minor_p_body, n2=n2),
        out_shape=SDS((FFT_K1, 2 * n2, 2 * BW), BF16),
        grid=(FFT_K1 // kb,),
        in_specs=[pl.BlockSpec((nq, nblk, kb, FFT_NB, LANE), lambda i: (0, 0, i, 0, 0)),
                  pl.BlockSpec((kb, 2 * n2, 2 * n2), lambda i: (i, 0, 0)),
                  pl.BlockSpec((1, 4 * BW), lambda i: (0, 0))],
        out_specs=pl.BlockSpec((kb, 2 * n2, 2 * BW), lambda i: (i, 0, 0)),
        compiler_params=_cparams(("parallel",), 40),
        name="hy_filter_minor_p",
    )(af_p, g, ss)


def _unpack_slab(ref_slab):
    hi, lo = _unpack2(ref_slab)
    return jnp.concatenate([hi, lo], axis=1)


def _dft_major_p_body(f_ref, x_ref, a_ref):
    for j in range(x_ref.shape[0]):
        x = _unpack_slab(x_ref[j]).astype(BF16)
        rr = jnp.dot(f_ref[...], x, preferred_element_type=F32)
        _store_spectrum_slab(a_ref, j, _pack2(rr[0:FFT_K1], rr[FFT_K1:2 * FFT_K1]))


def _dft_major_p(f1, xp, n2):
    nbk = FFT_NB
    return pl.pallas_call(
        _dft_major_p_body,
        out_shape=SDS((BW // LANE, n2 // nbk, FFT_K1 * nbk, LANE), U32),
        grid=(n2 // nbk,),
        in_specs=[pl.BlockSpec((2 * FFT_K1, FFT_N1 // 2), lambda i: (0, 0)),
                  pl.BlockSpec((nbk, FFT_N1 // 2, BW // 2), lambda i: (i, 0, 0))],
        out_specs=pl.BlockSpec((BW // LANE, None, FFT_K1 * nbk, LANE), lambda i: (0, i, 0, 0)),
        compiler_params=_cparams(("parallel",), 32),
        name="dft_major_p",
    )(f1, xp)


def _conv_minor_p_body(a_ref, h_ref, g_ref, gt_ref, d_ref, *, n2):
    for j in range(a_ref.shape[2]):
        re, im = _unpack2(_load_spectrum_k1(a_ref, j))
        a = jnp.concatenate([re, im], axis=0).astype(BF16)
        x = jnp.dot(g_ref[j], a, preferred_element_type=F32)
        h = h_ref[j].astype(F32)
        xr, xi, hr, hi = x[0:n2], x[n2:2 * n2], h[0:n2], h[n2:2 * n2]
        y = jnp.concatenate([xr * hr - xi * hi, xr * hi + xi * hr], axis=0)
        dv = jnp.dot(gt_ref[j], y.astype(BF16), preferred_element_type=F32)
        _store_spectrum_k1(d_ref, j, _pack2(dv[0:n2], dv[n2:2 * n2]))


def _conv_minor_p(a_p, hspec, g, gt, order, n2):
    kb = FFT_KB
    nq, nblk = a_p.shape[0], a_p.shape[1]
    a_p = a_p.reshape(nq, nblk, FFT_K1, FFT_NB, LANE)
    gspec = pl.BlockSpec((kb, 2 * n2, 2 * n2), lambda i: (i, 0, 0))
    sspec = pl.BlockSpec((nq, nblk, kb, FFT_NB, LANE), lambda i: (0, 0, i, 0, 0))
    return pl.pallas_call(
        functools.partial(_conv_minor_p_body, n2=n2),
        out_shape=SDS((nq, nblk, FFT_K1, FFT_NB, LANE), U32),
        grid=(FFT_K1 // kb,),
        in_specs=[sspec, pl.BlockSpec((kb, 2 * n2, BW), lambda i: (i, 0, order)), gspec, gspec],
        out_specs=sspec,
        compiler_params=_cparams(("parallel",), 32),
        name="hy_conv_minor_p",
    )(a_p, hspec, g, gt)


def _idft_gate_p_body(f_ref, d_ref, gate_ref, z_ref, b_ref, o_ref):
    for j in range(o_ref.shape[0]):
        re, im = _unpack2(_load_spectrum_slab(d_ref, j))
        d = jnp.concatenate([re, im], axis=0).astype(BF16)
        y = jnp.dot(f_ref[...], d, preferred_element_type=F32)
        out = _unpack_slab(gate_ref[j]) * (y + b_ref[...] * _unpack_slab(z_ref[j]))
        o_ref[j] = _pack2(out[:, 0:BW // 2], out[:, BW // 2:BW])


def _idft_gate_p(f1i, d_p, gate_p, z_p, bias, n2):
    nbk = FFT_NB
    nq, nblk = d_p.shape[0], d_p.shape[1]
    d_p = d_p.reshape(nq, nblk, FFT_K1 * FFT_NB, LANE)
    slab = pl.BlockSpec((nbk, FFT_N1 // 2, BW // 2), lambda i: (i, 0, 0))
    return pl.pallas_call(
        _idft_gate_p_body,
        out_shape=SDS((n2, FFT_N1 // 2, BW // 2), U32),
        grid=(n2 // nbk,),
        in_specs=[pl.BlockSpec((FFT_N1 // 2, 2 * FFT_K1), lambda i: (0, 0)),
                  pl.BlockSpec((nq, None, FFT_K1 * FFT_NB, LANE), lambda i: (0, i, 0, 0)),
                  slab, slab, pl.BlockSpec((1, BW), lambda i: (0, 0))],
        out_specs=slab,
        compiler_params=_cparams(("parallel",), 32),
        name="idft_gate_p",
    )(f1i, d_p, gate_p, z_p, bias)


def _hyena_latent(x1p, x2p, vp, feat_perm, tabs, hyw, layer, seq_len):
    n2 = seq_len // (FFT_N1 // 2)
    f1, f1i, g, gt = tabs
    w1p, b1, fr1, w2, b2, fr2, w3p_bf, deltas, hy_bias = hyw
    af_p, ss = _hy_filters_p(feat_perm, w1p, b1, fr1, w2, b2, fr2, w3p_bf, deltas, f1, layer)
    hspec = _filter_minor_p(af_p, g, ss.reshape(1, 4 * BW), n2)
    z = vp
    for o, gate in enumerate((x1p, x2p)):
        a_p = _dft_major_p(f1, z, n2)
        d_p = _conv_minor_p(a_p, hspec, g, gt, o, n2)
        z = _idft_gate_p(f1i, d_p, gate, z, hy_bias[layer, o][None, :], n2)
    return z


def _merge_body(att_ref, yp_ref, yhp_ref, yhc_ref, ys_ref, gate_ref, x_ref, wb_ref, wo_ref,
                g1_ref, lg_ref, lb_ref, sh2_ref, sc2_ref, rw_ref, rb_ref,
                x1_ref, m_ref, rt_ref, cnt_ref, *, nlat, n_ctx, alpha):
    i = pl.program_id(0)
    n2c = yhp_ref.shape[0]
    pieces = []
    for a in range(TM1 // n2c):
        hi, lo = _unpack2(yhp_ref[:, a, :])
        pieces.append(jnp.concatenate([hi, lo], axis=1).astype(BF16))
    yh_lat = jnp.concatenate(pieces, axis=0)
    yh_ctx = jnp.concatenate([yhc_ref[...], jnp.zeros((TM1 - n_ctx, BW), BF16)], axis=0)
    yh = jnp.where(i >= nlat, yh_ctx, yh_lat)
    ys = (att_ref[...], yp_ref[...], yh, ys_ref[...])
    merged = jnp.zeros((TM1, D), F32)
    for n in range(4):
        br = jnp.dot(ys[n], wb_ref[n], preferred_element_type=F32)
        gl = gate_ref[:, n * D:(n + 1) * D].astype(F32)
        merged = merged + br / (1.0 + jnp.exp(-gl))
    y = jnp.dot(merged.astype(BF16), wo_ref[...], preferred_element_type=F32)
    x1 =_layer_norm(alpha * x_ref[...] + g1_ref[...] * y) * lg_ref[...] + lb_ref[...]
    x1_ref[...] = x1
    m = _layer_norm(x1) * (1.0 + sc2_ref[...]) + sh2_ref[...]
    m_ref[...] = m.astype(BF16)

    logits = jnp.dot(m, rw_ref[...], precision=HIGHEST, preferred_element_type=F32) + rb_ref[...]
    lt = logits.T
    le = lt[0:N_EXPERTS]
    lgp = lt[N_EXPERTS:N_EXPERTS + SUB]
    big = 1 << 20
    gi = lax.broadcasted_iota(I32, lgp.shape, 0)
    gmax = jnp.max(lgp, axis=0, keepdims=True)
    gsel = jnp.min(jnp.where(lgp == gmax, gi, big), axis=0, keepdims=True)
    gate_g = 1.0 / jnp.sum(jnp.exp(lgp - gmax), axis=0, keepdims=True)
    ei = lax.broadcasted_iota(I32, le.shape, 0)
    lem = jnp.where(lax.shift_right_logical(ei, 3) == gsel, le, -3.0e38)
    v1 = jnp.max(lem, axis=0, keepdims=True)
    i1 = jnp.min(jnp.where(lem == v1, ei, big), axis=0, keepdims=True)
    lem2 = jnp.where(ei == i1, -3.0e38, lem)
    v2 = jnp.max(lem2, axis=0, keepdims=True)
    i2 = jnp.min(jnp.where(lem2 == v2, ei, big), axis=0, keepdims=True)
    e2 = jnp.exp(v2 - v1)
    wa = gate_g / (1.0 + e2)
    wb = gate_g * e2 / (1.0 + e2)
    tok = lax.broadcasted_iota(I32, (1, TM1), 1)
    valid = jnp.logical_or(i < nlat, tok < n_ctx)
    i1 = jnp.where(valid, i1, -1)
    i2 = jnp.where(valid, i2, -1)
    ri = lax.broadcasted_iota(I32, (SUB, TM1), 0)
    rt = jnp.where(ri == 0, i1.astype(F32), jnp.where(ri == 1, i2.astype(F32),
                   jnp.where(ri == 2, wa, jnp.where(ri == 3, wb, 0.0))))
    rt_ref[...] = rt
    ci = lax.broadcasted_iota(I32, (LANE, TM1), 0)
    oh = jnp.logical_or(ci == i1, ci == i2).astype(BF16)
    cnt_ref[0] = lax.dot_general(jnp.ones((SUB, TM1), BF16), oh, (((1,), (1,)), ((), ())),
                                 preferred_element_type=F32)


def _merge(att, ypool, yhp, yhc, ysc, gate, x, wb_bf, wo_bf, g1, ln_g, ln_b, sh2, sc2, rw, rb, layer, n_lat, n_ctx,
           alpha):
    nt = x.shape[0]
    nlat = n_lat // TM1
    ntl = nt // TM1
    n2c = yhp.shape[0]
    sel = lambda i: (jnp.where(i >= nlat, 1, 0), 0, 0)
    row = lambda c: pl.BlockSpec((TM1, c), lambda i: (i, 0))
    lsel = lambda *blk, **kw: pl.BlockSpec((None,) + blk, lambda i: (layer,) + (0,) * len(blk), **kw)
    msel = pl.BlockSpec((None, 1, D), sel)
    return pl.pallas_call(
        functools.partial(_merge_body, nlat=nlat, n_ctx=n_ctx, alpha=alpha),
        out_shape=[SDS((nt, D), F32), SDS((nt, D), BF16), SDS((SUB, nt), F32), SDS((ntl, SUB, LANE), F32)],
        grid=(ntl,),
        in_specs=[row(BW), row(BW),
                  pl.BlockSpec((n2c, TM1 // n2c, BW // 2), lambda i: (0, jnp.minimum(i, nlat - 1), 0)),
                  pl.BlockSpec((n_ctx, BW), lambda i: (0, 0)),
                  row(BW), row(4 * D), row(D),
                  lsel(4, BW, D, pipeline_mode=pl.Buffered(1)), lsel(D, D, pipeline_mode=pl.Buffered(1)),
                  msel, lsel(1, D), lsel(1, D), msel, msel,
                  lsel(D, LANE), lsel(1, LANE)],
        out_specs=[row(D), row(D), pl.BlockSpec((SUB, TM1), lambda i: (0, i)),
                   pl.BlockSpec((1, SUB, LANE), lambda i: (i, 0, 0))],
        compiler_params=_cparams(("parallel",), 56),
        name="merge_router",
    )(att, ypool, yhp, yhc, ysc, gate, x, wb_bf, wo_bf, g1, ln_g, ln_b, sh2, sc2, rw, rb)


def _piece_loops(np_ref, so_ref, do_ref, j, fn):
    for e in range(N_EXPERTS):
        n = np_ref[j * N_EXPERTS + e]
        so = so_ref[j * N_EXPERTS + e]
        do = do_ref[j * N_EXPERTS + e]

        def body(p, carry, so=so, do=do):
            fn(pl.multiple_of(so + p * MOE_PIECE, MOE_PIECE), pl.multiple_of(do + p * MOE_PIECE, MOE_PIECE))
            return carry

        lax.fori_loop(0, n, body, 0)


def _dispatch_body(np_ref, so_ref, do_ref, nv_ref, m_ref, rt_ref, u_ref, hs_hbm, pos_ref, hs_vmem, sem, *,
                   ntiles, nblk):
    j = pl.program_id(0)
    real = j < ntiles
    rt = rt_ref[...]
    e0 = jnp.where(real, rt[0:1].astype(I32), -1)
    e1 = jnp.where(real, rt[1:2].astype(I32), -1)
    ei = lax.broadcasted_iota(I32, (N_EXPERTS, TM1), 0)
    oh0 = (ei == e0).astype(F32)
    oh1 = (ei == e1).astype(F32)
    c0 = jnp.dot(oh0.astype(BF16), u_ref[...], preferred_element_type=F32)
    c1 = jnp.dot(oh1.astype(BF16), u_ref[...], preferred_element_type=F32)
    n0 = jnp.sum(oh0, axis=1, keepdims=True)
    ecol = lax.broadcasted_iota(I32, (N_EXPERTS, 1), 0)
    toff = jnp.zeros((N_EXPERTS, 1), F32)
    for e in range(N_EXPERTS):
        toff = jnp.where(ecol == e, so_ref[j * N_EXPERTS + e].astype(F32), toff)
    pos0 = jnp.sum(oh0 * (toff + c0), axis=0, keepdims=True)
    pos1 = jnp.sum(oh1 * (toff + n0 + c1), axis=0, keepdims=True)
    pos0 = jnp.where(e0 >= 0, pos0, -1.0)
    pos1 = jnp.where(e1 >= 0, pos1, -1.0)
    ri = lax.broadcasted_iota(I32, (SUB, TM1), 0)
    pos_ref[...] = jnp.where(ri == 0, pos0, jnp.where(ri == 1, pos1, 0.0))
    si = lax.broadcasted_iota(I32, (MOE_S, TM1), 0)
    perm = jnp.logical_or(si == pos0.astype(I32), si == pos1.astype(I32)).astype(BF16)
    hs_vmem[...] = jnp.dot(perm, m_ref[...], preferred_element_type=F32)

    def copy(so, do):
        return pltpu.make_async_copy(hs_vmem.at[pl.ds(so, MOE_PIECE)], hs_hbm.at[pl.ds(do, MOE_PIECE)], sem)

    _piece_loops(np_ref, so_ref, do_ref, j, lambda so, do: copy(so, do).start())
    _piece_loops(np_ref, so_ref, do_ref, j, lambda so, do: copy(so, do).wait())

    @pl.when(j == ntiles)
    def _():
        def blk_copy(b):
            return pltpu.make_async_copy(hs_vmem.at[pl.ds(0, MOE_MB)],
                                         hs_hbm.at[pl.ds(pl.multiple_of(b * MOE_MB, MOE_MB), MOE_MB)], sem)

        def start(b, carry):
            blk_copy(b).start()
            return carry

        def wait(b, carry):
            blk_copy(b).wait()
            return carry

        lax.fori_loop(nv_ref[0], nblk, start, 0)
        lax.fori_loop(nv_ref[0], nblk, wait, 0)


def _dispatch(npieces, soff, doff, nvb, m_bf, rt, upper, rmax):
    nt = m_bf.shape[0]
    ntiles = nt // TM1
    last = ntiles - 1
    return pl.pallas_call(
        functools.partial(_dispatch_body, ntiles=ntiles, nblk=rmax // MOE_MB),
        out_shape=[SDS((rmax, D), F32), SDS((SUB, (ntiles + 1) * TM1), F32)],
        grid_spec=pltpu.PrefetchScalarGridSpec(
            num_scalar_prefetch=4,
            grid=(ntiles + 1,),
            in_specs=[pl.BlockSpec((TM1, D), lambda j, *_: (jnp.minimum(j, last), 0)),
                      pl.BlockSpec((SUB, TM1), lambda j, *_: (0, jnp.minimum(j, last))),
                      pl.BlockSpec((TM1, TM1), lambda j, *_: (0, 0))],
            out_specs=[pl.BlockSpec(memory_space=pl.ANY),
                       pl.BlockSpec((SUB, TM1), lambda j, *_: (0, j))],
            scratch_shapes=[pltpu.VMEM((MOE_S, D), F32), pltpu.SemaphoreType.DMA(())]),
        compiler_params=_cparams(("arbitrary",), 40),
        name="moe_dispatch",
    )(npieces, soff, doff, nvb, m_bf, rt, upper)


def _expert_body(be_ref, nv_ref, x_ref, w1_ref, w3_ref, w2_ref, o_ref):
    b = pl.program_id(0)

    @pl.when(b < nv_ref[0])
    def _():
        xb = x_ref[...].astype(BF16)
        h1 = jnp.dot(xb, w1_ref[...].astype(BF16), preferred_element_type=F32)
        h3 = jnp.dot(xb, w3_ref[...].astype(BF16), preferred_element_type=F32)
        hh = (h1 / (1.0 + jnp.exp(-h1))) * h3
        o_ref[...] = jnp.dot(hh.astype(BF16), w2_ref[...].astype(BF16), preferred_element_type=F32)

    @pl.when(b >= nv_ref[0])
    def _():
        o_ref[...] = jnp.zeros_like(o_ref)


def _experts(blk_e, nvb, hs, ex_w1, ex_w3, ex_w2, layer):
    rmax = hs.shape[0]
    nb = rmax // MOE_MB
    bi = lambda b, be, nv: jnp.maximum(jnp.minimum(b, nv[0] - 1), 0)
    return pl.pallas_call(
        _expert_body,
        out_shape=SDS((rmax, D), F32),
        grid_spec=pltpu.PrefetchScalarGridSpec(
            num_scalar_prefetch=2,
            grid=(nb,),
            in_specs=[pl.BlockSpec((MOE_MB, D), lambda b, be, nv: (bi(b, be, nv), 0)),
                      pl.BlockSpec((None, None, D, EXPERT_HIDDEN), lambda b, be, nv: (layer, be[bi(b, be, nv)], 0, 0)),
                      pl.BlockSpec((None, None, D, EXPERT_HIDDEN), lambda b, be, nv: (layer, be[bi(b, be, nv)], 0, 0)),
                      pl.BlockSpec((None, None, EXPERT_HIDDEN, D), lambda b, be, nv: (layer, be[bi(b, be, nv)], 0, 0))],
            out_specs=pl.BlockSpec((MOE_MB, D), lambda b, be, nv: (b, 0))),
        compiler_params=_cparams(("arbitrary",), 48),
        name="moe_experts",
    )(blk_e, nvb, hs, ex_w1, ex_w3, ex_w2)


def _combine_body(np_ref, so_ref, do_ref, ys_hbm, pos_ref, rt_ref, x1_ref, g2_ref, lg_ref, lb_ref,
                  o_ref, ys_vmem, sem, *, alpha):
    j = pl.program_id(0)

    def copy(so, do):
        return pltpu.make_async_copy(ys_hbm.at[pl.ds(do, MOE_PIECE)], ys_vmem.at[pl.ds(so, MOE_PIECE)], sem)

    _piece_loops(np_ref, so_ref, do_ref, j, lambda so, do: copy(so, do).start())
    z = jnp.concatenate([pos_ref[...], rt_ref[...], jnp.zeros((LANE - 2 * SUB, TM1), F32)], axis=0)
    zt = z.T
    p0 = zt[:, 0:1].astype(I32)
    p1 = zt[:, 1:2].astype(I32)
    w0 = zt[:, SUB + 2:SUB + 3]
    w1 = zt[:, SUB + 3:SUB + 4]
    si = lax.broadcasted_iota(I32, (TM1, MOE_S), 1)
    wm = (jnp.where(si == p0, w0, 0.0) + jnp.where(si == p1, w1, 0.0)).astype(BF16)
    _piece_loops(np_ref, so_ref, do_ref, j, lambda so, do: copy(so, do).wait())
    last = j * N_EXPERTS + N_EXPERTS - 1
    total = so_ref[last] + np_ref[last] * MOE_PIECE
    srow = lax.broadcasted_iota(I32, (MOE_S, 1), 0)
    ys = jnp.where(srow < total, ys_vmem[...], 0.0).astype(BF16)
    f = jnp.dot(wm, ys, preferred_element_type=F32)
    o_ref[...] = _layer_norm(alpha * x1_ref[...] + g2_ref[...] * f) * lg_ref[...] + lb_ref[...]


def _combine(npieces, soff, doff, ys, pos, rt, x1, g2, ln_g, ln_b, layer, n_lat, n_out_tiles, n_out_rows, alpha):
    nlat = n_lat // TM1
    sel = lambda j, *_: (jnp.where(j >= nlat, 1, 0), 0, 0)
    lsel = lambda *blk: pl.BlockSpec((None,) + blk, lambda j, *_: (layer,) + (0,) * len(blk))
    return pl.pallas_call(
        functools.partial(_combine_body, alpha=alpha),
        out_shape=SDS((n_out_rows, D), F32),
        grid_spec=pltpu.PrefetchScalarGridSpec(
            num_scalar_prefetch=3,
            grid=(n_out_tiles,),
            in_specs=[pl.BlockSpec(memory_space=pl.ANY),
                      pl.BlockSpec((SUB, TM1), lambda j, *_: (0, j)),
                      pl.BlockSpec((SUB, TM1), lambda j, *_: (0, j)),
                      pl.BlockSpec((TM1, D), lambda j, *_: (j, 0)),
                      pl.BlockSpec((None, 1, D), sel), lsel(1, D), lsel(1, D)],
            out_specs=pl.BlockSpec((TM1, D), lambda j, *_: (j, 0)),
            scratch_shapes=[pltpu.VMEM((MOE_S, D), F32), pltpu.SemaphoreType.DMA(())]),
        compiler_params=_cparams(("arbitrary",), 40),
        name="moe_combine",
    )(npieces, soff, doff, ys, pos, rt, x1, g2, ln_g, ln_b)


def _moe_plan(cnt, ntiles):
    c = cnt[:, 0, :N_EXPERTS].astype(I32)
    pad8 = (c + MOE_PIECE - 1) // MOE_PIECE * MOE_PIECE
    toff = jnp.cumsum(pad8, axis=1) - pad8
    tot = pad8.sum(axis=0)
    totb = (tot + MOE_MB - 1) // MOE_MB * MOE_MB
    ends = jnp.cumsum(totb)
    base = ends - totb
    dest = base[None, :] + jnp.cumsum(pad8, axis=0) - pad8
    npieces = jnp.concatenate([pad8 // MOE_PIECE, ((totb - tot) // MOE_PIECE)[None, :]], axis=0)
    soff = jnp.concatenate([toff, jnp.zeros((1, N_EXPERTS), I32)], axis=0)
    doff = jnp.concatenate([dest, (base + tot)[None, :]], axis=0)
    return npieces.reshape(-1), soff.reshape(-1), doff.reshape(-1), ends


def _rope_tables(n_lat, nt):
    t = jnp.arange(n_lat)
    pos = jnp.stack([(t // GRID_W).astype(F32), (t % GRID_W).astype(F32)], axis=1)
    inv = ROPE_BASE ** (-jnp.arange(ROPE_PAIRS, dtype=F32) / ROPE_PAIRS)
    ang = pos[:, :, None] * inv[None, None, :]
    cos = jnp.repeat(jnp.cos(ang), 2, axis=1).reshape(n_lat, 2, 2, ROPE_PAIRS)
    sin = jnp.sin(ang)
    sin = jnp.stack([-sin, sin], axis=2)
    cos = jnp.tile(cos.reshape(n_lat, HEAD_DIM), (1, LANE // HEAD_DIM))
    sin = jnp.tile(sin.reshape(n_lat, HEAD_DIM), (1, LANE // HEAD_DIM))
    cos = jnp.concatenate([cos, jnp.ones((nt - n_lat, LANE), F32)], axis=0)
    sin = jnp.concatenate([sin, jnp.zeros((nt - n_lat, LANE), F32)], axis=0)
    return cos, sin


def _hy_features(l):
    t01 = jnp.linspace(0.0, 1.0, l, dtype=F32)
    fr = jnp.linspace(1e-4, HY_BANDS - 1, HY_BANDS, dtype=F32)
    ang = (2.0 * math.pi / l) * jnp.arange(l, dtype=F32)[:, None] * fr[None, :]
    feat = jnp.concatenate([t01[:, None], jnp.cos(ang), -jnp.sin(ang)], axis=-1)
    feat = jnp.pad(feat, ((0, 0), (0, LANE - HY_EMB)))
    return feat.reshape(FFT_N1 // 2, l // (FFT_N1 // 2), LANE).transpose(1, 0, 2).reshape(l, LANE)


def kernel(x, c, ctx, c_ctx, ada_w, ada_b, w_in, attn_sink, pool_w, pool_scale, hy_conv_w, hy_conv_b, hy_w1, hy_b1, hy_freq1, hy_w2, hy_b2, hy_freq2, hy_w3, hy_bias, sc_conv_w, w_branch, w_out, ln1_g, ln1_b, ln2_g, ln2_b, rg_w, rg_b, re_w, re_b, ex_w1, ex_w3, ex_w2):
    depth = w_in.shape[0]
    assert x.shape[0] == 1 and ctx.shape[0] == 1 and x.shape[2] == D
    n_lat = x.shape[1]
    n_ctx = ctx.shape[1]
    assert n_ctx == TM and n_lat % TM1 == 0 and n_lat >= 2 * TM1
    nt = n_lat + n_ctx + PADR
    ntiles1 = nt // TM1
    alpha = (2 * depth) ** 0.25

    cos_t, sin_t = _rope_tables(n_lat, nt)
    w_in_bf = w_in.astype(BF16)
    wb_bf = w_branch.astype(BF16)
    wo_bf = w_out.astype(BF16)
    pool_w_bf = pool_w.astype(BF16)
    r3 = lambda a: a.reshape(depth, 1, a.shape[-1])
    tabs_l = _dft_tables(n_lat // (FFT_N1 // 2))
    tabs_c = _dft_tables(n_ctx // (FFT_N1 // 2))
    feat_l, feat_c = _hy_features(n_lat), _hy_features(n_ctx)
    deltas = jnp.abs(jnp.linspace(math.log(HY_TARGET) / HY_SLOW_PCT, math.log(HY_TARGET) / HY_FAST_PCT,
                                  BW, dtype=F32))[None, :]
    w1p = jnp.pad(hy_w1, ((0, 0), (0, LANE - HY_EMB), (0, 0)))
    w3p_bf = hy_w3.reshape(depth, HY_HIDDEN, 2, 2, BW).transpose(0, 1, 3, 2, 4).reshape(depth, HY_HIDDEN, 4 * BW).astype(BF16)
    hyw = (w1p, r3(hy_b1), r3(hy_freq1), hy_w2, r3(hy_b2), r3(hy_freq2), w3p_bf, deltas, hy_bias)
    rw = jnp.concatenate([re_w, rg_w, jnp.zeros((depth, D, LANE - N_EXPERTS - N_GROUPS), F32)], axis=2)
    rb = jnp.concatenate([re_b, rg_b, jnp.full((depth, LANE - N_EXPERTS - N_GROUPS), NEG_INF, F32)], axis=1)
    rb = rb.reshape(depth, 1, LANE)
    upper = (jnp.arange(TM1)[:, None] < jnp.arange(TM1)[None, :]).astype(BF16)
    rmax = -(-(2 * (n_lat + n_ctx) + ntiles1 * N_EXPERTS * (MOE_PIECE - 1) + N_EXPERTS * (MOE_MB - 1)) // MOE_MB) * MOE_MB
    nblk = rmax // MOE_MB

    s8 = jnp.concatenate([c, c_ctx[None, :], jnp.zeros((SUB - 2, D), F32)], axis=0)
    mod = _mod_all(s8, ada_w, ada_b.reshape(depth, 1, 6 * D))

    xs = jnp.concatenate([x[0], ctx[0], jnp.zeros((PADR, D), F32)], axis=0)
    for i in range(depth):
        last = i == depth - 1
        mp = lambda j: mod[i, 0:2, j * D:(j + 1) * D].reshape(2, 1, D)
        sh1, sc1, g1, sh2, sc2, g2 = (mp(j) for j in range(6))

        q, k, v, pool_in, hy_in, sc_in, gate = _inproj(xs, sh1, sc1, cos_t, sin_t, w_in_bf, i, n_lat // TM1)
        att = _attention(attn_sink[i], q, k, v, n_lat)
        ypool, ysc, x1p, x2p, vp, x1c, x2c, vc = _local_ops(
            pool_in, sc_in, hy_in, pool_w_bf, pool_scale.reshape(depth, 1, BW),
            hy_conv_w, hy_conv_b.reshape(depth, 1, 3 * BW), sc_conv_w, i, n_lat, n_ctx)
        yhp = _hyena_latent(x1p, x2p, vp, feat_l, tabs_l, hyw, i, n_lat)
        if last:
            yhc = jnp.zeros((n_ctx, BW), BF16)
        else:
            yhc = _hyena(x1c, x2c, vc, feat_c, tabs_c, hyw, i, n_ctx, 0)
        x1, m_bf, rt, cnt = _merge(att, ypool, yhp, yhc, ysc, gate, xs, wb_bf, wo_bf, g1, r3(ln1_g), r3(ln1_b),
                                   sh2, sc2, rw, rb, i, n_lat, n_ctx, alpha)

        npieces, soff, doff, ends = _moe_plan(cnt, ntiles1)
        nvb = (ends[-1] // MOE_MB).astype(I32).reshape(1)
        blk_start = jnp.arange(nblk, dtype=I32) * MOE_MB
        blk_e = jnp.minimum(jnp.sum((ends[None, :] <= blk_start[:, None]).astype(I32), axis=1), N_EXPERTS - 1)
        hs, pos = _dispatch(npieces, soff, doff, nvb, m_bf, rt, upper, rmax)
        ys = _experts(blk_e, nvb, hs, ex_w1, ex_w3, ex_w2, i)
        if last:
            xs = _combine(npieces, soff, doff, ys, pos, rt, x1, g2, r3(ln2_g), r3(ln2_b), i, n_lat,
                          n_lat // TM1, n_lat, alpha)
        else:
            xs = _combine(npieces, soff, doff, ys, pos, rt, x1, g2, r3(ln2_g), r3(ln2_b), i, n_lat, ntiles1, nt,
                          alpha)
    return xs[None]
```

```python
import functools
import math

import jax
import jax.numpy as jnp
from jax import lax
from jax.experimental import pallas as pl
from jax.experimental.pallas import tpu as pltpu

F32 = jnp.float32
BF16 = jnp.bfloat16
I32 = jnp.int32
U32 = jnp.uint32
SDS = jax.ShapeDtypeStruct
HIGHEST = lax.Precision.HIGHEST

D = 1024
GRID_W = 64
HEADS = 8
KV_HEADS = 2
HEAD_DIM = 64
WINDOW = 128
ATT_BLK = 128
ROPE_PAIRS = 16
ROPE_BASE = 10000.0
BW = 512
POOL_WINDOWS = (2, 4, 8, 16)
POOL_GW = 128
HY_EMB = 33
HY_BANDS = 16
HY_HIDDEN = 64
HY_TARGET = 1e-2
HY_FAST_PCT = 0.3
HY_SLOW_PCT = 1.5
HY_MOD_SHIFT = 0.05
Q_COLS = HEADS * HEAD_DIM
KV_COLS = KV_HEADS * HEAD_DIM
MAIN_COLS = Q_COLS + 2 * KV_COLS + BW + 3 * BW + 3 * BW
N_GROUPS = 4
EPG = 8
N_EXPERTS = 32
EXPERT_HIDDEN = 512
LN_EPS = 1e-6
NEG_INF = -1e30

LANE = 128
SUB = 8
BSUB = 16
TM = 256
TM1 = 512
PADR = 256
HALO = 16
FFT_N1 = 512
FFT_K1 = 264
FFT_KB = 8
FFT_TN = 2048
FFT_NB = 8
MOE_PIECE = 8
MOE_MB = 256
MOE_S = 2 * TM1 + 256


def _cparams(sem, vmem_mb):
    return pltpu.CompilerParams(dimension_semantics=sem, vmem_limit_bytes=vmem_mb * 2 ** 20)


def _layer_norm(x):
    mu = jnp.mean(x, axis=-1, keepdims=True)
    xc = x - mu
    var = jnp.mean(xc * xc, axis=-1, keepdims=True)
    return xc * lax.rsqrt(var + LN_EPS)


def _mod_body(s_ref, w_ref, b_ref, o_ref):
    s = s_ref[...]
    s = s * (1.0 / (1.0 + jnp.exp(-s)))
    o_ref[...] = jnp.dot(s.astype(BF16), w_ref[...].astype(BF16), preferred_element_type=F32) + b_ref[...]


def _mod_all(s8, ada_w, ada_b3):
    depth = ada_w.shape[0]
    tn = 1024
    return pl.pallas_call(
        _mod_body,
        out_shape=SDS((depth, SUB, 6 * D), F32),
        grid=(depth, 6 * D // tn),
        in_specs=[pl.BlockSpec((SUB, D), lambda l, j: (0, 0)),
                  pl.BlockSpec((None, D, tn), lambda l, j: (l, 0, j)),
                  pl.BlockSpec((None, 1, tn), lambda l, j: (l, 0, j))],
        out_specs=pl.BlockSpec((None, SUB, tn), lambda l, j: (l, 0, j)),
        compiler_params=_cparams(("parallel", "parallel"), 32),
        name="ada_mod",
    )(s8, ada_w, ada_b3)


def _inproj_body(x_ref, sh_ref, sc_ref, cos_ref, sin_ref, w_ref,
                 q_ref, k_ref, v_ref, pool_ref, hy_ref, scv_ref, hmod_ref):
    h = _layer_norm(x_ref[...]) * (1.0 + sc_ref[...]) + sh_ref[...]
    hb = h.astype(BF16)

    def mm(a, b):
        return jnp.dot(hb, w_ref[:, a:b], preferred_element_type=F32)

    cs = cos_ref[...]
    sn = sin_ref[...]
    lane = lax.broadcasted_iota(I32, cs.shape, 1)
    first = (lane & ROPE_PAIRS) == 0

    def rope(t):
        sw = jnp.where(first, pltpu.roll(t, LANE - ROPE_PAIRS, 1), pltpu.roll(t, ROPE_PAIRS, 1))
        return t * cs + sw * sn

    for j in range(Q_COLS // LANE):
        q_ref[:, j * LANE:(j + 1) * LANE] = (rope(mm(j * LANE, (j + 1) * LANE)) * (HEAD_DIM ** -0.5)).astype(BF16)
    o = Q_COLS
    k_ref[...] = rope(mm(o, o + KV_COLS)).astype(BF16)
    o += KV_COLS
    v_ref[...] = mm(o, o + KV_COLS).astype(BF16)
    o += KV_COLS
    pool_ref[...] = mm(o, o + BW).astype(BF16)
    o += BW
    for j in range(3):
        hy_ref[:, j * BW:(j + 1) * BW] = mm(o + j * BW, o + (j + 1) * BW).astype(BF16)
    o += 3 * BW
    for j in range(3):
        scv_ref[:, j * BW:(j + 1) * BW] = mm(o + j * BW, o + (j + 1) * BW).astype(BF16)
    hmod_ref[...] = hb


def _inproj(x, sh, sc, cos_t, sin_t, w_in_bf, layer, n_lat_tiles):
    nt = x.shape[0]
    sel = lambda i: (jnp.where(i >= n_lat_tiles, 1, 0), 0, 0)
    row = lambda c: pl.BlockSpec((TM1, c), lambda i: (i, 0))
    outs = [SDS((nt, Q_COLS), BF16), SDS((nt, KV_COLS), BF16), SDS((nt, KV_COLS), BF16), SDS((nt, BW), BF16),
            SDS((nt, 3 * BW), BF16), SDS((nt, 3 * BW), BF16), SDS((nt, D), BF16)]
    return pl.pallas_call(
        _inproj_body,
        out_shape=outs,
        grid=(nt // TM1,),
        in_specs=[row(D),
                  pl.BlockSpec((None, 1, D), sel), pl.BlockSpec((None, 1, D), sel),
                  row(LANE), row(LANE),
                  pl.BlockSpec((None, D, MAIN_COLS), lambda i: (layer, 0, 0), pipeline_mode=pl.Buffered(1))],
        out_specs=[row(Q_COLS), row(KV_COLS), row(KV_COLS), row(BW), row(3 * BW), row(3 * BW), row(D)],
        compiler_params=_cparams(("parallel",), 58),
        name="inproj",
    )(x, sh, sc, cos_t, sin_t, w_in_bf)


def _attn_body(sink_ref, q_ref, kp_ref, kc_ref, kn_ref, vp_ref, vc_ref, vn_ref, kx_ref, vx_ref, o_ref, *, nlb):
    i = pl.program_id(0)
    is_lat = i < nlb
    lat_i = jnp.where(is_lat, 1, 0)
    prev_i = jnp.where(jnp.logical_and(is_lat, i > 0), 1, 0)
    next_i = jnp.where(jnp.logical_and(is_lat, i < nlb - 1), 1, 0)
    ncx = kx_ref.shape[0]
    nk = ncx + 3 * ATT_BLK

    r = lax.broadcasted_iota(I32, (ATT_BLK, nk), 0)
    c = lax.broadcasted_iota(I32, (ATT_BLK, nk), 1) - ncx
    band = jnp.logical_and(c >= r, c <= r + 2 * WINDOW)
    grp = jnp.where(c < ATT_BLK, prev_i, jnp.where(c < 2 * ATT_BLK, lat_i, next_i))
    mask = jnp.logical_or(c < 0, jnp.logical_and(band, grp > 0))

    kcat = jnp.concatenate([kx_ref[...], kp_ref[...], kc_ref[...], kn_ref[...]], axis=0).astype(F32)
    vcat = jnp.concatenate([vx_ref[...], vp_ref[...], vc_ref[...], vn_ref[...]], axis=0).astype(F32)
    lo = lax.broadcasted_iota(I32, kcat.shape, 1) < HEAD_DIM
    kroll = pltpu.roll(kcat, HEAD_DIM, 1)
    vroll = pltpu.roll(vcat, HEAD_DIM, 1)
    zero = jnp.zeros_like(kcat)
    kvar = [[jnp.where(lo, kcat, zero), jnp.where(lo, zero, kroll)],
            [jnp.where(lo, kroll, zero), jnp.where(lo, zero, kcat)]]
    vvar = [[jnp.where(lo, vcat, zero), jnp.where(lo, zero, vroll)],
            [jnp.where(lo, vroll, zero), jnp.where(lo, zero, vcat)]]
    kvar = [[a.astype(BF16) for a in row] for row in kvar]
    vvar = [[a.astype(BF16) for a in row] for row in vvar]

    for p in range(HEADS // 2):
        qp = q_ref[:, p * LANE:(p + 1) * LANE]
        kh = (2 * p) // (HEADS // KV_HEADS)
        acc = jnp.zeros((ATT_BLK, LANE), F32)
        for rr in range(2):
            sink = sink_ref[2 * p + rr]
            s = lax.dot_general(qp, kvar[kh][rr], (((1,), (1,)), ((), ())), preferred_element_type=F32)
            s = jnp.where(mask, s, NEG_INF)
            m = jnp.maximum(jnp.max(s, axis=1, keepdims=True), sink)
            e = jnp.exp(s - m)
            den = jnp.sum(e, axis=1, keepdims=True) + jnp.exp(sink - m)
            o = jnp.dot(e.astype(BF16), vvar[kh][rr], preferred_element_type=F32)
            acc = acc + o * (1.0 / den)
        o_ref[:, p * LANE:(p + 1) * LANE] = acc.astype(BF16)


def _attention(sink, q, k, v, n_lat):
    nt = q.shape[0]
    nlb = n_lat // ATT_BLK
    cxb = n_lat // TM
    cl = lambda i: jnp.clip(i, 0, nlb - 1)
    kspec = lambda f: pl.BlockSpec((ATT_BLK, KV_COLS), lambda i: (f(i), 0))
    prev = lambda i: jnp.clip(i - 1, 0, nlb - 1)
    nxt = lambda i: jnp.clip(i + 1, 0, nlb - 1)
    cx = pl.BlockSpec((TM, KV_COLS), lambda i: (cxb, 0))
    return pl.pallas_call(
        functools.partial(_attn_body, nlb=nlb),
        out_shape=SDS((nt, Q_COLS), BF16),
        grid=(nt // ATT_BLK,),
        in_specs=[pl.BlockSpec(memory_space=pltpu.SMEM),
                  pl.BlockSpec((ATT_BLK, Q_COLS), lambda i: (i, 0)),
                  kspec(prev), kspec(cl), kspec(nxt), kspec(prev), kspec(cl), kspec(nxt), cx, cx],
        out_specs=pl.BlockSpec((ATT_BLK, Q_COLS), lambda i: (i, 0)),
        compiler_params=_cparams(("parallel",), 32),
        name="attention",
    )(sink, q, k, k, k, v, v, v, k, v)


def _pack2(a, b):
    hi = lax.bitcast_convert_type(a.astype(BF16).astype(F32), U32)
    lo = lax.bitcast_convert_type(b.astype(BF16).astype(F32), U32)
    return hi | (lo >> 16)


def _unpack2(w):
    return (lax.bitcast_convert_type(w & jnp.uint32(0xFFFF0000), F32),
            lax.bitcast_convert_type(w << 16, F32))


def _local_body(pm_ref, pa_ref, pb_ref, sm_ref, sa_ref, sb_ref, hm_ref, ha_ref, hb_ref,
                pw_ref, ps_ref, hw_ref, hbias_ref, sw_ref,
                ypool_ref, ysc_ref, x1p_ref, x2p_ref, vp_ref, x1c_ref, x2c_ref, vc_ref,
                pext, sext, hext, *, nlat, n_lat, n_ctx):
    i = pl.program_id(0)
    is_lat = i < nlat
    pf = jnp.where(jnp.logical_and(is_lat, i > 0), 1.0, 0.0)
    nf = jnp.where(jnp.logical_and(is_lat, i < nlat - 1), 1.0, 0.0)
    r = lax.broadcasted_iota(I32, (TM1, 1), 0)
    keep = jnp.where(jnp.logical_or(is_lat, r < n_ctx), 1.0, 0.0)

    def fill(ext, a_ref, m_ref, b_ref):
        ext[0:HALO, :] = a_ref[...].astype(F32) * pf
        ext[HALO:HALO + TM1, :] = m_ref[...].astype(F32) * keep
        ext[HALO + TM1:HALO + TM1 + HALO, :] = b_ref[...].astype(F32) * nf

    def sh(ext, d, c0=None, c1=None):
        if c0 is None:
            return ext[pl.ds(HALO + d, TM1), :]
        return ext[pl.ds(HALO + d, TM1), c0:c1]

    fill(pext, pa_ref, pm_ref, pb_ref)
    t = jnp.where(is_lat, i * TM1, 0) + r
    ln = jnp.where(is_lat, n_lat, n_ctx)
    for g, w in enumerate(POOL_WINDOWS):
        left = w // 2
        right = w - left - 1
        c0, c1 = g * POOL_GW, (g + 1) * POOL_GW
        acc = sh(pext, -left, c0, c1)
        for dd in range(-left + 1, right + 1):
            acc = acc + sh(pext, dd, c0, c1)
        cnt = jnp.maximum(jnp.minimum(t + right + 1, ln) - jnp.maximum(t - left, 0), 1).astype(F32)
        dpool = acc / cnt - sh(pext, 0, c0, c1)
        y = jnp.dot(dpool.astype(BF16), pw_ref[g], preferred_element_type=F32)
        ypool_ref[:, c0:c1] = (y * ps_ref[:, c0:c1]).astype(BF16)

    sext[0:HALO, :] = (sa_ref[:, BW:2 * BW].astype(F32) * sa_ref[:, 2 * BW:3 * BW].astype(F32)) * pf
    sext[HALO:HALO + TM1, :] = sm_ref[:, BW:2 * BW].astype(F32) * sm_ref[:, 2 * BW:3 * BW].astype(F32) * keep
    sext[HALO + TM1:HALO + TM1 + HALO, :] = (sb_ref[:, BW:2 * BW].astype(F32) * sb_ref[:, 2 * BW:3 * BW].astype(F32)) * nf
    conv = sh(sext, -1) * sw_ref[0:1, :] + sh(sext, 0) * sw_ref[1:2, :] + sh(sext, 1) * sw_ref[2:3, :]
    ysc_ref[...] = (sm_ref[:, 0:BW].astype(F32) * conv).astype(BF16)

    fill(hext, ha_ref, hm_ref, hb_ref)
    nat = (x1c_ref, x2c_ref, vc_ref)
    pk = (x1p_ref, x2p_ref, vp_ref)
    n2c = x1p_ref.shape[0]
    for j in range(3):
        c0, c1 = j * BW, (j + 1) * BW
        uc = (sh(hext, -1, c0, c1) * hw_ref[0:1, c0:c1] + sh(hext, 0, c0, c1) * hw_ref[1:2, c0:c1]
              + sh(hext, 1, c0, c1) * hw_ref[2:3, c0:c1] + hbias_ref[:, c0:c1])
        nat[j][...] = uc.astype(BF16)
        word = _pack2(uc[:, 0:BW // 2], uc[:, BW // 2:BW])
        for a in range(TM1 // n2c):
            pk[j][:, a, :] = word[a * n2c:(a + 1) * n2c]


def _local_ops(pool_in, sc_in, hy_in, pool_w_bf, pool_scale, hy_conv_w, hy_conv_b, sc_conv_w, layer, n_lat, n_ctx):
    nt = pool_in.shape[0]
    nlat = n_lat // TM1
    n2c = n_lat // (FFT_N1 // 2)
    per_tile = TM1 // n2c
    nhb = nt // HALO
    per = TM1 // HALO
    main = lambda c: pl.BlockSpec((TM1, c), lambda i: (i, 0))
    before = lambda c: pl.BlockSpec((HALO, c), lambda i: (jnp.maximum(i * per - 1, 0), 0))
    after = lambda c: pl.BlockSpec((HALO, c), lambda i: (jnp.minimum((i + 1) * per, nhb - 1), 0))
    lsel = lambda *blk: pl.BlockSpec((None,) + blk, lambda i: (layer,) + (0,) * len(blk))
    out = SDS((nt, BW), BF16)
    packed = SDS((n2c, (nlat + 1) * per_tile, BW // 2), U32)
    ctx_out = SDS((TM1, BW), BF16)
    pspec = pl.BlockSpec((n2c, per_tile, BW // 2), lambda i: (0, i, 0))
    cspec = pl.BlockSpec((TM1, BW), lambda i: (0, 0))
    return pl.pallas_call(
        functools.partial(_local_body, nlat=nlat, n_lat=n_lat, n_ctx=n_ctx),
        out_shape=[out, out, packed, packed, packed, ctx_out, ctx_out, ctx_out],
        grid=(nlat + 1,),
        in_specs=[main(BW), before(BW), after(BW),
                  main(3 * BW), before(3 * BW), after(3 * BW),
                  main(3 * BW), before(3 * BW), after(3 * BW),
                  lsel(4, POOL_GW, POOL_GW), lsel(1, BW), lsel(3, 3 * BW), lsel(1, 3 * BW), lsel(3, BW)],
        out_specs=[main(BW), main(BW), pspec, pspec, pspec, cspec, cspec, cspec],
        scratch_shapes=[pltpu.VMEM((TM1 + 2 * HALO, BW), F32), pltpu.VMEM((TM1 + 2 * HALO, BW), F32),
                        pltpu.VMEM((TM1 + 2 * HALO, 3 * BW), F32)],
        compiler_params=_cparams(("arbitrary",), 48),
        name="local_ops",
    )(pool_in, pool_in, pool_in, sc_in, sc_in, sc_in, hy_in, hy_in, hy_in,
      pool_w_bf, pool_scale, hy_conv_w, hy_conv_b, sc_conv_w)


def _dft_tables(n2):
    n = FFT_N1 * n2
    k1 = jnp.arange(FFT_K1, dtype=I32)
    n1 = jnp.arange(FFT_N1 // 2, dtype=I32)
    ok = (k1 <= FFT_N1 // 2)
    ang = (2.0 * math.pi / FFT_N1) * ((k1[:, None] * n1[None, :]) % FFT_N1).astype(F32)
    c1 = jnp.where(ok[:, None], jnp.cos(ang), 0.0)
    s1 = jnp.where(ok[:, None], jnp.sin(ang), 0.0)
    f1 = jnp.concatenate([c1, -s1], axis=0).astype(BF16)
    wt = jnp.where((k1 == 0) | (k1 == FFT_N1 // 2), 1.0, 2.0) / n
    f1i = jnp.concatenate([(c1 * wt[:, None]).T, (-s1 * wt[:, None]).T], axis=1).astype(BF16)
    if n2 == 1:
        return f1, f1i, None, None
    k2 = jnp.arange(n2, dtype=I32)
    m2 = jnp.arange(n2, dtype=I32)
    kk = k1[:, None, None] + FFT_N1 * k2[None, :, None]
    th = (2.0 * math.pi / n) * ((kk * m2[None, None, :]) % n).astype(F32)
    okb = ok[:, None, None]
    cc = jnp.where(okb, jnp.cos(th), 0.0)
    ss = jnp.where(okb, jnp.sin(th), 0.0)
    g = jnp.concatenate([jnp.concatenate([cc, ss], axis=2), jnp.concatenate([-ss, cc], axis=2)], axis=1)
    return f1, f1i, g.astype(BF16), jnp.swapaxes(g, 1, 2).astype(BF16)


def _filter_body(feat_ref, w1_ref, b1_ref, f1_ref, w2_ref, b2_ref, f2_ref, w3_ref, dl_ref, dft_ref,
                 af_ref, ss_ref, filt):
    i = pl.program_id(0)
    feat = feat_ref[...]
    h = jnp.sin(f1_ref[...] * (jnp.dot(feat, w1_ref[...], precision=HIGHEST, preferred_element_type=F32) + b1_ref[...]))
    h = jnp.sin(f2_ref[...] * (jnp.dot(h, w2_ref[...], precision=HIGHEST, preferred_element_type=F32) + b2_ref[...]))
    raw = jnp.dot(h.astype(BF16), w3_ref[...], preferred_element_type=F32)
    win = jnp.exp(-feat[:, 0:1] * dl_ref[...]) + HY_MOD_SHIFT
    row = lax.broadcasted_iota(I32, (TM, 1), 0) + i * TM

    @pl.when(i == 0)
    def _():
        ss_ref[...] = jnp.zeros_like(ss_ref)

    for j in range(4):
        f = raw[:, j * BW:(j + 1) * BW] * win
        if j >= 2:
            f = jnp.where(row == 0, 0.0, f)
        filt[:, j * BW:(j + 1) * BW] = f.astype(BF16)
        ss_ref[:, j * BW:(j + 1) * BW] += jnp.sum(f * f, axis=0, keepdims=True)
    af_ref[...] = jnp.dot(dft_ref[...], filt[...], preferred_element_type=F32).astype(BF16)


def _hy_filters(feat_perm, w1p, b1, fr1, w2, b2, fr2, w3p_bf, deltas, f1, layer):
    l = feat_perm.shape[0]
    n2 = l // TM
    lsel = lambda *blk: pl.BlockSpec((None,) + blk, lambda i: (layer,) + (0,) * len(blk))
    return pl.pallas_call(
        _filter_body,
        out_shape=[SDS((2 * FFT_K1, n2 * 4 * BW), BF16), SDS((1, 4 * BW), F32)],
        grid=(n2,),
        in_specs=[pl.BlockSpec((TM, LANE), lambda i: (i, 0)),
                  lsel(LANE, HY_HIDDEN), lsel(1, HY_HIDDEN), lsel(1, HY_HIDDEN),
                  lsel(HY_HIDDEN, HY_HIDDEN), lsel(1, HY_HIDDEN), lsel(1, HY_HIDDEN),
                  lsel(HY_HIDDEN, 4 * BW), pl.BlockSpec((1, BW), lambda i: (0, 0)),
                  pl.BlockSpec((2 * FFT_K1, FFT_N1 // 2), lambda i: (0, 0))],
        out_specs=[pl.BlockSpec((2 * FFT_K1, 4 * BW), lambda i: (0, i)), pl.BlockSpec((1, 4 * BW), lambda i: (0, 0))],
        scratch_shapes=[pltpu.VMEM((TM, 4 * BW), BF16)],
        compiler_params=_cparams(("arbitrary",), 32),
        name="hy_filters",
    )(feat_perm, w1p, b1, fr1, w2, b2, fr2, w3p_bf, deltas, f1)


def _dft_major_body(f_ref, x_ref, o_ref):
    o_ref[...] = jnp.dot(f_ref[...], x_ref[...], preferred_element_type=F32).astype(BF16)


def _dft_major(f1, x2d, row_blk, ncols):
    tn = min(FFT_TN, ncols)
    return pl.pallas_call(
        _dft_major_body,
        out_shape=SDS((2 * FFT_K1, ncols), BF16),
        grid=(ncols // tn,),
        in_specs=[pl.BlockSpec((2 * FFT_K1, FFT_N1 // 2), lambda j: (0, 0)),
                  pl.BlockSpec((FFT_N1 // 2, tn), lambda j: (row_blk, j))],
        out_specs=pl.BlockSpec((2 * FFT_K1, tn), lambda j: (0, j)),
        compiler_params=_cparams(("parallel",), 32),
        name="dft_major",
    )(f1, x2d)


def _filter_minor_body(a_ref, g_ref, ss_ref, h_ref, *, n2):
    ssv = ss_ref[...]
    scale = lax.rsqrt(ssv[:, 0:2 * BW] + ssv[:, 2 * BW:4 * BW] + LN_EPS)
    for j in range(a_ref.shape[1]):
        if n2 == 1:
            s = jnp.concatenate([a_ref[0, j], a_ref[1, j]], axis=0).astype(F32)
        else:
            a = jnp.concatenate([a_ref[0, j], a_ref[1, j]], axis=0)
            s = jnp.dot(g_ref[j], a, preferred_element_type=F32)
        sf, sb = s[:, 0:2 * BW], s[:, 2 * BW:4 * BW]
        hr = (sf[0:n2] + sb[0:n2]) * scale
        hi = (sf[n2:2 * n2] - sb[n2:2 * n2]) * scale
        h_ref[j] = jnp.concatenate([hr, hi], axis=0).astype(BF16)


def _conv_minor_body(a_ref, h_ref, g_ref, gt_ref, d_ref, *, n2):
    for j in range(a_ref.shape[1]):
        a = jnp.concatenate([a_ref[0, j], a_ref[1, j]], axis=0)
        if n2 == 1:
            x = a.astype(F32)
        else:
            x = jnp.dot(g_ref[j], a, preferred_element_type=F32)
        h = h_ref[j].astype(F32)
        xr, xi, hr, hi = x[0:n2], x[n2:2 * n2], h[0:n2], h[n2:2 * n2]
        y = jnp.concatenate([xr * hr - xi * hi, xr * hi + xi * hr], axis=0)
        if n2 == 1:
            dv = y
        else:
            dv = jnp.dot(gt_ref[j], y.astype(BF16), preferred_element_type=F32)
        d_ref[0, j] = dv[0:n2].astype(BF16)
        d_ref[1, j] = dv[n2:2 * n2].astype(BF16)


def _filter_minor(af, g, ss, n2):
    a4 = af.reshape(2, FFT_K1, n2, 4 * BW)
    kb = FFT_KB
    gspec = (pl.BlockSpec((kb, 2 * n2, 2 * n2), lambda i: (i, 0, 0)) if n2 > 1
             else pl.BlockSpec((SUB, LANE), lambda i: (0, 0)))
    return pl.pallas_call(
        functools.partial(_filter_minor_body, n2=n2),
        out_shape=SDS((FFT_K1, 2 * n2, 2 * BW), BF16),
        grid=(FFT_K1 // kb,),
        in_specs=[pl.BlockSpec((2, kb, n2, 4 * BW), lambda i: (0, i, 0, 0)), gspec,
                  pl.BlockSpec((1, 4 * BW), lambda i: (0, 0))],
        out_specs=pl.BlockSpec((kb, 2 * n2, 2 * BW), lambda i: (i, 0, 0)),
        compiler_params=_cparams(("parallel",), 40),
        name="hy_filter_minor",
    )(a4, g if n2 > 1 else jnp.zeros((SUB, LANE), BF16), ss)


def _conv_minor(a, hspec, g, gt, order, n2):
    a4 = a.reshape(2, FFT_K1, n2, BW)
    kb = FFT_KB
    dummy = jnp.zeros((SUB, LANE), BF16)
    gspec = (pl.BlockSpec((kb, 2 * n2, 2 * n2), lambda i: (i, 0, 0)) if n2 > 1
             else pl.BlockSpec((SUB, LANE), lambda i: (0, 0)))
    d4 = pl.pallas_call(
        functools.partial(_conv_minor_body, n2=n2),
        out_shape=SDS((2, FFT_K1, n2, BW), BF16),
        grid=(FFT_K1 // kb,),
        in_specs=[pl.BlockSpec((2, kb, n2, BW), lambda i: (0, i, 0, 0)),
                  pl.BlockSpec((kb, 2 * n2, BW), lambda i: (i, 0, order)), gspec, gspec],
        out_specs=pl.BlockSpec((2, kb, n2, BW), lambda i: (0, i, 0, 0)),
        compiler_params=_cparams(("parallel",), 32),
        name="hy_conv_minor",
    )(a4, hspec, g if n2 > 1 else dummy, gt if n2 > 1 else dummy)
    return d4.reshape(2 * FFT_K1, n2 * BW)


def _idft_gate_body(f_ref, d_ref, gate_ref, z_ref, b_ref, o_ref):
    y = jnp.dot(f_ref[...], d_ref[...], preferred_element_type=F32)
    z = z_ref[...].astype(F32)
    o_ref[...] = (gate_ref[...].astype(F32) * (y + b_ref[...] * z)).astype(BF16)


def _idft_gate(f1i, d, gate2d, gate_blk, z2d, z_blk, bias_t, ncols):
    tn = min(FFT_TN, ncols)
    return pl.pallas_call(
        _idft_gate_body,
        out_shape=SDS((FFT_N1 // 2, ncols), BF16),
        grid=(ncols // tn,),
        in_specs=[pl.BlockSpec((FFT_N1 // 2, 2 * FFT_K1), lambda j: (0, 0)),
                  pl.BlockSpec((2 * FFT_K1, tn), lambda j: (0, j)),
                  pl.BlockSpec((FFT_N1 // 2, tn), lambda j: (gate_blk, j)),
                  pl.BlockSpec((FFT_N1 // 2, tn), lambda j: (z_blk, j)),
                  pl.BlockSpec((1, tn), lambda j: (0, 0))],
        out_specs=pl.BlockSpec((FFT_N1 // 2, tn), lambda j: (0, j)),
        compiler_params=_cparams(("parallel",), 32),
        name="idft_gate",
    )(f1i, d, gate2d, z2d, bias_t)


def _hyena(x1, x2, v, feat, tabs, hyw, layer, seq_len, row0):
    n2 = seq_len // (FFT_N1 // 2)
    f1, f1i, g, gt = tabs
    w1p, b1, fr1, w2, b2, fr2, w3p_bf, deltas, hy_bias = hyw
    nt = x1.shape[0]
    af, ss = _hy_filters(feat, w1p, b1, fr1, w2, b2, fr2, w3p_bf, deltas, f1, layer)
    hspec = _filter_minor(af, g, ss, n2)
    view = lambda a: a.reshape(nt // n2, n2 * BW)
    blk = row0 // seq_len if n2 == 1 else 0
    ncols = n2 * BW
    tn = min(FFT_TN, ncols)
    z, zv, zblk = v, view(v), blk
    gates = (x1, x2)
    for o in range(2):
        a = _dft_major(f1, zv, zblk, ncols)
        d = _conv_minor(a, hspec, g, gt, o, n2)
        bias_t = jnp.tile(hy_bias[layer, o][None, :], (1, tn // BW))
        z2 = _idft_gate(f1i, d, view(gates[o]), blk, zv, zblk, bias_t, ncols)
        zv, zblk = z2, 0
    return zv.reshape(seq_len, BW)


def _store_spectrum_slab(ref, j, word):
    for q in range(word.shape[1] // LANE):
        ref[q, pl.ds(j, FFT_K1, stride=FFT_NB), :] = word[:, q * LANE:(q + 1) * LANE]


def _load_spectrum_slab(ref, j):
    return jnp.concatenate([ref[q, pl.ds(j, FFT_K1, stride=FFT_NB), :] for q in range(ref.shape[0])], axis=1)


def _load_spectrum_k1(ref, j):
    nq, nblk = ref.shape[0], ref.shape[1]
    return jnp.concatenate([jnp.concatenate([ref[q, b, j] for q in range(nq)], axis=1) for b in range(nblk)], axis=0)


def _store_spectrum_k1(ref, j, word):
    nq, nblk = ref.shape[0], ref.shape[1]
    for b in range(nblk):
        for q in range(nq):
            ref[q, b, j] = word[b * FFT_NB:(b + 1) * FFT_NB, q * LANE:(q + 1) * LANE]


def _filter_p_body(feat_ref, w1_ref, b1_ref, f1_ref, w2_ref, b2_ref, f2_ref, w3_ref, dl_ref, dft_ref,
                   af_ref, ss_ref, h2):
    nb = pl.program_id(0)
    cb = pl.program_id(1)
    nbk = FFT_NB

    @pl.when(jnp.logical_and(nb == 0, cb == 0))
    def _():
        ss_ref[...] = jnp.zeros_like(ss_ref)

    @pl.when(cb == 0)
    def _():
        feat = feat_ref[...]
        h = jnp.sin(f1_ref[...] * (jnp.dot(feat, w1_ref[...], precision=HIGHEST, preferred_element_type=F32) + b1_ref[...]))
        h = jnp.sin(f2_ref[...] * (jnp.dot(h, w2_ref[...], precision=HIGHEST, preferred_element_type=F32) + b2_ref[...]))
        h2[...] = h.astype(BF16)

    row = lax.broadcasted_iota(I32, (TM, 1), 0)
    lag0_bwd = jnp.logical_and(jnp.logical_and(nb == 0, cb >= 2), row == 0)
    ssum = jnp.zeros((1, BW), F32)
    for j in range(nbk):
        raw = jnp.dot(h2[j * TM:(j + 1) * TM, :], w3_ref[...], preferred_element_type=F32)
        win = jnp.exp(-feat_ref[j * TM:(j + 1) * TM, 0:1] * dl_ref[...]) + HY_MOD_SHIFT
        f = raw * win
        if j == 0:
            f = jnp.where(lag0_bwd, 0.0, f)
        ssum = ssum + jnp.sum(f * f, axis=0, keepdims=True)
        rr = jnp.dot(dft_ref[...], f.astype(BF16), preferred_element_type=F32)
        _store_spectrum_slab(af_ref, j, _pack2(rr[0:FFT_K1], rr[FFT_K1:2 * FFT_K1]))
    ss_ref[cb] = ss_ref[cb] + ssum


def _hy_filters_p(feat_perm, w1p, b1, fr1, w2, b2, fr2, w3p_bf, deltas, f1, layer):
    l = feat_perm.shape[0]
    n2 = l // TM
    nbk = FFT_NB
    lsel = lambda *blk: pl.BlockSpec((None,) + blk, lambda nb, cb: (layer,) + (0,) * len(blk))
    return pl.pallas_call(
        _filter_p_body,
        out_shape=[SDS((4 * BW // LANE, n2 // nbk, FFT_K1 * nbk, LANE), U32), SDS((4, 1, BW), F32)],
        grid=(n2 // nbk, 4),
        in_specs=[pl.BlockSpec((nbk * TM, LANE), lambda nb, cb: (nb, 0)),
                  lsel(LANE, HY_HIDDEN), lsel(1, HY_HIDDEN), lsel(1, HY_HIDDEN),
                  lsel(HY_HIDDEN, HY_HIDDEN), lsel(1, HY_HIDDEN), lsel(1, HY_HIDDEN),
                  pl.BlockSpec((None, HY_HIDDEN, BW), lambda nb, cb: (layer, 0, cb)),
                  pl.BlockSpec((1, BW), lambda nb, cb: (0, 0)),
                  pl.BlockSpec((2 * FFT_K1, FFT_N1 // 2), lambda nb, cb: (0, 0))],
        out_specs=[pl.BlockSpec((BW // LANE, None, FFT_K1 * nbk, LANE), lambda nb, cb: (cb, nb, 0, 0)),
                   pl.BlockSpec((4, 1, BW), lambda nb, cb: (0, 0, 0))],
        scratch_shapes=[pltpu.VMEM((nbk * TM, HY_HIDDEN), BF16)],
        compiler_params=_cparams(("arbitrary", "arbitrary"), 40),
        name="hy_filters_p",
    )(feat_perm, w1p, b1, fr1, w2, b2, fr2, w3p_bf, deltas, f1)


def _filter_minor_p_body(a_ref, g_ref, ss_ref, h_ref, *, n2):
    ssv = ss_ref[...]
    scale = lax.rsqrt(ssv[:, 0:2 * BW] + ssv[:, 2 * BW:4 * BW] + LN_EPS)
    for j in range(a_ref.shape[2]):
        re, im = _unpack2(_load_spectrum_k1(a_ref, j))
        a = jnp.concatenate([re, im], axis=0).astype(BF16)
        s = jnp.dot(g_ref[j], a, preferred_element_type=F32)
        sf, sb = s[:, 0:2 * BW], s[:, 2 * BW:4 * BW]
        hr = (sf[0:n2] + sb[0:n2]) * scale
        hi = (sf[n2:2 * n2] - sb[n2:2 * n2]) * scale
        h_ref[j] = jnp.concatenate([hr, hi], axis=0).astype(BF16)


def _filter_minor_p(af_p, g, ss, n2):
    kb = FFT_KB
    nq, nblk = af_p.shape[0], af_p.shape[1]
    af_p = af_p.reshape(nq, nblk, FFT_K1, FFT_NB, LANE)
    return pl.pallas_call(
        functools.partial(_filter_minor_p_body, n2=n2),
        out_shape=SDS((FFT_K1, 2 * n2, 2 * BW), BF16),
        grid=(FFT_K1 // kb,),
        in_specs=[pl.BlockSpec((nq, nblk, kb, FFT_NB, LANE), lambda i: (0, 0, i, 0, 0)),
                  pl.BlockSpec((kb, 2 * n2, 2 * n2), lambda i: (i, 0, 0)),
                  pl.BlockSpec((1, 4 * BW), lambda i: (0, 0))],
        out_specs=pl.BlockSpec((kb, 2 * n2, 2 * BW), lambda i: (i, 0, 0)),
        compiler_params=_cparams(("parallel",), 40),
        name="hy_filter_minor_p",
    )(af_p, g, ss)


def _unpack_slab(ref_slab):
    hi, lo = _unpack2(ref_slab)
    return jnp.concatenate([hi, lo], axis=1)


def _dft_major_p_body(f_ref, x_ref, a_ref):
    for j in range(x_ref.shape[0]):
        x = _unpack_slab(x_ref[j]).astype(BF16)
        rr = jnp.dot(f_ref[...], x, preferred_element_type=F32)
        _store_spectrum_slab(a_ref, j, _pack2(rr[0:FFT_K1], rr[FFT_K1:2 * FFT_K1]))


def _dft_major_p(f1, xp, n2):
    nbk = FFT_NB
    return pl.pallas_call(
        _dft_major_p_body,
        out_shape=SDS((BW // LANE, n2 // nbk, FFT_K1 * nbk, LANE), U32),
        grid=(n2 // nbk,),
        in_specs=[pl.BlockSpec((2 * FFT_K1, FFT_N1 // 2), lambda i: (0, 0)),
                  pl.BlockSpec((nbk, FFT_N1 // 2, BW // 2), lambda i: (i, 0, 0))],
        out_specs=pl.BlockSpec((BW // LANE, None, FFT_K1 * nbk, LANE), lambda i: (0, i, 0, 0)),
        compiler_params=_cparams(("parallel",), 32),
        name="dft_major_p",
    )(f1, xp)


def _conv_minor_p_body(a_ref, h_ref, g_ref, gt_ref, d_ref, *, n2):
    for j in range(a_ref.shape[2]):
        re, im = _unpack2(_load_spectrum_k1(a_ref, j))
        a = jnp.concatenate([re, im], axis=0).astype(BF16)
        x = jnp.dot(g_ref[j], a, preferred_element_type=F32)
        h = h_ref[j].astype(F32)
        xr, xi, hr, hi = x[0:n2], x[n2:2 * n2], h[0:n2], h[n2:2 * n2]
        y = jnp.concatenate([xr * hr - xi * hi, xr * hi + xi * hr], axis=0)
        dv = jnp.dot(gt_ref[j], y.astype(BF16), preferred_element_type=F32)
        _store_spectrum_k1(d_ref, j, _pack2(dv[0:n2], dv[n2:2 * n2]))


def _conv_minor_p(a_p, hspec, g, gt, order, n2):
    kb = FFT_KB
    nq, nblk = a_p.shape[0], a_p.shape[1]
    a_p = a_p.reshape(nq, nblk, FFT_K1, FFT_NB, LANE)
    gspec = pl.BlockSpec((kb, 2 * n2, 2 * n2), lambda i: (i, 0, 0))
    sspec = pl.BlockSpec((nq, nblk, kb, FFT_NB, LANE), lambda i: (0, 0, i, 0, 0))
    return pl.pallas_call(
        functools.partial(_conv_minor_p_body, n2=n2),
        out_shape=SDS((nq, nblk, FFT_K1, FFT_NB, LANE), U32),
        grid=(FFT_K1 // kb,),
        in_specs=[sspec, pl.BlockSpec((kb, 2 * n2, BW), lambda i: (i, 0, order)), gspec, gspec],
        out_specs=sspec,
        compiler_params=_cparams(("parallel",), 32),
        name="hy_conv_minor_p",
    )(a_p, hspec, g, gt)


def _idft_gate_p_body(f_ref, d_ref, gate_ref, z_ref, b_ref, o_ref):
    for j in range(o_ref.shape[0]):
        re, im = _unpack2(_load_spectrum_slab(d_ref, j))
        d = jnp.concatenate([re, im], axis=0).astype(BF16)
        y = jnp.dot(f_ref[...], d, preferred_element_type=F32)
        out = _unpack_slab(gate_ref[j]) * (y + b_ref[...] * _unpack_slab(z_ref[j]))
        o_ref[j] = _pack2(out[:, 0:BW // 2], out[:, BW // 2:BW])


def _idft_gate_p(f1i, d_p, gate_p, z_p, bias, n2):
    nbk = FFT_NB
    nq, nblk = d_p.shape[0], d_p.shape[1]
    d_p = d_p.reshape(nq, nblk, FFT_K1 * FFT_NB, LANE)
    slab = pl.BlockSpec((nbk, FFT_N1 // 2, BW // 2), lambda i: (i, 0, 0))
    return pl.pallas_call(
        _idft_gate_p_body,
        out_shape=SDS((n2, FFT_N1 // 2, BW // 2), U32),
        grid=(n2 // nbk,),
        in_specs=[pl.BlockSpec((FFT_N1 // 2, 2 * FFT_K1), lambda i: (0, 0)),
                  pl.BlockSpec((nq, None, FFT_K1 * FFT_NB, LANE), lambda i: (0, i, 0, 0)),
                  slab, slab, pl.BlockSpec((1, BW), lambda i: (0, 0))],
        out_specs=slab,
        compiler_params=_cparams(("parallel",), 32),
        name="idft_gate_p",
    )(f1i, d_p, gate_p, z_p, bias)


def _hyena_latent(x1p, x2p, vp, feat_perm, tabs, hyw, layer, seq_len):
    n2 = seq_len // (FFT_N1 // 2)
    f1, f1i, g, gt = tabs
    w1p, b1, fr1, w2, b2, fr2, w3p_bf, deltas, hy_bias = hyw
    af_p, ss = _hy_filters_p(feat_perm, w1p, b1, fr1, w2, b2, fr2, w3p_bf, deltas, f1, layer)
    hspec = _filter_minor_p(af_p, g, ss.reshape(1, 4 * BW), n2)
    z = vp
    for o, gate in enumerate((x1p, x2p)):
        a_p = _dft_major_p(f1, z, n2)
        d_p = _conv_minor_p(a_p, hspec, g, gt, o, n2)
        z = _idft_gate_p(f1i, d_p, gate, z, hy_bias[layer, o][None, :], n2)
    return z


def _merge_body(att_ref, yp_ref, yhp_ref, yhc_ref, ys_ref, hmod_ref, wg_ref, x_ref, wb_ref, wo_ref,
                g1_ref, lg_ref, lb_ref, sh2_ref, sc2_ref, rw_ref, rb_ref,
                x1_ref, m_ref, rt_ref, cnt_ref, *, nlat, n_ctx, alpha):
    i = pl.program_id(0)
    n2c = yhp_ref.shape[0]
    pieces = []
    for a in range(TM1 // n2c):
        hi, lo = _unpack2(yhp_ref[:, a, :])
        pieces.append(jnp.concatenate([hi, lo], axis=1).astype(BF16))
    yh_lat = jnp.concatenate(pieces, axis=0)
    yh_ctx = jnp.concatenate([yhc_ref[...], jnp.zeros((TM1 - n_ctx, BW), BF16)], axis=0)
    yh = jnp.where(i >= nlat, yh_ctx, yh_lat)
    ys = (att_ref[...], yp_ref[...], yh, ys_ref[...])
    merged = jnp.zeros((TM1, D), F32)
    for n in range(4):
        br = jnp.dot(ys[n], wb_ref[n], preferred_element_type=F32)
        gl = jnp.dot(hmod_ref[...], wg_ref[:, n * D:(n + 1) * D], preferred_element_type=F32)
        merged = merged + br * (0.5 * jnp.tanh(0.5 * gl) + 0.5)
    y = jnp.dot(merged.astype(BF16), wo_ref[...], preferred_element_type=F32)
    x1 =_layer_norm(alpha * x_ref[...] + g1_ref[...] * y) * lg_ref[...] + lb_ref[...]
    x1_ref[...] = x1
    m = _layer_norm(x1) * (1.0 + sc2_ref[...]) + sh2_ref[...]
    m_ref[...] = m.astype(BF16)

    logits = jnp.dot(m, rw_ref[...], precision=HIGHEST, preferred_element_type=F32) + rb_ref[...]
    lt = logits.T
    le = lt[0:N_EXPERTS]
    lgp = lt[N_EXPERTS:N_EXPERTS + SUB]
    big = 1 << 20
    gi = lax.broadcasted_iota(I32, lgp.shape, 0)
    gmax = jnp.max(lgp, axis=0, keepdims=True)
    gsel = jnp.min(jnp.where(lgp == gmax, gi, big), axis=0, keepdims=True)
    gate_g = 1.0 / jnp.sum(jnp.exp(lgp - gmax), axis=0, keepdims=True)
    ei = lax.broadcasted_iota(I32, le.shape, 0)
    lem = jnp.where(lax.shift_right_logical(ei, 3) == gsel, le, -3.0e38)
    v1 = jnp.max(lem, axis=0, keepdims=True)
    i1 = jnp.min(jnp.where(lem == v1, ei, big), axis=0, keepdims=True)
    lem2 = jnp.where(ei == i1, -3.0e38, lem)
    v2 = jnp.max(lem2, axis=0, keepdims=True)
    i2 = jnp.min(jnp.where(lem2 == v2, ei, big), axis=0, keepdims=True)
    e2 = jnp.exp(v2 - v1)
    wa = gate_g / (1.0 + e2)
    wb = gate_g * e2 / (1.0 + e2)
    tok = lax.broadcasted_iota(I32, (1, TM1), 1)
    valid = jnp.logical_or(i < nlat, tok < n_ctx)
    i1 = jnp.where(valid, i1, -1)
    i2 = jnp.where(valid, i2, -1)
    ri = lax.broadcasted_iota(I32, (SUB, TM1), 0)
    rt = jnp.where(ri == 0, i1.astype(F32), jnp.where(ri == 1, i2.astype(F32),
                   jnp.where(ri == 2, wa, jnp.where(ri == 3, wb, 0.0))))
    rt_ref[...] = rt
    ci = lax.broadcasted_iota(I32, (LANE, TM1), 0)
    oh = jnp.logical_or(ci == i1, ci == i2).astype(BF16)
    cnt_ref[0] = lax.dot_general(jnp.ones((SUB, TM1), BF16), oh, (((1,), (1,)), ((), ())),
                                 preferred_element_type=F32)


def _merge(att, ypool, yhp, yhc, ysc, hmod, wg_bf, x, wb_bf, wo_bf, g1, ln_g, ln_b, sh2, sc2, rw, rb, layer, n_lat,
           n_ctx, alpha):
    nt = x.shape[0]
    nlat = n_lat // TM1
    ntl = nt // TM1
    n2c = yhp.shape[0]
    sel = lambda i: (jnp.where(i >= nlat, 1, 0), 0, 0)
    row = lambda c: pl.BlockSpec((TM1, c), lambda i: (i, 0))
    lsel = lambda *blk, **kw: pl.BlockSpec((None,) + blk, lambda i: (layer,) + (0,) * len(blk), **kw)
    msel = pl.BlockSpec((None, 1, D), sel)
    return pl.pallas_call(
        functools.partial(_merge_body, nlat=nlat, n_ctx=n_ctx, alpha=alpha),
        out_shape=[SDS((nt, D), F32), SDS((nt, D), BF16), SDS((SUB, nt), F32), SDS((ntl, SUB, LANE), F32)],
        grid=(ntl,),
        in_specs=[row(BW), row(BW),
                  pl.BlockSpec((n2c, TM1 // n2c, BW // 2), lambda i: (0, jnp.minimum(i, nlat - 1), 0)),
                  pl.BlockSpec((n_ctx, BW), lambda i: (0, 0)),
                  row(BW), row(D), lsel(D, 4 * D, pipeline_mode=pl.Buffered(1)), row(D),
                  lsel(4, BW, D, pipeline_mode=pl.Buffered(1)), lsel(D, D, pipeline_mode=pl.Buffered(1)),
                  msel, lsel(1, D), lsel(1, D), msel, msel,
                  lsel(D, LANE), lsel(1, LANE)],
        out_specs=[row(D), row(D), pl.BlockSpec((SUB, TM1), lambda i: (0, i)),
                   pl.BlockSpec((1, SUB, LANE), lambda i: (i, 0, 0))],
        compiler_params=_cparams(("parallel",), 56),
        name="merge_router",
    )(att, ypool, yhp, yhc, ysc, hmod, wg_bf, x, wb_bf, wo_bf, g1, ln_g, ln_b, sh2, sc2, rw, rb)


def _piece_loops(np_ref, so_ref, do_ref, j, fn):
    for e in range(N_EXPERTS):
        n = np_ref[j * N_EXPERTS + e]
        so = so_ref[j * N_EXPERTS + e]
        do = do_ref[j * N_EXPERTS + e]

        def body(p, carry, so=so, do=do):
            fn(pl.multiple_of(so + p * MOE_PIECE, MOE_PIECE), pl.multiple_of(do + p * MOE_PIECE, MOE_PIECE))
            return carry

        lax.fori_loop(0, n, body, 0)


def _dispatch_body(np_ref, so_ref, do_ref, nv_ref, m_ref, rt_ref, u_ref, hs_hbm, pos_ref, hs_vmem, sem, *,
                   ntiles, nblk):
    j = pl.program_id(0)
    real = j < ntiles
    rt = rt_ref[...]
    e0 = jnp.where(real, rt[0:1].astype(I32), -1)
    e1 = jnp.where(real, rt[1:2].astype(I32), -1)
    ei = lax.broadcasted_iota(I32, (N_EXPERTS, TM1), 0)
    oh0 = (ei == e0).astype(F32)
    oh1 = (ei == e1).astype(F32)
    c0 = jnp.dot(oh0.astype(BF16), u_ref[...], preferred_element_type=F32)
    c1 = jnp.dot(oh1.astype(BF16), u_ref[...], preferred_element_type=F32)
    n0 = jnp.sum(oh0, axis=1, keepdims=True)
    ecol = lax.broadcasted_iota(I32, (N_EXPERTS, 1), 0)
    toff = jnp.zeros((N_EXPERTS, 1), F32)
    for e in range(N_EXPERTS):
        toff = jnp.where(ecol == e, so_ref[j * N_EXPERTS + e].astype(F32), toff)
    pos0 = jnp.sum(oh0 * (toff + c0), axis=0, keepdims=True)
    pos1 = jnp.sum(oh1 * (toff + n0 + c1), axis=0, keepdims=True)
    pos0 = jnp.where(e0 >= 0, pos0, -1.0)
    pos1 = jnp.where(e1 >= 0, pos1, -1.0)
    ri = lax.broadcasted_iota(I32, (SUB, TM1), 0)
    pos_ref[...] = jnp.where(ri == 0, pos0, jnp.where(ri == 1, pos1, 0.0))
    si = lax.broadcasted_iota(I32, (MOE_S, TM1), 0)
    perm = jnp.logical_or(si == pos0.astype(I32), si == pos1.astype(I32)).astype(BF16)
    hs = jnp.dot(perm, m_ref[...], preferred_element_type=F32)
    hs_vmem[...] = _pack2(hs[:, 0:D // 2], hs[:, D // 2:D])

    def copy(so, do):
        return pltpu.make_async_copy(hs_vmem.at[pl.ds(so, MOE_PIECE)], hs_hbm.at[pl.ds(do, MOE_PIECE)], sem)

    _piece_loops(np_ref, so_ref, do_ref, j, lambda so, do: copy(so, do).start())
    _piece_loops(np_ref, so_ref, do_ref, j, lambda so, do: copy(so, do).wait())

    @pl.when(j == ntiles)
    def _():
        def blk_copy(b):
            return pltpu.make_async_copy(hs_vmem.at[pl.ds(0, MOE_MB)],
                                         hs_hbm.at[pl.ds(pl.multiple_of(b * MOE_MB, MOE_MB), MOE_MB)], sem)

        def start(b, carry):
            blk_copy(b).start()
            return carry

        def wait(b, carry):
            blk_copy(b).wait()
            return carry

        lax.fori_loop(nv_ref[0], nblk, start, 0)
        lax.fori_loop(nv_ref[0], nblk, wait, 0)


def _dispatch(npieces, soff, doff, nvb, m_bf, rt, upper, rmax):
    nt = m_bf.shape[0]
    ntiles = nt // TM1
    last = ntiles - 1
    return pl.pallas_call(
        functools.partial(_dispatch_body, ntiles=ntiles, nblk=rmax // MOE_MB),
        out_shape=[SDS((rmax, D // 2), U32), SDS((SUB, (ntiles + 1) * TM1), F32)],
        grid_spec=pltpu.PrefetchScalarGridSpec(
            num_scalar_prefetch=4,
            grid=(ntiles + 1,),
            in_specs=[pl.BlockSpec((TM1, D), lambda j, *_: (jnp.minimum(j, last), 0)),
                      pl.BlockSpec((SUB, TM1), lambda j, *_: (0, jnp.minimum(j, last))),
                      pl.BlockSpec((TM1, TM1), lambda j, *_: (0, 0))],
            out_specs=[pl.BlockSpec(memory_space=pl.ANY),
                       pl.BlockSpec((SUB, TM1), lambda j, *_: (0, j))],
            scratch_shapes=[pltpu.VMEM((MOE_S, D // 2), U32), pltpu.SemaphoreType.DMA(())]),
        compiler_params=_cparams(("arbitrary",), 40),
        name="moe_dispatch",
    )(npieces, soff, doff, nvb, m_bf, rt, upper)


def _expert_body(be_ref, nv_ref, x_ref, w1_ref, w3_ref, w2_ref, o_ref, w1b, w3b, w2b):
    b = pl.program_id(0)
    valid = b < nv_ref[0]
    prev = be_ref[jnp.maximum(b - 1, 0)]
    fresh = jnp.logical_or(b == 0, be_ref[b] != prev)

    @pl.when(jnp.logical_and(valid, fresh))
    def _():
        w1b[...] = w1_ref[...].astype(BF16)
        w3b[...] = w3_ref[...].astype(BF16)
        w2b[...] = w2_ref[...].astype(BF16)

    @pl.when(valid)
    def _():
        xh, xl = _unpack2(x_ref[...])
        xh, xl = xh.astype(BF16), xl.astype(BF16)
        half = D // 2
        h1 = (jnp.dot(xh, w1b[0:half, :], preferred_element_type=F32)
              + jnp.dot(xl, w1b[half:D, :], preferred_element_type=F32))
        h3 = (jnp.dot(xh, w3b[0:half, :], preferred_element_type=F32)
              + jnp.dot(xl, w3b[half:D, :], preferred_element_type=F32))
        hh = (h1 * (0.5 * jnp.tanh(0.5 * h1) + 0.5)) * h3
        y = jnp.dot(hh.astype(BF16), w2b[...], preferred_element_type=F32)
        o_ref[...] = _pack2(y[:, 0:half], y[:, half:D])

    @pl.when(b >= nv_ref[0])
    def _():
        o_ref[...] = jnp.zeros_like(o_ref)


def _experts(blk_e, nvb, hs, ex_w1, ex_w3, ex_w2, layer):
    rmax = hs.shape[0]
    nb = rmax // MOE_MB
    bi = lambda b, be, nv: jnp.maximum(jnp.minimum(b, nv[0] - 1), 0)
    return pl.pallas_call(
        _expert_body,
        out_shape=SDS((rmax, D // 2), U32),
        grid_spec=pltpu.PrefetchScalarGridSpec(
            num_scalar_prefetch=2,
            grid=(nb,),
            in_specs=[pl.BlockSpec((MOE_MB, D // 2), lambda b, be, nv: (bi(b, be, nv), 0)),
                      pl.BlockSpec((None, None, D, EXPERT_HIDDEN), lambda b, be, nv: (layer, be[bi(b, be, nv)], 0, 0)),
                      pl.BlockSpec((None, None, D, EXPERT_HIDDEN), lambda b, be, nv: (layer, be[bi(b, be, nv)], 0, 0)),
                      pl.BlockSpec((None, None, EXPERT_HIDDEN, D), lambda b, be, nv: (layer, be[bi(b, be, nv)], 0, 0))],
            out_specs=pl.BlockSpec((MOE_MB, D // 2), lambda b, be, nv: (b, 0)),
            scratch_shapes=[pltpu.VMEM((D, EXPERT_HIDDEN), BF16), pltpu.VMEM((D, EXPERT_HIDDEN), BF16),
                            pltpu.VMEM((EXPERT_HIDDEN, D), BF16)]),
        compiler_params=_cparams(("arbitrary",), 48),
        name="moe_experts",
    )(blk_e, nvb, hs, ex_w1, ex_w3, ex_w2)


def _combine_body(np_ref, so_ref, do_ref, ys_hbm, pos_ref, rt_ref, x1_ref, g2_ref, lg_ref, lb_ref,
                  o_ref, ys_vmem, sem, *, alpha):
    j = pl.program_id(0)

    def copy(so, do):
        return pltpu.make_async_copy(ys_hbm.at[pl.ds(do, MOE_PIECE)], ys_vmem.at[pl.ds(so, MOE_PIECE)], sem)

    _piece_loops(np_ref, so_ref, do_ref, j, lambda so, do: copy(so, do).start())
    z = jnp.concatenate([pos_ref[...], rt_ref[...], jnp.zeros((LANE - 2 * SUB, TM1), F32)], axis=0)
    zt = z.T
    p0 = zt[:, 0:1].astype(I32)
    p1 = zt[:, 1:2].astype(I32)
    w0 = zt[:, SUB + 2:SUB + 3]
    w1 = zt[:, SUB + 3:SUB + 4]
    si = lax.broadcasted_iota(I32, (TM1, MOE_S), 1)
    wm = (jnp.where(si == p0, w0, 0.0) + jnp.where(si == p1, w1, 0.0)).astype(BF16)
    _piece_loops(np_ref, so_ref, do_ref, j, lambda so, do: copy(so, do).wait())
    last = j * N_EXPERTS + N_EXPERTS - 1
    total = so_ref[last] + np_ref[last] * MOE_PIECE
    srow = lax.broadcasted_iota(I32, (MOE_S, 1), 0)
    yh, yl = _unpack2(jnp.where(srow < total, ys_vmem[...], jnp.uint32(0)))
    f = jnp.concatenate([jnp.dot(wm, yh.astype(BF16), preferred_element_type=F32),
                         jnp.dot(wm, yl.astype(BF16), preferred_element_type=F32)], axis=1)
    o_ref[...] = _layer_norm(alpha * x1_ref[...] + g2_ref[...] * f) * lg_ref[...] + lb_ref[...]


def _combine(npieces, soff, doff, ys, pos, rt, x1, g2, ln_g, ln_b, layer, n_lat, n_out_tiles, n_out_rows, alpha):
    nlat = n_lat // TM1
    sel = lambda j, *_: (jnp.where(j >= nlat, 1, 0), 0, 0)
    lsel = lambda *blk: pl.BlockSpec((None,) + blk, lambda j, *_: (layer,) + (0,) * len(blk))
    return pl.pallas_call(
        functools.partial(_combine_body, alpha=alpha),
        out_shape=SDS((n_out_rows, D), F32),
        grid_spec=pltpu.PrefetchScalarGridSpec(
            num_scalar_prefetch=3,
            grid=(n_out_tiles,),
            in_specs=[pl.BlockSpec(memory_space=pl.ANY),
                      pl.BlockSpec((SUB, TM1), lambda j, *_: (0, j)),
                      pl.BlockSpec((SUB, TM1), lambda j, *_: (0, j)),
                      pl.BlockSpec((TM1, D), lambda j, *_: (j, 0)),
                      pl.BlockSpec((None, 1, D), sel), lsel(1, D), lsel(1, D)],
            out_specs=pl.BlockSpec((TM1, D), lambda j, *_: (j, 0)),
            scratch_shapes=[pltpu.VMEM((MOE_S, D // 2), U32), pltpu.SemaphoreType.DMA(())]),
        compiler_params=_cparams(("arbitrary",), 40),
        name="moe_combine",
    )(npieces, soff, doff, ys, pos, rt, x1, g2, ln_g, ln_b)


def _moe_plan(cnt, ntiles):
    c = cnt[:, 0, :N_EXPERTS].astype(I32)
    pad8 = (c + MOE_PIECE - 1) // MOE_PIECE * MOE_PIECE
    toff = jnp.cumsum(pad8, axis=1) - pad8
    tot = pad8.sum(axis=0)
    totb = (tot + MOE_MB - 1) // MOE_MB * MOE_MB
    ends = jnp.cumsum(totb)
    base = ends - totb
    dest = base[None, :] + jnp.cumsum(pad8, axis=0) - pad8
    npieces = jnp.concatenate([pad8 // MOE_PIECE, ((totb - tot) // MOE_PIECE)[None, :]], axis=0)
    soff = jnp.concatenate([toff, jnp.zeros((1, N_EXPERTS), I32)], axis=0)
    doff = jnp.concatenate([dest, (base + tot)[None, :]], axis=0)
    return npieces.reshape(-1), soff.reshape(-1), doff.reshape(-1), ends


def _rope_tables(n_lat, nt):
    t = jnp.arange(n_lat)
    pos = jnp.stack([(t // GRID_W).astype(F32), (t % GRID_W).astype(F32)], axis=1)
    inv = ROPE_BASE ** (-jnp.arange(ROPE_PAIRS, dtype=F32) / ROPE_PAIRS)
    ang = pos[:, :, None] * inv[None, None, :]
    cos = jnp.repeat(jnp.cos(ang), 2, axis=1).reshape(n_lat, 2, 2, ROPE_PAIRS)
    sin = jnp.sin(ang)
    sin = jnp.stack([-sin, sin], axis=2)
    cos = jnp.tile(cos.reshape(n_lat, HEAD_DIM), (1, LANE // HEAD_DIM))
    sin = jnp.tile(sin.reshape(n_lat, HEAD_DIM), (1, LANE // HEAD_DIM))
    cos = jnp.concatenate([cos, jnp.ones((nt - n_lat, LANE), F32)], axis=0)
    sin = jnp.concatenate([sin, jnp.zeros((nt - n_lat, LANE), F32)], axis=0)
    return cos, sin


def _hy_features(l):
    t01 = jnp.linspace(0.0, 1.0, l, dtype=F32)
    fr = jnp.linspace(1e-4, HY_BANDS - 1, HY_BANDS, dtype=F32)
    ang = (2.0 * math.pi / l) * jnp.arange(l, dtype=F32)[:, None] * fr[None, :]
    feat = jnp.concatenate([t01[:, None], jnp.cos(ang), -jnp.sin(ang)], axis=-1)
    feat = jnp.pad(feat, ((0, 0), (0, LANE - HY_EMB)))
    return feat.reshape(FFT_N1 // 2, l // (FFT_N1 // 2), LANE).transpose(1, 0, 2).reshape(l, LANE)


def kernel(x, c, ctx, c_ctx, ada_w, ada_b, w_in, attn_sink, pool_w, pool_scale, hy_conv_w, hy_conv_b, hy_w1, hy_b1, hy_freq1, hy_w2, hy_b2, hy_freq2, hy_w3, hy_bias, sc_conv_w, w_branch, w_out, ln1_g, ln1_b, ln2_g, ln2_b, rg_w, rg_b, re_w, re_b, ex_w1, ex_w3, ex_w2):
    depth = w_in.shape[0]
    assert x.shape[0] == 1 and ctx.shape[0] == 1 and x.shape[2] == D
    n_lat = x.shape[1]
    n_ctx = ctx.shape[1]
    assert n_ctx == TM and n_lat % TM1 == 0 and n_lat >= 2 * TM1
    nt = n_lat + n_ctx + PADR
    ntiles1 = nt // TM1
    alpha = (2 * depth) ** 0.25

    cos_t, sin_t = _rope_tables(n_lat, nt)
    w_main_bf = w_in[:, :, :MAIN_COLS].astype(BF16)
    wg_bf = w_in[:, :, MAIN_COLS:].astype(BF16)
    wb_bf = w_branch.astype(BF16)
    wo_bf = w_out.astype(BF16)
    pool_w_bf = pool_w.astype(BF16)
    r3 = lambda a: a.reshape(depth, 1, a.shape[-1])
    tabs_l = _dft_tables(n_lat // (FFT_N1 // 2))
    tabs_c = _dft_tables(n_ctx // (FFT_N1 // 2))
    feat_l, feat_c = _hy_features(n_lat), _hy_features(n_ctx)
    deltas = jnp.abs(jnp.linspace(math.log(HY_TARGET) / HY_SLOW_PCT, math.log(HY_TARGET) / HY_FAST_PCT,
                                  BW, dtype=F32))[None, :]
    w1p = jnp.pad(hy_w1, ((0, 0), (0, LANE - HY_EMB), (0, 0)))
    w3p_bf = hy_w3.reshape(depth, HY_HIDDEN, 2, 2, BW).transpose(0, 1, 3, 2, 4).reshape(depth, HY_HIDDEN, 4 * BW).astype(BF16)
    hyw = (w1p, r3(hy_b1), r3(hy_freq1), hy_w2, r3(hy_b2), r3(hy_freq2), w3p_bf, deltas, hy_bias)
    rw = jnp.concatenate([re_w, rg_w, jnp.zeros((depth, D, LANE - N_EXPERTS - N_GROUPS), F32)], axis=2)
    rb = jnp.concatenate([re_b, rg_b, jnp.full((depth, LANE - N_EXPERTS - N_GROUPS), NEG_INF, F32)], axis=1)
    rb = rb.reshape(depth, 1, LANE)
    upper = (jnp.arange(TM1)[:, None] < jnp.arange(TM1)[None, :]).astype(BF16)
    rmax = -(-(2 * (n_lat + n_ctx) + ntiles1 * N_EXPERTS * (MOE_PIECE - 1) + N_EXPERTS * (MOE_MB - 1)) // MOE_MB) * MOE_MB
    nblk = rmax // MOE_MB

    s8 = jnp.concatenate([c, c_ctx[None, :], jnp.zeros((SUB - 2, D), F32)], axis=0)
    mod = _mod_all(s8, ada_w, ada_b.reshape(depth, 1, 6 * D))

    xs = jnp.concatenate([x[0], ctx[0], jnp.zeros((PADR, D), F32)], axis=0)
    for i in range(depth):
        last = i == depth - 1
        mp = lambda j: mod[i, 0:2, j * D:(j + 1) * D].reshape(2, 1, D)
        sh1, sc1, g1, sh2, sc2, g2 = (mp(j) for j in range(6))

        q, k, v, pool_in, hy_in, sc_in, hmod = _inproj(xs, sh1, sc1, cos_t, sin_t, w_main_bf, i, n_lat // TM1)
        att = _attention(attn_sink[i], q, k, v, n_lat)
        ypool, ysc, x1p, x2p, vp, x1c, x2c, vc = _local_ops(
            pool_in, sc_in, hy_in, pool_w_bf, pool_scale.reshape(depth, 1, BW),
            hy_conv_w, hy_conv_b.reshape(depth, 1, 3 * BW), sc_conv_w, i, n_lat, n_ctx)
        yhp = _hyena_latent(x1p, x2p, vp, feat_l, tabs_l, hyw, i, n_lat)
        if last:
            yhc = jnp.zeros((n_ctx, BW), BF16)
        else:
            yhc = _hyena(x1c, x2c, vc, feat_c, tabs_c, hyw, i, n_ctx, 0)
        x1, m_bf, rt, cnt = _merge(att, ypool, yhp, yhc, ysc, hmod, wg_bf, xs, wb_bf, wo_bf, g1, r3(ln1_g), r3(ln1_b),
                                   sh2, sc2, rw, rb, i, n_lat, n_ctx, alpha)

        npieces, soff, doff, ends = _moe_plan(cnt, ntiles1)
        nvb = (ends[-1] // MOE_MB).astype(I32).reshape(1)
        blk_start = jnp.arange(nblk, dtype=I32) * MOE_MB
        blk_e = jnp.minimum(jnp.sum((ends[None, :] <= blk_start[:, None]).astype(I32), axis=1), N_EXPERTS - 1)
        hs, pos = _dispatch(npieces, soff, doff, nvb, m_bf, rt, upper, rmax)
        ys = _experts(blk_e, nvb, hs, ex_w1, ex_w3, ex_w2, i)
        if last:
            xs = _combine(npieces, soff, doff, ys, pos, rt, x1, g2, r3(ln2_g), r3(ln2_b), i, n_lat,
                          n_lat // TM1, n_lat, alpha)
        else:
            xs = _combine(npieces, soff, doff, ys, pos, rt, x1, g2, r3(ln2_g), r3(ln2_b), i, n_lat, ntiles1, nt,
                          alpha)
    return xs[None]
```

```python
import functools
import math

import jax
import jax.numpy as jnp
from jax import lax
from jax.experimental import pallas as pl
from jax.experimental.pallas import tpu as pltpu

F32 = jnp.float32
BF16 = jnp.bfloat16
I32 = jnp.int32
U32 = jnp.uint32
SDS = jax.ShapeDtypeStruct
HIGHEST = lax.Precision.HIGHEST

D = 1024
GRID_W = 64
HEADS = 8
KV_HEADS = 2
HEAD_DIM = 64
WINDOW = 128
ATT_BLK = 128
ROPE_PAIRS = 16
ROPE_BASE = 10000.0
BW = 512
POOL_WINDOWS = (2, 4, 8, 16)
POOL_GW = 128
HY_EMB = 33
HY_BANDS = 16
HY_HIDDEN = 64
HY_TARGET = 1e-2
HY_FAST_PCT = 0.3
HY_SLOW_PCT = 1.5
HY_MOD_SHIFT = 0.05
Q_COLS = HEADS * HEAD_DIM
KV_COLS = KV_HEADS * HEAD_DIM
MAIN_COLS = Q_COLS + 2 * KV_COLS + BW + 3 * BW + 3 * BW
N_GROUPS = 4
EPG = 8
N_EXPERTS = 32
EXPERT_HIDDEN = 512
LN_EPS = 1e-6
NEG_INF = -1e30

LANE = 128
SUB = 8
BSUB = 16
TM = 256
TM1 = 512
PADR = 256
HALO = 16
FFT_N1 = 512
FFT_K1 = 264
FFT_KB = 8
FFT_TN = 2048
FFT_NB = 8
MOE_PIECE = 8
MOE_MB = 256
MOE_S = 2 * TM1 + 256


def _cparams(sem, vmem_mb):
    return pltpu.CompilerParams(dimension_semantics=sem, vmem_limit_bytes=vmem_mb * 2 ** 20)


def _layer_norm(x):
    mu = jnp.mean(x, axis=-1, keepdims=True)
    xc = x - mu
    var = jnp.mean(xc * xc, axis=-1, keepdims=True)
    return xc * lax.rsqrt(var + LN_EPS)


def _mod_body(s_ref, w_ref, b_ref, o_ref):
    s = s_ref[...]
    s = s * (1.0 / (1.0 + jnp.exp(-s)))
    o_ref[...] = jnp.dot(s.astype(BF16), w_ref[...].astype(BF16), preferred_element_type=F32) + b_ref[...]


def _mod_all(s8, ada_w, ada_b3):
    depth = ada_w.shape[0]
    tn = 1024
    return pl.pallas_call(
        _mod_body,
        out_shape=SDS((depth, SUB, 6 * D), F32),
        grid=(depth, 6 * D // tn),
        in_specs=[pl.BlockSpec((SUB, D), lambda l, j: (0, 0)),
                  pl.BlockSpec((None, D, tn), lambda l, j: (l, 0, j)),
                  pl.BlockSpec((None, 1, tn), lambda l, j: (l, 0, j))],
        out_specs=pl.BlockSpec((None, SUB, tn), lambda l, j: (l, 0, j)),
        compiler_params=_cparams(("parallel", "parallel"), 32),
        name="ada_mod",
    )(s8, ada_w, ada_b3)


def _inproj_body(x_ref, sh_ref, sc_ref, cos_ref, sin_ref, w_ref,
                 q_ref, k_ref, v_ref, pool_ref, hy_ref, scv_ref, hmod_ref):
    h = _layer_norm(x_ref[...]) * (1.0 + sc_ref[...]) + sh_ref[...]
    hb = h.astype(BF16)

    def mm(a, b):
        return jnp.dot(hb, w_ref[:, a:b], preferred_element_type=F32)

    cs = cos_ref[...]
    sn = sin_ref[...]
    lane = lax.broadcasted_iota(I32, cs.shape, 1)
    first = (lane & ROPE_PAIRS) == 0

    def rope(t):
        sw = jnp.where(first, pltpu.roll(t, LANE - ROPE_PAIRS, 1), pltpu.roll(t, ROPE_PAIRS, 1))
        return t * cs + sw * sn

    for j in range(Q_COLS // LANE):
        q_ref[:, j * LANE:(j + 1) * LANE] = (rope(mm(j * LANE, (j + 1) * LANE)) * (HEAD_DIM ** -0.5)).astype(BF16)
    o = Q_COLS
    k_ref[...] = rope(mm(o, o + KV_COLS)).astype(BF16)
    o += KV_COLS
    v_ref[...] = mm(o, o + KV_COLS).astype(BF16)
    o += KV_COLS
    pool_ref[...] = mm(o, o + BW).astype(BF16)
    o += BW
    for j in range(3):
        hy_ref[:, j * BW:(j + 1) * BW] = mm(o + j * BW, o + (j + 1) * BW).astype(BF16)
    o += 3 * BW
    for j in range(3):
        scv_ref[:, j * BW:(j + 1) * BW] = mm(o + j * BW, o + (j + 1) * BW).astype(BF16)
    hmod_ref[...] = hb


def _inproj(x, sh, sc, cos_t, sin_t, w_in_bf, layer, n_lat_tiles):
    nt = x.shape[0]
    sel = lambda i: (jnp.where(i >= n_lat_tiles, 1, 0), 0, 0)
    row = lambda c: pl.BlockSpec((TM1, c), lambda i: (i, 0))
    outs = [SDS((nt, Q_COLS), BF16), SDS((nt, KV_COLS), BF16), SDS((nt, KV_COLS), BF16), SDS((nt, BW), BF16),
            SDS((nt, 3 * BW), BF16), SDS((nt, 3 * BW), BF16), SDS((nt, D), BF16)]
    return pl.pallas_call(
        _inproj_body,
        out_shape=outs,
        grid=(nt // TM1,),
        in_specs=[row(D),
                  pl.BlockSpec((None, 1, D), sel), pl.BlockSpec((None, 1, D), sel),
                  row(LANE), row(LANE),
                  pl.BlockSpec((None, D, MAIN_COLS), lambda i: (layer, 0, 0), pipeline_mode=pl.Buffered(1))],
        out_specs=[row(Q_COLS), row(KV_COLS), row(KV_COLS), row(BW), row(3 * BW), row(3 * BW), row(D)],
        compiler_params=_cparams(("parallel",), 58),
        name="inproj",
    )(x, sh, sc, cos_t, sin_t, w_in_bf)


def _attn_body(sink_ref, q_ref, kp_ref, kc_ref, kn_ref, vp_ref, vc_ref, vn_ref, kx_ref, vx_ref, o_ref, *, nlb):
    i = pl.program_id(0)
    is_lat = i < nlb
    lat_i = jnp.where(is_lat, 1, 0)
    prev_i = jnp.where(jnp.logical_and(is_lat, i > 0), 1, 0)
    next_i = jnp.where(jnp.logical_and(is_lat, i < nlb - 1), 1, 0)
    ncx = kx_ref.shape[0]
    nk = ncx + 3 * ATT_BLK

    r = lax.broadcasted_iota(I32, (ATT_BLK, nk), 0)
    c = lax.broadcasted_iota(I32, (ATT_BLK, nk), 1) - ncx
    band = jnp.logical_and(c >= r, c <= r + 2 * WINDOW)
    grp = jnp.where(c < ATT_BLK, prev_i, jnp.where(c < 2 * ATT_BLK, lat_i, next_i))
    mask = jnp.logical_or(c < 0, jnp.logical_and(band, grp > 0))

    kcat = jnp.concatenate([kx_ref[...], kp_ref[...], kc_ref[...], kn_ref[...]], axis=0).astype(F32)
    vcat = jnp.concatenate([vx_ref[...], vp_ref[...], vc_ref[...], vn_ref[...]], axis=0).astype(F32)
    lo = lax.broadcasted_iota(I32, kcat.shape, 1) < HEAD_DIM
    kroll = pltpu.roll(kcat, HEAD_DIM, 1)
    vroll = pltpu.roll(vcat, HEAD_DIM, 1)
    zero = jnp.zeros_like(kcat)
    kvar = [[jnp.where(lo, kcat, zero), jnp.where(lo, zero, kroll)],
            [jnp.where(lo, kroll, zero), jnp.where(lo, zero, kcat)]]
    vvar = [[jnp.where(lo, vcat, zero), jnp.where(lo, zero, vroll)],
            [jnp.where(lo, vroll, zero), jnp.where(lo, zero, vcat)]]
    kvar = [[a.astype(BF16) for a in row] for row in kvar]
    vvar = [[a.astype(BF16) for a in row] for row in vvar]

    for p in range(HEADS // 2):
        qp = q_ref[:, p * LANE:(p + 1) * LANE]
        kh = (2 * p) // (HEADS // KV_HEADS)
        acc = jnp.zeros((ATT_BLK, LANE), F32)
        for rr in range(2):
            sink = sink_ref[2 * p + rr]
            s = lax.dot_general(qp, kvar[kh][rr], (((1,), (1,)), ((), ())), preferred_element_type=F32)
            s = jnp.where(mask, s, NEG_INF)
            m = jnp.maximum(jnp.max(s, axis=1, keepdims=True), sink)
            e = jnp.exp(s - m)
            den = jnp.sum(e, axis=1, keepdims=True) + jnp.exp(sink - m)
            o = jnp.dot(e.astype(BF16), vvar[kh][rr], preferred_element_type=F32)
            acc = acc + o * (1.0 / den)
        o_ref[:, p * LANE:(p + 1) * LANE] = acc.astype(BF16)


def _attention(sink, q, k, v, n_lat):
    nt = q.shape[0]
    nlb = n_lat // ATT_BLK
    cxb = n_lat // TM
    cl = lambda i: jnp.clip(i, 0, nlb - 1)
    kspec = lambda f: pl.BlockSpec((ATT_BLK, KV_COLS), lambda i: (f(i), 0))
    prev = lambda i: jnp.clip(i - 1, 0, nlb - 1)
    nxt = lambda i: jnp.clip(i + 1, 0, nlb - 1)
    cx = pl.BlockSpec((TM, KV_COLS), lambda i: (cxb, 0))
    return pl.pallas_call(
        functools.partial(_attn_body, nlb=nlb),
        out_shape=SDS((nt, Q_COLS), BF16),
        grid=(nt // ATT_BLK,),
        in_specs=[pl.BlockSpec(memory_space=pltpu.SMEM),
                  pl.BlockSpec((ATT_BLK, Q_COLS), lambda i: (i, 0)),
                  kspec(prev), kspec(cl), kspec(nxt), kspec(prev), kspec(cl), kspec(nxt), cx, cx],
        out_specs=pl.BlockSpec((ATT_BLK, Q_COLS), lambda i: (i, 0)),
        compiler_params=_cparams(("parallel",), 32),
        name="attention",
    )(sink, q, k, k, k, v, v, v, k, v)


def _pack2(a, b):
    hi = lax.bitcast_convert_type(a.astype(BF16).astype(F32), U32)
    lo = lax.bitcast_convert_type(b.astype(BF16).astype(F32), U32)
    return hi | (lo >> 16)


def _unpack2(w):
    return (lax.bitcast_convert_type(w & jnp.uint32(0xFFFF0000), F32),
            lax.bitcast_convert_type(w << 16, F32))


def _local_body(pm_ref, pa_ref, pb_ref, sm_ref, sa_ref, sb_ref, hm_ref, ha_ref, hb_ref,
                pw_ref, ps_ref, hw_ref, hbias_ref, sw_ref, band_ref,
                ypool_ref, ysc_ref, x1p_ref, x2p_ref, vp_ref, x1c_ref, x2c_ref, vc_ref,
                pext, sext, hext, *, nlat, n_lat, n_ctx):
    i = pl.program_id(0)
    is_lat = i < nlat
    pf = jnp.where(jnp.logical_and(is_lat, i > 0), 1.0, 0.0)
    nf = jnp.where(jnp.logical_and(is_lat, i < nlat - 1), 1.0, 0.0)
    r = lax.broadcasted_iota(I32, (TM1, 1), 0)
    keep = jnp.where(jnp.logical_or(is_lat, r < n_ctx), 1.0, 0.0)

    def fill(ext, a_ref, m_ref, b_ref):
        ext[0:HALO, :] = a_ref[...].astype(F32) * pf
        ext[HALO:HALO + TM1, :] = m_ref[...].astype(F32) * keep
        ext[HALO + TM1:HALO + TM1 + HALO, :] = b_ref[...].astype(F32) * nf

    def sh(ext, d, c0=None, c1=None):
        if c0 is None:
            return ext[pl.ds(HALO + d, TM1), :]
        return ext[pl.ds(HALO + d, TM1), c0:c1]

    fill(pext, pa_ref, pm_ref, pb_ref)
    t = jnp.where(is_lat, i * TM1, 0) + r
    ln = jnp.where(is_lat, n_lat, n_ctx)
    for g, w in enumerate(POOL_WINDOWS):
        left = w // 2
        right = w - left - 1
        c0, c1 = g * POOL_GW, (g + 1) * POOL_GW
        acc = jnp.dot(band_ref[g], pext[:, c0:c1].astype(BF16), preferred_element_type=F32)
        cnt = jnp.maximum(jnp.minimum(t + right + 1, ln) - jnp.maximum(t - left, 0), 1).astype(F32)
        dpool = acc / cnt - sh(pext, 0, c0, c1)
        y = jnp.dot(dpool.astype(BF16), pw_ref[g], preferred_element_type=F32)
        ypool_ref[:, c0:c1] = (y * ps_ref[:, c0:c1]).astype(BF16)

    sext[0:HALO, :] = (sa_ref[:, BW:2 * BW].astype(F32) * sa_ref[:, 2 * BW:3 * BW].astype(F32)) * pf
    sext[HALO:HALO + TM1, :] = sm_ref[:, BW:2 * BW].astype(F32) * sm_ref[:, 2 * BW:3 * BW].astype(F32) * keep
    sext[HALO + TM1:HALO + TM1 + HALO, :] = (sb_ref[:, BW:2 * BW].astype(F32) * sb_ref[:, 2 * BW:3 * BW].astype(F32)) * nf
    conv = sh(sext, -1) * sw_ref[0:1, :] + sh(sext, 0) * sw_ref[1:2, :] + sh(sext, 1) * sw_ref[2:3, :]
    ysc_ref[...] = (sm_ref[:, 0:BW].astype(F32) * conv).astype(BF16)

    fill(hext, ha_ref, hm_ref, hb_ref)
    nat = (x1c_ref, x2c_ref, vc_ref)
    pk = (x1p_ref, x2p_ref, vp_ref)
    n2c = x1p_ref.shape[2] // FFT_NB
    for j in range(3):
        c0, c1 = j * BW, (j + 1) * BW
        uc = (sh(hext, -1, c0, c1) * hw_ref[0:1, c0:c1] + sh(hext, 0, c0, c1) * hw_ref[1:2, c0:c1]
              + sh(hext, 1, c0, c1) * hw_ref[2:3, c0:c1] + hbias_ref[:, c0:c1])
        nat[j][...] = uc.astype(BF16)
        word = _pack2(uc[:, 0:BW // 2], uc[:, BW // 2:BW])
        for a in range(TM1 // n2c):
            for q in range(BW // 2 // LANE):
                pk[j][q, a // FFT_NB, pl.ds(a % FFT_NB, n2c, stride=FFT_NB), :] = (
                    word[a * n2c:(a + 1) * n2c, q * LANE:(q + 1) * LANE])


def _pool_bands():
    r = jnp.arange(TM1)[:, None]
    c = jnp.arange(TM1 + 2 * HALO)[None, :] - HALO
    return jnp.stack([((c >= r - w // 2) & (c <= r + (w - w // 2 - 1))).astype(BF16) for w in POOL_WINDOWS])


def _local_ops(pool_in, sc_in, hy_in, pool_w_bf, pool_scale, hy_conv_w, hy_conv_b, sc_conv_w, layer, n_lat, n_ctx):
    nt = pool_in.shape[0]
    nlat = n_lat // TM1
    n2c = n_lat // (FFT_N1 // 2)
    per_tile = TM1 // n2c
    nhb = nt // HALO
    per = TM1 // HALO
    main = lambda c: pl.BlockSpec((TM1, c), lambda i: (i, 0))
    before = lambda c: pl.BlockSpec((HALO, c), lambda i: (jnp.maximum(i * per - 1, 0), 0))
    after = lambda c: pl.BlockSpec((HALO, c), lambda i: (jnp.minimum((i + 1) * per, nhb - 1), 0))
    lsel = lambda *blk: pl.BlockSpec((None,) + blk, lambda i: (layer,) + (0,) * len(blk))
    out = SDS((nt, BW), BF16)
    nq, g8 = BW // 2 // LANE, per_tile // FFT_NB
    packed = SDS((nq, (nlat + 1) * g8, n2c * FFT_NB, LANE), U32)
    ctx_out = SDS((TM1, BW), BF16)
    pspec = pl.BlockSpec((nq, g8, n2c * FFT_NB, LANE), lambda i: (0, i, 0, 0))
    cspec = pl.BlockSpec((TM1, BW), lambda i: (0, 0))
    return pl.pallas_call(
        functools.partial(_local_body, nlat=nlat, n_lat=n_lat, n_ctx=n_ctx),
        out_shape=[out, out, packed, packed, packed, ctx_out, ctx_out, ctx_out],
        grid=(nlat + 1,),
        in_specs=[main(BW), before(BW), after(BW),
                  main(3 * BW), before(3 * BW), after(3 * BW),
                  main(3 * BW), before(3 * BW), after(3 * BW),
                  lsel(4, POOL_GW, POOL_GW), lsel(1, BW), lsel(3, 3 * BW), lsel(1, 3 * BW), lsel(3, BW),
                  pl.BlockSpec((len(POOL_WINDOWS), TM1, TM1 + 2 * HALO), lambda i: (0, 0, 0))],
        out_specs=[main(BW), main(BW), pspec, pspec, pspec, cspec, cspec, cspec],
        scratch_shapes=[pltpu.VMEM((TM1 + 2 * HALO, BW), F32), pltpu.VMEM((TM1 + 2 * HALO, BW), F32),
                        pltpu.VMEM((TM1 + 2 * HALO, 3 * BW), F32)],
        compiler_params=_cparams(("arbitrary",), 48),
        name="local_ops",
    )(pool_in, pool_in, pool_in, sc_in, sc_in, sc_in, hy_in, hy_in, hy_in,
      pool_w_bf, pool_scale, hy_conv_w, hy_conv_b, sc_conv_w, _pool_bands())


def _dft_tables(n2):
    n = FFT_N1 * n2
    k1 = jnp.arange(FFT_K1, dtype=I32)
    n1 = jnp.arange(FFT_N1 // 2, dtype=I32)
    ok = (k1 <= FFT_N1 // 2)
    ang = (2.0 * math.pi / FFT_N1) * ((k1[:, None] * n1[None, :]) % FFT_N1).astype(F32)
    c1 = jnp.where(ok[:, None], jnp.cos(ang), 0.0)
    s1 = jnp.where(ok[:, None], jnp.sin(ang), 0.0)
    f1 = jnp.concatenate([c1, -s1], axis=0).astype(BF16)
    wt = jnp.where((k1 == 0) | (k1 == FFT_N1 // 2), 1.0, 2.0) / n
    f1i = jnp.concatenate([(c1 * wt[:, None]).T, (-s1 * wt[:, None]).T], axis=1).astype(BF16)
    if n2 == 1:
        return f1, f1i, None, None
    k2 = jnp.arange(n2, dtype=I32)
    m2 = jnp.arange(n2, dtype=I32)
    kk = k1[:, None, None] + FFT_N1 * k2[None, :, None]
    th = (2.0 * math.pi / n) * ((kk * m2[None, None, :]) % n).astype(F32)
    okb = ok[:, None, None]
    cc = jnp.where(okb, jnp.cos(th), 0.0)
    ss = jnp.where(okb, jnp.sin(th), 0.0)
    g = jnp.concatenate([jnp.concatenate([cc, ss], axis=2), jnp.concatenate([-ss, cc], axis=2)], axis=1)
    return f1, f1i, g.astype(BF16), jnp.swapaxes(g, 1, 2).astype(BF16)


def _filter_body(feat_ref, w1_ref, b1_ref, f1_ref, w2_ref, b2_ref, f2_ref, w3_ref, dl_ref, dft_ref,
                 af_ref, ss_ref, filt):
    i = pl.program_id(0)
    feat = feat_ref[...]
    h = jnp.sin(f1_ref[...] * (jnp.dot(feat, w1_ref[...], precision=HIGHEST, preferred_element_type=F32) + b1_ref[...]))
    h = jnp.sin(f2_ref[...] * (jnp.dot(h, w2_ref[...], precision=HIGHEST, preferred_element_type=F32) + b2_ref[...]))
    raw = jnp.dot(h.astype(BF16), w3_ref[...], preferred_element_type=F32)
    win = jnp.exp(-feat[:, 0:1] * dl_ref[...]) + HY_MOD_SHIFT
    row = lax.broadcasted_iota(I32, (TM, 1), 0) + i * TM

    @pl.when(i == 0)
    def _():
        ss_ref[...] = jnp.zeros_like(ss_ref)

    for j in range(4):
        f = raw[:, j * BW:(j + 1) * BW] * win
        if j >= 2:
            f = jnp.where(row == 0, 0.0, f)
        filt[:, j * BW:(j + 1) * BW] = f.astype(BF16)
        ss_ref[:, j * BW:(j + 1) * BW] += jnp.sum(f * f, axis=0, keepdims=True)
    af_ref[...] = jnp.dot(dft_ref[...], filt[...], preferred_element_type=F32).astype(BF16)


def _hy_filters(feat_perm, w1p, b1, fr1, w2, b2, fr2, w3p_bf, deltas, f1, layer):
    l = feat_perm.shape[0]
    n2 = l // TM
    lsel = lambda *blk: pl.BlockSpec((None,) + blk, lambda i: (layer,) + (0,) * len(blk))
    return pl.pallas_call(
        _filter_body,
        out_shape=[SDS((2 * FFT_K1, n2 * 4 * BW), BF16), SDS((1, 4 * BW), F32)],
        grid=(n2,),
        in_specs=[pl.BlockSpec((TM, LANE), lambda i: (i, 0)),
                  lsel(LANE, HY_HIDDEN), lsel(1, HY_HIDDEN), lsel(1, HY_HIDDEN),
                  lsel(HY_HIDDEN, HY_HIDDEN), lsel(1, HY_HIDDEN), lsel(1, HY_HIDDEN),
                  lsel(HY_HIDDEN, 4 * BW), pl.BlockSpec((1, BW), lambda i: (0, 0)),
                  pl.BlockSpec((2 * FFT_K1, FFT_N1 // 2), lambda i: (0, 0))],
        out_specs=[pl.BlockSpec((2 * FFT_K1, 4 * BW), lambda i: (0, i)), pl.BlockSpec((1, 4 * BW), lambda i: (0, 0))],
        scratch_shapes=[pltpu.VMEM((TM, 4 * BW), BF16)],
        compiler_params=_cparams(("arbitrary",), 32),
        name="hy_filters",
    )(feat_perm, w1p, b1, fr1, w2, b2, fr2, w3p_bf, deltas, f1)


def _dft_major_body(f_ref, x_ref, o_ref):
    o_ref[...] = jnp.dot(f_ref[...], x_ref[...], preferred_element_type=F32).astype(BF16)


def _dft_major(f1, x2d, row_blk, ncols):
    tn = min(FFT_TN, ncols)
    return pl.pallas_call(
        _dft_major_body,
        out_shape=SDS((2 * FFT_K1, ncols), BF16),
        grid=(ncols // tn,),
        in_specs=[pl.BlockSpec((2 * FFT_K1, FFT_N1 // 2), lambda j: (0, 0)),
                  pl.BlockSpec((FFT_N1 // 2, tn), lambda j: (row_blk, j))],
        out_specs=pl.BlockSpec((2 * FFT_K1, tn), lambda j: (0, j)),
        compiler_params=_cparams(("parallel",), 32),
        name="dft_major",
    )(f1, x2d)


def _filter_minor_body(a_ref, g_ref, ss_ref, h_ref, *, n2):
    ssv = ss_ref[...]
    scale = lax.rsqrt(ssv[:, 0:2 * BW] + ssv[:, 2 * BW:4 * BW] + LN_EPS)
    for j in range(a_ref.shape[1]):
        if n2 == 1:
            s = jnp.concatenate([a_ref[0, j], a_ref[1, j]], axis=0).astype(F32)
        else:
            a = jnp.concatenate([a_ref[0, j], a_ref[1, j]], axis=0)
            s = jnp.dot(g_ref[j], a, preferred_element_type=F32)
        sf, sb = s[:, 0:2 * BW], s[:, 2 * BW:4 * BW]
        hr = (sf[0:n2] + sb[0:n2]) * scale
        hi = (sf[n2:2 * n2] - sb[n2:2 * n2]) * scale
        h_ref[j] = jnp.concatenate([hr, hi], axis=0).astype(BF16)


def _conv_minor_body(a_ref, h_ref, g_ref, gt_ref, d_ref, *, n2):
    for j in range(a_ref.shape[1]):
        a = jnp.concatenate([a_ref[0, j], a_ref[1, j]], axis=0)
        if n2 == 1:
            x = a.astype(F32)
        else:
            x = jnp.dot(g_ref[j], a, preferred_element_type=F32)
        h = h_ref[j].astype(F32)
        xr, xi, hr, hi = x[0:n2], x[n2:2 * n2], h[0:n2], h[n2:2 * n2]
        y = jnp.concatenate([xr * hr - xi * hi, xr * hi + xi * hr], axis=0)
        if n2 == 1:
            dv = y
        else:
            dv = jnp.dot(gt_ref[j], y.astype(BF16), preferred_element_type=F32)
        d_ref[0, j] = dv[0:n2].astype(BF16)
        d_ref[1, j] = dv[n2:2 * n2].astype(BF16)


def _filter_minor(af, g, ss, n2):
    a4 = af.reshape(2, FFT_K1, n2, 4 * BW)
    kb = FFT_KB
    gspec = (pl.BlockSpec((kb, 2 * n2, 2 * n2), lambda i: (i, 0, 0)) if n2 > 1
             else pl.BlockSpec((SUB, LANE), lambda i: (0, 0)))
    return pl.pallas_call(
        functools.partial(_filter_minor_body, n2=n2),
        out_shape=SDS((FFT_K1, 2 * n2, 2 * BW), BF16),
        grid=(FFT_K1 // kb,),
        in_specs=[pl.BlockSpec((2, kb, n2, 4 * BW), lambda i: (0, i, 0, 0)), gspec,
                  pl.BlockSpec((1, 4 * BW), lambda i: (0, 0))],
        out_specs=pl.BlockSpec((kb, 2 * n2, 2 * BW), lambda i: (i, 0, 0)),
        compiler_params=_cparams(("parallel",), 40),
        name="hy_filter_minor",
    )(a4, g if n2 > 1 else jnp.zeros((SUB, LANE), BF16), ss)


def _conv_minor(a, hspec, g, gt, order, n2):
    a4 = a.reshape(2, FFT_K1, n2, BW)
    kb = FFT_KB
    dummy = jnp.zeros((SUB, LANE), BF16)
    gspec = (pl.BlockSpec((kb, 2 * n2, 2 * n2), lambda i: (i, 0, 0)) if n2 > 1
             else pl.BlockSpec((SUB, LANE), lambda i: (0, 0)))
    d4 = pl.pallas_call(
        functools.partial(_conv_minor_body, n2=n2),
        out_shape=SDS((2, FFT_K1, n2, BW), BF16),
        grid=(FFT_K1 // kb,),
        in_specs=[pl.BlockSpec((2, kb, n2, BW), lambda i: (0, i, 0, 0)),
                  pl.BlockSpec((kb, 2 * n2, BW), lambda i: (i, 0, order)), gspec, gspec],
        out_specs=pl.BlockSpec((2, kb, n2, BW), lambda i: (0, i, 0, 0)),
        compiler_params=_cparams(("parallel",), 32),
        name="hy_conv_minor",
    )(a4, hspec, g if n2 > 1 else dummy, gt if n2 > 1 else dummy)
    return d4.reshape(2 * FFT_K1, n2 * BW)


def _idft_gate_body(f_ref, d_ref, gate_ref, z_ref, b_ref, o_ref):
    y = jnp.dot(f_ref[...], d_ref[...], preferred_element_type=F32)
    z = z_ref[...].astype(F32)
    o_ref[...] = (gate_ref[...].astype(F32) * (y + b_ref[...] * z)).astype(BF16)


def _idft_gate(f1i, d, gate2d, gate_blk, z2d, z_blk, bias_t, ncols):
    tn = min(FFT_TN, ncols)
    return pl.pallas_call(
        _idft_gate_body,
        out_shape=SDS((FFT_N1 // 2, ncols), BF16),
        grid=(ncols // tn,),
        in_specs=[pl.BlockSpec((FFT_N1 // 2, 2 * FFT_K1), lambda j: (0, 0)),
                  pl.BlockSpec((2 * FFT_K1, tn), lambda j: (0, j)),
                  pl.BlockSpec((FFT_N1 // 2, tn), lambda j: (gate_blk, j)),
                  pl.BlockSpec((FFT_N1 // 2, tn), lambda j: (z_blk, j)),
                  pl.BlockSpec((1, tn), lambda j: (0, 0))],
        out_specs=pl.BlockSpec((FFT_N1 // 2, tn), lambda j: (0, j)),
        compiler_params=_cparams(("parallel",), 32),
        name="idft_gate",
    )(f1i, d, gate2d, z2d, bias_t)


def _hyena(x1, x2, v, feat, tabs, hyw, layer, seq_len, row0):
    n2 = seq_len // (FFT_N1 // 2)
    f1, f1i, g, gt = tabs
    w1p, b1, fr1, w2, b2, fr2, w3p_bf, deltas, hy_bias = hyw
    nt = x1.shape[0]
    af, ss = _hy_filters(feat, w1p, b1, fr1, w2, b2, fr2, w3p_bf, deltas, f1, layer)
    hspec = _filter_minor(af, g, ss, n2)
    view = lambda a: a.reshape(nt // n2, n2 * BW)
    blk = row0 // seq_len if n2 == 1 else 0
    ncols = n2 * BW
    tn = min(FFT_TN, ncols)
    z, zv, zblk = v, view(v), blk
    gates = (x1, x2)
    for o in range(2):
        a = _dft_major(f1, zv, zblk, ncols)
        d = _conv_minor(a, hspec, g, gt, o, n2)
        bias_t = jnp.tile(hy_bias[layer, o][None, :], (1, tn // BW))
        z2 = _idft_gate(f1i, d, view(gates[o]), blk, zv, zblk, bias_t, ncols)
        zv, zblk = z2, 0
    return zv.reshape(seq_len, BW)


def _store_spectrum_slab(ref, j, word):
    for q in range(word.shape[1] // LANE):
        ref[q, pl.ds(j, FFT_K1, stride=FFT_NB), :] = word[:, q * LANE:(q + 1) * LANE]


def _load_spectrum_slab(ref, j):
    return jnp.concatenate([ref[q, pl.ds(j, FFT_K1, stride=FFT_NB), :] for q in range(ref.shape[0])], axis=1)


def _load_spectrum_k1(ref, j):
    nq, nblk = ref.shape[0], ref.shape[1]
    return jnp.concatenate([jnp.concatenate([ref[q, b, j] for q in range(nq)], axis=1) for b in range(nblk)], axis=0)


def _store_spectrum_k1(ref, j, word):
    nq, nblk = ref.shape[0], ref.shape[1]
    for b in range(nblk):
        for q in range(nq):
            ref[q, b, j] = word[b * FFT_NB:(b + 1) * FFT_NB, q * LANE:(q + 1) * LANE]


def _filter_p_body(feat_ref, w1_ref, b1_ref, f1_ref, w2_ref, b2_ref, f2_ref, w3_ref, dl_ref, dft_ref,
                   af_ref, ss_ref, h2):
    nb = pl.program_id(0)
    cb = pl.program_id(1)
    nbk = FFT_NB

    @pl.when(jnp.logical_and(nb == 0, cb == 0))
    def _():
        ss_ref[...] = jnp.zeros_like(ss_ref)

    @pl.when(cb == 0)
    def _():
        feat = feat_ref[...]
        h = jnp.sin(f1_ref[...] * (jnp.dot(feat, w1_ref[...], precision=HIGHEST, preferred_element_type=F32) + b1_ref[...]))
        h = jnp.sin(f2_ref[...] * (jnp.dot(h, w2_ref[...], precision=HIGHEST, preferred_element_type=F32) + b2_ref[...]))
        h2[...] = h.astype(BF16)

    row = lax.broadcasted_iota(I32, (TM, 1), 0)
    lag0_bwd = jnp.logical_and(jnp.logical_and(nb == 0, cb >= 2), row == 0)
    ssum = jnp.zeros((1, BW), F32)
    for j in range(nbk):
        raw = jnp.dot(h2[j * TM:(j + 1) * TM, :], w3_ref[...], preferred_element_type=F32)
        win = jnp.exp(-feat_ref[j * TM:(j + 1) * TM, 0:1] * dl_ref[...]) + HY_MOD_SHIFT
        f = raw * win
        if j == 0:
            f = jnp.where(lag0_bwd, 0.0, f)
        ssum = ssum + jnp.sum(f * f, axis=0, keepdims=True)
        rr = jnp.dot(dft_ref[...], f.astype(BF16), preferred_element_type=F32)
        _store_spectrum_slab(af_ref, j, _pack2(rr[0:FFT_K1], rr[FFT_K1:2 * FFT_K1]))
    ss_ref[cb] = ss_ref[cb] + ssum


def _hy_filters_p(feat_perm, w1p, b1, fr1, w2, b2, fr2, w3p_bf, deltas, f1, layer):
    l = feat_perm.shape[0]
    n2 = l // TM
    nbk = FFT_NB
    lsel = lambda *blk: pl.BlockSpec((None,) + blk, lambda nb, cb: (layer,) + (0,) * len(blk))
    return pl.pallas_call(
        _filter_p_body,
        out_shape=[SDS((4 * BW // LANE, n2 // nbk, FFT_K1 * nbk, LANE), U32), SDS((4, 1, BW), F32)],
        grid=(n2 // nbk, 4),
        in_specs=[pl.BlockSpec((nbk * TM, LANE), lambda nb, cb: (nb, 0)),
                  lsel(LANE, HY_HIDDEN), lsel(1, HY_HIDDEN), lsel(1, HY_HIDDEN),
                  lsel(HY_HIDDEN, HY_HIDDEN), lsel(1, HY_HIDDEN), lsel(1, HY_HIDDEN),
                  pl.BlockSpec((None, HY_HIDDEN, BW), lambda nb, cb: (layer, 0, cb)),
                  pl.BlockSpec((1, BW), lambda nb, cb: (0, 0)),
                  pl.BlockSpec((2 * FFT_K1, FFT_N1 // 2), lambda nb, cb: (0, 0))],
        out_specs=[pl.BlockSpec((BW // LANE, None, FFT_K1 * nbk, LANE), lambda nb, cb: (cb, nb, 0, 0)),
                   pl.BlockSpec((4, 1, BW), lambda nb, cb: (0, 0, 0))],
        scratch_shapes=[pltpu.VMEM((nbk * TM, HY_HIDDEN), BF16)],
        compiler_params=_cparams(("arbitrary", "arbitrary"), 40),
        name="hy_filters_p",
    )(feat_perm, w1p, b1, fr1, w2, b2, fr2, w3p_bf, deltas, f1)


def _filter_minor_p_body(a_ref, g_ref, ss_ref, h_ref, *, n2):
    ssv = ss_ref[...]
    scale = lax.rsqrt(ssv[:, 0:2 * BW] + ssv[:, 2 * BW:4 * BW] + LN_EPS)
    for j in range(a_ref.shape[2]):
        re, im = _unpack2(_load_spectrum_k1(a_ref, j))
        a = jnp.concatenate([re, im], axis=0).astype(BF16)
        s = jnp.dot(g_ref[j], a, preferred_element_type=F32)
        sf, sb = s[:, 0:2 * BW], s[:, 2 * BW:4 * BW]
        hr = (sf[0:n2] + sb[0:n2]) * scale
        hi = (sf[n2:2 * n2] - sb[n2:2 * n2]) * scale
        h_ref[j] = jnp.concatenate([hr, hi], axis=0).astype(BF16)


def _filter_minor_p(af_p, g, ss, n2):
    kb = FFT_KB
    nq, nblk = af_p.shape[0], af_p.shape[1]
    af_p = af_p.reshape(nq, nblk, FFT_K1, FFT_NB, LANE)
    return pl.pallas_call(
        functools.partial(_filter_minor_p_body, n2=n2),
        out_shape=SDS((FFT_K1, 2 * n2, 2 * BW), BF16),
        grid=(FFT_K1 // kb,),
        in_specs=[pl.BlockSpec((nq, nblk, kb, FFT_NB, LANE), lambda i: (0, 0, i, 0, 0)),
                  pl.BlockSpec((kb, 2 * n2, 2 * n2), lambda i: (i, 0, 0)),
                  pl.BlockSpec((1, 4 * BW), lambda i: (0, 0))],
        out_specs=pl.BlockSpec((kb, 2 * n2, 2 * BW), lambda i: (i, 0, 0)),
        compiler_params=_cparams(("parallel",), 40),
        name="hy_filter_minor_p",
    )(af_p, g, ss)


def _unpack_slab(ref, j):
    nq, nblk = ref.shape[0], ref.shape[1]
    word = jnp.concatenate([jnp.concatenate([ref[q, b, j] for q in range(nq)], axis=1) for b in range(nblk)], axis=0)
    hi, lo = _unpack2(word)
    return jnp.concatenate([hi, lo], axis=1)


def _store_slab(ref, j, word):
    nq, nblk = ref.shape[0], ref.shape[1]
    for b in range(nblk):
        for q in range(nq):
            ref[q, b, j] = word[b * FFT_NB:(b + 1) * FFT_NB, q * LANE:(q + 1) * LANE]


def _slab_view(xp, n2):
    nq = xp.shape[0]
    x5 = xp.reshape(nq, xp.shape[1], n2, FFT_NB, LANE)
    spec = pl.BlockSpec((nq, FFT_N1 // 2 // FFT_NB, FFT_NB, FFT_NB, LANE), lambda i: (0, 0, i, 0, 0))
    return x5, spec


def _dft_major_p_body(f_ref, x_ref, a_ref):
    for j in range(x_ref.shape[2]):
        x = _unpack_slab(x_ref, j).astype(BF16)
        rr = jnp.dot(f_ref[...], x, preferred_element_type=F32)
        _store_spectrum_slab(a_ref, j, _pack2(rr[0:FFT_K1], rr[FFT_K1:2 * FFT_K1]))


def _dft_major_p(f1, xp, n2):
    nbk = FFT_NB
    xp, xspec = _slab_view(xp, n2)
    return pl.pallas_call(
        _dft_major_p_body,
        out_shape=SDS((BW // LANE, n2 // nbk, FFT_K1 * nbk, LANE), U32),
        grid=(n2 // nbk,),
        in_specs=[pl.BlockSpec((2 * FFT_K1, FFT_N1 // 2), lambda i: (0, 0)),
                  xspec],
        out_specs=pl.BlockSpec((BW // LANE, None, FFT_K1 * nbk, LANE), lambda i: (0, i, 0, 0)),
        compiler_params=_cparams(("parallel",), 32),
        name="dft_major_p",
    )(f1, xp)


def _conv_minor_p_body(a_ref, h_ref, g_ref, gt_ref, d_ref, *, n2):
    for j in range(a_ref.shape[2]):
        re, im = _unpack2(_load_spectrum_k1(a_ref, j))
        a = jnp.concatenate([re, im], axis=0).astype(BF16)
        x = jnp.dot(g_ref[j], a, preferred_element_type=F32)
        h = h_ref[j].astype(F32)
        xr, xi, hr, hi = x[0:n2], x[n2:2 * n2], h[0:n2], h[n2:2 * n2]
        y = jnp.concatenate([xr * hr - xi * hi, xr * hi + xi * hr], axis=0)
        dv = jnp.dot(gt_ref[j], y.astype(BF16), preferred_element_type=F32)
        _store_spectrum_k1(d_ref, j, _pack2(dv[0:n2], dv[n2:2 * n2]))


def _conv_minor_p(a_p, hspec, g, gt, order, n2):
    kb = FFT_KB
    nq, nblk = a_p.shape[0], a_p.shape[1]
    a_p = a_p.reshape(nq, nblk, FFT_K1, FFT_NB, LANE)
    gspec = pl.BlockSpec((kb, 2 * n2, 2 * n2), lambda i: (i, 0, 0))
    sspec = pl.BlockSpec((nq, nblk, kb, FFT_NB, LANE), lambda i: (0, 0, i, 0, 0))
    return pl.pallas_call(
        functools.partial(_conv_minor_p_body, n2=n2),
        out_shape=SDS((nq, nblk, FFT_K1, FFT_NB, LANE), U32),
        grid=(FFT_K1 // kb,),
        in_specs=[sspec, pl.BlockSpec((kb, 2 * n2, BW), lambda i: (i, 0, order)), gspec, gspec],
        out_specs=sspec,
        compiler_params=_cparams(("parallel",), 32),
        name="hy_conv_minor_p",
    )(a_p, hspec, g, gt)


def _idft_gate_p_body(f_ref, d_ref, gate_ref, z_ref, b_ref, o_ref):
    for j in range(o_ref.shape[2]):
        re, im = _unpack2(_load_spectrum_slab(d_ref, j))
        d = jnp.concatenate([re, im], axis=0).astype(BF16)
        y = jnp.dot(f_ref[...], d, preferred_element_type=F32)
        out = _unpack_slab(gate_ref, j) * (y + b_ref[...] * _unpack_slab(z_ref, j))
        _store_slab(o_ref, j, _pack2(out[:, 0:BW // 2], out[:, BW // 2:BW]))


def _idft_gate_p(f1i, d_p, gate_p, z_p, bias, n2):
    nbk = FFT_NB
    nq, nblk = d_p.shape[0], d_p.shape[1]
    d_p = d_p.reshape(nq, nblk, FFT_K1 * FFT_NB, LANE)
    gate_p, slab = _slab_view(gate_p, n2)
    z_p, _ = _slab_view(z_p, n2)
    nqt = gate_p.shape[0]
    return pl.pallas_call(
        _idft_gate_p_body,
        out_shape=SDS((nqt, FFT_N1 // 2 // FFT_NB, n2, FFT_NB, LANE), U32),
        grid=(n2 // nbk,),
        in_specs=[pl.BlockSpec((FFT_N1 // 2, 2 * FFT_K1), lambda i: (0, 0)),
                  pl.BlockSpec((nq, None, FFT_K1 * FFT_NB, LANE), lambda i: (0, i, 0, 0)),
                  slab, slab, pl.BlockSpec((1, BW), lambda i: (0, 0))],
        out_specs=slab,
        compiler_params=_cparams(("parallel",), 32),
        name="idft_gate_p",
    )(f1i, d_p, gate_p, z_p, bias)


def _hyena_latent(x1p, x2p, vp, feat_perm, tabs, hyw, layer, seq_len):
    n2 = seq_len // (FFT_N1 // 2)
    f1, f1i, g, gt = tabs
    w1p, b1, fr1, w2, b2, fr2, w3p_bf, deltas, hy_bias = hyw
    af_p, ss = _hy_filters_p(feat_perm, w1p, b1, fr1, w2, b2, fr2, w3p_bf, deltas, f1, layer)
    hspec = _filter_minor_p(af_p, g, ss.reshape(1, 4 * BW), n2)
    z = vp
    for o, gate in enumerate((x1p, x2p)):
        a_p = _dft_major_p(f1, z, n2)
        d_p = _conv_minor_p(a_p, hspec, g, gt, o, n2)
        z = _idft_gate_p(f1i, d_p, gate, z, hy_bias[layer, o][None, :], n2)
    return z.reshape(z.shape[0], z.shape[1], n2 * FFT_NB, LANE)


def _merge_body(att_ref, yp_ref, yhp_ref, yhc_ref, ys_ref, hmod_ref, wg_ref, x_ref, wb_ref, wo_ref,
                g1_ref, lg_ref, lb_ref, sh2_ref, sc2_ref, rw_ref, rb_ref,
                x1_ref, m_ref, rt_ref, cnt_ref, *, nlat, n_ctx, alpha):
    i = pl.program_id(0)
    n2c = yhp_ref.shape[2] // FFT_NB
    pieces = []
    for a in range(TM1 // n2c):
        word = jnp.concatenate([yhp_ref[q, a // FFT_NB, pl.ds(a % FFT_NB, n2c, stride=FFT_NB), :]
                                for q in range(yhp_ref.shape[0])], axis=1)
        hi, lo = _unpack2(word)
        pieces.append(jnp.concatenate([hi, lo], axis=1).astype(BF16))
    yh_lat = jnp.concatenate(pieces, axis=0)
    yh_ctx = jnp.concatenate([yhc_ref[...], jnp.zeros((TM1 - n_ctx, BW), BF16)], axis=0)
    yh = jnp.where(i >= nlat, yh_ctx, yh_lat)
    ys = (att_ref[...], yp_ref[...], yh, ys_ref[...])
    half = TM1 // 2
    logit_parts = []
    for hh in range(2):
        rows = slice(hh * half, (hh + 1) * half)
        hm = hmod_ref[rows, :]
        merged = jnp.zeros((half, D), F32)
        for n in range(4):
            br = jnp.dot(ys[n][rows], wb_ref[n], preferred_element_type=F32)
            gl = jnp.dot(hm, wg_ref[:, n * D:(n + 1) * D], preferred_element_type=F32)
            merged = merged + br * (0.5 * jnp.tanh(0.5 * gl) + 0.5)
        y = jnp.dot(merged.astype(BF16), wo_ref[...], preferred_element_type=F32)
        x1 = _layer_norm(alpha * x_ref[rows, :] + g1_ref[...] * y) * lg_ref[...] + lb_ref[...]
        x1_ref[rows, :] = x1
        m = _layer_norm(x1) * (1.0 + sc2_ref[...]) + sh2_ref[...]
        m_ref[rows, :] = m.astype(BF16)
        m_hi = m.astype(BF16)
        m_lo = (m - m_hi.astype(F32)).astype(BF16)
        logit_parts.append(jnp.dot(m_hi, rw_ref[0], preferred_element_type=F32)
                           + jnp.dot(m_lo, rw_ref[0], preferred_element_type=F32)
                           + jnp.dot(m_hi, rw_ref[1], preferred_element_type=F32) + rb_ref[...])
    logits = jnp.concatenate(logit_parts, axis=0)
    lt = logits.T
    le = lt[0:N_EXPERTS]
    lgp = lt[N_EXPERTS:N_EXPERTS + SUB]
    big = 1 << 20
    gi = lax.broadcasted_iota(I32, lgp.shape, 0)
    gmax = jnp.max(lgp, axis=0, keepdims=True)
    gsel = jnp.min(jnp.where(lgp == gmax, gi, big), axis=0, keepdims=True)
    gate_g = 1.0 / jnp.sum(jnp.exp(lgp - gmax), axis=0, keepdims=True)
    ei = lax.broadcasted_iota(I32, le.shape, 0)
    lem = jnp.where(lax.shift_right_logical(ei, 3) == gsel, le, -3.0e38)
    v1 = jnp.max(lem, axis=0, keepdims=True)
    i1 = jnp.min(jnp.where(lem == v1, ei, big), axis=0, keepdims=True)
    lem2 = jnp.where(ei == i1, -3.0e38, lem)
    v2 = jnp.max(lem2, axis=0, keepdims=True)
    i2 = jnp.min(jnp.where(lem2 == v2, ei, big), axis=0, keepdims=True)
    e2 = jnp.exp(v2 - v1)
    wa = gate_g / (1.0 + e2)
    wb = gate_g * e2 / (1.0 + e2)
    tok = lax.broadcasted_iota(I32, (1, TM1), 1)
    valid = jnp.logical_or(i < nlat, tok < n_ctx)
    i1 = jnp.where(valid, i1, -1)
    i2 = jnp.where(valid, i2, -1)
    ri = lax.broadcasted_iota(I32, (SUB, TM1), 0)
    rt = jnp.where(ri == 0, i1.astype(F32), jnp.where(ri == 1, i2.astype(F32),
                   jnp.where(ri == 2, wa, jnp.where(ri == 3, wb, 0.0))))
    rt_ref[...] = rt
    ci = lax.broadcasted_iota(I32, (LANE, TM1), 0)
    oh = jnp.logical_or(ci == i1, ci == i2).astype(BF16)
    cnt_ref[0] = lax.dot_general(jnp.ones((SUB, TM1), BF16), oh, (((1,), (1,)), ((), ())),
                                 preferred_element_type=F32)


def _merge(att, ypool, yhp, yhc, ysc, hmod, wg_bf, x, wb_bf, wo_bf, g1, ln_g, ln_b, sh2, sc2, rw, rb, layer, n_lat,
           n_ctx, alpha):
    nt = x.shape[0]
    nlat = n_lat // TM1
    ntl = nt // TM1
    n2c = yhp.shape[2] // FFT_NB
    sel = lambda i: (jnp.where(i >= nlat, 1, 0), 0, 0)
    row = lambda c: pl.BlockSpec((TM1, c), lambda i: (i, 0))
    lsel = lambda *blk, **kw: pl.BlockSpec((None,) + blk, lambda i: (layer,) + (0,) * len(blk), **kw)
    msel = pl.BlockSpec((None, 1, D), sel)
    return pl.pallas_call(
        functools.partial(_merge_body, nlat=nlat, n_ctx=n_ctx, alpha=alpha),
        out_shape=[SDS((nt, D), F32), SDS((nt, D), BF16), SDS((SUB, nt), F32), SDS((ntl, SUB, LANE), F32)],
        grid=(ntl,),
        in_specs=[row(BW), row(BW),
                  pl.BlockSpec((yhp.shape[0], TM1 // n2c // FFT_NB, n2c * FFT_NB, LANE),
                               lambda i: (0, jnp.minimum(i, nlat - 1), 0, 0)),
                  pl.BlockSpec((n_ctx, BW), lambda i: (0, 0)),
                  row(BW), row(D), lsel(D, 4 * D, pipeline_mode=pl.Buffered(1)), row(D),
                  lsel(4, BW, D, pipeline_mode=pl.Buffered(1)), lsel(D, D, pipeline_mode=pl.Buffered(1)),
                  msel, lsel(1, D), lsel(1, D), msel, msel,
                  lsel(2, D, LANE), lsel(1, LANE)],
        out_specs=[row(D), row(D), pl.BlockSpec((SUB, TM1), lambda i: (0, i)),
                   pl.BlockSpec((1, SUB, LANE), lambda i: (i, 0, 0))],
        compiler_params=_cparams(("parallel",), 56),
        name="merge_router",
    )(att, ypool, yhp, yhc, ysc, hmod, wg_bf, x, wb_bf, wo_bf, g1, ln_g, ln_b, sh2, sc2, rw, rb)


def _piece_loops(np_ref, so_ref, do_ref, j, fn):
    for e in range(N_EXPERTS):
        n = np_ref[j * N_EXPERTS + e]
        so = so_ref[j * N_EXPERTS + e]
        do = do_ref[j * N_EXPERTS + e]

        def body(p, carry, so=so, do=do):
            fn(pl.multiple_of(so + p * MOE_PIECE, MOE_PIECE), pl.multiple_of(do + p * MOE_PIECE, MOE_PIECE))
            return carry

        lax.fori_loop(0, n, body, 0)


def _dispatch_body(np_ref, so_ref, do_ref, nv_ref, m_ref, rt_ref, u_ref, hs_hbm, pos_ref, hs_vmem, sem, *,
                   ntiles, nblk):
    j = pl.program_id(0)
    real = j < ntiles
    rt = rt_ref[...]
    e0 = jnp.where(real, rt[0:1].astype(I32), -1)
    e1 = jnp.where(real, rt[1:2].astype(I32), -1)
    ei = lax.broadcasted_iota(I32, (N_EXPERTS, TM1), 0)
    oh0 = (ei == e0).astype(F32)
    oh1 = (ei == e1).astype(F32)
    c0 = jnp.dot(oh0.astype(BF16), u_ref[...], preferred_element_type=F32)
    c1 = jnp.dot(oh1.astype(BF16), u_ref[...], preferred_element_type=F32)
    n0 = jnp.sum(oh0, axis=1, keepdims=True)
    ecol = lax.broadcasted_iota(I32, (N_EXPERTS, 1), 0)
    toff = jnp.zeros((N_EXPERTS, 1), F32)
    for e in range(N_EXPERTS):
        toff = jnp.where(ecol == e, so_ref[j * N_EXPERTS + e].astype(F32), toff)
    pos0 = jnp.sum(oh0 * (toff + c0), axis=0, keepdims=True)
    pos1 = jnp.sum(oh1 * (toff + n0 + c1), axis=0, keepdims=True)
    pos0 = jnp.where(e0 >= 0, pos0, -1.0)
    pos1 = jnp.where(e1 >= 0, pos1, -1.0)
    ri = lax.broadcasted_iota(I32, (SUB, TM1), 0)
    pos_ref[...] = jnp.where(ri == 0, pos0, jnp.where(ri == 1, pos1, 0.0))
    si = lax.broadcasted_iota(I32, (MOE_S, TM1), 0)
    perm = jnp.logical_or(si == pos0.astype(I32), si == pos1.astype(I32)).astype(BF16)
    hs = jnp.dot(perm, m_ref[...], preferred_element_type=F32)
    hs_vmem[...] = _pack2(hs[:, 0:D // 2], hs[:, D // 2:D])

    def copy(so, do):
        return pltpu.make_async_copy(hs_vmem.at[pl.ds(so, MOE_PIECE)], hs_hbm.at[pl.ds(do, MOE_PIECE)], sem)

    _piece_loops(np_ref, so_ref, do_ref, j, lambda so, do: copy(so, do).start())
    _piece_loops(np_ref, so_ref, do_ref, j, lambda so, do: copy(so, do).wait())

    @pl.when(j == ntiles)
    def _():
        def blk_copy(b):
            return pltpu.make_async_copy(hs_vmem.at[pl.ds(0, MOE_MB)],
                                         hs_hbm.at[pl.ds(pl.multiple_of(b * MOE_MB, MOE_MB), MOE_MB)], sem)

        def start(b, carry):
            blk_copy(b).start()
            return carry

        def wait(b, carry):
            blk_copy(b).wait()
            return carry

        lax.fori_loop(nv_ref[0], nblk, start, 0)
        lax.fori_loop(nv_ref[0], nblk, wait, 0)


def _dispatch(npieces, soff, doff, nvb, m_bf, rt, upper, rmax):
    nt = m_bf.shape[0]
    ntiles = nt // TM1
    last = ntiles - 1
    return pl.pallas_call(
        functools.partial(_dispatch_body, ntiles=ntiles, nblk=rmax // MOE_MB),
        out_shape=[SDS((rmax, D // 2), U32), SDS((SUB, (ntiles + 1) * TM1), F32)],
        grid_spec=pltpu.PrefetchScalarGridSpec(
            num_scalar_prefetch=4,
            grid=(ntiles + 1,),
            in_specs=[pl.BlockSpec((TM1, D), lambda j, *_: (jnp.minimum(j, last), 0)),
                      pl.BlockSpec((SUB, TM1), lambda j, *_: (0, jnp.minimum(j, last))),
                      pl.BlockSpec((TM1, TM1), lambda j, *_: (0, 0))],
            out_specs=[pl.BlockSpec(memory_space=pl.ANY),
                       pl.BlockSpec((SUB, TM1), lambda j, *_: (0, j))],
            scratch_shapes=[pltpu.VMEM((MOE_S, D // 2), U32), pltpu.SemaphoreType.DMA(())]),
        compiler_params=_cparams(("arbitrary",), 40),
        name="moe_dispatch",
    )(npieces, soff, doff, nvb, m_bf, rt, upper)


def _expert_body(be_ref, nv_ref, x_ref, w1_ref, w3_ref, w2_ref, o_ref, w1b, w3b, w2b):
    b = pl.program_id(0)
    valid = b < nv_ref[0]
    prev = be_ref[jnp.maximum(b - 1, 0)]
    fresh = jnp.logical_or(b == 0, be_ref[b] != prev)

    @pl.when(jnp.logical_and(valid, fresh))
    def _():
        w1b[...] = w1_ref[...].astype(BF16)
        w3b[...] = w3_ref[...].astype(BF16)
        w2b[...] = w2_ref[...].astype(BF16)

    @pl.when(valid)
    def _():
        xh, xl = _unpack2(x_ref[...])
        xh, xl = xh.astype(BF16), xl.astype(BF16)
        half = D // 2
        h1 = (jnp.dot(xh, w1b[0:half, :], preferred_element_type=F32)
              + jnp.dot(xl, w1b[half:D, :], preferred_element_type=F32))
        h3 = (jnp.dot(xh, w3b[0:half, :], preferred_element_type=F32)
              + jnp.dot(xl, w3b[half:D, :], preferred_element_type=F32))
        hh = (h1 * (0.5 * jnp.tanh(0.5 * h1) + 0.5)) * h3
        y = jnp.dot(hh.astype(BF16), w2b[...], preferred_element_type=F32)
        o_ref[...] = _pack2(y[:, 0:half], y[:, half:D])

    @pl.when(b >= nv_ref[0])
    def _():
        o_ref[...] = jnp.zeros_like(o_ref)


def _experts(blk_e, nvb, hs, ex_w1, ex_w3, ex_w2, layer):
    rmax = hs.shape[0]
    nb = rmax // MOE_MB
    bi = lambda b, be, nv: jnp.maximum(jnp.minimum(b, nv[0] - 1), 0)
    return pl.pallas_call(
        _expert_body,
        out_shape=SDS((rmax, D // 2), U32),
        grid_spec=pltpu.PrefetchScalarGridSpec(
            num_scalar_prefetch=2,
            grid=(nb,),
            in_specs=[pl.BlockSpec((MOE_MB, D // 2), lambda b, be, nv: (bi(b, be, nv), 0)),
                      pl.BlockSpec((None, None, D, EXPERT_HIDDEN), lambda b, be, nv: (layer, be[bi(b, be, nv)], 0, 0)),
                      pl.BlockSpec((None, None, D, EXPERT_HIDDEN), lambda b, be, nv: (layer, be[bi(b, be, nv)], 0, 0)),
                      pl.BlockSpec((None, None, EXPERT_HIDDEN, D), lambda b, be, nv: (layer, be[bi(b, be, nv)], 0, 0))],
            out_specs=pl.BlockSpec((MOE_MB, D // 2), lambda b, be, nv: (b, 0)),
            scratch_shapes=[pltpu.VMEM((D, EXPERT_HIDDEN), BF16), pltpu.VMEM((D, EXPERT_HIDDEN), BF16),
                            pltpu.VMEM((EXPERT_HIDDEN, D), BF16)]),
        compiler_params=_cparams(("arbitrary",), 48),
        name="moe_experts",
    )(blk_e, nvb, hs, ex_w1, ex_w3, ex_w2)


def _combine_body(np_ref, so_ref, do_ref, ys_hbm, pos_ref, rt_ref, x1_ref, g2_ref, lg_ref, lb_ref,
                  o_ref, ys_vmem, sem, *, alpha):
    j = pl.program_id(0)

    def copy(so, do):
        return pltpu.make_async_copy(ys_hbm.at[pl.ds(do, MOE_PIECE)], ys_vmem.at[pl.ds(so, MOE_PIECE)], sem)

    _piece_loops(np_ref, so_ref, do_ref, j, lambda so, do: copy(so, do).start())
    z = jnp.concatenate([pos_ref[...], rt_ref[...], jnp.zeros((LANE - 2 * SUB, TM1), F32)], axis=0)
    zt = z.T
    p0 = zt[:, 0:1].astype(I32)
    p1 = zt[:, 1:2].astype(I32)
    w0 = zt[:, SUB + 2:SUB + 3]
    w1 = zt[:, SUB + 3:SUB + 4]
    si = lax.broadcasted_iota(I32, (TM1, MOE_S), 1)
    wm = (jnp.where(si == p0, w0, 0.0) + jnp.where(si == p1, w1, 0.0)).astype(BF16)
    _piece_loops(np_ref, so_ref, do_ref, j, lambda so, do: copy(so, do).wait())
    last = j * N_EXPERTS + N_EXPERTS - 1
    total = so_ref[last] + np_ref[last] * MOE_PIECE
    srow = lax.broadcasted_iota(I32, (MOE_S, 1), 0)
    yh, yl = _unpack2(jnp.where(srow < total, ys_vmem[...], jnp.uint32(0)))
    f = jnp.concatenate([jnp.dot(wm, yh.astype(BF16), preferred_element_type=F32),
                         jnp.dot(wm, yl.astype(BF16), preferred_element_type=F32)], axis=1)
    o_ref[...] = _layer_norm(alpha * x1_ref[...] + g2_ref[...] * f) * lg_ref[...] + lb_ref[...]


def _combine(npieces, soff, doff, ys, pos, rt, x1, g2, ln_g, ln_b, layer, n_lat, n_out_tiles, n_out_rows, alpha):
    nlat = n_lat // TM1
    sel = lambda j, *_: (jnp.where(j >= nlat, 1, 0), 0, 0)
    lsel = lambda *blk: pl.BlockSpec((None,) + blk, lambda j, *_: (layer,) + (0,) * len(blk))
    return pl.pallas_call(
        functools.partial(_combine_body, alpha=alpha),
        out_shape=SDS((n_out_rows, D), F32),
        grid_spec=pltpu.PrefetchScalarGridSpec(
            num_scalar_prefetch=3,
            grid=(n_out_tiles,),
            in_specs=[pl.BlockSpec(memory_space=pl.ANY),
                      pl.BlockSpec((SUB, TM1), lambda j, *_: (0, j)),
                      pl.BlockSpec((SUB, TM1), lambda j, *_: (0, j)),
                      pl.BlockSpec((TM1, D), lambda j, *_: (j, 0)),
                      pl.BlockSpec((None, 1, D), sel), lsel(1, D), lsel(1, D)],
            out_specs=pl.BlockSpec((TM1, D), lambda j, *_: (j, 0)),
            scratch_shapes=[pltpu.VMEM((MOE_S, D // 2), U32), pltpu.SemaphoreType.DMA(())]),
        compiler_params=_cparams(("arbitrary",), 40),
        name="moe_combine",
    )(npieces, soff, doff, ys, pos, rt, x1, g2, ln_g, ln_b)


def _moe_plan(cnt, ntiles):
    c = cnt[:, 0, :N_EXPERTS].astype(I32)
    pad8 = (c + MOE_PIECE - 1) // MOE_PIECE * MOE_PIECE
    toff = jnp.cumsum(pad8, axis=1) - pad8
    tot = pad8.sum(axis=0)
    totb = (tot + MOE_MB - 1) // MOE_MB * MOE_MB
    ends = jnp.cumsum(totb)
    base = ends - totb
    dest = base[None, :] + jnp.cumsum(pad8, axis=0) - pad8
    npieces = jnp.concatenate([pad8 // MOE_PIECE, ((totb - tot) // MOE_PIECE)[None, :]], axis=0)
    soff = jnp.concatenate([toff, jnp.zeros((1, N_EXPERTS), I32)], axis=0)
    doff = jnp.concatenate([dest, (base + tot)[None, :]], axis=0)
    return npieces.reshape(-1), soff.reshape(-1), doff.reshape(-1), ends


def _rope_tables(n_lat, nt):
    t = jnp.arange(n_lat)
    pos = jnp.stack([(t // GRID_W).astype(F32), (t % GRID_W).astype(F32)], axis=1)
    inv = ROPE_BASE ** (-jnp.arange(ROPE_PAIRS, dtype=F32) / ROPE_PAIRS)
    ang = pos[:, :, None] * inv[None, None, :]
    cos = jnp.repeat(jnp.cos(ang), 2, axis=1).reshape(n_lat, 2, 2, ROPE_PAIRS)
    sin = jnp.sin(ang)
    sin = jnp.stack([-sin, sin], axis=2)
    cos = jnp.tile(cos.reshape(n_lat, HEAD_DIM), (1, LANE // HEAD_DIM))
    sin = jnp.tile(sin.reshape(n_lat, HEAD_DIM), (1, LANE // HEAD_DIM))
    cos = jnp.concatenate([cos, jnp.ones((nt - n_lat, LANE), F32)], axis=0)
    sin = jnp.concatenate([sin, jnp.zeros((nt - n_lat, LANE), F32)], axis=0)
    return cos, sin


def _hy_features(l):
    t01 = jnp.linspace(0.0, 1.0, l, dtype=F32)
    fr = jnp.linspace(1e-4, HY_BANDS - 1, HY_BANDS, dtype=F32)
    ang = (2.0 * math.pi / l) * jnp.arange(l, dtype=F32)[:, None] * fr[None, :]
    feat = jnp.concatenate([t01[:, None], jnp.cos(ang), -jnp.sin(ang)], axis=-1)
    feat = jnp.pad(feat, ((0, 0), (0, LANE - HY_EMB)))
    return feat.reshape(FFT_N1 // 2, l // (FFT_N1 // 2), LANE).transpose(1, 0, 2).reshape(l, LANE)


def kernel(x, c, ctx, c_ctx, ada_w, ada_b, w_in, attn_sink, pool_w, pool_scale, hy_conv_w, hy_conv_b, hy_w1, hy_b1, hy_freq1, hy_w2, hy_b2, hy_freq2, hy_w3, hy_bias, sc_conv_w, w_branch, w_out, ln1_g, ln1_b, ln2_g, ln2_b, rg_w, rg_b, re_w, re_b, ex_w1, ex_w3, ex_w2):
    depth = w_in.shape[0]
    assert x.shape[0] == 1 and ctx.shape[0] == 1 and x.shape[2] == D
    n_lat = x.shape[1]
    n_ctx = ctx.shape[1]
    assert n_ctx == TM and n_lat % TM1 == 0 and n_lat >= 2 * TM1
    nt = n_lat + n_ctx + PADR
    ntiles1 = nt // TM1
    alpha = (2 * depth) ** 0.25

    cos_t, sin_t = _rope_tables(n_lat, nt)
    w_main_bf = w_in[:, :, :MAIN_COLS].astype(BF16)
    wg_bf = w_in[:, :, MAIN_COLS:].astype(BF16)
    wb_bf = w_branch.astype(BF16)
    wo_bf = w_out.astype(BF16)
    pool_w_bf = pool_w.astype(BF16)
    r3 = lambda a: a.reshape(depth, 1, a.shape[-1])
    tabs_l = _dft_tables(n_lat // (FFT_N1 // 2))
    tabs_c = _dft_tables(n_ctx // (FFT_N1 // 2))
    feat_l, feat_c = _hy_features(n_lat), _hy_features(n_ctx)
    deltas = jnp.abs(jnp.linspace(math.log(HY_TARGET) / HY_SLOW_PCT, math.log(HY_TARGET) / HY_FAST_PCT,
                                  BW, dtype=F32))[None, :]
    w1p = jnp.pad(hy_w1, ((0, 0), (0, LANE - HY_EMB), (0, 0)))
    w3p_bf = hy_w3.reshape(depth, HY_HIDDEN, 2, 2, BW).transpose(0, 1, 3, 2, 4).reshape(depth, HY_HIDDEN, 4 * BW).astype(BF16)
    hyw = (w1p, r3(hy_b1), r3(hy_freq1), hy_w2, r3(hy_b2), r3(hy_freq2), w3p_bf, deltas, hy_bias)
    rw = jnp.concatenate([re_w, rg_w, jnp.zeros((depth, D, LANE - N_EXPERTS - N_GROUPS), F32)], axis=2)
    rw_hi = rw.astype(BF16)
    rw = jnp.stack([rw_hi, (rw - rw_hi.astype(F32)).astype(BF16)], axis=1)
    rb = jnp.concatenate([re_b, rg_b, jnp.full((depth, LANE - N_EXPERTS - N_GROUPS), NEG_INF, F32)], axis=1)
    rb = rb.reshape(depth, 1, LANE)
    upper = (jnp.arange(TM1)[:, None] < jnp.arange(TM1)[None, :]).astype(BF16)
    rmax = -(-(2 * (n_lat + n_ctx) + ntiles1 * N_EXPERTS * (MOE_PIECE - 1) + N_EXPERTS * (MOE_MB - 1)) // MOE_MB) * MOE_MB
    nblk = rmax // MOE_MB

    s8 = jnp.concatenate([c, c_ctx[None, :], jnp.zeros((SUB - 2, D), F32)], axis=0)
    mod = _mod_all(s8, ada_w, ada_b.reshape(depth, 1, 6 * D))

    xs = jnp.concatenate([x[0], ctx[0], jnp.zeros((PADR, D), F32)], axis=0)
    for i in range(depth):
        last = i == depth - 1
        mp = lambda j: mod[i, 0:2, j * D:(j + 1) * D].reshape(2, 1, D)
        sh1, sc1, g1, sh2, sc2, g2 = (mp(j) for j in range(6))

        q, k, v, pool_in, hy_in, sc_in, hmod = _inproj(xs, sh1, sc1, cos_t, sin_t, w_main_bf, i, n_lat // TM1)
        att = _attention(attn_sink[i], q, k, v, n_lat)
        ypool, ysc, x1p, x2p, vp, x1c, x2c, vc = _local_ops(
            pool_in, sc_in, hy_in, pool_w_bf, pool_scale.reshape(depth, 1, BW),
            hy_conv_w, hy_conv_b.reshape(depth, 1, 3 * BW), sc_conv_w, i, n_lat, n_ctx)
        yhp = _hyena_latent(x1p, x2p, vp, feat_l, tabs_l, hyw, i, n_lat)
        if last:
            yhc = jnp.zeros((n_ctx, BW), BF16)
        else:
            yhc = _hyena(x1c, x2c, vc, feat_c, tabs_c, hyw, i, n_ctx, 0)
        x1, m_bf, rt, cnt = _merge(att, ypool, yhp, yhc, ysc, hmod, wg_bf, xs, wb_bf, wo_bf, g1, r3(ln1_g), r3(ln1_b),
                                   sh2, sc2, rw, rb, i, n_lat, n_ctx, alpha)

        npieces, soff, doff, ends = _moe_plan(cnt, ntiles1)
        nvb = (ends[-1] // MOE_MB).astype(I32).reshape(1)
        blk_start = jnp.arange(nblk, dtype=I32) * MOE_MB
        blk_e = jnp.minimum(jnp.sum((ends[None, :] <= blk_start[:, None]).astype(I32), axis=1), N_EXPERTS - 1)
        hs, pos = _dispatch(npieces, soff, doff, nvb, m_bf, rt, upper, rmax)
        ys = _experts(blk_e, nvb, hs, ex_w1, ex_w3, ex_w2, i)
        if last:
            xs = _combine(npieces, soff, doff, ys, pos, rt, x1, g2, r3(ln2_g), r3(ln2_b), i, n_lat,
                          n_lat // TM1, n_lat, alpha)
        else:
            xs = _combine(npieces, soff, doff, ys, pos, rt, x1, g2, r3(ln2_g), r3(ln2_b), i, n_lat, ntiles1, nt,
                          alpha)
    return xs[None]
```

```python
import functools
import math

import jax
import jax.numpy as jnp
from jax import lax
from jax.experimental import pallas as pl
from jax.experimental.pallas import tpu as pltpu

F32 = jnp.float32
BF16 = jnp.bfloat16
I32 = jnp.int32
U32 = jnp.uint32
SDS = jax.ShapeDtypeStruct
HIGHEST = lax.Precision.HIGHEST

D = 1024
GRID_W = 64
HEADS = 8
KV_HEADS = 2
HEAD_DIM = 64
WINDOW = 128
ATT_BLK = 128
ROPE_PAIRS = 16
ROPE_BASE = 10000.0
BW = 512
POOL_WINDOWS = (2, 4, 8, 16)
POOL_GW = 128
HY_EMB = 33
HY_BANDS = 16
HY_HIDDEN = 64
HY_TARGET = 1e-2
HY_FAST_PCT = 0.3
HY_SLOW_PCT = 1.5
HY_MOD_SHIFT = 0.05
Q_COLS = HEADS * HEAD_DIM
KV_COLS = KV_HEADS * HEAD_DIM
MAIN_COLS = Q_COLS + 2 * KV_COLS + BW + 3 * BW + 3 * BW
N_GROUPS = 4
EPG = 8
N_EXPERTS = 32
EXPERT_HIDDEN = 512
LN_EPS = 1e-6
NEG_INF = -1e30

LANE = 128
SUB = 8
BSUB = 16
TM = 256
TM1 = 512
PADR = 256
HALO = 16
FFT_N1 = 512
FFT_K1 = 264
FFT_KB = 8
FFT_TN = 2048
FFT_NB = 8
MOE_PIECE = 8
MOE_MB = 256
MOE_S = 2 * TM1 + 256


def _cparams(sem, vmem_mb):
    return pltpu.CompilerParams(dimension_semantics=sem, vmem_limit_bytes=vmem_mb * 2 ** 20)


def _dot_bf16x3(a, b):
    a_hi = a.astype(BF16)
    a_lo = (a - a_hi.astype(F32)).astype(BF16)
    b_hi = b.astype(BF16)
    b_lo = (b - b_hi.astype(F32)).astype(BF16)
    return (jnp.dot(a_hi, b_hi, preferred_element_type=F32) + jnp.dot(a_lo, b_hi, preferred_element_type=F32)
            + jnp.dot(a_hi, b_lo, preferred_element_type=F32))


def _layer_norm(x):
    mu = jnp.mean(x, axis=-1, keepdims=True)
    xc = x - mu
    var = jnp.mean(xc * xc, axis=-1, keepdims=True)
    return xc * lax.rsqrt(var + LN_EPS)


def _mod_body(s_ref, w_ref, b_ref, o_ref):
    s = s_ref[...]
    s = s * (1.0 / (1.0 + jnp.exp(-s)))
    o_ref[...] = jnp.dot(s.astype(BF16), w_ref[...].astype(BF16), preferred_element_type=F32) + b_ref[...]


def _mod_all(s8, ada_w, ada_b3):
    depth = ada_w.shape[0]
    tn = 1024
    return pl.pallas_call(
        _mod_body,
        out_shape=SDS((depth, SUB, 6 * D), F32),
        grid=(depth, 6 * D // tn),
        in_specs=[pl.BlockSpec((SUB, D), lambda l, j: (0, 0)),
                  pl.BlockSpec((None, D, tn), lambda l, j: (l, 0, j)),
                  pl.BlockSpec((None, 1, tn), lambda l, j: (l, 0, j))],
        out_specs=pl.BlockSpec((None, SUB, tn), lambda l, j: (l, 0, j)),
        compiler_params=_cparams(("parallel", "parallel"), 32),
        name="ada_mod",
    )(s8, ada_w, ada_b3)


def _inproj_body(x_ref, sh_ref, sc_ref, cos_ref, sin_ref, w_ref,
                 q_ref, k_ref, v_ref, pool_ref, hy_ref, scv_ref, hmod_ref):
    h = _layer_norm(x_ref[...]) * (1.0 + sc_ref[...]) + sh_ref[...]
    hb = h.astype(BF16)

    def mm(a, b):
        return jnp.dot(hb, w_ref[:, a:b], preferred_element_type=F32)

    cs = cos_ref[...]
    sn = sin_ref[...]
    lane = lax.broadcasted_iota(I32, cs.shape, 1)
    first = (lane & ROPE_PAIRS) == 0

    def rope(t):
        sw = jnp.where(first, pltpu.roll(t, LANE - ROPE_PAIRS, 1), pltpu.roll(t, ROPE_PAIRS, 1))
        return t * cs + sw * sn

    for j in range(Q_COLS // LANE):
        q_ref[:, j * LANE:(j + 1) * LANE] = (rope(mm(j * LANE, (j + 1) * LANE)) * (HEAD_DIM ** -0.5)).astype(BF16)
    o = Q_COLS
    k_ref[...] = rope(mm(o, o + KV_COLS)).astype(BF16)
    o += KV_COLS
    v_ref[...] = mm(o, o + KV_COLS).astype(BF16)
    o += KV_COLS
    pool_ref[...] = mm(o, o + BW).astype(BF16)
    o += BW
    for j in range(3):
        hy_ref[:, j * BW:(j + 1) * BW] = mm(o + j * BW, o + (j + 1) * BW).astype(BF16)
    o += 3 * BW
    for j in range(3):
        scv_ref[:, j * BW:(j + 1) * BW] = mm(o + j * BW, o + (j + 1) * BW).astype(BF16)
    hmod_ref[...] = hb


def _inproj(x, sh, sc, cos_t, sin_t, w_in_bf, layer, n_lat_tiles):
    nt = x.shape[0]
    sel = lambda i: (jnp.where(i >= n_lat_tiles, 1, 0), 0, 0)
    row = lambda c: pl.BlockSpec((TM1, c), lambda i: (i, 0))
    outs = [SDS((nt, Q_COLS), BF16), SDS((nt, KV_COLS), BF16), SDS((nt, KV_COLS), BF16), SDS((nt, BW), BF16),
            SDS((nt, 3 * BW), BF16), SDS((nt, 3 * BW), BF16), SDS((nt, D), BF16)]
    return pl.pallas_call(
        _inproj_body,
        out_shape=outs,
        grid=(nt // TM1,),
        in_specs=[row(D),
                  pl.BlockSpec((None, 1, D), sel), pl.BlockSpec((None, 1, D), sel),
                  row(LANE), row(LANE),
                  pl.BlockSpec((None, D, MAIN_COLS), lambda i: (layer, 0, 0), pipeline_mode=pl.Buffered(1))],
        out_specs=[row(Q_COLS), row(KV_COLS), row(KV_COLS), row(BW), row(3 * BW), row(3 * BW), row(D)],
        compiler_params=_cparams(("parallel",), 58),
        name="inproj",
    )(x, sh, sc, cos_t, sin_t, w_in_bf)


def _attn_body(sink_ref, q_ref, kp_ref, kc_ref, kn_ref, vp_ref, vc_ref, vn_ref, kx_ref, vx_ref, o_ref, *, nlb):
    i = pl.program_id(0)
    is_lat = i < nlb
    lat_i = jnp.where(is_lat, 1, 0)
    prev_i = jnp.where(jnp.logical_and(is_lat, i > 0), 1, 0)
    next_i = jnp.where(jnp.logical_and(is_lat, i < nlb - 1), 1, 0)
    ncx = kx_ref.shape[0]
    nk = ncx + 3 * ATT_BLK

    r = lax.broadcasted_iota(I32, (ATT_BLK, nk), 0)
    c = lax.broadcasted_iota(I32, (ATT_BLK, nk), 1) - ncx
    band = jnp.logical_and(c >= r, c <= r + 2 * WINDOW)
    grp = jnp.where(c < ATT_BLK, prev_i, jnp.where(c < 2 * ATT_BLK, lat_i, next_i))
    mask = jnp.logical_or(c < 0, jnp.logical_and(band, grp > 0))

    kcat = jnp.concatenate([kx_ref[...], kp_ref[...], kc_ref[...], kn_ref[...]], axis=0).astype(F32)
    vcat = jnp.concatenate([vx_ref[...], vp_ref[...], vc_ref[...], vn_ref[...]], axis=0).astype(F32)
    lo = lax.broadcasted_iota(I32, kcat.shape, 1) < HEAD_DIM
    kroll = pltpu.roll(kcat, HEAD_DIM, 1)
    vroll = pltpu.roll(vcat, HEAD_DIM, 1)
    zero = jnp.zeros_like(kcat)
    kvar = [[jnp.where(lo, kcat, zero), jnp.where(lo, zero, kroll)],
            [jnp.where(lo, kroll, zero), jnp.where(lo, zero, kcat)]]
    vvar = [[jnp.where(lo, vcat, zero), jnp.where(lo, zero, vroll)],
            [jnp.where(lo, vroll, zero), jnp.where(lo, zero, vcat)]]
    kvar = [[a.astype(BF16) for a in row] for row in kvar]
    vvar = [[a.astype(BF16) for a in row] for row in vvar]

    for p in range(HEADS // 2):
        qp = q_ref[:, p * LANE:(p + 1) * LANE]
        kh = (2 * p) // (HEADS // KV_HEADS)
        acc = jnp.zeros((ATT_BLK, LANE), F32)
        for rr in range(2):
            sink = sink_ref[2 * p + rr]
            s = lax.dot_general(qp, kvar[kh][rr], (((1,), (1,)), ((), ())), preferred_element_type=F32)
            s = jnp.where(mask, s, NEG_INF)
            m = jnp.maximum(jnp.max(s, axis=1, keepdims=True), sink)
            e = jnp.exp(s - m)
            den = jnp.sum(e, axis=1, keepdims=True) + jnp.exp(sink - m)
            o = jnp.dot(e.astype(BF16), vvar[kh][rr], preferred_element_type=F32)
            acc = acc + o * (1.0 / den)
        o_ref[:, p * LANE:(p + 1) * LANE] = acc.astype(BF16)


def _attention(sink, q, k, v, n_lat):
    nt = q.shape[0]
    nlb = n_lat // ATT_BLK
    cxb = n_lat // TM
    cl = lambda i: jnp.clip(i, 0, nlb - 1)
    kspec = lambda f: pl.BlockSpec((ATT_BLK, KV_COLS), lambda i: (f(i), 0))
    prev = lambda i: jnp.clip(i - 1, 0, nlb - 1)
    nxt = lambda i: jnp.clip(i + 1, 0, nlb - 1)
    cx = pl.BlockSpec((TM, KV_COLS), lambda i: (cxb, 0))
    return pl.pallas_call(
        functools.partial(_attn_body, nlb=nlb),
        out_shape=SDS((nt, Q_COLS), BF16),
        grid=(nt // ATT_BLK,),
        in_specs=[pl.BlockSpec(memory_space=pltpu.SMEM),
                  pl.BlockSpec((ATT_BLK, Q_COLS), lambda i: (i, 0)),
                  kspec(prev), kspec(cl), kspec(nxt), kspec(prev), kspec(cl), kspec(nxt), cx, cx],
        out_specs=pl.BlockSpec((ATT_BLK, Q_COLS), lambda i: (i, 0)),
        compiler_params=_cparams(("parallel",), 32),
        name="attention",
    )(sink, q, k, k, k, v, v, v, k, v)


def _pack2(a, b):
    hi = lax.bitcast_convert_type(a.astype(BF16).astype(F32), U32)
    lo = lax.bitcast_convert_type(b.astype(BF16).astype(F32), U32)
    return hi | (lo >> 16)


def _unpack2(w):
    return (lax.bitcast_convert_type(w & jnp.uint32(0xFFFF0000), F32),
            lax.bitcast_convert_type(w << 16, F32))


def _local_body(pm_ref, pa_ref, pb_ref, sm_ref, sa_ref, sb_ref, hm_ref, ha_ref, hb_ref,
                pw_ref, ps_ref, hw_ref, hbias_ref, sw_ref, band_ref,
                ypool_ref, ysc_ref, x1p_ref, x2p_ref, vp_ref, x1c_ref, x2c_ref, vc_ref,
                pext, sext, hext, *, nlat, n_lat, n_ctx):
    i = pl.program_id(0)
    is_lat = i < nlat
    pf = jnp.where(jnp.logical_and(is_lat, i > 0), 1.0, 0.0)
    nf = jnp.where(jnp.logical_and(is_lat, i < nlat - 1), 1.0, 0.0)
    r = lax.broadcasted_iota(I32, (TM1, 1), 0)
    keep = jnp.where(jnp.logical_or(is_lat, r < n_ctx), 1.0, 0.0)

    def fill(ext, a_ref, m_ref, b_ref):
        ext[0:HALO, :] = a_ref[...].astype(F32) * pf
        ext[HALO:HALO + TM1, :] = m_ref[...].astype(F32) * keep
        ext[HALO + TM1:HALO + TM1 + HALO, :] = b_ref[...].astype(F32) * nf

    def sh(ext, d, c0=None, c1=None):
        if c0 is None:
            return ext[pl.ds(HALO + d, TM1), :]
        return ext[pl.ds(HALO + d, TM1), c0:c1]

    fill(pext, pa_ref, pm_ref, pb_ref)
    t = jnp.where(is_lat, i * TM1, 0) + r
    ln = jnp.where(is_lat, n_lat, n_ctx)
    for g, w in enumerate(POOL_WINDOWS):
        left = w // 2
        right = w - left - 1
        c0, c1 = g * POOL_GW, (g + 1) * POOL_GW
        acc = jnp.dot(band_ref[g], pext[:, c0:c1].astype(BF16), preferred_element_type=F32)
        cnt = jnp.maximum(jnp.minimum(t + right + 1, ln) - jnp.maximum(t - left, 0), 1).astype(F32)
        dpool = acc / cnt - sh(pext, 0, c0, c1)
        y = jnp.dot(dpool.astype(BF16), pw_ref[g], preferred_element_type=F32)
        ypool_ref[:, c0:c1] = (y * ps_ref[:, c0:c1]).astype(BF16)

    sext[0:HALO, :] = (sa_ref[:, BW:2 * BW].astype(F32) * sa_ref[:, 2 * BW:3 * BW].astype(F32)) * pf
    sext[HALO:HALO + TM1, :] = sm_ref[:, BW:2 * BW].astype(F32) * sm_ref[:, 2 * BW:3 * BW].astype(F32) * keep
    sext[HALO + TM1:HALO + TM1 + HALO, :] = (sb_ref[:, BW:2 * BW].astype(F32) * sb_ref[:, 2 * BW:3 * BW].astype(F32)) * nf
    conv = sh(sext, -1) * sw_ref[0:1, :] + sh(sext, 0) * sw_ref[1:2, :] + sh(sext, 1) * sw_ref[2:3, :]
    ysc_ref[...] = (sm_ref[:, 0:BW].astype(F32) * conv).astype(BF16)

    fill(hext, ha_ref, hm_ref, hb_ref)
    nat = (x1c_ref, x2c_ref, vc_ref)
    pk = (x1p_ref, x2p_ref, vp_ref)
    n2c = x1p_ref.shape[2] // FFT_NB
    for j in range(3):
        c0, c1 = j * BW, (j + 1) * BW
        uc = (sh(hext, -1, c0, c1) * hw_ref[0:1, c0:c1] + sh(hext, 0, c0, c1) * hw_ref[1:2, c0:c1]
              + sh(hext, 1, c0, c1) * hw_ref[2:3, c0:c1] + hbias_ref[:, c0:c1])
        nat[j][...] = uc.astype(BF16)
        word = _pack2(uc[:, 0:BW // 2], uc[:, BW // 2:BW])
        for a in range(TM1 // n2c):
            for q in range(BW // 2 // LANE):
                pk[j][q, a // FFT_NB, pl.ds(a % FFT_NB, n2c, stride=FFT_NB), :] = (
                    word[a * n2c:(a + 1) * n2c, q * LANE:(q + 1) * LANE])


def _pool_bands():
    r = jnp.arange(TM1)[:, None]
    c = jnp.arange(TM1 + 2 * HALO)[None, :] - HALO
    return jnp.stack([((c >= r - w // 2) & (c <= r + (w - w // 2 - 1))).astype(BF16) for w in POOL_WINDOWS])


def _local_ops(pool_in, sc_in, hy_in, pool_w_bf, pool_scale, hy_conv_w, hy_conv_b, sc_conv_w, layer, n_lat, n_ctx):
    nt = pool_in.shape[0]
    nlat = n_lat // TM1
    n2c = n_lat // (FFT_N1 // 2)
    per_tile = TM1 // n2c
    nhb = nt // HALO
    per = TM1 // HALO
    main = lambda c: pl.BlockSpec((TM1, c), lambda i: (i, 0))
    before = lambda c: pl.BlockSpec((HALO, c), lambda i: (jnp.maximum(i * per - 1, 0), 0))
    after = lambda c: pl.BlockSpec((HALO, c), lambda i: (jnp.minimum((i + 1) * per, nhb - 1), 0))
    lsel = lambda *blk: pl.BlockSpec((None,) + blk, lambda i: (layer,) + (0,) * len(blk))
    out = SDS((nt, BW), BF16)
    nq, g8 = BW // 2 // LANE, per_tile // FFT_NB
    packed = SDS((nq, (nlat + 1) * g8, n2c * FFT_NB, LANE), U32)
    ctx_out = SDS((TM1, BW), BF16)
    pspec = pl.BlockSpec((nq, g8, n2c * FFT_NB, LANE), lambda i: (0, i, 0, 0))
    cspec = pl.BlockSpec((TM1, BW), lambda i: (0, 0))
    return pl.pallas_call(
        functools.partial(_local_body, nlat=nlat, n_lat=n_lat, n_ctx=n_ctx),
        out_shape=[out, out, packed, packed, packed, ctx_out, ctx_out, ctx_out],
        grid=(nlat + 1,),
        in_specs=[main(BW), before(BW), after(BW),
                  main(3 * BW), before(3 * BW), after(3 * BW),
                  main(3 * BW), before(3 * BW), after(3 * BW),
                  lsel(4, POOL_GW, POOL_GW), lsel(1, BW), lsel(3, 3 * BW), lsel(1, 3 * BW), lsel(3, BW),
                  pl.BlockSpec((len(POOL_WINDOWS), TM1, TM1 + 2 * HALO), lambda i: (0, 0, 0))],
        out_specs=[main(BW), main(BW), pspec, pspec, pspec, cspec, cspec, cspec],
        scratch_shapes=[pltpu.VMEM((TM1 + 2 * HALO, BW), F32), pltpu.VMEM((TM1 + 2 * HALO, BW), F32),
                        pltpu.VMEM((TM1 + 2 * HALO, 3 * BW), F32)],
        compiler_params=_cparams(("arbitrary",), 48),
        name="local_ops",
    )(pool_in, pool_in, pool_in, sc_in, sc_in, sc_in, hy_in, hy_in, hy_in,
      pool_w_bf, pool_scale, hy_conv_w, hy_conv_b, sc_conv_w, _pool_bands())


def _dft_tables(n2):
    n = FFT_N1 * n2
    k1 = jnp.arange(FFT_K1, dtype=I32)
    n1 = jnp.arange(FFT_N1 // 2, dtype=I32)
    ok = (k1 <= FFT_N1 // 2)
    ang = (2.0 * math.pi / FFT_N1) * ((k1[:, None] * n1[None, :]) % FFT_N1).astype(F32)
    c1 = jnp.where(ok[:, None], jnp.cos(ang), 0.0)
    s1 = jnp.where(ok[:, None], jnp.sin(ang), 0.0)
    f1 = jnp.concatenate([c1, -s1], axis=0).astype(BF16)
    wt = jnp.where((k1 == 0) | (k1 == FFT_N1 // 2), 1.0, 2.0) / n
    f1i = jnp.concatenate([(c1 * wt[:, None]).T, (-s1 * wt[:, None]).T], axis=1).astype(BF16)
    if n2 == 1:
        return f1, f1i, None, None
    k2 = jnp.arange(n2, dtype=I32)
    m2 = jnp.arange(n2, dtype=I32)
    kk = k1[:, None, None] + FFT_N1 * k2[None, :, None]
    th = (2.0 * math.pi / n) * ((kk * m2[None, None, :]) % n).astype(F32)
    okb = ok[:, None, None]
    cc = jnp.where(okb, jnp.cos(th), 0.0)
    ss = jnp.where(okb, jnp.sin(th), 0.0)
    g = jnp.concatenate([jnp.concatenate([cc, ss], axis=2), jnp.concatenate([-ss, cc], axis=2)], axis=1)
    return f1, f1i, g.astype(BF16), jnp.swapaxes(g, 1, 2).astype(BF16)


def _filter_body(feat_ref, w1_ref, b1_ref, f1_ref, w2_ref, b2_ref, f2_ref, w3_ref, dl_ref, dft_ref,
                 af_ref, ss_ref, filt):
    i = pl.program_id(0)
    feat = feat_ref[...]
    h = jnp.sin(f1_ref[...] * (_dot_bf16x3(feat, w1_ref[...]) + b1_ref[...]))
    h = jnp.sin(f2_ref[...] * (_dot_bf16x3(h, w2_ref[...]) + b2_ref[...]))
    raw = jnp.dot(h.astype(BF16), w3_ref[...], preferred_element_type=F32)
    win = jnp.exp(-feat[:, 0:1] * dl_ref[...]) + HY_MOD_SHIFT
    row = lax.broadcasted_iota(I32, (TM, 1), 0) + i * TM

    @pl.when(i == 0)
    def _():
        ss_ref[...] = jnp.zeros_like(ss_ref)

    for j in range(4):
        f = raw[:, j * BW:(j + 1) * BW] * win
        if j >= 2:
            f = jnp.where(row == 0, 0.0, f)
        filt[:, j * BW:(j + 1) * BW] = f.astype(BF16)
        ss_ref[:, j * BW:(j + 1) * BW] += jnp.sum(f * f, axis=0, keepdims=True)
    af_ref[...] = jnp.dot(dft_ref[...], filt[...], preferred_element_type=F32).astype(BF16)


def _hy_filters(feat_perm, w1p, b1, fr1, w2, b2, fr2, w3p_bf, deltas, f1, layer):
    l = feat_perm.shape[0]
    n2 = l // TM
    lsel = lambda *blk: pl.BlockSpec((None,) + blk, lambda i: (layer,) + (0,) * len(blk))
    return pl.pallas_call(
        _filter_body,
        out_shape=[SDS((2 * FFT_K1, n2 * 4 * BW), BF16), SDS((1, 4 * BW), F32)],
        grid=(n2,),
        in_specs=[pl.BlockSpec((TM, LANE), lambda i: (i, 0)),
                  lsel(LANE, HY_HIDDEN), lsel(1, HY_HIDDEN), lsel(1, HY_HIDDEN),
                  lsel(HY_HIDDEN, HY_HIDDEN), lsel(1, HY_HIDDEN), lsel(1, HY_HIDDEN),
                  lsel(HY_HIDDEN, 4 * BW), pl.BlockSpec((1, BW), lambda i: (0, 0)),
                  pl.BlockSpec((2 * FFT_K1, FFT_N1 // 2), lambda i: (0, 0))],
        out_specs=[pl.BlockSpec((2 * FFT_K1, 4 * BW), lambda i: (0, i)), pl.BlockSpec((1, 4 * BW), lambda i: (0, 0))],
        scratch_shapes=[pltpu.VMEM((TM, 4 * BW), BF16)],
        compiler_params=_cparams(("arbitrary",), 32),
        name="hy_filters",
    )(feat_perm, w1p, b1, fr1, w2, b2, fr2, w3p_bf, deltas, f1)


def _dft_major_body(f_ref, x_ref, o_ref):
    o_ref[...] = jnp.dot(f_ref[...], x_ref[...], preferred_element_type=F32).astype(BF16)


def _dft_major(f1, x2d, row_blk, ncols):
    tn = min(FFT_TN, ncols)
    return pl.pallas_call(
        _dft_major_body,
        out_shape=SDS((2 * FFT_K1, ncols), BF16),
        grid=(ncols // tn,),
        in_specs=[pl.BlockSpec((2 * FFT_K1, FFT_N1 // 2), lambda j: (0, 0)),
                  pl.BlockSpec((FFT_N1 // 2, tn), lambda j: (row_blk, j))],
        out_specs=pl.BlockSpec((2 * FFT_K1, tn), lambda j: (0, j)),
        compiler_params=_cparams(("parallel",), 32),
        name="dft_major",
    )(f1, x2d)


def _filter_minor_body(a_ref, g_ref, ss_ref, h_ref, *, n2):
    ssv = ss_ref[...]
    scale = lax.rsqrt(ssv[:, 0:2 * BW] + ssv[:, 2 * BW:4 * BW] + LN_EPS)
    for j in range(a_ref.shape[1]):
        if n2 == 1:
            s = jnp.concatenate([a_ref[0, j], a_ref[1, j]], axis=0).astype(F32)
        else:
            a = jnp.concatenate([a_ref[0, j], a_ref[1, j]], axis=0)
            s = jnp.dot(g_ref[j], a, preferred_element_type=F32)
        sf, sb = s[:, 0:2 * BW], s[:, 2 * BW:4 * BW]
        hr = (sf[0:n2] + sb[0:n2]) * scale
        hi = (sf[n2:2 * n2] - sb[n2:2 * n2]) * scale
        h_ref[j] = jnp.concatenate([hr, hi], axis=0).astype(BF16)


def _conv_minor_body(a_ref, h_ref, g_ref, gt_ref, d_ref, *, n2):
    for j in range(a_ref.shape[1]):
        a = jnp.concatenate([a_ref[0, j], a_ref[1, j]], axis=0)
        if n2 == 1:
            x = a.astype(F32)
        else:
            x = jnp.dot(g_ref[j], a, preferred_element_type=F32)
        h = h_ref[j].astype(F32)
        xr, xi, hr, hi = x[0:n2], x[n2:2 * n2], h[0:n2], h[n2:2 * n2]
        y = jnp.concatenate([xr * hr - xi * hi, xr * hi + xi * hr], axis=0)
        if n2 == 1:
            dv = y
        else:
            dv = jnp.dot(gt_ref[j], y.astype(BF16), preferred_element_type=F32)
        d_ref[0, j] = dv[0:n2].astype(BF16)
        d_ref[1, j] = dv[n2:2 * n2].astype(BF16)


def _filter_minor(af, g, ss, n2):
    a4 = af.reshape(2, FFT_K1, n2, 4 * BW)
    kb = FFT_KB
    gspec = (pl.BlockSpec((kb, 2 * n2, 2 * n2), lambda i: (i, 0, 0)) if n2 > 1
             else pl.BlockSpec((SUB, LANE), lambda i: (0, 0)))
    return pl.pallas_call(
        functools.partial(_filter_minor_body, n2=n2),
        out_shape=SDS((FFT_K1, 2 * n2, 2 * BW), BF16),
        grid=(FFT_K1 // kb,),
        in_specs=[pl.BlockSpec((2, kb, n2, 4 * BW), lambda i: (0, i, 0, 0)), gspec,
                  pl.BlockSpec((1, 4 * BW), lambda i: (0, 0))],
        out_specs=pl.BlockSpec((kb, 2 * n2, 2 * BW), lambda i: (i, 0, 0)),
        compiler_params=_cparams(("parallel",), 40),
        name="hy_filter_minor",
    )(a4, g if n2 > 1 else jnp.zeros((SUB, LANE), BF16), ss)


def _conv_minor(a, hspec, g, gt, order, n2):
    a4 = a.reshape(2, FFT_K1, n2, BW)
    kb = FFT_KB
    dummy = jnp.zeros((SUB, LANE), BF16)
    gspec = (pl.BlockSpec((kb, 2 * n2, 2 * n2), lambda i: (i, 0, 0)) if n2 > 1
             else pl.BlockSpec((SUB, LANE), lambda i: (0, 0)))
    d4 = pl.pallas_call(
        functools.partial(_conv_minor_body, n2=n2),
        out_shape=SDS((2, FFT_K1, n2, BW), BF16),
        grid=(FFT_K1 // kb,),
        in_specs=[pl.BlockSpec((2, kb, n2, BW), lambda i: (0, i, 0, 0)),
                  pl.BlockSpec((kb, 2 * n2, BW), lambda i: (i, 0, order)), gspec, gspec],
        out_specs=pl.BlockSpec((2, kb, n2, BW), lambda i: (0, i, 0, 0)),
        compiler_params=_cparams(("parallel",), 32),
        name="hy_conv_minor",
    )(a4, hspec, g if n2 > 1 else dummy, gt if n2 > 1 else dummy)
    return d4.reshape(2 * FFT_K1, n2 * BW)


def _idft_gate_body(f_ref, d_ref, gate_ref, z_ref, b_ref, o_ref):
    y = jnp.dot(f_ref[...], d_ref[...], preferred_element_type=F32)
    z = z_ref[...].astype(F32)
    o_ref[...] = (gate_ref[...].astype(F32) * (y + b_ref[...] * z)).astype(BF16)


def _idft_gate(f1i, d, gate2d, gate_blk, z2d, z_blk, bias_t, ncols):
    tn = min(FFT_TN, ncols)
    return pl.pallas_call(
        _idft_gate_body,
        out_shape=SDS((FFT_N1 // 2, ncols), BF16),
        grid=(ncols // tn,),
        in_specs=[pl.BlockSpec((FFT_N1 // 2, 2 * FFT_K1), lambda j: (0, 0)),
                  pl.BlockSpec((2 * FFT_K1, tn), lambda j: (0, j)),
                  pl.BlockSpec((FFT_N1 // 2, tn), lambda j: (gate_blk, j)),
                  pl.BlockSpec((FFT_N1 // 2, tn), lambda j: (z_blk, j)),
                  pl.BlockSpec((1, tn), lambda j: (0, 0))],
        out_specs=pl.BlockSpec((FFT_N1 // 2, tn), lambda j: (0, j)),
        compiler_params=_cparams(("parallel",), 32),
        name="idft_gate",
    )(f1i, d, gate2d, z2d, bias_t)


def _hyena(x1, x2, v, feat, tabs, hyw, layer, seq_len, row0):
    n2 = seq_len // (FFT_N1 // 2)
    f1, f1i, g, gt = tabs
    w1p, b1, fr1, w2, b2, fr2, w3p_bf, deltas, hy_bias = hyw
    nt = x1.shape[0]
    af, ss = _hy_filters(feat, w1p, b1, fr1, w2, b2, fr2, w3p_bf, deltas, f1, layer)
    hspec = _filter_minor(af, g, ss, n2)
    view = lambda a: a.reshape(nt // n2, n2 * BW)
    blk = row0 // seq_len if n2 == 1 else 0
    ncols = n2 * BW
    tn = min(FFT_TN, ncols)
    z, zv, zblk = v, view(v), blk
    gates = (x1, x2)
    for o in range(2):
        a = _dft_major(f1, zv, zblk, ncols)
        d = _conv_minor(a, hspec, g, gt, o, n2)
        bias_t = jnp.tile(hy_bias[layer, o][None, :], (1, tn // BW))
        z2 = _idft_gate(f1i, d, view(gates[o]), blk, zv, zblk, bias_t, ncols)
        zv, zblk = z2, 0
    return zv.reshape(seq_len, BW)


def _store_spectrum_slab(ref, j, word):
    for q in range(word.shape[1] // LANE):
        ref[q, pl.ds(j, FFT_K1, stride=FFT_NB), :] = word[:, q * LANE:(q + 1) * LANE]


def _load_spectrum_slab(ref, j):
    return jnp.concatenate([ref[q, pl.ds(j, FFT_K1, stride=FFT_NB), :] for q in range(ref.shape[0])], axis=1)


def _load_spectrum_k1(ref, j):
    nq, nblk = ref.shape[0], ref.shape[1]
    return jnp.concatenate([jnp.concatenate([ref[q, b, j] for q in range(nq)], axis=1) for b in range(nblk)], axis=0)


def _store_spectrum_k1(ref, j, word):
    nq, nblk = ref.shape[0], ref.shape[1]
    for b in range(nblk):
        for q in range(nq):
            ref[q, b, j] = word[b * FFT_NB:(b + 1) * FFT_NB, q * LANE:(q + 1) * LANE]


def _filter_p_body(feat_ref, w1_ref, b1_ref, f1_ref, w2_ref, b2_ref, f2_ref, w3_ref, dl_ref, dft_ref,
                   af_ref, ss_ref, h2):
    nb = pl.program_id(0)
    cb = pl.program_id(1)
    nbk = FFT_NB

    @pl.when(jnp.logical_and(nb == 0, cb == 0))
    def _():
        ss_ref[...] = jnp.zeros_like(ss_ref)

    @pl.when(cb == 0)
    def _():
        feat = feat_ref[...]
        h = jnp.sin(f1_ref[...] * (_dot_bf16x3(feat, w1_ref[...]) + b1_ref[...]))
        h = jnp.sin(f2_ref[...] * (_dot_bf16x3(h, w2_ref[...]) + b2_ref[...]))
        h2[...] = h.astype(BF16)

    row = lax.broadcasted_iota(I32, (TM, 1), 0)
    lag0_bwd = jnp.logical_and(jnp.logical_and(nb == 0, cb >= 2), row == 0)
    ssum = jnp.zeros((1, BW), F32)
    for j in range(nbk):
        raw = jnp.dot(h2[j * TM:(j + 1) * TM, :], w3_ref[...], preferred_element_type=F32)
        win = jnp.exp(-feat_ref[j * TM:(j + 1) * TM, 0:1] * dl_ref[...]) + HY_MOD_SHIFT
        f = raw * win
        if j == 0:
            f = jnp.where(lag0_bwd, 0.0, f)
        ssum = ssum + jnp.sum(f * f, axis=0, keepdims=True)
        rr = jnp.dot(dft_ref[...], f.astype(BF16), preferred_element_type=F32)
        _store_spectrum_slab(af_ref, j, _pack2(rr[0:FFT_K1], rr[FFT_K1:2 * FFT_K1]))
    ss_ref[cb] = ss_ref[cb] + ssum


def _hy_filters_p(feat_perm, w1p, b1, fr1, w2, b2, fr2, w3p_bf, deltas, f1, layer):
    l = feat_perm.shape[0]
    n2 = l // TM
    nbk = FFT_NB
    lsel = lambda *blk: pl.BlockSpec((None,) + blk, lambda nb, cb: (layer,) + (0,) * len(blk))
    return pl.pallas_call(
        _filter_p_body,
        out_shape=[SDS((4 * BW // LANE, n2 // nbk, FFT_K1 * nbk, LANE), U32), SDS((4, 1, BW), F32)],
        grid=(n2 // nbk, 4),
        in_specs=[pl.BlockSpec((nbk * TM, LANE), lambda nb, cb: (nb, 0)),
                  lsel(LANE, HY_HIDDEN), lsel(1, HY_HIDDEN), lsel(1, HY_HIDDEN),
                  lsel(HY_HIDDEN, HY_HIDDEN), lsel(1, HY_HIDDEN), lsel(1, HY_HIDDEN),
                  pl.BlockSpec((None, HY_HIDDEN, BW), lambda nb, cb: (layer, 0, cb)),
                  pl.BlockSpec((1, BW), lambda nb, cb: (0, 0)),
                  pl.BlockSpec((2 * FFT_K1, FFT_N1 // 2), lambda nb, cb: (0, 0))],
        out_specs=[pl.BlockSpec((BW // LANE, None, FFT_K1 * nbk, LANE), lambda nb, cb: (cb, nb, 0, 0)),
                   pl.BlockSpec((4, 1, BW), lambda nb, cb: (0, 0, 0))],
        scratch_shapes=[pltpu.VMEM((nbk * TM, HY_HIDDEN), BF16)],
        compiler_params=_cparams(("arbitrary", "arbitrary"), 40),
        name="hy_filters_p",
    )(feat_perm, w1p, b1, fr1, w2, b2, fr2, w3p_bf, deltas, f1)


def _filter_minor_p_body(a_ref, g_ref, ss_ref, h_ref, *, n2):
    ssv = ss_ref[...]
    scale = lax.rsqrt(ssv[:, 0:2 * BW] + ssv[:, 2 * BW:4 * BW] + LN_EPS)
    for j in range(a_ref.shape[2]):
        re, im = _unpack2(_load_spectrum_k1(a_ref, j))
        a = jnp.concatenate([re, im], axis=0).astype(BF16)
        s = jnp.dot(g_ref[j], a, preferred_element_type=F32)
        sf, sb = s[:, 0:2 * BW], s[:, 2 * BW:4 * BW]
        hr = (sf[0:n2] + sb[0:n2]) * scale
        hi = (sf[n2:2 * n2] - sb[n2:2 * n2]) * scale
        h_ref[j] = jnp.concatenate([hr, hi], axis=0).astype(BF16)


def _filter_minor_p(af_p, g, ss, n2):
    kb = FFT_KB
    nq, nblk = af_p.shape[0], af_p.shape[1]
    af_p = af_p.reshape(nq, nblk, FFT_K1, FFT_NB, LANE)
    return pl.pallas_call(
        functools.partial(_filter_minor_p_body, n2=n2),
        out_shape=SDS((FFT_K1, 2 * n2, 2 * BW), BF16),
        grid=(FFT_K1 // kb,),
        in_specs=[pl.BlockSpec((nq, nblk, kb, FFT_NB, LANE), lambda i: (0, 0, i, 0, 0)),
                  pl.BlockSpec((kb, 2 * n2, 2 * n2), lambda i: (i, 0, 0)),
                  pl.BlockSpec((1, 4 * BW), lambda i: (0, 0))],
        out_specs=pl.BlockSpec((kb, 2 * n2, 2 * BW), lambda i: (i, 0, 0)),
        compiler_params=_cparams(("parallel",), 40),
        name="hy_filter_minor_p",
    )(af_p, g, ss)


def _unpack_slab(ref, j):
    nq, nblk = ref.shape[0], ref.shape[1]
    word = jnp.concatenate([jnp.concatenate([ref[q, b, j] for q in range(nq)], axis=1) for b in range(nblk)], axis=0)
    hi, lo = _unpack2(word)
    return jnp.concatenate([hi, lo], axis=1)


def _store_slab(ref, j, word):
    nq, nblk = ref.shape[0], ref.shape[1]
    for b in range(nblk):
        for q in range(nq):
            ref[q, b, j] = word[b * FFT_NB:(b + 1) * FFT_NB, q * LANE:(q + 1) * LANE]


def _slab_view(xp, n2):
    nq = xp.shape[0]
    x5 = xp.reshape(nq, xp.shape[1], n2, FFT_NB, LANE)
    spec = pl.BlockSpec((nq, FFT_N1 // 2 // FFT_NB, FFT_NB, FFT_NB, LANE), lambda i: (0, 0, i, 0, 0))
    return x5, spec


def _dft_major_p_body(f_ref, x_ref, a_ref):
    for j in range(x_ref.shape[2]):
        x = _unpack_slab(x_ref, j).astype(BF16)
        rr = jnp.dot(f_ref[...], x, preferred_element_type=F32)
        _store_spectrum_slab(a_ref, j, _pack2(rr[0:FFT_K1], rr[FFT_K1:2 * FFT_K1]))


def _dft_major_p(f1, xp, n2):
    nbk = FFT_NB
    xp, xspec = _slab_view(xp, n2)
    return pl.pallas_call(
        _dft_major_p_body,
        out_shape=SDS((BW // LANE, n2 // nbk, FFT_K1 * nbk, LANE), U32),
        grid=(n2 // nbk,),
        in_specs=[pl.BlockSpec((2 * FFT_K1, FFT_N1 // 2), lambda i: (0, 0)),
                  xspec],
        out_specs=pl.BlockSpec((BW // LANE, None, FFT_K1 * nbk, LANE), lambda i: (0, i, 0, 0)),
        compiler_params=_cparams(("parallel",), 32),
        name="dft_major_p",
    )(f1, xp)


def _conv_minor_p_body(a_ref, h_ref, g_ref, gt_ref, d_ref, *, n2):
    for j in range(a_ref.shape[2]):
        re, im = _unpack2(_load_spectrum_k1(a_ref, j))
        a = jnp.concatenate([re, im], axis=0).astype(BF16)
        x = jnp.dot(g_ref[j], a, preferred_element_type=F32)
        h = h_ref[j].astype(F32)
        xr, xi, hr, hi = x[0:n2], x[n2:2 * n2], h[0:n2], h[n2:2 * n2]
        y = jnp.concatenate([xr * hr - xi * hi, xr * hi + xi * hr], axis=0)
        dv = jnp.dot(gt_ref[j], y.astype(BF16), preferred_element_type=F32)
        _store_spectrum_k1(d_ref, j, _pack2(dv[0:n2], dv[n2:2 * n2]))


def _conv_minor_p(a_p, hspec, g, gt, order, n2):
    kb = FFT_KB
    nq, nblk = a_p.shape[0], a_p.shape[1]
    a_p = a_p.reshape(nq, nblk, FFT_K1, FFT_NB, LANE)
    gspec = pl.BlockSpec((kb, 2 * n2, 2 * n2), lambda i: (i, 0, 0))
    sspec = pl.BlockSpec((nq, nblk, kb, FFT_NB, LANE), lambda i: (0, 0, i, 0, 0))
    return pl.pallas_call(
        functools.partial(_conv_minor_p_body, n2=n2),
        out_shape=SDS((nq, nblk, FFT_K1, FFT_NB, LANE), U32),
        grid=(FFT_K1 // kb,),
        in_specs=[sspec, pl.BlockSpec((kb, 2 * n2, BW), lambda i: (i, 0, order)), gspec, gspec],
        out_specs=sspec,
        compiler_params=_cparams(("parallel",), 32),
        name="hy_conv_minor_p",
    )(a_p, hspec, g, gt)


def _idft_gate_p_body(f_ref, d_ref, gate_ref, z_ref, b_ref, o_ref):
    for j in range(o_ref.shape[2]):
        re, im = _unpack2(_load_spectrum_slab(d_ref, j))
        d = jnp.concatenate([re, im], axis=0).astype(BF16)
        y = jnp.dot(f_ref[...], d, preferred_element_type=F32)
        out = _unpack_slab(gate_ref, j) * (y + b_ref[...] * _unpack_slab(z_ref, j))
        _store_slab(o_ref, j, _pack2(out[:, 0:BW // 2], out[:, BW // 2:BW]))


def _idft_gate_p(f1i, d_p, gate_p, z_p, bias, n2):
    nbk = FFT_NB
    nq, nblk = d_p.shape[0], d_p.shape[1]
    d_p = d_p.reshape(nq, nblk, FFT_K1 * FFT_NB, LANE)
    gate_p, slab = _slab_view(gate_p, n2)
    z_p, _ = _slab_view(z_p, n2)
    nqt = gate_p.shape[0]
    return pl.pallas_call(
        _idft_gate_p_body,
        out_shape=SDS((nqt, FFT_N1 // 2 // FFT_NB, n2, FFT_NB, LANE), U32),
        grid=(n2 // nbk,),
        in_specs=[pl.BlockSpec((FFT_N1 // 2, 2 * FFT_K1), lambda i: (0, 0)),
                  pl.BlockSpec((nq, None, FFT_K1 * FFT_NB, LANE), lambda i: (0, i, 0, 0)),
                  slab, slab, pl.BlockSpec((1, BW), lambda i: (0, 0))],
        out_specs=slab,
        compiler_params=_cparams(("parallel",), 32),
        name="idft_gate_p",
    )(f1i, d_p, gate_p, z_p, bias)


def _hyena_latent(x1p, x2p, vp, feat_perm, tabs, hyw, layer, seq_len):
    n2 = seq_len // (FFT_N1 // 2)
    f1, f1i, g, gt = tabs
    w1p, b1, fr1, w2, b2, fr2, w3p_bf, deltas, hy_bias = hyw
    af_p, ss = _hy_filters_p(feat_perm, w1p, b1, fr1, w2, b2, fr2, w3p_bf, deltas, f1, layer)
    hspec = _filter_minor_p(af_p, g, ss.reshape(1, 4 * BW), n2)
    z = vp
    for o, gate in enumerate((x1p, x2p)):
        a_p = _dft_major_p(f1, z, n2)
        d_p = _conv_minor_p(a_p, hspec, g, gt, o, n2)
        z = _idft_gate_p(f1i, d_p, gate, z, hy_bias[layer, o][None, :], n2)
    return z.reshape(z.shape[0], z.shape[1], n2 * FFT_NB, LANE)


def _merge_body(att_ref, yp_ref, yhp_ref, yhc_ref, ys_ref, hmod_ref, wg_ref, x_ref, wb_ref, wo_ref,
                g1_ref, lg_ref, lb_ref, sh2_ref, sc2_ref, rw_ref, rb_ref,
                x1_ref, m_ref, rt_ref, cnt_ref, *, nlat, n_ctx, alpha):
    i = pl.program_id(0)
    n2c = yhp_ref.shape[2] // FFT_NB
    pieces = []
    for a in range(TM1 // n2c):
        word = jnp.concatenate([yhp_ref[q, a // FFT_NB, pl.ds(a % FFT_NB, n2c, stride=FFT_NB), :]
                                for q in range(yhp_ref.shape[0])], axis=1)
        hi, lo = _unpack2(word)
        pieces.append(jnp.concatenate([hi, lo], axis=1).astype(BF16))
    yh_lat = jnp.concatenate(pieces, axis=0)
    yh_ctx = jnp.concatenate([yhc_ref[...], jnp.zeros((TM1 - n_ctx, BW), BF16)], axis=0)
    yh = jnp.where(i >= nlat, yh_ctx, yh_lat)
    ys = (att_ref[...], yp_ref[...], yh, ys_ref[...])
    half = TM1 // 2
    logit_parts = []
    for hh in range(2):
        rows = slice(hh * half, (hh + 1) * half)
        hm = hmod_ref[rows, :]
        merged = jnp.zeros((half, D), F32)
        for n in range(4):
            br = jnp.dot(ys[n][rows], wb_ref[n], preferred_element_type=F32)
            gl = jnp.dot(hm, wg_ref[:, n * D:(n + 1) * D], preferred_element_type=F32)
            merged = merged + br * (0.5 * jnp.tanh(0.5 * gl) + 0.5)
        y = jnp.dot(merged.astype(BF16), wo_ref[...], preferred_element_type=F32)
        x1 = _layer_norm(alpha * x_ref[rows, :] + g1_ref[...] * y) * lg_ref[...] + lb_ref[...]
        x1_ref[rows, :] = x1
        m = _layer_norm(x1) * (1.0 + sc2_ref[...]) + sh2_ref[...]
        m_ref[rows, :] = m.astype(BF16)
        m_hi = m.astype(BF16)
        m_lo = (m - m_hi.astype(F32)).astype(BF16)
        logit_parts.append(jnp.dot(m_hi, rw_ref[0], preferred_element_type=F32)
                           + jnp.dot(m_lo, rw_ref[0], preferred_element_type=F32)
                           + jnp.dot(m_hi, rw_ref[1], preferred_element_type=F32) + rb_ref[...])
    logits = jnp.concatenate(logit_parts, axis=0)
    lt = logits.T
    le = lt[0:N_EXPERTS]
    lgp = lt[N_EXPERTS:N_EXPERTS + SUB]
    big = 1 << 20
    gi = lax.broadcasted_iota(I32, lgp.shape, 0)
    gmax = jnp.max(lgp, axis=0, keepdims=True)
    gsel = jnp.min(jnp.where(lgp == gmax, gi, big), axis=0, keepdims=True)
    gate_g = 1.0 / jnp.sum(jnp.exp(lgp - gmax), axis=0, keepdims=True)
    ei = lax.broadcasted_iota(I32, le.shape, 0)
    lem = jnp.where(lax.shift_right_logical(ei, 3) == gsel, le, -3.0e38)
    v1 = jnp.max(lem, axis=0, keepdims=True)
    i1 = jnp.min(jnp.where(lem == v1, ei, big), axis=0, keepdims=True)
    lem2 = jnp.where(ei == i1, -3.0e38, lem)
    v2 = jnp.max(lem2, axis=0, keepdims=True)
    i2 = jnp.min(jnp.where(lem2 == v2, ei, big), axis=0, keepdims=True)
    e2 = jnp.exp(v2 - v1)
    wa = gate_g / (1.0 + e2)
    wb = gate_g * e2 / (1.0 + e2)
    tok = lax.broadcasted_iota(I32, (1, TM1), 1)
    valid = jnp.logical_or(i < nlat, tok < n_ctx)
    i1 = jnp.where(valid, i1, -1)
    i2 = jnp.where(valid, i2, -1)
    ri = lax.broadcasted_iota(I32, (SUB, TM1), 0)
    rt = jnp.where(ri == 0, i1.astype(F32), jnp.where(ri == 1, i2.astype(F32),
                   jnp.where(ri == 2, wa, jnp.where(ri == 3, wb, 0.0))))
    rt_ref[...] = rt
    ci = lax.broadcasted_iota(I32, (LANE, TM1), 0)
    oh = jnp.logical_or(ci == i1, ci == i2).astype(BF16)
    cnt_ref[0] = lax.dot_general(jnp.ones((SUB, TM1), BF16), oh, (((1,), (1,)), ((), ())),
                                 preferred_element_type=F32)


def _merge(att, ypool, yhp, yhc, ysc, hmod, wg_bf, x, wb_bf, wo_bf, g1, ln_g, ln_b, sh2, sc2, rw, rb, layer, n_lat,
           n_ctx, alpha):
    nt = x.shape[0]
    nlat = n_lat // TM1
    ntl = nt // TM1
    n2c = yhp.shape[2] // FFT_NB
    sel = lambda i: (jnp.where(i >= nlat, 1, 0), 0, 0)
    row = lambda c: pl.BlockSpec((TM1, c), lambda i: (i, 0))
    lsel = lambda *blk, **kw: pl.BlockSpec((None,) + blk, lambda i: (layer,) + (0,) * len(blk), **kw)
    msel = pl.BlockSpec((None, 1, D), sel)
    return pl.pallas_call(
        functools.partial(_merge_body, nlat=nlat, n_ctx=n_ctx, alpha=alpha),
        out_shape=[SDS((nt, D), F32), SDS((nt, D), BF16), SDS((SUB, nt), F32), SDS((ntl, SUB, LANE), F32)],
        grid=(ntl,),
        in_specs=[row(BW), row(BW),
                  pl.BlockSpec((yhp.shape[0], TM1 // n2c // FFT_NB, n2c * FFT_NB, LANE),
                               lambda i: (0, jnp.minimum(i, nlat - 1), 0, 0)),
                  pl.BlockSpec((n_ctx, BW), lambda i: (0, 0)),
                  row(BW), row(D), lsel(D, 4 * D, pipeline_mode=pl.Buffered(1)), row(D),
                  lsel(4, BW, D, pipeline_mode=pl.Buffered(1)), lsel(D, D, pipeline_mode=pl.Buffered(1)),
                  msel, lsel(1, D), lsel(1, D), msel, msel,
                  lsel(2, D, LANE), lsel(1, LANE)],
        out_specs=[row(D), row(D), pl.BlockSpec((SUB, TM1), lambda i: (0, i)),
                   pl.BlockSpec((1, SUB, LANE), lambda i: (i, 0, 0))],
        compiler_params=_cparams(("parallel",), 56),
        name="merge_router",
    )(att, ypool, yhp, yhc, ysc, hmod, wg_bf, x, wb_bf, wo_bf, g1, ln_g, ln_b, sh2, sc2, rw, rb)


def _piece_loops(np_ref, so_ref, do_ref, j, fn):
    for e in range(N_EXPERTS):
        n = np_ref[j * N_EXPERTS + e]
        so = so_ref[j * N_EXPERTS + e]
        do = do_ref[j * N_EXPERTS + e]

        def body(p, carry, so=so, do=do):
            fn(pl.multiple_of(so + p * MOE_PIECE, MOE_PIECE), pl.multiple_of(do + p * MOE_PIECE, MOE_PIECE))
            return carry

        lax.fori_loop(0, n, body, 0)


def _dispatch_body(np_ref, so_ref, do_ref, nv_ref, m_ref, rt_ref, u_ref, hs_hbm, pos_ref, hs_vmem, sem, *,
                   ntiles, nblk):
    j = pl.program_id(0)
    real = j < ntiles
    rt = rt_ref[...]
    e0 = jnp.where(real, rt[0:1].astype(I32), -1)
    e1 = jnp.where(real, rt[1:2].astype(I32), -1)
    ei = lax.broadcasted_iota(I32, (N_EXPERTS, TM1), 0)
    oh0 = (ei == e0).astype(F32)
    oh1 = (ei == e1).astype(F32)
    c0 = jnp.dot(oh0.astype(BF16), u_ref[...], preferred_element_type=F32)
    c1 = jnp.dot(oh1.astype(BF16), u_ref[...], preferred_element_type=F32)
    n0 = jnp.sum(oh0, axis=1, keepdims=True)
    ecol = lax.broadcasted_iota(I32, (N_EXPERTS, 1), 0)
    toff = jnp.zeros((N_EXPERTS, 1), F32)
    for e in range(N_EXPERTS):
        toff = jnp.where(ecol == e, so_ref[j * N_EXPERTS + e].astype(F32), toff)
    pos0 = jnp.sum(oh0 * (toff + c0), axis=0, keepdims=True)
    pos1 = jnp.sum(oh1 * (toff + n0 + c1), axis=0, keepdims=True)
    pos0 = jnp.where(e0 >= 0, pos0, -1.0)
    pos1 = jnp.where(e1 >= 0, pos1, -1.0)
    ri = lax.broadcasted_iota(I32, (SUB, TM1), 0)
    pos_ref[...] = jnp.where(ri == 0, pos0, jnp.where(ri == 1, pos1, 0.0))
    si = lax.broadcasted_iota(I32, (MOE_S, TM1), 0)
    perm = jnp.logical_or(si == pos0.astype(I32), si == pos1.astype(I32)).astype(BF16)
    hs = jnp.dot(perm, m_ref[...], preferred_element_type=F32)
    slot = lax.rem(j, 2)
    hs_vmem[slot] = _pack2(hs[:, 0:D // 2], hs[:, D // 2:D])

    def copy(s, so, do):
        return pltpu.make_async_copy(hs_vmem.at[s, pl.ds(so, MOE_PIECE)], hs_hbm.at[pl.ds(do, MOE_PIECE)], sem.at[s])

    _piece_loops(np_ref, so_ref, do_ref, j, lambda so, do: copy(slot, so, do).start())

    @pl.when(j > 0)
    def _():
        _piece_loops(np_ref, so_ref, do_ref, j - 1, lambda so, do: copy(1 - slot, so, do).wait())

    @pl.when(j == ntiles)
    def _():
        def blk_copy(b):
            return pltpu.make_async_copy(hs_vmem.at[slot, pl.ds(0, MOE_MB)],
                                         hs_hbm.at[pl.ds(pl.multiple_of(b * MOE_MB, MOE_MB), MOE_MB)], sem.at[slot])

        def start(b, carry):
            blk_copy(b).start()
            return carry

        def wait(b, carry):
            blk_copy(b).wait()
            return carry

        lax.fori_loop(nv_ref[0], nblk, start, 0)
        _piece_loops(np_ref, so_ref, do_ref, j, lambda so, do: copy(slot, so, do).wait())
        lax.fori_loop(nv_ref[0], nblk, wait, 0)


def _dispatch(npieces, soff, doff, nvb, m_bf, rt, upper, rmax):
    nt = m_bf.shape[0]
    ntiles = nt // TM1
    last = ntiles - 1
    return pl.pallas_call(
        functools.partial(_dispatch_body, ntiles=ntiles, nblk=rmax // MOE_MB),
        out_shape=[SDS((rmax, D // 2), U32), SDS((SUB, (ntiles + 1) * TM1), F32)],
        grid_spec=pltpu.PrefetchScalarGridSpec(
            num_scalar_prefetch=4,
            grid=(ntiles + 1,),
            in_specs=[pl.BlockSpec((TM1, D), lambda j, *_: (jnp.minimum(j, last), 0)),
                      pl.BlockSpec((SUB, TM1), lambda j, *_: (0, jnp.minimum(j, last))),
                      pl.BlockSpec((TM1, TM1), lambda j, *_: (0, 0))],
            out_specs=[pl.BlockSpec(memory_space=pl.ANY),
                       pl.BlockSpec((SUB, TM1), lambda j, *_: (0, j))],
            scratch_shapes=[pltpu.VMEM((2, MOE_S, D // 2), U32), pltpu.SemaphoreType.DMA((2,))]),
        compiler_params=_cparams(("arbitrary",), 40),
        name="moe_dispatch",
    )(npieces, soff, doff, nvb, m_bf, rt, upper)


def _expert_body(be_ref, nv_ref, x_ref, w1_ref, w3_ref, w2_ref, o_ref, w1b, w3b, w2b):
    b = pl.program_id(0)
    valid = b < nv_ref[0]
    prev = be_ref[jnp.maximum(b - 1, 0)]
    fresh = jnp.logical_or(b == 0, be_ref[b] != prev)

    @pl.when(jnp.logical_and(valid, fresh))
    def _():
        w1b[...] = w1_ref[...].astype(BF16)
        w3b[...] = w3_ref[...].astype(BF16)
        w2b[...] = w2_ref[...].astype(BF16)

    @pl.when(valid)
    def _():
        xh, xl = _unpack2(x_ref[...])
        xh, xl = xh.astype(BF16), xl.astype(BF16)
        half = D // 2
        h1 = (jnp.dot(xh, w1b[0:half, :], preferred_element_type=F32)
              + jnp.dot(xl, w1b[half:D, :], preferred_element_type=F32))
        h3 = (jnp.dot(xh, w3b[0:half, :], preferred_element_type=F32)
              + jnp.dot(xl, w3b[half:D, :], preferred_element_type=F32))
        hh = (h1 * (0.5 * jnp.tanh(0.5 * h1) + 0.5)) * h3
        y = jnp.dot(hh.astype(BF16), w2b[...], preferred_element_type=F32)
        o_ref[...] = _pack2(y[:, 0:half], y[:, half:D])

    @pl.when(b >= nv_ref[0])
    def _():
        o_ref[...] = jnp.zeros_like(o_ref)


def _experts(blk_e, nvb, hs, ex_w1, ex_w3, ex_w2, layer):
    rmax = hs.shape[0]
    nb = rmax // MOE_MB
    bi = lambda b, be, nv: jnp.maximum(jnp.minimum(b, nv[0] - 1), 0)
    return pl.pallas_call(
        _expert_body,
        out_shape=SDS((rmax, D // 2), U32),
        grid_spec=pltpu.PrefetchScalarGridSpec(
            num_scalar_prefetch=2,
            grid=(nb,),
            in_specs=[pl.BlockSpec((MOE_MB, D // 2), lambda b, be, nv: (bi(b, be, nv), 0)),
                      pl.BlockSpec((None, None, D, EXPERT_HIDDEN), lambda b, be, nv: (layer, be[bi(b, be, nv)], 0, 0)),
                      pl.BlockSpec((None, None, D, EXPERT_HIDDEN), lambda b, be, nv: (layer, be[bi(b, be, nv)], 0, 0)),
                      pl.BlockSpec((None, None, EXPERT_HIDDEN, D), lambda b, be, nv: (layer, be[bi(b, be, nv)], 0, 0))],
            out_specs=pl.BlockSpec((MOE_MB, D // 2), lambda b, be, nv: (b, 0)),
            scratch_shapes=[pltpu.VMEM((D, EXPERT_HIDDEN), BF16), pltpu.VMEM((D, EXPERT_HIDDEN), BF16),
                            pltpu.VMEM((EXPERT_HIDDEN, D), BF16)]),
        compiler_params=_cparams(("arbitrary",), 48),
        name="moe_experts",
    )(blk_e, nvb, hs, ex_w1, ex_w3, ex_w2)


def _combine_body(np_ref, so_ref, do_ref, ys_hbm, pos_ref, rt_ref, x1_ref, g2_ref, lg_ref, lb_ref,
                  o_ref, ys_vmem, sem, *, alpha, nsteps):
    j = pl.program_id(0)
    slot = lax.rem(j, 2)

    def copy(s, so, do):
        return pltpu.make_async_copy(ys_hbm.at[pl.ds(do, MOE_PIECE)], ys_vmem.at[s, pl.ds(so, MOE_PIECE)], sem.at[s])

    @pl.when(j == 0)
    def _():
        _piece_loops(np_ref, so_ref, do_ref, j, lambda so, do: copy(slot, so, do).start())

    @pl.when(j + 1 < nsteps)
    def _():
        _piece_loops(np_ref, so_ref, do_ref, j + 1, lambda so, do: copy(1 - slot, so, do).start())
    z = jnp.concatenate([pos_ref[...], rt_ref[...], jnp.zeros((LANE - 2 * SUB, TM1), F32)], axis=0)
    zt = z.T
    p0 = zt[:, 0:1].astype(I32)
    p1 = zt[:, 1:2].astype(I32)
    w0 = zt[:, SUB + 2:SUB + 3]
    w1 = zt[:, SUB + 3:SUB + 4]
    si = lax.broadcasted_iota(I32, (TM1, MOE_S), 1)
    wm = (jnp.where(si == p0, w0, 0.0) + jnp.where(si == p1, w1, 0.0)).astype(BF16)
    _piece_loops(np_ref, so_ref, do_ref, j, lambda so, do: copy(slot, so, do).wait())
    last = j * N_EXPERTS + N_EXPERTS - 1
    total = so_ref[last] + np_ref[last] * MOE_PIECE
    srow = lax.broadcasted_iota(I32, (MOE_S, 1), 0)
    yh, yl = _unpack2(jnp.where(srow < total, ys_vmem[slot], jnp.uint32(0)))
    f = jnp.concatenate([jnp.dot(wm, yh.astype(BF16), preferred_element_type=F32),
                         jnp.dot(wm, yl.astype(BF16), preferred_element_type=F32)], axis=1)
    o_ref[...] = _layer_norm(alpha * x1_ref[...] + g2_ref[...] * f) * lg_ref[...] + lb_ref[...]


def _combine(npieces, soff, doff, ys, pos, rt, x1, g2, ln_g, ln_b, layer, n_lat, n_out_tiles, n_out_rows, alpha):
    nlat = n_lat // TM1
    sel = lambda j, *_: (jnp.where(j >= nlat, 1, 0), 0, 0)
    lsel = lambda *blk: pl.BlockSpec((None,) + blk, lambda j, *_: (layer,) + (0,) * len(blk))
    return pl.pallas_call(
        functools.partial(_combine_body, alpha=alpha, nsteps=n_out_tiles),
        out_shape=SDS((n_out_rows, D), F32),
        grid_spec=pltpu.PrefetchScalarGridSpec(
            num_scalar_prefetch=3,
            grid=(n_out_tiles,),
            in_specs=[pl.BlockSpec(memory_space=pl.ANY),
                      pl.BlockSpec((SUB, TM1), lambda j, *_: (0, j)),
                      pl.BlockSpec((SUB, TM1), lambda j, *_: (0, j)),
                      pl.BlockSpec((TM1, D), lambda j, *_: (j, 0)),
                      pl.BlockSpec((None, 1, D), sel), lsel(1, D), lsel(1, D)],
            out_specs=pl.BlockSpec((TM1, D), lambda j, *_: (j, 0)),
            scratch_shapes=[pltpu.VMEM((2, MOE_S, D // 2), U32), pltpu.SemaphoreType.DMA((2,))]),
        compiler_params=_cparams(("arbitrary",), 40),
        name="moe_combine",
    )(npieces, soff, doff, ys, pos, rt, x1, g2, ln_g, ln_b)


def _moe_plan(cnt, ntiles):
    c = cnt[:, 0, :N_EXPERTS].astype(I32)
    pad8 = (c + MOE_PIECE - 1) // MOE_PIECE * MOE_PIECE
    toff = jnp.cumsum(pad8, axis=1) - pad8
    tot = pad8.sum(axis=0)
    totb = (tot + MOE_MB - 1) // MOE_MB * MOE_MB
    ends = jnp.cumsum(totb)
    base = ends - totb
    dest = base[None, :] + jnp.cumsum(pad8, axis=0) - pad8
    npieces = jnp.concatenate([pad8 // MOE_PIECE, ((totb - tot) // MOE_PIECE)[None, :]], axis=0)
    soff = jnp.concatenate([toff, jnp.zeros((1, N_EXPERTS), I32)], axis=0)
    doff = jnp.concatenate([dest, (base + tot)[None, :]], axis=0)
    return npieces.reshape(-1), soff.reshape(-1), doff.reshape(-1), ends


def _rope_tables(n_lat, nt):
    t = jnp.arange(n_lat)
    pos = jnp.stack([(t // GRID_W).astype(F32), (t % GRID_W).astype(F32)], axis=1)
    inv = ROPE_BASE ** (-jnp.arange(ROPE_PAIRS, dtype=F32) / ROPE_PAIRS)
    ang = pos[:, :, None] * inv[None, None, :]
    cos = jnp.repeat(jnp.cos(ang), 2, axis=1).reshape(n_lat, 2, 2, ROPE_PAIRS)
    sin = jnp.sin(ang)
    sin = jnp.stack([-sin, sin], axis=2)
    cos = jnp.tile(cos.reshape(n_lat, HEAD_DIM), (1, LANE // HEAD_DIM))
    sin = jnp.tile(sin.reshape(n_lat, HEAD_DIM), (1, LANE // HEAD_DIM))
    cos = jnp.concatenate([cos, jnp.ones((nt - n_lat, LANE), F32)], axis=0)
    sin = jnp.concatenate([sin, jnp.zeros((nt - n_lat, LANE), F32)], axis=0)
    return cos, sin


def _hy_features(l):
    t01 = jnp.linspace(0.0, 1.0, l, dtype=F32)
    fr = jnp.linspace(1e-4, HY_BANDS - 1, HY_BANDS, dtype=F32)
    ang = (2.0 * math.pi / l) * jnp.arange(l, dtype=F32)[:, None] * fr[None, :]
    feat = jnp.concatenate([t01[:, None], jnp.cos(ang), -jnp.sin(ang)], axis=-1)
    feat = jnp.pad(feat, ((0, 0), (0, LANE - HY_EMB)))
    return feat.reshape(FFT_N1 // 2, l // (FFT_N1 // 2), LANE).transpose(1, 0, 2).reshape(l, LANE)


def kernel(x, c, ctx, c_ctx, ada_w, ada_b, w_in, attn_sink, pool_w, pool_scale, hy_conv_w, hy_conv_b, hy_w1, hy_b1, hy_freq1, hy_w2, hy_b2, hy_freq2, hy_w3, hy_bias, sc_conv_w, w_branch, w_out, ln1_g, ln1_b, ln2_g, ln2_b, rg_w, rg_b, re_w, re_b, ex_w1, ex_w3, ex_w2):
    depth = w_in.shape[0]
    assert x.shape[0] == 1 and ctx.shape[0] == 1 and x.shape[2] == D
    n_lat = x.shape[1]
    n_ctx = ctx.shape[1]
    assert n_ctx == TM and n_lat % TM1 == 0 and n_lat >= 2 * TM1
    nt = n_lat + n_ctx + PADR
    ntiles1 = nt // TM1
    alpha = (2 * depth) ** 0.25

    cos_t, sin_t = _rope_tables(n_lat, nt)
    w_main_bf = w_in[:, :, :MAIN_COLS].astype(BF16)
    wg_bf = w_in[:, :, MAIN_COLS:].astype(BF16)
    wb_bf = w_branch.astype(BF16)
    wo_bf = w_out.astype(BF16)
    pool_w_bf = pool_w.astype(BF16)
    r3 = lambda a: a.reshape(depth, 1, a.shape[-1])
    tabs_l = _dft_tables(n_lat // (FFT_N1 // 2))
    tabs_c = _dft_tables(n_ctx // (FFT_N1 // 2))
    feat_l, feat_c = _hy_features(n_lat), _hy_features(n_ctx)
    deltas = jnp.abs(jnp.linspace(math.log(HY_TARGET) / HY_SLOW_PCT, math.log(HY_TARGET) / HY_FAST_PCT,
                                  BW, dtype=F32))[None, :]
    w1p = jnp.pad(hy_w1, ((0, 0), (0, LANE - HY_EMB), (0, 0)))
    w3p_bf = hy_w3.reshape(depth, HY_HIDDEN, 2, 2, BW).transpose(0, 1, 3, 2, 4).reshape(depth, HY_HIDDEN, 4 * BW).astype(BF16)
    hyw = (w1p, r3(hy_b1), r3(hy_freq1), hy_w2, r3(hy_b2), r3(hy_freq2), w3p_bf, deltas, hy_bias)
    rw = jnp.concatenate([re_w, rg_w, jnp.zeros((depth, D, LANE - N_EXPERTS - N_GROUPS), F32)], axis=2)
    rw_hi = rw.astype(BF16)
    rw = jnp.stack([rw_hi, (rw - rw_hi.astype(F32)).astype(BF16)], axis=1)
    rb = jnp.concatenate([re_b, rg_b, jnp.full((depth, LANE - N_EXPERTS - N_GROUPS), NEG_INF, F32)], axis=1)
    rb = rb.reshape(depth, 1, LANE)
    upper = (jnp.arange(TM1)[:, None] < jnp.arange(TM1)[None, :]).astype(BF16)
    rmax = -(-(2 * (n_lat + n_ctx) + ntiles1 * N_EXPERTS * (MOE_PIECE - 1) + N_EXPERTS * (MOE_MB - 1)) // MOE_MB) * MOE_MB
    nblk = rmax // MOE_MB

    s8 = jnp.concatenate([c, c_ctx[None, :], jnp.zeros((SUB - 2, D), F32)], axis=0)
    mod = _mod_all(s8, ada_w, ada_b.reshape(depth, 1, 6 * D))

    xs = jnp.concatenate([x[0], ctx[0], jnp.zeros((PADR, D), F32)], axis=0)
    for i in range(depth):
        last = i == depth - 1
        mp = lambda j: mod[i, 0:2, j * D:(j + 1) * D].reshape(2, 1, D)
        sh1, sc1, g1, sh2, sc2, g2 = (mp(j) for j in range(6))

        q, k, v, pool_in, hy_in, sc_in, hmod = _inproj(xs, sh1, sc1, cos_t, sin_t, w_main_bf, i, n_lat // TM1)
        att = _attention(attn_sink[i], q, k, v, n_lat)
        ypool, ysc, x1p, x2p, vp, x1c, x2c, vc = _local_ops(
            pool_in, sc_in, hy_in, pool_w_bf, pool_scale.reshape(depth, 1, BW),
            hy_conv_w, hy_conv_b.reshape(depth, 1, 3 * BW), sc_conv_w, i, n_lat, n_ctx)
        yhp = _hyena_latent(x1p, x2p, vp, feat_l, tabs_l, hyw, i, n_lat)
        if last:
            yhc = jnp.zeros((n_ctx, BW), BF16)
        else:
            yhc = _hyena(x1c, x2c, vc, feat_c, tabs_c, hyw, i, n_ctx, 0)
        x1, m_bf, rt, cnt = _merge(att, ypool, yhp, yhc, ysc, hmod, wg_bf, xs, wb_bf, wo_bf, g1, r3(ln1_g), r3(ln1_b),
                                   sh2, sc2, rw, rb, i, n_lat, n_ctx, alpha)

        npieces, soff, doff, ends = _moe_plan(cnt, ntiles1)
        nvb = (ends[-1] // MOE_MB).astype(I32).reshape(1)
        blk_start = jnp.arange(nblk, dtype=I32) * MOE_MB
        blk_e = jnp.minimum(jnp.sum((ends[None, :] <= blk_start[:, None]).astype(I32), axis=1), N_EXPERTS - 1)
        hs, pos = _dispatch(npieces, soff, doff, nvb, m_bf, rt, upper, rmax)
        ys = _experts(blk_e, nvb, hs, ex_w1, ex_w3, ex_w2, i)
        if last:
            xs = _combine(npieces, soff, doff, ys, pos, rt, x1, g2, r3(ln2_g), r3(ln2_b), i, n_lat,
                          n_lat // TM1, n_lat, alpha)
        else:
            xs = _combine(npieces, soff, doff, ys, pos, rt, x1, g2, r3(ln2_g), r3(ln2_b), i, n_lat, ntiles1, nt,
                          alpha)
    return xs[None]
```

```python
import functools
import math

import jax
import jax.numpy as jnp
from jax import lax
from jax.experimental import pallas as pl
from jax.experimental.pallas import tpu as pltpu

F32 = jnp.float32
BF16 = jnp.bfloat16
I32 = jnp.int32
U32 = jnp.uint32
SDS = jax.ShapeDtypeStruct
HIGHEST = lax.Precision.HIGHEST

D = 1024
GRID_W = 64
HEADS = 8
KV_HEADS = 2
HEAD_DIM = 64
WINDOW = 128
ATT_BLK = 128
ROPE_PAIRS = 16
ROPE_BASE = 10000.0
BW = 512
POOL_WINDOWS = (2, 4, 8, 16)
POOL_GW = 128
HY_EMB = 33
HY_BANDS = 16
HY_HIDDEN = 64
HY_TARGET = 1e-2
HY_FAST_PCT = 0.3
HY_SLOW_PCT = 1.5
HY_MOD_SHIFT = 0.05
Q_COLS = HEADS * HEAD_DIM
KV_COLS = KV_HEADS * HEAD_DIM
MAIN_COLS = Q_COLS + 2 * KV_COLS + BW + 3 * BW + 3 * BW
N_GROUPS = 4
EPG = 8
N_EXPERTS = 32
EXPERT_HIDDEN = 512
LN_EPS = 1e-6
NEG_INF = -1e30

LANE = 128
SUB = 8
BSUB = 16
TM = 256
TM1 = 512
PADR = 256
HALO = 16
FFT_N1 = 512
FFT_K1 = 264
FFT_KB = 8
FFT_TN = 2048
FFT_NB = 8
MOE_PIECE = 8
MOE_MB = 256
MOE_S = 2 * TM1 + 256


def _cparams(sem, vmem_mb):
    return pltpu.CompilerParams(dimension_semantics=sem, vmem_limit_bytes=vmem_mb * 2 ** 20)


def _dot_bf16x3(a, b):
    a_hi = a.astype(BF16)
    a_lo = (a - a_hi.astype(F32)).astype(BF16)
    b_hi = b.astype(BF16)
    b_lo = (b - b_hi.astype(F32)).astype(BF16)
    return (jnp.dot(a_hi, b_hi, preferred_element_type=F32) + jnp.dot(a_lo, b_hi, preferred_element_type=F32)
            + jnp.dot(a_hi, b_lo, preferred_element_type=F32))


def _layer_norm(x):
    mu = jnp.mean(x, axis=-1, keepdims=True)
    xc = x - mu
    var = jnp.mean(xc * xc, axis=-1, keepdims=True)
    return xc * lax.rsqrt(var + LN_EPS)


def _mod_body(s_ref, w_ref, b_ref, o_ref):
    s = s_ref[...]
    s = s * (1.0 / (1.0 + jnp.exp(-s)))
    o_ref[...] = jnp.dot(s.astype(BF16), w_ref[...].astype(BF16), preferred_element_type=F32) + b_ref[...]


def _mod_all(s8, ada_w, ada_b3):
    depth = ada_w.shape[0]
    tn = 1024
    return pl.pallas_call(
        _mod_body,
        out_shape=SDS((depth, SUB, 6 * D), F32),
        grid=(depth, 6 * D // tn),
        in_specs=[pl.BlockSpec((SUB, D), lambda l, j: (0, 0)),
                  pl.BlockSpec((None, D, tn), lambda l, j: (l, 0, j)),
                  pl.BlockSpec((None, 1, tn), lambda l, j: (l, 0, j))],
        out_specs=pl.BlockSpec((None, SUB, tn), lambda l, j: (l, 0, j)),
        compiler_params=_cparams(("parallel", "parallel"), 32),
        name="ada_mod",
    )(s8, ada_w, ada_b3)


def _inproj_body(x_ref, sh_ref, sc_ref, cos_ref, sin_ref, w_ref,
                 q_ref, k_ref, v_ref, pool_ref, hy_ref, scv_ref, hmod_ref):
    h = _layer_norm(x_ref[...]) * (1.0 + sc_ref[...]) + sh_ref[...]
    hb = h.astype(BF16)

    def mm(a, b):
        return jnp.dot(hb, w_ref[:, a:b], preferred_element_type=F32)

    cs = cos_ref[...]
    sn = sin_ref[...]
    lane = lax.broadcasted_iota(I32, cs.shape, 1)
    first = (lane & ROPE_PAIRS) == 0

    def rope(t):
        sw = jnp.where(first, pltpu.roll(t, LANE - ROPE_PAIRS, 1), pltpu.roll(t, ROPE_PAIRS, 1))
        return t * cs + sw * sn

    for j in range(Q_COLS // LANE):
        q_ref[:, j * LANE:(j + 1) * LANE] = (rope(mm(j * LANE, (j + 1) * LANE)) * (HEAD_DIM ** -0.5)).astype(BF16)
    o = Q_COLS
    k_ref[...] = rope(mm(o, o + KV_COLS)).astype(BF16)
    o += KV_COLS
    v_ref[...] = mm(o, o + KV_COLS).astype(BF16)
    o += KV_COLS
    pool_ref[...] = mm(o, o + BW).astype(BF16)
    o += BW
    for j in range(3):
        hy_ref[:, j * BW:(j + 1) * BW] = mm(o + j * BW, o + (j + 1) * BW).astype(BF16)
    o += 3 * BW
    for j in range(3):
        scv_ref[:, j * BW:(j + 1) * BW] = mm(o + j * BW, o + (j + 1) * BW).astype(BF16)
    hmod_ref[...] = hb


def _inproj(x, sh, sc, cos_t, sin_t, w_in_bf, layer, n_lat_tiles):
    nt = x.shape[0]
    sel = lambda i: (jnp.where(i >= n_lat_tiles, 1, 0), 0, 0)
    row = lambda c: pl.BlockSpec((TM1, c), lambda i: (i, 0))
    outs = [SDS((nt, Q_COLS), BF16), SDS((nt, KV_COLS), BF16), SDS((nt, KV_COLS), BF16), SDS((nt, BW), BF16),
            SDS((nt, 3 * BW), BF16), SDS((nt, 3 * BW), BF16), SDS((nt, D), BF16)]
    return pl.pallas_call(
        _inproj_body,
        out_shape=outs,
        grid=(nt // TM1,),
        in_specs=[row(D),
                  pl.BlockSpec((None, 1, D), sel), pl.BlockSpec((None, 1, D), sel),
                  row(LANE), row(LANE),
                  pl.BlockSpec((None, D, MAIN_COLS), lambda i: (layer, 0, 0), pipeline_mode=pl.Buffered(1))],
        out_specs=[row(Q_COLS), row(KV_COLS), row(KV_COLS), row(BW), row(3 * BW), row(3 * BW), row(D)],
        compiler_params=_cparams(("parallel",), 58),
        name="inproj",
    )(x, sh, sc, cos_t, sin_t, w_in_bf)


def _attn_body(sink_ref, q_ref, kp_ref, km_ref, kn_ref, vp_ref, vm_ref, vn_ref, kx_ref, vx_ref, o_ref, *, nlt):
    i = pl.program_id(0)
    is_lat = i < nlt
    lat_i = jnp.where(is_lat, 1, 0)
    first_i = jnp.where(jnp.logical_and(is_lat, i > 0), 1, 0)
    last_i = jnp.where(jnp.logical_and(is_lat, i < nlt - 1), 1, 0)
    nsub = TM1 // ATT_BLK

    def variants(x):
        xf = x.astype(F32)
        lo = lax.broadcasted_iota(I32, xf.shape, 1) < HEAD_DIM
        xr = pltpu.roll(xf, HEAD_DIM, 1)
        zero = jnp.zeros_like(xf)
        out = [[jnp.where(lo, xf, zero), jnp.where(lo, zero, xr)], [jnp.where(lo, xr, zero), jnp.where(lo, zero, xf)]]
        return [[a.astype(BF16) for a in row] for row in out]

    k_ctx, v_ctx = variants(kx_ref[...]), variants(vx_ref[...])
    k_loc = variants(jnp.concatenate([kp_ref[...], km_ref[...], kn_ref[...]], axis=0))
    v_loc = variants(jnp.concatenate([vp_ref[...], vm_ref[...], vn_ref[...]], axis=0))

    r = lax.broadcasted_iota(I32, (ATT_BLK, 3 * ATT_BLK), 0)
    c = lax.broadcasted_iota(I32, (ATT_BLK, 3 * ATT_BLK), 1)
    band = jnp.logical_and(c >= r, c <= r + 2 * WINDOW)
    nt_dims = (((1,), (1,)), ((), ()))
    for s in range(nsub):
        prev_i = lat_i if s > 0 else first_i
        next_i = lat_i if s < nsub - 1 else last_i
        grp = jnp.where(c < ATT_BLK, prev_i, jnp.where(c < 2 * ATT_BLK, lat_i, next_i))
        mask = jnp.logical_and(band, grp > 0)
        win = slice(s * ATT_BLK, (s + 3) * ATT_BLK)
        rows = slice(s * ATT_BLK, (s + 1) * ATT_BLK)
        for p in range(HEADS // 2):
            qp = q_ref[rows, p * LANE:(p + 1) * LANE]
            kh = (2 * p) // (HEADS // KV_HEADS)
            acc = jnp.zeros((ATT_BLK, LANE), F32)
            for rr in range(2):
                sink = sink_ref[2 * p + rr]
                s_ctx = lax.dot_general(qp, k_ctx[kh][rr], nt_dims, preferred_element_type=F32)
                s_loc = lax.dot_general(qp, k_loc[kh][rr][win], nt_dims, preferred_element_type=F32)
                s_loc = jnp.where(mask, s_loc, NEG_INF)
                m = jnp.maximum(jnp.maximum(jnp.max(s_ctx, axis=1, keepdims=True),
                                            jnp.max(s_loc, axis=1, keepdims=True)), sink)
                e_ctx = jnp.exp(s_ctx - m)
                e_loc = jnp.exp(s_loc - m)
                den = (jnp.sum(e_ctx, axis=1, keepdims=True) + jnp.sum(e_loc, axis=1, keepdims=True)
                       + jnp.exp(sink - m))
                o = (jnp.dot(e_ctx.astype(BF16), v_ctx[kh][rr], preferred_element_type=F32)
                     + jnp.dot(e_loc.astype(BF16), v_loc[kh][rr][win], preferred_element_type=F32))
                acc = acc + o * (1.0 / den)
            o_ref[rows, p * LANE:(p + 1) * LANE] = acc.astype(BF16)


def _attention(sink, q, k, v, n_lat):
    nt = q.shape[0]
    nlt = n_lat // TM1
    nlb = n_lat // ATT_BLK
    per = TM1 // ATT_BLK
    cxb = n_lat // TM
    side = lambda f: pl.BlockSpec((ATT_BLK, KV_COLS), lambda i: (jnp.clip(f(i), 0, nlb - 1), 0))
    prev = side(lambda i: i * per - 1)
    nxt = side(lambda i: (i + 1) * per)
    main = pl.BlockSpec((TM1, KV_COLS), lambda i: (jnp.minimum(i, nlt - 1), 0))
    cx = pl.BlockSpec((TM, KV_COLS), lambda i: (cxb, 0))
    return pl.pallas_call(
        functools.partial(_attn_body, nlt=nlt),
        out_shape=SDS((nt, Q_COLS), BF16),
        grid=(nt // TM1,),
        in_specs=[pl.BlockSpec(memory_space=pltpu.SMEM),
                  pl.BlockSpec((TM1, Q_COLS), lambda i: (i, 0)),
                  prev, main, nxt, prev, main, nxt, cx, cx],
        out_specs=pl.BlockSpec((TM1, Q_COLS), lambda i: (i, 0)),
        compiler_params=_cparams(("parallel",), 40),
        name="attention",
    )(sink, q, k, k, k, v, v, v, k, v)


def _pack2(a, b):
    hi = lax.bitcast_convert_type(a.astype(BF16).astype(F32), U32)
    lo = lax.bitcast_convert_type(b.astype(BF16).astype(F32), U32)
    return hi | (lo >> 16)


def _unpack2(w):
    return (lax.bitcast_convert_type(w & jnp.uint32(0xFFFF0000), F32),
            lax.bitcast_convert_type(w << 16, F32))


def _local_body(pm_ref, pa_ref, pb_ref, sm_ref, sa_ref, sb_ref, hm_ref, ha_ref, hb_ref,
                pw_ref, ps_ref, hw_ref, hbias_ref, sw_ref, band_ref,
                ypool_ref, ysc_ref, x1p_ref, x2p_ref, vp_ref, x1c_ref, x2c_ref, vc_ref,
                pext, sext, hext, *, nlat, n_lat, n_ctx):
    i = pl.program_id(0)
    is_lat = i < nlat
    pf = jnp.where(jnp.logical_and(is_lat, i > 0), 1.0, 0.0)
    nf = jnp.where(jnp.logical_and(is_lat, i < nlat - 1), 1.0, 0.0)
    r = lax.broadcasted_iota(I32, (TM1, 1), 0)
    keep = jnp.where(jnp.logical_or(is_lat, r < n_ctx), 1.0, 0.0)

    def fill(ext, a_ref, m_ref, b_ref):
        ext[0:HALO, :] = a_ref[...].astype(F32) * pf
        ext[HALO:HALO + TM1, :] = m_ref[...].astype(F32) * keep
        ext[HALO + TM1:HALO + TM1 + HALO, :] = b_ref[...].astype(F32) * nf

    def sh(ext, d, c0=None, c1=None):
        if c0 is None:
            return ext[pl.ds(HALO + d, TM1), :]
        return ext[pl.ds(HALO + d, TM1), c0:c1]

    fill(pext, pa_ref, pm_ref, pb_ref)
    t = jnp.where(is_lat, i * TM1, 0) + r
    ln = jnp.where(is_lat, n_lat, n_ctx)
    for g, w in enumerate(POOL_WINDOWS):
        left = w // 2
        right = w - left - 1
        c0, c1 = g * POOL_GW, (g + 1) * POOL_GW
        acc = jnp.dot(band_ref[g], pext[:, c0:c1].astype(BF16), preferred_element_type=F32)
        cnt = jnp.maximum(jnp.minimum(t + right + 1, ln) - jnp.maximum(t - left, 0), 1).astype(F32)
        dpool = acc / cnt - sh(pext, 0, c0, c1)
        y = jnp.dot(dpool.astype(BF16), pw_ref[g], preferred_element_type=F32)
        ypool_ref[:, c0:c1] = (y * ps_ref[:, c0:c1]).astype(BF16)

    sext[0:HALO, :] = (sa_ref[:, BW:2 * BW].astype(F32) * sa_ref[:, 2 * BW:3 * BW].astype(F32)) * pf
    sext[HALO:HALO + TM1, :] = sm_ref[:, BW:2 * BW].astype(F32) * sm_ref[:, 2 * BW:3 * BW].astype(F32) * keep
    sext[HALO + TM1:HALO + TM1 + HALO, :] = (sb_ref[:, BW:2 * BW].astype(F32) * sb_ref[:, 2 * BW:3 * BW].astype(F32)) * nf
    conv = sh(sext, -1) * sw_ref[0:1, :] + sh(sext, 0) * sw_ref[1:2, :] + sh(sext, 1) * sw_ref[2:3, :]
    ysc_ref[...] = (sm_ref[:, 0:BW].astype(F32) * conv).astype(BF16)

    fill(hext, ha_ref, hm_ref, hb_ref)
    nat = (x1c_ref, x2c_ref, vc_ref)
    pk = (x1p_ref, x2p_ref, vp_ref)
    n2c = x1p_ref.shape[2] // FFT_NB
    for j in range(3):
        c0, c1 = j * BW, (j + 1) * BW
        uc = (sh(hext, -1, c0, c1) * hw_ref[0:1, c0:c1] + sh(hext, 0, c0, c1) * hw_ref[1:2, c0:c1]
              + sh(hext, 1, c0, c1) * hw_ref[2:3, c0:c1] + hbias_ref[:, c0:c1])
        nat[j][...] = uc.astype(BF16)
        word = _pack2(uc[:, 0:BW // 2], uc[:, BW // 2:BW])
        for a in range(TM1 // n2c):
            for q in range(BW // 2 // LANE):
                pk[j][q, a // FFT_NB, pl.ds(a % FFT_NB, n2c, stride=FFT_NB), :] = (
                    word[a * n2c:(a + 1) * n2c, q * LANE:(q + 1) * LANE])


def _pool_bands():
    r = jnp.arange(TM1)[:, None]
    c = jnp.arange(TM1 + 2 * HALO)[None, :] - HALO
    return jnp.stack([((c >= r - w // 2) & (c <= r + (w - w // 2 - 1))).astype(BF16) for w in POOL_WINDOWS])


def _local_ops(pool_in, sc_in, hy_in, pool_w_bf, pool_scale, hy_conv_w, hy_conv_b, sc_conv_w, layer, n_lat, n_ctx):
    nt = pool_in.shape[0]
    nlat = n_lat // TM1
    n2c = n_lat // (FFT_N1 // 2)
    per_tile = TM1 // n2c
    nhb = nt // HALO
    per = TM1 // HALO
    main = lambda c: pl.BlockSpec((TM1, c), lambda i: (i, 0))
    before = lambda c: pl.BlockSpec((HALO, c), lambda i: (jnp.maximum(i * per - 1, 0), 0))
    after = lambda c: pl.BlockSpec((HALO, c), lambda i: (jnp.minimum((i + 1) * per, nhb - 1), 0))
    lsel = lambda *blk: pl.BlockSpec((None,) + blk, lambda i: (layer,) + (0,) * len(blk))
    out = SDS((nt, BW), BF16)
    nq, g8 = BW // 2 // LANE, per_tile // FFT_NB
    packed = SDS((nq, (nlat + 1) * g8, n2c * FFT_NB, LANE), U32)
    ctx_out = SDS((TM1, BW), BF16)
    pspec = pl.BlockSpec((nq, g8, n2c * FFT_NB, LANE), lambda i: (0, i, 0, 0))
    cspec = pl.BlockSpec((TM1, BW), lambda i: (0, 0))
    return pl.pallas_call(
        functools.partial(_local_body, nlat=nlat, n_lat=n_lat, n_ctx=n_ctx),
        out_shape=[out, out, packed, packed, packed, ctx_out, ctx_out, ctx_out],
        grid=(nlat + 1,),
        in_specs=[main(BW), before(BW), after(BW),
                  main(3 * BW), before(3 * BW), after(3 * BW),
                  main(3 * BW), before(3 * BW), after(3 * BW),
                  lsel(4, POOL_GW, POOL_GW), lsel(1, BW), lsel(3, 3 * BW), lsel(1, 3 * BW), lsel(3, BW),
                  pl.BlockSpec((len(POOL_WINDOWS), TM1, TM1 + 2 * HALO), lambda i: (0, 0, 0))],
        out_specs=[main(BW), main(BW), pspec, pspec, pspec, cspec, cspec, cspec],
        scratch_shapes=[pltpu.VMEM((TM1 + 2 * HALO, BW), F32), pltpu.VMEM((TM1 + 2 * HALO, BW), F32),
                        pltpu.VMEM((TM1 + 2 * HALO, 3 * BW), F32)],
        compiler_params=_cparams(("arbitrary",), 48),
        name="local_ops",
    )(pool_in, pool_in, pool_in, sc_in, sc_in, sc_in, hy_in, hy_in, hy_in,
      pool_w_bf, pool_scale, hy_conv_w, hy_conv_b, sc_conv_w, _pool_bands())


def _dft_tables(n2):
    n = FFT_N1 * n2
    k1 = jnp.arange(FFT_K1, dtype=I32)
    n1 = jnp.arange(FFT_N1 // 2, dtype=I32)
    ok = (k1 <= FFT_N1 // 2)
    ang = (2.0 * math.pi / FFT_N1) * ((k1[:, None] * n1[None, :]) % FFT_N1).astype(F32)
    c1 = jnp.where(ok[:, None], jnp.cos(ang), 0.0)
    s1 = jnp.where(ok[:, None], jnp.sin(ang), 0.0)
    f1 = jnp.concatenate([c1, -s1], axis=0).astype(BF16)
    wt = jnp.where((k1 == 0) | (k1 == FFT_N1 // 2), 1.0, 2.0) / n
    f1i = jnp.concatenate([(c1 * wt[:, None]).T, (-s1 * wt[:, None]).T], axis=1).astype(BF16)
    if n2 == 1:
        return f1, f1i, None, None
    k2 = jnp.arange(n2, dtype=I32)
    m2 = jnp.arange(n2, dtype=I32)
    kk = k1[:, None, None] + FFT_N1 * k2[None, :, None]
    th = (2.0 * math.pi / n) * ((kk * m2[None, None, :]) % n).astype(F32)
    okb = ok[:, None, None]
    cc = jnp.where(okb, jnp.cos(th), 0.0)
    ss = jnp.where(okb, jnp.sin(th), 0.0)
    g = jnp.concatenate([jnp.concatenate([cc, ss], axis=2), jnp.concatenate([-ss, cc], axis=2)], axis=1)
    return f1, f1i, g.astype(BF16), jnp.swapaxes(g, 1, 2).astype(BF16)


def _filter_body(feat_ref, w1_ref, b1_ref, f1_ref, w2_ref, b2_ref, f2_ref, w3_ref, dl_ref, dft_ref,
                 af_ref, ss_ref, filt):
    i = pl.program_id(0)
    feat = feat_ref[...]
    h = jnp.sin(f1_ref[...] * (_dot_bf16x3(feat, w1_ref[...]) + b1_ref[...]))
    h = jnp.sin(f2_ref[...] * (_dot_bf16x3(h, w2_ref[...]) + b2_ref[...]))
    raw = jnp.dot(h.astype(BF16), w3_ref[...], preferred_element_type=F32)
    win = jnp.exp(-feat[:, 0:1] * dl_ref[...]) + HY_MOD_SHIFT
    row = lax.broadcasted_iota(I32, (TM, 1), 0) + i * TM

    @pl.when(i == 0)
    def _():
        ss_ref[...] = jnp.zeros_like(ss_ref)

    for j in range(4):
        f = raw[:, j * BW:(j + 1) * BW] * win
        if j >= 2:
            f = jnp.where(row == 0, 0.0, f)
        filt[:, j * BW:(j + 1) * BW] = f.astype(BF16)
        ss_ref[:, j * BW:(j + 1) * BW] += jnp.sum(f * f, axis=0, keepdims=True)
    af_ref[...] = jnp.dot(dft_ref[...], filt[...], preferred_element_type=F32).astype(BF16)


def _hy_filters(feat_perm, w1p, b1, fr1, w2, b2, fr2, w3p_bf, deltas, f1, layer):
    l = feat_perm.shape[0]
    n2 = l // TM
    lsel = lambda *blk: pl.BlockSpec((None,) + blk, lambda i: (layer,) + (0,) * len(blk))
    return pl.pallas_call(
        _filter_body,
        out_shape=[SDS((2 * FFT_K1, n2 * 4 * BW), BF16), SDS((1, 4 * BW), F32)],
        grid=(n2,),
        in_specs=[pl.BlockSpec((TM, LANE), lambda i: (i, 0)),
                  lsel(LANE, HY_HIDDEN), lsel(1, HY_HIDDEN), lsel(1, HY_HIDDEN),
                  lsel(HY_HIDDEN, HY_HIDDEN), lsel(1, HY_HIDDEN), lsel(1, HY_HIDDEN),
                  lsel(HY_HIDDEN, 4 * BW), pl.BlockSpec((1, BW), lambda i: (0, 0)),
                  pl.BlockSpec((2 * FFT_K1, FFT_N1 // 2), lambda i: (0, 0))],
        out_specs=[pl.BlockSpec((2 * FFT_K1, 4 * BW), lambda i: (0, i)), pl.BlockSpec((1, 4 * BW), lambda i: (0, 0))],
        scratch_shapes=[pltpu.VMEM((TM, 4 * BW), BF16)],
        compiler_params=_cparams(("arbitrary",), 32),
        name="hy_filters",
    )(feat_perm, w1p, b1, fr1, w2, b2, fr2, w3p_bf, deltas, f1)


def _dft_major_body(f_ref, x_ref, o_ref):
    o_ref[...] = jnp.dot(f_ref[...], x_ref[...], preferred_element_type=F32).astype(BF16)


def _dft_major(f1, x2d, row_blk, ncols):
    tn = min(FFT_TN, ncols)
    return pl.pallas_call(
        _dft_major_body,
        out_shape=SDS((2 * FFT_K1, ncols), BF16),
        grid=(ncols // tn,),
        in_specs=[pl.BlockSpec((2 * FFT_K1, FFT_N1 // 2), lambda j: (0, 0)),
                  pl.BlockSpec((FFT_N1 // 2, tn), lambda j: (row_blk, j))],
        out_specs=pl.BlockSpec((2 * FFT_K1, tn), lambda j: (0, j)),
        compiler_params=_cparams(("parallel",), 32),
        name="dft_major",
    )(f1, x2d)


def _filter_minor_body(a_ref, g_ref, ss_ref, h_ref, *, n2):
    ssv = ss_ref[...]
    scale = lax.rsqrt(ssv[:, 0:2 * BW] + ssv[:, 2 * BW:4 * BW] + LN_EPS)
    for j in range(a_ref.shape[1]):
        if n2 == 1:
            s = jnp.concatenate([a_ref[0, j], a_ref[1, j]], axis=0).astype(F32)
        else:
            a = jnp.concatenate([a_ref[0, j], a_ref[1, j]], axis=0)
            s = jnp.dot(g_ref[j], a, preferred_element_type=F32)
        sf, sb = s[:, 0:2 * BW], s[:, 2 * BW:4 * BW]
        hr = (sf[0:n2] + sb[0:n2]) * scale
        hi = (sf[n2:2 * n2] - sb[n2:2 * n2]) * scale
        h_ref[j] = jnp.concatenate([hr, hi], axis=0).astype(BF16)


def _conv_minor_body(a_ref, h_ref, g_ref, gt_ref, d_ref, *, n2):
    for j in range(a_ref.shape[1]):
        a = jnp.concatenate([a_ref[0, j], a_ref[1, j]], axis=0)
        if n2 == 1:
            x = a.astype(F32)
        else:
            x = jnp.dot(g_ref[j], a, preferred_element_type=F32)
        h = h_ref[j].astype(F32)
        xr, xi, hr, hi = x[0:n2], x[n2:2 * n2], h[0:n2], h[n2:2 * n2]
        y = jnp.concatenate([xr * hr - xi * hi, xr * hi + xi * hr], axis=0)
        if n2 == 1:
            dv = y
        else:
            dv = jnp.dot(gt_ref[j], y.astype(BF16), preferred_element_type=F32)
        d_ref[0, j] = dv[0:n2].astype(BF16)
        d_ref[1, j] = dv[n2:2 * n2].astype(BF16)


def _filter_minor(af, g, ss, n2):
    a4 = af.reshape(2, FFT_K1, n2, 4 * BW)
    kb = FFT_KB
    gspec = (pl.BlockSpec((kb, 2 * n2, 2 * n2), lambda i: (i, 0, 0)) if n2 > 1
             else pl.BlockSpec((SUB, LANE), lambda i: (0, 0)))
    return pl.pallas_call(
        functools.partial(_filter_minor_body, n2=n2),
        out_shape=SDS((FFT_K1, 2 * n2, 2 * BW), BF16),
        grid=(FFT_K1 // kb,),
        in_specs=[pl.BlockSpec((2, kb, n2, 4 * BW), lambda i: (0, i, 0, 0)), gspec,
                  pl.BlockSpec((1, 4 * BW), lambda i: (0, 0))],
        out_specs=pl.BlockSpec((kb, 2 * n2, 2 * BW), lambda i: (i, 0, 0)),
        compiler_params=_cparams(("parallel",), 40),
        name="hy_filter_minor",
    )(a4, g if n2 > 1 else jnp.zeros((SUB, LANE), BF16), ss)


def _conv_minor(a, hspec, g, gt, order, n2):
    a4 = a.reshape(2, FFT_K1, n2, BW)
    kb = FFT_KB
    dummy = jnp.zeros((SUB, LANE), BF16)
    gspec = (pl.BlockSpec((kb, 2 * n2, 2 * n2), lambda i: (i, 0, 0)) if n2 > 1
             else pl.BlockSpec((SUB, LANE), lambda i: (0, 0)))
    d4 = pl.pallas_call(
        functools.partial(_conv_minor_body, n2=n2),
        out_shape=SDS((2, FFT_K1, n2, BW), BF16),
        grid=(FFT_K1 // kb,),
        in_specs=[pl.BlockSpec((2, kb, n2, BW), lambda i: (0, i, 0, 0)),
                  pl.BlockSpec((kb, 2 * n2, BW), lambda i: (i, 0, order)), gspec, gspec],
        out_specs=pl.BlockSpec((2, kb, n2, BW), lambda i: (0, i, 0, 0)),
        compiler_params=_cparams(("parallel",), 32),
        name="hy_conv_minor",
    )(a4, hspec, g if n2 > 1 else dummy, gt if n2 > 1 else dummy)
    return d4.reshape(2 * FFT_K1, n2 * BW)


def _idft_gate_body(f_ref, d_ref, gate_ref, z_ref, b_ref, o_ref):
    y = jnp.dot(f_ref[...], d_ref[...], preferred_element_type=F32)
    z = z_ref[...].astype(F32)
    o_ref[...] = (gate_ref[...].astype(F32) * (y + b_ref[...] * z)).astype(BF16)


def _idft_gate(f1i, d, gate2d, gate_blk, z2d, z_blk, bias_t, ncols):
    tn = min(FFT_TN, ncols)
    return pl.pallas_call(
        _idft_gate_body,
        out_shape=SDS((FFT_N1 // 2, ncols), BF16),
        grid=(ncols // tn,),
        in_specs=[pl.BlockSpec((FFT_N1 // 2, 2 * FFT_K1), lambda j: (0, 0)),
                  pl.BlockSpec((2 * FFT_K1, tn), lambda j: (0, j)),
                  pl.BlockSpec((FFT_N1 // 2, tn), lambda j: (gate_blk, j)),
                  pl.BlockSpec((FFT_N1 // 2, tn), lambda j: (z_blk, j)),
                  pl.BlockSpec((1, tn), lambda j: (0, 0))],
        out_specs=pl.BlockSpec((FFT_N1 // 2, tn), lambda j: (0, j)),
        compiler_params=_cparams(("parallel",), 32),
        name="idft_gate",
    )(f1i, d, gate2d, z2d, bias_t)


def _hyena(x1, x2, v, feat, tabs, hyw, layer, seq_len, row0):
    n2 = seq_len // (FFT_N1 // 2)
    f1, f1i, g, gt = tabs
    w1p, b1, fr1, w2, b2, fr2, w3p_bf, deltas, hy_bias = hyw
    nt = x1.shape[0]
    af, ss = _hy_filters(feat, w1p, b1, fr1, w2, b2, fr2, w3p_bf, deltas, f1, layer)
    hspec = _filter_minor(af, g, ss, n2)
    view = lambda a: a.reshape(nt // n2, n2 * BW)
    blk = row0 // seq_len if n2 == 1 else 0
    ncols = n2 * BW
    tn = min(FFT_TN, ncols)
    z, zv, zblk = v, view(v), blk
    gates = (x1, x2)
    for o in range(2):
        a = _dft_major(f1, zv, zblk, ncols)
        d = _conv_minor(a, hspec, g, gt, o, n2)
        bias_t = jnp.tile(hy_bias[layer, o][None, :], (1, tn // BW))
        z2 = _idft_gate(f1i, d, view(gates[o]), blk, zv, zblk, bias_t, ncols)
        zv, zblk = z2, 0
    return zv.reshape(seq_len, BW)


def _store_spectrum_slab(ref, j, word):
    for q in range(word.shape[1] // LANE):
        ref[q, pl.ds(j, FFT_K1, stride=FFT_NB), :] = word[:, q * LANE:(q + 1) * LANE]


def _load_spectrum_slab(ref, j):
    return jnp.concatenate([ref[q, pl.ds(j, FFT_K1, stride=FFT_NB), :] for q in range(ref.shape[0])], axis=1)


def _load_spectrum_k1(ref, j):
    nq, nblk = ref.shape[0], ref.shape[1]
    return jnp.concatenate([jnp.concatenate([ref[q, b, j] for q in range(nq)], axis=1) for b in range(nblk)], axis=0)


def _store_spectrum_k1(ref, j, word):
    nq, nblk = ref.shape[0], ref.shape[1]
    for b in range(nblk):
        for q in range(nq):
            ref[q, b, j] = word[b * FFT_NB:(b + 1) * FFT_NB, q * LANE:(q + 1) * LANE]


def _filter_p_body(feat_ref, w1_ref, b1_ref, f1_ref, w2_ref, b2_ref, f2_ref, w3_ref, dl_ref, dft_ref,
                   af_ref, ss_ref, h2):
    nb = pl.program_id(0)
    cb = pl.program_id(1)
    nbk = FFT_NB

    @pl.when(jnp.logical_and(nb == 0, cb == 0))
    def _():
        ss_ref[...] = jnp.zeros_like(ss_ref)

    @pl.when(cb == 0)
    def _():
        feat = feat_ref[...]
        h = jnp.sin(f1_ref[...] * (_dot_bf16x3(feat, w1_ref[...]) + b1_ref[...]))
        h = jnp.sin(f2_ref[...] * (_dot_bf16x3(h, w2_ref[...]) + b2_ref[...]))
        h2[...] = h.astype(BF16)

    row = lax.broadcasted_iota(I32, (TM, 1), 0)
    lag0_bwd = jnp.logical_and(jnp.logical_and(nb == 0, cb >= 2), row == 0)
    ssum = jnp.zeros((1, BW), F32)
    for j in range(nbk):
        raw = jnp.dot(h2[j * TM:(j + 1) * TM, :], w3_ref[...], preferred_element_type=F32)
        win = jnp.exp(-feat_ref[j * TM:(j + 1) * TM, 0:1] * dl_ref[...]) + HY_MOD_SHIFT
        f = raw * win
        if j == 0:
            f = jnp.where(lag0_bwd, 0.0, f)
        ssum = ssum + jnp.sum(f * f, axis=0, keepdims=True)
        rr = jnp.dot(dft_ref[...], f.astype(BF16), preferred_element_type=F32)
        _store_spectrum_slab(af_ref, j, _pack2(rr[0:FFT_K1], rr[FFT_K1:2 * FFT_K1]))
    ss_ref[cb] = ss_ref[cb] + ssum


def _hy_filters_p(feat_perm, w1p, b1, fr1, w2, b2, fr2, w3p_bf, deltas, f1, layer):
    l = feat_perm.shape[0]
    n2 = l // TM
    nbk = FFT_NB
    lsel = lambda *blk: pl.BlockSpec((None,) + blk, lambda nb, cb: (layer,) + (0,) * len(blk))
    return pl.pallas_call(
        _filter_p_body,
        out_shape=[SDS((4 * BW // LANE, n2 // nbk, FFT_K1 * nbk, LANE), U32), SDS((4, 1, BW), F32)],
        grid=(n2 // nbk, 4),
        in_specs=[pl.BlockSpec((nbk * TM, LANE), lambda nb, cb: (nb, 0)),
                  lsel(LANE, HY_HIDDEN), lsel(1, HY_HIDDEN), lsel(1, HY_HIDDEN),
                  lsel(HY_HIDDEN, HY_HIDDEN), lsel(1, HY_HIDDEN), lsel(1, HY_HIDDEN),
                  pl.BlockSpec((None, HY_HIDDEN, BW), lambda nb, cb: (layer, 0, cb)),
                  pl.BlockSpec((1, BW), lambda nb, cb: (0, 0)),
                  pl.BlockSpec((2 * FFT_K1, FFT_N1 // 2), lambda nb, cb: (0, 0))],
        out_specs=[pl.BlockSpec((BW // LANE, None, FFT_K1 * nbk, LANE), lambda nb, cb: (cb, nb, 0, 0)),
                   pl.BlockSpec((4, 1, BW), lambda nb, cb: (0, 0, 0))],
        scratch_shapes=[pltpu.VMEM((nbk * TM, HY_HIDDEN), BF16)],
        compiler_params=_cparams(("arbitrary", "arbitrary"), 40),
        name="hy_filters_p",
    )(feat_perm, w1p, b1, fr1, w2, b2, fr2, w3p_bf, deltas, f1)


def _filter_minor_p_body(a_ref, g_ref, ss_ref, h_ref, *, n2):
    ssv = ss_ref[...]
    scale = lax.rsqrt(ssv[:, 0:2 * BW] + ssv[:, 2 * BW:4 * BW] + LN_EPS)
    for j in range(a_ref.shape[2]):
        re, im = _unpack2(_load_spectrum_k1(a_ref, j))
        a = jnp.concatenate([re, im], axis=0).astype(BF16)
        s = jnp.dot(g_ref[j], a, preferred_element_type=F32)
        sf, sb = s[:, 0:2 * BW], s[:, 2 * BW:4 * BW]
        hr = (sf[0:n2] + sb[0:n2]) * scale
        hi = (sf[n2:2 * n2] - sb[n2:2 * n2]) * scale
        h_ref[j] = jnp.concatenate([hr, hi], axis=0).astype(BF16)


def _filter_minor_p(af_p, g, ss, n2):
    kb = FFT_KB
    nq, nblk = af_p.shape[0], af_p.shape[1]
    af_p = af_p.reshape(nq, nblk, FFT_K1, FFT_NB, LANE)
    return pl.pallas_call(
        functools.partial(_filter_minor_p_body, n2=n2),
        out_shape=SDS((FFT_K1, 2 * n2, 2 * BW), BF16),
        grid=(FFT_K1 // kb,),
        in_specs=[pl.BlockSpec((nq, nblk, kb, FFT_NB, LANE), lambda i: (0, 0, i, 0, 0)),
                  pl.BlockSpec((kb, 2 * n2, 2 * n2), lambda i: (i, 0, 0)),
                  pl.BlockSpec((1, 4 * BW), lambda i: (0, 0))],
        out_specs=pl.BlockSpec((kb, 2 * n2, 2 * BW), lambda i: (i, 0, 0)),
        compiler_params=_cparams(("parallel",), 40),
        name="hy_filter_minor_p",
    )(af_p, g, ss)


def _unpack_slab(ref, j):
    nq, nblk = ref.shape[0], ref.shape[1]
    word = jnp.concatenate([jnp.concatenate([ref[q, b, j] for q in range(nq)], axis=1) for b in range(nblk)], axis=0)
    hi, lo = _unpack2(word)
    return jnp.concatenate([hi, lo], axis=1)


def _store_slab(ref, j, word):
    nq, nblk = ref.shape[0], ref.shape[1]
    for b in range(nblk):
        for q in range(nq):
            ref[q, b, j] = word[b * FFT_NB:(b + 1) * FFT_NB, q * LANE:(q + 1) * LANE]


def _slab_view(xp, n2):
    nq = xp.shape[0]
    x5 = xp.reshape(nq, xp.shape[1], n2, FFT_NB, LANE)
    spec = pl.BlockSpec((nq, FFT_N1 // 2 // FFT_NB, FFT_NB, FFT_NB, LANE), lambda i: (0, 0, i, 0, 0))
    return x5, spec


def _dft_major_p_body(f_ref, x_ref, a_ref):
    for j in range(x_ref.shape[2]):
        x = _unpack_slab(x_ref, j).astype(BF16)
        rr = jnp.dot(f_ref[...], x, preferred_element_type=F32)
        _store_spectrum_slab(a_ref, j, _pack2(rr[0:FFT_K1], rr[FFT_K1:2 * FFT_K1]))


def _dft_major_p(f1, xp, n2):
    nbk = FFT_NB
    xp, xspec = _slab_view(xp, n2)
    return pl.pallas_call(
        _dft_major_p_body,
        out_shape=SDS((BW // LANE, n2 // nbk, FFT_K1 * nbk, LANE), U32),
        grid=(n2 // nbk,),
        in_specs=[pl.BlockSpec((2 * FFT_K1, FFT_N1 // 2), lambda i: (0, 0)),
                  xspec],
        out_specs=pl.BlockSpec((BW // LANE, None, FFT_K1 * nbk, LANE), lambda i: (0, i, 0, 0)),
        compiler_params=_cparams(("parallel",), 32),
        name="dft_major_p",
    )(f1, xp)


def _conv_minor_p_body(a_ref, h_ref, g_ref, gt_ref, d_ref, *, n2):
    for j in range(a_ref.shape[2]):
        re, im = _unpack2(_load_spectrum_k1(a_ref, j))
        a = jnp.concatenate([re, im], axis=0).astype(BF16)
        x = jnp.dot(g_ref[j], a, preferred_element_type=F32)
        h = h_ref[j].astype(F32)
        xr, xi, hr, hi = x[0:n2], x[n2:2 * n2], h[0:n2], h[n2:2 * n2]
        y = jnp.concatenate([xr * hr - xi * hi, xr * hi + xi * hr], axis=0)
        dv = jnp.dot(gt_ref[j], y.astype(BF16), preferred_element_type=F32)
        _store_spectrum_k1(d_ref, j, _pack2(dv[0:n2], dv[n2:2 * n2]))


def _conv_minor_p(a_p, hspec, g, gt, order, n2):
    kb = FFT_KB
    nq, nblk = a_p.shape[0], a_p.shape[1]
    a_p = a_p.reshape(nq, nblk, FFT_K1, FFT_NB, LANE)
    gspec = pl.BlockSpec((kb, 2 * n2, 2 * n2), lambda i: (i, 0, 0))
    sspec = pl.BlockSpec((nq, nblk, kb, FFT_NB, LANE), lambda i: (0, 0, i, 0, 0))
    return pl.pallas_call(
        functools.partial(_conv_minor_p_body, n2=n2),
        out_shape=SDS((nq, nblk, FFT_K1, FFT_NB, LANE), U32),
        grid=(FFT_K1 // kb,),
        in_specs=[sspec, pl.BlockSpec((kb, 2 * n2, BW), lambda i: (i, 0, order)), gspec, gspec],
        out_specs=sspec,
        compiler_params=_cparams(("parallel",), 32),
        name="hy_conv_minor_p",
    )(a_p, hspec, g, gt)


def _idft_gate_p_body(f_ref, d_ref, gate_ref, z_ref, b_ref, o_ref):
    for j in range(o_ref.shape[2]):
        re, im = _unpack2(_load_spectrum_slab(d_ref, j))
        d = jnp.concatenate([re, im], axis=0).astype(BF16)
        y = jnp.dot(f_ref[...], d, preferred_element_type=F32)
        out = _unpack_slab(gate_ref, j) * (y + b_ref[...] * _unpack_slab(z_ref, j))
        _store_slab(o_ref, j, _pack2(out[:, 0:BW // 2], out[:, BW // 2:BW]))


def _idft_gate_p(f1i, d_p, gate_p, z_p, bias, n2):
    nbk = FFT_NB
    nq, nblk = d_p.shape[0], d_p.shape[1]
    d_p = d_p.reshape(nq, nblk, FFT_K1 * FFT_NB, LANE)
    gate_p, slab = _slab_view(gate_p, n2)
    z_p, _ = _slab_view(z_p, n2)
    nqt = gate_p.shape[0]
    return pl.pallas_call(
        _idft_gate_p_body,
        out_shape=SDS((nqt, FFT_N1 // 2 // FFT_NB, n2, FFT_NB, LANE), U32),
        grid=(n2 // nbk,),
        in_specs=[pl.BlockSpec((FFT_N1 // 2, 2 * FFT_K1), lambda i: (0, 0)),
                  pl.BlockSpec((nq, None, FFT_K1 * FFT_NB, LANE), lambda i: (0, i, 0, 0)),
                  slab, slab, pl.BlockSpec((1, BW), lambda i: (0, 0))],
        out_specs=slab,
        compiler_params=_cparams(("parallel",), 32),
        name="idft_gate_p",
    )(f1i, d_p, gate_p, z_p, bias)


def _hyena_latent(x1p, x2p, vp, feat_perm, tabs, hyw, layer, seq_len):
    n2 = seq_len // (FFT_N1 // 2)
    f1, f1i, g, gt = tabs
    w1p, b1, fr1, w2, b2, fr2, w3p_bf, deltas, hy_bias = hyw
    af_p, ss = _hy_filters_p(feat_perm, w1p, b1, fr1, w2, b2, fr2, w3p_bf, deltas, f1, layer)
    hspec = _filter_minor_p(af_p, g, ss.reshape(1, 4 * BW), n2)
    z = vp
    for o, gate in enumerate((x1p, x2p)):
        a_p = _dft_major_p(f1, z, n2)
        d_p = _conv_minor_p(a_p, hspec, g, gt, o, n2)
        z = _idft_gate_p(f1i, d_p, gate, z, hy_bias[layer, o][None, :], n2)
    return z.reshape(z.shape[0], z.shape[1], n2 * FFT_NB, LANE)


def _merge_body(att_ref, yp_ref, yhp_ref, yhc_ref, ys_ref, hmod_ref, wg_ref, x_ref, wb_ref, wo_ref,
                g1_ref, lg_ref, lb_ref, sh2_ref, sc2_ref, rw_ref, rb_ref,
                x1_ref, m_ref, rt_ref, cnt_ref, *, nlat, n_ctx, alpha):
    i = pl.program_id(0)
    n2c = yhp_ref.shape[2] // FFT_NB
    pieces = []
    for a in range(TM1 // n2c):
        word = jnp.concatenate([yhp_ref[q, a // FFT_NB, pl.ds(a % FFT_NB, n2c, stride=FFT_NB), :]
                                for q in range(yhp_ref.shape[0])], axis=1)
        hi, lo = _unpack2(word)
        pieces.append(jnp.concatenate([hi, lo], axis=1).astype(BF16))
    yh_lat = jnp.concatenate(pieces, axis=0)
    yh_ctx = jnp.concatenate([yhc_ref[...], jnp.zeros((TM1 - n_ctx, BW), BF16)], axis=0)
    yh = jnp.where(i >= nlat, yh_ctx, yh_lat)
    ys = (att_ref[...], yp_ref[...], yh, ys_ref[...])
    half = TM1 // 2
    logit_parts = []
    for hh in range(2):
        rows = slice(hh * half, (hh + 1) * half)
        hm = hmod_ref[rows, :]
        merged = jnp.zeros((half, D), F32)
        for n in range(4):
            br = jnp.dot(ys[n][rows], wb_ref[n], preferred_element_type=F32)
            gl = jnp.dot(hm, wg_ref[:, n * D:(n + 1) * D], preferred_element_type=F32)
            merged = merged + br * (0.5 * jnp.tanh(0.5 * gl) + 0.5)
        y = jnp.dot(merged.astype(BF16), wo_ref[...], preferred_element_type=F32)
        x1 = _layer_norm(alpha * x_ref[rows, :] + g1_ref[...] * y) * lg_ref[...] + lb_ref[...]
        x1_ref[rows, :] = x1
        m = _layer_norm(x1) * (1.0 + sc2_ref[...]) + sh2_ref[...]
        m_ref[rows, :] = m.astype(BF16)
        m_hi = m.astype(BF16)
        m_lo = (m - m_hi.astype(F32)).astype(BF16)
        logit_parts.append(jnp.dot(m_hi, rw_ref[0], preferred_element_type=F32)
                           + jnp.dot(m_lo, rw_ref[0], preferred_element_type=F32)
                           + jnp.dot(m_hi, rw_ref[1], preferred_element_type=F32) + rb_ref[...])
    logits = jnp.concatenate(logit_parts, axis=0)
    lt = logits.T
    le = lt[0:N_EXPERTS]
    lgp = lt[N_EXPERTS:N_EXPERTS + SUB]
    big = 1 << 20
    gi = lax.broadcasted_iota(I32, lgp.shape, 0)
    gmax = jnp.max(lgp, axis=0, keepdims=True)
    gsel = jnp.min(jnp.where(lgp == gmax, gi, big), axis=0, keepdims=True)
    gate_g = 1.0 / jnp.sum(jnp.exp(lgp - gmax), axis=0, keepdims=True)
    ei = lax.broadcasted_iota(I32, le.shape, 0)
    lem = jnp.where(lax.shift_right_logical(ei, 3) == gsel, le, -3.0e38)
    v1 = jnp.max(lem, axis=0, keepdims=True)
    i1 = jnp.min(jnp.where(lem == v1, ei, big), axis=0, keepdims=True)
    lem2 = jnp.where(ei == i1, -3.0e38, lem)
    v2 = jnp.max(lem2, axis=0, keepdims=True)
    i2 = jnp.min(jnp.where(lem2 == v2, ei, big), axis=0, keepdims=True)
    e2 = jnp.exp(v2 - v1)
    wa = gate_g / (1.0 + e2)
    wb = gate_g * e2 / (1.0 + e2)
    tok = lax.broadcasted_iota(I32, (1, TM1), 1)
    valid = jnp.logical_or(i < nlat, tok < n_ctx)
    i1 = jnp.where(valid, i1, -1)
    i2 = jnp.where(valid, i2, -1)
    ri = lax.broadcasted_iota(I32, (SUB, TM1), 0)
    rt = jnp.where(ri == 0, i1.astype(F32), jnp.where(ri == 1, i2.astype(F32),
                   jnp.where(ri == 2, wa, jnp.where(ri == 3, wb, 0.0))))
    rt_ref[...] = rt
    ci = lax.broadcasted_iota(I32, (LANE, TM1), 0)
    oh = jnp.logical_or(ci == i1, ci == i2).astype(BF16)
    cnt_ref[0] = lax.dot_general(jnp.ones((SUB, TM1), BF16), oh, (((1,), (1,)), ((), ())),
                                 preferred_element_type=F32)


def _merge(att, ypool, yhp, yhc, ysc, hmod, wg_bf, x, wb_bf, wo_bf, g1, ln_g, ln_b, sh2, sc2, rw, rb, layer, n_lat,
           n_ctx, alpha):
    nt = x.shape[0]
    nlat = n_lat // TM1
    ntl = nt // TM1
    n2c = yhp.shape[2] // FFT_NB
    sel = lambda i: (jnp.where(i >= nlat, 1, 0), 0, 0)
    row = lambda c: pl.BlockSpec((TM1, c), lambda i: (i, 0))
    lsel = lambda *blk, **kw: pl.BlockSpec((None,) + blk, lambda i: (layer,) + (0,) * len(blk), **kw)
    msel = pl.BlockSpec((None, 1, D), sel)
    return pl.pallas_call(
        functools.partial(_merge_body, nlat=nlat, n_ctx=n_ctx, alpha=alpha),
        out_shape=[SDS((nt, D), F32), SDS((nt, D), BF16), SDS((SUB, nt), F32), SDS((ntl, SUB, LANE), F32)],
        grid=(ntl,),
        in_specs=[row(BW), row(BW),
                  pl.BlockSpec((yhp.shape[0], TM1 // n2c // FFT_NB, n2c * FFT_NB, LANE),
                               lambda i: (0, jnp.minimum(i, nlat - 1), 0, 0)),
                  pl.BlockSpec((n_ctx, BW), lambda i: (0, 0)),
                  row(BW), row(D), lsel(D, 4 * D, pipeline_mode=pl.Buffered(1)), row(D),
                  lsel(4, BW, D, pipeline_mode=pl.Buffered(1)), lsel(D, D, pipeline_mode=pl.Buffered(1)),
                  msel, lsel(1, D), lsel(1, D), msel, msel,
                  lsel(2, D, LANE), lsel(1, LANE)],
        out_specs=[row(D), row(D), pl.BlockSpec((SUB, TM1), lambda i: (0, i)),
                   pl.BlockSpec((1, SUB, LANE), lambda i: (i, 0, 0))],
        compiler_params=_cparams(("parallel",), 56),
        name="merge_router",
    )(att, ypool, yhp, yhc, ysc, hmod, wg_bf, x, wb_bf, wo_bf, g1, ln_g, ln_b, sh2, sc2, rw, rb)


def _piece_loops(np_ref, so_ref, do_ref, j, fn):
    for e in range(N_EXPERTS):
        n = np_ref[j * N_EXPERTS + e]
        so = so_ref[j * N_EXPERTS + e]
        do = do_ref[j * N_EXPERTS + e]

        def body(p, carry, so=so, do=do):
            fn(pl.multiple_of(so + p * MOE_PIECE, MOE_PIECE), pl.multiple_of(do + p * MOE_PIECE, MOE_PIECE))
            return carry

        lax.fori_loop(0, n, body, 0)


def _dispatch_body(np_ref, so_ref, do_ref, nv_ref, m_ref, rt_ref, u_ref, hs_hbm, pos_ref, hs_vmem, sem, *,
                   ntiles, nblk):
    j = pl.program_id(0)
    real = j < ntiles
    rt = rt_ref[...]
    e0 = jnp.where(real, rt[0:1].astype(I32), -1)
    e1 = jnp.where(real, rt[1:2].astype(I32), -1)
    ei = lax.broadcasted_iota(I32, (N_EXPERTS, TM1), 0)
    oh0 = (ei == e0).astype(F32)
    oh1 = (ei == e1).astype(F32)
    c0 = jnp.dot(oh0.astype(BF16), u_ref[...], preferred_element_type=F32)
    c1 = jnp.dot(oh1.astype(BF16), u_ref[...], preferred_element_type=F32)
    n0 = jnp.sum(oh0, axis=1, keepdims=True)
    ecol = lax.broadcasted_iota(I32, (N_EXPERTS, 1), 0)
    toff = jnp.zeros((N_EXPERTS, 1), F32)
    for e in range(N_EXPERTS):
        toff = jnp.where(ecol == e, so_ref[j * N_EXPERTS + e].astype(F32), toff)
    pos0 = jnp.sum(oh0 * (toff + c0), axis=0, keepdims=True)
    pos1 = jnp.sum(oh1 * (toff + n0 + c1), axis=0, keepdims=True)
    pos0 = jnp.where(e0 >= 0, pos0, -1.0)
    pos1 = jnp.where(e1 >= 0, pos1, -1.0)
    ri = lax.broadcasted_iota(I32, (SUB, TM1), 0)
    pos_ref[...] = jnp.where(ri == 0, pos0, jnp.where(ri == 1, pos1, 0.0))
    si = lax.broadcasted_iota(I32, (MOE_S, TM1), 0)
    perm = jnp.logical_or(si == pos0.astype(I32), si == pos1.astype(I32)).astype(BF16)
    hs = jnp.dot(perm, m_ref[...], preferred_element_type=F32)
    slot = lax.rem(j, 2)
    hs_vmem[slot] = _pack2(hs[:, 0:D // 2], hs[:, D // 2:D])

    def copy(s, so, do):
        return pltpu.make_async_copy(hs_vmem.at[s, pl.ds(so, MOE_PIECE)], hs_hbm.at[pl.ds(do, MOE_PIECE)], sem.at[s])

    _piece_loops(np_ref, so_ref, do_ref, j, lambda so, do: copy(slot, so, do).start())

    @pl.when(j > 0)
    def _():
        _piece_loops(np_ref, so_ref, do_ref, j - 1, lambda so, do: copy(1 - slot, so, do).wait())

    @pl.when(j == ntiles)
    def _():
        def blk_copy(b):
            return pltpu.make_async_copy(hs_vmem.at[slot, pl.ds(0, MOE_MB)],
                                         hs_hbm.at[pl.ds(pl.multiple_of(b * MOE_MB, MOE_MB), MOE_MB)], sem.at[slot])

        def start(b, carry):
            blk_copy(b).start()
            return carry

        def wait(b, carry):
            blk_copy(b).wait()
            return carry

        lax.fori_loop(nv_ref[0], nblk, start, 0)
        _piece_loops(np_ref, so_ref, do_ref, j, lambda so, do: copy(slot, so, do).wait())
        lax.fori_loop(nv_ref[0], nblk, wait, 0)


def _dispatch(npieces, soff, doff, nvb, m_bf, rt, upper, rmax):
    nt = m_bf.shape[0]
    ntiles = nt // TM1
    last = ntiles - 1
    return pl.pallas_call(
        functools.partial(_dispatch_body, ntiles=ntiles, nblk=rmax // MOE_MB),
        out_shape=[SDS((rmax, D // 2), U32), SDS((SUB, (ntiles + 1) * TM1), F32)],
        grid_spec=pltpu.PrefetchScalarGridSpec(
            num_scalar_prefetch=4,
            grid=(ntiles + 1,),
            in_specs=[pl.BlockSpec((TM1, D), lambda j, *_: (jnp.minimum(j, last), 0)),
                      pl.BlockSpec((SUB, TM1), lambda j, *_: (0, jnp.minimum(j, last))),
                      pl.BlockSpec((TM1, TM1), lambda j, *_: (0, 0))],
            out_specs=[pl.BlockSpec(memory_space=pl.ANY),
                       pl.BlockSpec((SUB, TM1), lambda j, *_: (0, j))],
            scratch_shapes=[pltpu.VMEM((2, MOE_S, D // 2), U32), pltpu.SemaphoreType.DMA((2,))]),
        compiler_params=_cparams(("arbitrary",), 40),
        name="moe_dispatch",
    )(npieces, soff, doff, nvb, m_bf, rt, upper)


def _expert_body(be_ref, nv_ref, x_ref, w1_ref, w3_ref, w2_ref, o_ref, w1b, w3b, w2b):
    b = pl.program_id(0)
    valid = b < nv_ref[0]
    prev = be_ref[jnp.maximum(b - 1, 0)]
    fresh = jnp.logical_or(b == 0, be_ref[b] != prev)

    @pl.when(jnp.logical_and(valid, fresh))
    def _():
        w1b[...] = w1_ref[...].astype(BF16)
        w3b[...] = w3_ref[...].astype(BF16)
        w2b[...] = w2_ref[...].astype(BF16)

    @pl.when(valid)
    def _():
        xh, xl = _unpack2(x_ref[...])
        xh, xl = xh.astype(BF16), xl.astype(BF16)
        half = D // 2
        h1 = (jnp.dot(xh, w1b[0:half, :], preferred_element_type=F32)
              + jnp.dot(xl, w1b[half:D, :], preferred_element_type=F32))
        h3 = (jnp.dot(xh, w3b[0:half, :], preferred_element_type=F32)
              + jnp.dot(xl, w3b[half:D, :], preferred_element_type=F32))
        hh = (h1 * (0.5 * jnp.tanh(0.5 * h1) + 0.5)) * h3
        y = jnp.dot(hh.astype(BF16), w2b[...], preferred_element_type=F32)
        o_ref[...] = _pack2(y[:, 0:half], y[:, half:D])

    @pl.when(b >= nv_ref[0])
    def _():
        o_ref[...] = jnp.zeros_like(o_ref)


def _experts(blk_e, nvb, hs, ex_w1, ex_w3, ex_w2, layer):
    rmax = hs.shape[0]
    nb = rmax // MOE_MB
    bi = lambda b, be, nv: jnp.maximum(jnp.minimum(b, nv[0] - 1), 0)
    return pl.pallas_call(
        _expert_body,
        out_shape=SDS((rmax, D // 2), U32),
        grid_spec=pltpu.PrefetchScalarGridSpec(
            num_scalar_prefetch=2,
            grid=(nb,),
            in_specs=[pl.BlockSpec((MOE_MB, D // 2), lambda b, be, nv: (bi(b, be, nv), 0)),
                      pl.BlockSpec((None, None, D, EXPERT_HIDDEN), lambda b, be, nv: (layer, be[bi(b, be, nv)], 0, 0)),
                      pl.BlockSpec((None, None, D, EXPERT_HIDDEN), lambda b, be, nv: (layer, be[bi(b, be, nv)], 0, 0)),
                      pl.BlockSpec((None, None, EXPERT_HIDDEN, D), lambda b, be, nv: (layer, be[bi(b, be, nv)], 0, 0))],
            out_specs=pl.BlockSpec((MOE_MB, D // 2), lambda b, be, nv: (b, 0)),
            scratch_shapes=[pltpu.VMEM((D, EXPERT_HIDDEN), BF16), pltpu.VMEM((D, EXPERT_HIDDEN), BF16),
                            pltpu.VMEM((EXPERT_HIDDEN, D), BF16)]),
        compiler_params=_cparams(("arbitrary",), 48),
        name="moe_experts",
    )(blk_e, nvb, hs, ex_w1, ex_w3, ex_w2)


def _combine_body(np_ref, so_ref, do_ref, ys_hbm, pos_ref, rt_ref, x1_ref, g2_ref, lg_ref, lb_ref,
                  o_ref, ys_vmem, sem, *, alpha, nsteps):
    j = pl.program_id(0)
    slot = lax.rem(j, 2)

    def copy(s, so, do):
        return pltpu.make_async_copy(ys_hbm.at[pl.ds(do, MOE_PIECE)], ys_vmem.at[s, pl.ds(so, MOE_PIECE)], sem.at[s])

    @pl.when(j == 0)
    def _():
        _piece_loops(np_ref, so_ref, do_ref, j, lambda so, do: copy(slot, so, do).start())

    @pl.when(j + 1 < nsteps)
    def _():
        _piece_loops(np_ref, so_ref, do_ref, j + 1, lambda so, do: copy(1 - slot, so, do).start())
    z = jnp.concatenate([pos_ref[...], rt_ref[...], jnp.zeros((LANE - 2 * SUB, TM1), F32)], axis=0)
    zt = z.T
    p0 = zt[:, 0:1].astype(I32)
    p1 = zt[:, 1:2].astype(I32)
    w0 = zt[:, SUB + 2:SUB + 3]
    w1 = zt[:, SUB + 3:SUB + 4]
    si = lax.broadcasted_iota(I32, (TM1, MOE_S), 1)
    wm = (jnp.where(si == p0, w0, 0.0) + jnp.where(si == p1, w1, 0.0)).astype(BF16)
    _piece_loops(np_ref, so_ref, do_ref, j, lambda so, do: copy(slot, so, do).wait())
    last = j * N_EXPERTS + N_EXPERTS - 1
    total = so_ref[last] + np_ref[last] * MOE_PIECE
    srow = lax.broadcasted_iota(I32, (MOE_S, 1), 0)
    yh, yl = _unpack2(jnp.where(srow < total, ys_vmem[slot], jnp.uint32(0)))
    f = jnp.concatenate([jnp.dot(wm, yh.astype(BF16), preferred_element_type=F32),
                         jnp.dot(wm, yl.astype(BF16), preferred_element_type=F32)], axis=1)
    o_ref[...] = _layer_norm(alpha * x1_ref[...] + g2_ref[...] * f) * lg_ref[...] + lb_ref[...]


def _combine(npieces, soff, doff, ys, pos, rt, x1, g2, ln_g, ln_b, layer, n_lat, n_out_tiles, n_out_rows, alpha):
    nlat = n_lat // TM1
    sel = lambda j, *_: (jnp.where(j >= nlat, 1, 0), 0, 0)
    lsel = lambda *blk: pl.BlockSpec((None,) + blk, lambda j, *_: (layer,) + (0,) * len(blk))
    return pl.pallas_call(
        functools.partial(_combine_body, alpha=alpha, nsteps=n_out_tiles),
        out_shape=SDS((n_out_rows, D), F32),
        grid_spec=pltpu.PrefetchScalarGridSpec(
            num_scalar_prefetch=3,
            grid=(n_out_tiles,),
            in_specs=[pl.BlockSpec(memory_space=pl.ANY),
                      pl.BlockSpec((SUB, TM1), lambda j, *_: (0, j)),
                      pl.BlockSpec((SUB, TM1), lambda j, *_: (0, j)),
                      pl.BlockSpec((TM1, D), lambda j, *_: (j, 0)),
                      pl.BlockSpec((None, 1, D), sel), lsel(1, D), lsel(1, D)],
            out_specs=pl.BlockSpec((TM1, D), lambda j, *_: (j, 0)),
            scratch_shapes=[pltpu.VMEM((2, MOE_S, D // 2), U32), pltpu.SemaphoreType.DMA((2,))]),
        compiler_params=_cparams(("arbitrary",), 40),
        name="moe_combine",
    )(npieces, soff, doff, ys, pos, rt, x1, g2, ln_g, ln_b)


def _moe_plan(cnt, ntiles):
    c = cnt[:, 0, :N_EXPERTS].astype(I32)
    pad8 = (c + MOE_PIECE - 1) // MOE_PIECE * MOE_PIECE
    toff = jnp.cumsum(pad8, axis=1) - pad8
    tot = pad8.sum(axis=0)
    totb = (tot + MOE_MB - 1) // MOE_MB * MOE_MB
    ends = jnp.cumsum(totb)
    base = ends - totb
    dest = base[None, :] + jnp.cumsum(pad8, axis=0) - pad8
    npieces = jnp.concatenate([pad8 // MOE_PIECE, ((totb - tot) // MOE_PIECE)[None, :]], axis=0)
    soff = jnp.concatenate([toff, jnp.zeros((1, N_EXPERTS), I32)], axis=0)
    doff = jnp.concatenate([dest, (base + tot)[None, :]], axis=0)
    return npieces.reshape(-1), soff.reshape(-1), doff.reshape(-1), ends


def _rope_tables(n_lat, nt):
    t = jnp.arange(n_lat)
    pos = jnp.stack([(t // GRID_W).astype(F32), (t % GRID_W).astype(F32)], axis=1)
    inv = ROPE_BASE ** (-jnp.arange(ROPE_PAIRS, dtype=F32) / ROPE_PAIRS)
    ang = pos[:, :, None] * inv[None, None, :]
    cos = jnp.repeat(jnp.cos(ang), 2, axis=1).reshape(n_lat, 2, 2, ROPE_PAIRS)
    sin = jnp.sin(ang)
    sin = jnp.stack([-sin, sin], axis=2)
    cos = jnp.tile(cos.reshape(n_lat, HEAD_DIM), (1, LANE // HEAD_DIM))
    sin = jnp.tile(sin.reshape(n_lat, HEAD_DIM), (1, LANE // HEAD_DIM))
    cos = jnp.concatenate([cos, jnp.ones((nt - n_lat, LANE), F32)], axis=0)
    sin = jnp.concatenate([sin, jnp.zeros((nt - n_lat, LANE), F32)], axis=0)
    return cos, sin


def _hy_features(l):
    t01 = jnp.linspace(0.0, 1.0, l, dtype=F32)
    fr = jnp.linspace(1e-4, HY_BANDS - 1, HY_BANDS, dtype=F32)
    ang = (2.0 * math.pi / l) * jnp.arange(l, dtype=F32)[:, None] * fr[None, :]
    feat = jnp.concatenate([t01[:, None], jnp.cos(ang), -jnp.sin(ang)], axis=-1)
    feat = jnp.pad(feat, ((0, 0), (0, LANE - HY_EMB)))
    return feat.reshape(FFT_N1 // 2, l // (FFT_N1 // 2), LANE).transpose(1, 0, 2).reshape(l, LANE)


def kernel(x, c, ctx, c_ctx, ada_w, ada_b, w_in, attn_sink, pool_w, pool_scale, hy_conv_w, hy_conv_b, hy_w1, hy_b1, hy_freq1, hy_w2, hy_b2, hy_freq2, hy_w3, hy_bias, sc_conv_w, w_branch, w_out, ln1_g, ln1_b, ln2_g, ln2_b, rg_w, rg_b, re_w, re_b, ex_w1, ex_w3, ex_w2):
    depth = w_in.shape[0]
    assert x.shape[0] == 1 and ctx.shape[0] == 1 and x.shape[2] == D
    n_lat = x.shape[1]
    n_ctx = ctx.shape[1]
    assert n_ctx == TM and n_lat % TM1 == 0 and n_lat >= 2 * TM1
    nt = n_lat + n_ctx + PADR
    ntiles1 = nt // TM1
    alpha = (2 * depth) ** 0.25

    cos_t, sin_t = _rope_tables(n_lat, nt)
    w_main_bf = w_in[:, :, :MAIN_COLS].astype(BF16)
    wg_bf = w_in[:, :, MAIN_COLS:].astype(BF16)
    wb_bf = w_branch.astype(BF16)
    wo_bf = w_out.astype(BF16)
    pool_w_bf = pool_w.astype(BF16)
    r3 = lambda a: a.reshape(depth, 1, a.shape[-1])
    tabs_l = _dft_tables(n_lat // (FFT_N1 // 2))
    tabs_c = _dft_tables(n_ctx // (FFT_N1 // 2))
    feat_l, feat_c = _hy_features(n_lat), _hy_features(n_ctx)
    deltas = jnp.abs(jnp.linspace(math.log(HY_TARGET) / HY_SLOW_PCT, math.log(HY_TARGET) / HY_FAST_PCT,
                                  BW, dtype=F32))[None, :]
    w1p = jnp.pad(hy_w1, ((0, 0), (0, LANE - HY_EMB), (0, 0)))
    w3p_bf = hy_w3.reshape(depth, HY_HIDDEN, 2, 2, BW).transpose(0, 1, 3, 2, 4).reshape(depth, HY_HIDDEN, 4 * BW).astype(BF16)
    hyw = (w1p, r3(hy_b1), r3(hy_freq1), hy_w2, r3(hy_b2), r3(hy_freq2), w3p_bf, deltas, hy_bias)
    rw = jnp.concatenate([re_w, rg_w, jnp.zeros((depth, D, LANE - N_EXPERTS - N_GROUPS), F32)], axis=2)
    rw_hi = rw.astype(BF16)
    rw = jnp.stack([rw_hi, (rw - rw_hi.astype(F32)).astype(BF16)], axis=1)
    rb = jnp.concatenate([re_b, rg_b, jnp.full((depth, LANE - N_EXPERTS - N_GROUPS), NEG_INF, F32)], axis=1)
    rb = rb.reshape(depth, 1, LANE)
    upper = (jnp.arange(TM1)[:, None] < jnp.arange(TM1)[None, :]).astype(BF16)
    rmax = -(-(2 * (n_lat + n_ctx) + ntiles1 * N_EXPERTS * (MOE_PIECE - 1) + N_EXPERTS * (MOE_MB - 1)) // MOE_MB) * MOE_MB
    nblk = rmax // MOE_MB

    s8 = jnp.concatenate([c, c_ctx[None, :], jnp.zeros((SUB - 2, D), F32)], axis=0)
    mod = _mod_all(s8, ada_w, ada_b.reshape(depth, 1, 6 * D))

    xs = jnp.concatenate([x[0], ctx[0], jnp.zeros((PADR, D), F32)], axis=0)
    for i in range(depth):
        last = i == depth - 1
        mp = lambda j: mod[i, 0:2, j * D:(j + 1) * D].reshape(2, 1, D)
        sh1, sc1, g1, sh2, sc2, g2 = (mp(j) for j in range(6))

        q, k, v, pool_in, hy_in, sc_in, hmod = _inproj(xs, sh1, sc1, cos_t, sin_t, w_main_bf, i, n_lat // TM1)
        att = _attention(attn_sink[i], q, k, v, n_lat)
        ypool, ysc, x1p, x2p, vp, x1c, x2c, vc = _local_ops(
            pool_in, sc_in, hy_in, pool_w_bf, pool_scale.reshape(depth, 1, BW),
            hy_conv_w, hy_conv_b.reshape(depth, 1, 3 * BW), sc_conv_w, i, n_lat, n_ctx)
        yhp = _hyena_latent(x1p, x2p, vp, feat_l, tabs_l, hyw, i, n_lat)
        if last:
            yhc = jnp.zeros((n_ctx, BW), BF16)
        else:
            yhc = _hyena(x1c, x2c, vc, feat_c, tabs_c, hyw, i, n_ctx, 0)
        x1, m_bf, rt, cnt = _merge(att, ypool, yhp, yhc, ysc, hmod, wg_bf, xs, wb_bf, wo_bf, g1, r3(ln1_g), r3(ln1_b),
                                   sh2, sc2, rw, rb, i, n_lat, n_ctx, alpha)

        npieces, soff, doff, ends = _moe_plan(cnt, ntiles1)
        nvb = (ends[-1] // MOE_MB).astype(I32).reshape(1)
        blk_start = jnp.arange(nblk, dtype=I32) * MOE_MB
        blk_e = jnp.minimum(jnp.sum((ends[None, :] <= blk_start[:, None]).astype(I32), axis=1), N_EXPERTS - 1)
        hs, pos = _dispatch(npieces, soff, doff, nvb, m_bf, rt, upper, rmax)
        ys = _experts(blk_e, nvb, hs, ex_w1, ex_w3, ex_w2, i)
        if last:
            xs = _combine(npieces, soff, doff, ys, pos, rt, x1, g2, r3(ln2_g), r3(ln2_b), i, n_lat,
                          n_lat // TM1, n_lat, alpha)
        else:
            xs = _combine(npieces, soff, doff, ys, pos, rt, x1, g2, r3(ln2_g), r3(ln2_b), i, n_lat, ntiles1, nt,
                          alpha)
    return xs[None]
```

```python
import functools
import math

import jax
import jax.numpy as jnp
from jax import lax
from jax.experimental import pallas as pl
from jax.experimental.pallas import tpu as pltpu

F32 = jnp.float32
BF16 = jnp.bfloat16
I32 = jnp.int32
U32 = jnp.uint32
SDS = jax.ShapeDtypeStruct
HIGHEST = lax.Precision.HIGHEST

D = 1024
GRID_W = 64
HEADS = 8
KV_HEADS = 2
HEAD_DIM = 64
WINDOW = 128
ATT_BLK = 128
ROPE_PAIRS = 16
ROPE_BASE = 10000.0
BW = 512
POOL_WINDOWS = (2, 4, 8, 16)
POOL_GW = 128
HY_EMB = 33
HY_BANDS = 16
HY_HIDDEN = 64
HY_TARGET = 1e-2
HY_FAST_PCT = 0.3
HY_SLOW_PCT = 1.5
HY_MOD_SHIFT = 0.05
Q_COLS = HEADS * HEAD_DIM
KV_COLS = KV_HEADS * HEAD_DIM
MAIN_COLS = Q_COLS + 2 * KV_COLS + BW + 3 * BW + 3 * BW
N_GROUPS = 4
EPG = 8
N_EXPERTS = 32
EXPERT_HIDDEN = 512
LN_EPS = 1e-6
NEG_INF = -1e30

LANE = 128
SUB = 8
BSUB = 16
TM = 256
TM1 = 512
PADR = 256
HALO = 16
FFT_N1 = 512
FFT_K1 = 264
FFT_KB = 8
FFT_TN = 2048
FFT_NB = 8
MOE_PIECE = 8
MOE_BIG = 4
MOE_MB = 512
MOE_S = 2 * TM1 + 256


def _cparams(sem, vmem_mb):
    return pltpu.CompilerParams(dimension_semantics=sem, vmem_limit_bytes=vmem_mb * 2 ** 20)


def _dot_bf16x3(a, b):
    a_hi = a.astype(BF16)
    a_lo = (a - a_hi.astype(F32)).astype(BF16)
    b_hi = b.astype(BF16)
    b_lo = (b - b_hi.astype(F32)).astype(BF16)
    return (jnp.dot(a_hi, b_hi, preferred_element_type=F32) + jnp.dot(a_lo, b_hi, preferred_element_type=F32)
            + jnp.dot(a_hi, b_lo, preferred_element_type=F32))


def _layer_norm(x):
    mu = jnp.mean(x, axis=-1, keepdims=True)
    xc = x - mu
    var = jnp.mean(xc * xc, axis=-1, keepdims=True)
    return xc * lax.rsqrt(var + LN_EPS)


def _mod_body(s_ref, w_ref, b_ref, o_ref):
    s = s_ref[...]
    s = s * (1.0 / (1.0 + jnp.exp(-s)))
    o_ref[...] = jnp.dot(s.astype(BF16), w_ref[...].astype(BF16), preferred_element_type=F32) + b_ref[...]


def _mod_all(s8, ada_w, ada_b3):
    depth = ada_w.shape[0]
    tn = 1024
    return pl.pallas_call(
        _mod_body,
        out_shape=SDS((depth, SUB, 6 * D), F32),
        grid=(depth, 6 * D // tn),
        in_specs=[pl.BlockSpec((SUB, D), lambda l, j: (0, 0)),
                  pl.BlockSpec((None, D, tn), lambda l, j: (l, 0, j)),
                  pl.BlockSpec((None, 1, tn), lambda l, j: (l, 0, j))],
        out_specs=pl.BlockSpec((None, SUB, tn), lambda l, j: (l, 0, j)),
        compiler_params=_cparams(("parallel", "parallel"), 32),
        name="ada_mod",
    )(s8, ada_w, ada_b3)


def _inproj_body(x_ref, sh_ref, sc_ref, cos_ref, sin_ref, w_ref,
                 q_ref, k_ref, v_ref, pool_ref, hy_ref, scv_ref, hmod_ref):
    h = _layer_norm(x_ref[...]) * (1.0 + sc_ref[...]) + sh_ref[...]
    hb = h.astype(BF16)

    def mm(a, b):
        return jnp.dot(hb, w_ref[:, a:b], preferred_element_type=F32)

    cs = cos_ref[...]
    sn = sin_ref[...]
    lane = lax.broadcasted_iota(I32, cs.shape, 1)
    first = (lane & ROPE_PAIRS) == 0

    def rope(t):
        sw = jnp.where(first, pltpu.roll(t, LANE - ROPE_PAIRS, 1), pltpu.roll(t, ROPE_PAIRS, 1))
        return t * cs + sw * sn

    for j in range(Q_COLS // LANE):
        q_ref[:, j * LANE:(j + 1) * LANE] = (rope(mm(j * LANE, (j + 1) * LANE)) * (HEAD_DIM ** -0.5)).astype(BF16)
    o = Q_COLS
    k_ref[...] = rope(mm(o, o + KV_COLS)).astype(BF16)
    o += KV_COLS
    v_ref[...] = mm(o, o + KV_COLS).astype(BF16)
    o += KV_COLS
    pool_ref[...] = mm(o, o + BW).astype(BF16)
    o += BW
    for j in range(3):
        hy_ref[:, j * BW:(j + 1) * BW] = mm(o + j * BW, o + (j + 1) * BW).astype(BF16)
    o += 3 * BW
    for j in range(3):
        scv_ref[:, j * BW:(j + 1) * BW] = mm(o + j * BW, o + (j + 1) * BW).astype(BF16)
    hmod_ref[...] = hb


def _inproj(x, sh, sc, cos_t, sin_t, w_in_bf, layer, n_lat_tiles):
    nt = x.shape[0]
    sel = lambda i: (jnp.where(i >= n_lat_tiles, 1, 0), 0, 0)
    row = lambda c: pl.BlockSpec((TM1, c), lambda i: (i, 0))
    outs = [SDS((nt, Q_COLS), BF16), SDS((nt, KV_COLS), BF16), SDS((nt, KV_COLS), BF16), SDS((nt, BW), BF16),
            SDS((nt, 3 * BW), BF16), SDS((nt, 3 * BW), BF16), SDS((nt, D), BF16)]
    return pl.pallas_call(
        _inproj_body,
        out_shape=outs,
        grid=(nt // TM1,),
        in_specs=[row(D),
                  pl.BlockSpec((None, 1, D), sel), pl.BlockSpec((None, 1, D), sel),
                  row(LANE), row(LANE),
                  pl.BlockSpec((None, D, MAIN_COLS), lambda i: (layer, 0, 0), pipeline_mode=pl.Buffered(1))],
        out_specs=[row(Q_COLS), row(KV_COLS), row(KV_COLS), row(BW), row(3 * BW), row(3 * BW), row(D)],
        compiler_params=_cparams(("parallel",), 58),
        name="inproj",
    )(x, sh, sc, cos_t, sin_t, w_in_bf)


def _attn_body(sink_ref, q_ref, kp_ref, km_ref, kn_ref, vp_ref, vm_ref, vn_ref, kx_ref, vx_ref, o_ref, *, nlt):
    i = pl.program_id(0)
    is_lat = i < nlt
    lat_i = jnp.where(is_lat, 1, 0)
    first_i = jnp.where(jnp.logical_and(is_lat, i > 0), 1, 0)
    last_i = jnp.where(jnp.logical_and(is_lat, i < nlt - 1), 1, 0)
    nsub = TM1 // ATT_BLK

    def variants(x):
        xf = x.astype(F32)
        lo = lax.broadcasted_iota(I32, xf.shape, 1) < HEAD_DIM
        xr = pltpu.roll(xf, HEAD_DIM, 1)
        zero = jnp.zeros_like(xf)
        out = [[jnp.where(lo, xf, zero), jnp.where(lo, zero, xr)], [jnp.where(lo, xr, zero), jnp.where(lo, zero, xf)]]
        return [[a.astype(BF16) for a in row] for row in out]

    k_ctx, v_ctx = variants(kx_ref[...]), variants(vx_ref[...])
    k_loc = variants(jnp.concatenate([kp_ref[...], km_ref[...], kn_ref[...]], axis=0))
    v_loc = variants(jnp.concatenate([vp_ref[...], vm_ref[...], vn_ref[...]], axis=0))

    r = lax.broadcasted_iota(I32, (ATT_BLK, 3 * ATT_BLK), 0)
    c = lax.broadcasted_iota(I32, (ATT_BLK, 3 * ATT_BLK), 1)
    band = jnp.logical_and(c >= r, c <= r + 2 * WINDOW)
    nt_dims = (((1,), (1,)), ((), ()))
    for s in range(nsub):
        prev_i = lat_i if s > 0 else first_i
        next_i = lat_i if s < nsub - 1 else last_i
        grp = jnp.where(c < ATT_BLK, prev_i, jnp.where(c < 2 * ATT_BLK, lat_i, next_i))
        mask = jnp.logical_and(band, grp > 0)
        win = slice(s * ATT_BLK, (s + 3) * ATT_BLK)
        rows = slice(s * ATT_BLK, (s + 1) * ATT_BLK)
        for p in range(HEADS // 2):
            qp = q_ref[rows, p * LANE:(p + 1) * LANE]
            kh = (2 * p) // (HEADS // KV_HEADS)
            acc = jnp.zeros((ATT_BLK, LANE), F32)
            for rr in range(2):
                sink = sink_ref[2 * p + rr]
                s_ctx = lax.dot_general(qp, k_ctx[kh][rr], nt_dims, preferred_element_type=F32)
                s_loc = lax.dot_general(qp, k_loc[kh][rr][win], nt_dims, preferred_element_type=F32)
                s_loc = jnp.where(mask, s_loc, NEG_INF)
                m = jnp.maximum(jnp.maximum(jnp.max(s_ctx, axis=1, keepdims=True),
                                            jnp.max(s_loc, axis=1, keepdims=True)), sink)
                e_ctx = jnp.exp(s_ctx - m)
                e_loc = jnp.exp(s_loc - m)
                den = (jnp.sum(e_ctx, axis=1, keepdims=True) + jnp.sum(e_loc, axis=1, keepdims=True)
                       + jnp.exp(sink - m))
                o = (jnp.dot(e_ctx.astype(BF16), v_ctx[kh][rr], preferred_element_type=F32)
                     + jnp.dot(e_loc.astype(BF16), v_loc[kh][rr][win], preferred_element_type=F32))
                acc = acc + o * (1.0 / den)
            o_ref[rows, p * LANE:(p + 1) * LANE] = acc.astype(BF16)


def _attention(sink, q, k, v, n_lat):
    nt = q.shape[0]
    nlt = n_lat // TM1
    nlb = n_lat // ATT_BLK
    per = TM1 // ATT_BLK
    cxb = n_lat // TM
    side = lambda f: pl.BlockSpec((ATT_BLK, KV_COLS), lambda i: (jnp.clip(f(i), 0, nlb - 1), 0))
    prev = side(lambda i: i * per - 1)
    nxt = side(lambda i: (i + 1) * per)
    main = pl.BlockSpec((TM1, KV_COLS), lambda i: (jnp.minimum(i, nlt - 1), 0))
    cx = pl.BlockSpec((TM, KV_COLS), lambda i: (cxb, 0))
    return pl.pallas_call(
        functools.partial(_attn_body, nlt=nlt),
        out_shape=SDS((nt, Q_COLS), BF16),
        grid=(nt // TM1,),
        in_specs=[pl.BlockSpec(memory_space=pltpu.SMEM),
                  pl.BlockSpec((TM1, Q_COLS), lambda i: (i, 0)),
                  prev, main, nxt, prev, main, nxt, cx, cx],
        out_specs=pl.BlockSpec((TM1, Q_COLS), lambda i: (i, 0)),
        compiler_params=_cparams(("parallel",), 40),
        name="attention",
    )(sink, q, k, k, k, v, v, v, k, v)


def _pack2(a, b):
    hi = lax.bitcast_convert_type(a.astype(BF16).astype(F32), U32)
    lo = lax.bitcast_convert_type(b.astype(BF16).astype(F32), U32)
    return hi | (lo >> 16)


def _unpack2(w):
    return (lax.bitcast_convert_type(w & jnp.uint32(0xFFFF0000), F32),
            lax.bitcast_convert_type(w << 16, F32))


def _local_body(pm_ref, pa_ref, pb_ref, sm_ref, sa_ref, sb_ref, hm_ref, ha_ref, hb_ref,
                pw_ref, ps_ref, hw_ref, hbias_ref, sw_ref, band_ref,
                ypool_ref, ysc_ref, x1p_ref, x2p_ref, vp_ref, x1c_ref, x2c_ref, vc_ref,
                pext, sext, hext, *, nlat, n_lat, n_ctx):
    i = pl.program_id(0)
    is_lat = i < nlat
    pf = jnp.where(jnp.logical_and(is_lat, i > 0), 1.0, 0.0)
    nf = jnp.where(jnp.logical_and(is_lat, i < nlat - 1), 1.0, 0.0)
    r = lax.broadcasted_iota(I32, (TM1, 1), 0)
    keep = jnp.where(jnp.logical_or(is_lat, r < n_ctx), 1.0, 0.0)

    def fill(ext, a_ref, m_ref, b_ref):
        ext[0:HALO, :] = a_ref[...].astype(F32) * pf
        ext[HALO:HALO + TM1, :] = m_ref[...].astype(F32) * keep
        ext[HALO + TM1:HALO + TM1 + HALO, :] = b_ref[...].astype(F32) * nf

    def sh(ext, d, c0=None, c1=None):
        if c0 is None:
            return ext[pl.ds(HALO + d, TM1), :]
        return ext[pl.ds(HALO + d, TM1), c0:c1]

    fill(pext, pa_ref, pm_ref, pb_ref)
    t = jnp.where(is_lat, i * TM1, 0) + r
    ln = jnp.where(is_lat, n_lat, n_ctx)
    for g, w in enumerate(POOL_WINDOWS):
        left = w // 2
        right = w - left - 1
        c0, c1 = g * POOL_GW, (g + 1) * POOL_GW
        acc = jnp.dot(band_ref[g], pext[:, c0:c1].astype(BF16), preferred_element_type=F32)
        cnt = jnp.maximum(jnp.minimum(t + right + 1, ln) - jnp.maximum(t - left, 0), 1).astype(F32)
        dpool = acc / cnt - sh(pext, 0, c0, c1)
        y = jnp.dot(dpool.astype(BF16), pw_ref[g], preferred_element_type=F32)
        ypool_ref[:, c0:c1] = (y * ps_ref[:, c0:c1]).astype(BF16)

    sext[0:HALO, :] = (sa_ref[:, BW:2 * BW].astype(F32) * sa_ref[:, 2 * BW:3 * BW].astype(F32)) * pf
    sext[HALO:HALO + TM1, :] = sm_ref[:, BW:2 * BW].astype(F32) * sm_ref[:, 2 * BW:3 * BW].astype(F32) * keep
    sext[HALO + TM1:HALO + TM1 + HALO, :] = (sb_ref[:, BW:2 * BW].astype(F32) * sb_ref[:, 2 * BW:3 * BW].astype(F32)) * nf
    conv = sh(sext, -1) * sw_ref[0:1, :] + sh(sext, 0) * sw_ref[1:2, :] + sh(sext, 1) * sw_ref[2:3, :]
    ysc_ref[...] = (sm_ref[:, 0:BW].astype(F32) * conv).astype(BF16)

    fill(hext, ha_ref, hm_ref, hb_ref)
    nat = (x1c_ref, x2c_ref, vc_ref)
    pk = (x1p_ref, x2p_ref, vp_ref)
    n2c = x1p_ref.shape[2] // FFT_NB
    for j in range(3):
        c0, c1 = j * BW, (j + 1) * BW
        uc = (sh(hext, -1, c0, c1) * hw_ref[0:1, c0:c1] + sh(hext, 0, c0, c1) * hw_ref[1:2, c0:c1]
              + sh(hext, 1, c0, c1) * hw_ref[2:3, c0:c1] + hbias_ref[:, c0:c1])
        nat[j][...] = uc.astype(BF16)
        word = _pack2(uc[:, 0:BW // 2], uc[:, BW // 2:BW])
        for a in range(TM1 // n2c):
            for q in range(BW // 2 // LANE):
                pk[j][q, a // FFT_NB, pl.ds(a % FFT_NB, n2c, stride=FFT_NB), :] = (
                    word[a * n2c:(a + 1) * n2c, q * LANE:(q + 1) * LANE])


def _pool_bands():
    r = jnp.arange(TM1)[:, None]
    c = jnp.arange(TM1 + 2 * HALO)[None, :] - HALO
    return jnp.stack([((c >= r - w // 2) & (c <= r + (w - w // 2 - 1))).astype(BF16) for w in POOL_WINDOWS])


def _local_ops(pool_in, sc_in, hy_in, pool_w_bf, pool_scale, hy_conv_w, hy_conv_b, sc_conv_w, layer, n_lat, n_ctx):
    nt = pool_in.shape[0]
    nlat = n_lat // TM1
    n2c = n_lat // (FFT_N1 // 2)
    per_tile = TM1 // n2c
    nhb = nt // HALO
    per = TM1 // HALO
    main = lambda c: pl.BlockSpec((TM1, c), lambda i: (i, 0))
    before = lambda c: pl.BlockSpec((HALO, c), lambda i: (jnp.maximum(i * per - 1, 0), 0))
    after = lambda c: pl.BlockSpec((HALO, c), lambda i: (jnp.minimum((i + 1) * per, nhb - 1), 0))
    lsel = lambda *blk: pl.BlockSpec((None,) + blk, lambda i: (layer,) + (0,) * len(blk))
    out = SDS((nt, BW), BF16)
    nq, g8 = BW // 2 // LANE, per_tile // FFT_NB
    packed = SDS((nq, (nlat + 1) * g8, n2c * FFT_NB, LANE), U32)
    ctx_out = SDS((TM1, BW), BF16)
    pspec = pl.BlockSpec((nq, g8, n2c * FFT_NB, LANE), lambda i: (0, i, 0, 0))
    cspec = pl.BlockSpec((TM1, BW), lambda i: (0, 0))
    return pl.pallas_call(
        functools.partial(_local_body, nlat=nlat, n_lat=n_lat, n_ctx=n_ctx),
        out_shape=[out, out, packed, packed, packed, ctx_out, ctx_out, ctx_out],
        grid=(nlat + 1,),
        in_specs=[main(BW), before(BW), after(BW),
                  main(3 * BW), before(3 * BW), after(3 * BW),
                  main(3 * BW), before(3 * BW), after(3 * BW),
                  lsel(4, POOL_GW, POOL_GW), lsel(1, BW), lsel(3, 3 * BW), lsel(1, 3 * BW), lsel(3, BW),
                  pl.BlockSpec((len(POOL_WINDOWS), TM1, TM1 + 2 * HALO), lambda i: (0, 0, 0))],
        out_specs=[main(BW), main(BW), pspec, pspec, pspec, cspec, cspec, cspec],
        scratch_shapes=[pltpu.VMEM((TM1 + 2 * HALO, BW), F32), pltpu.VMEM((TM1 + 2 * HALO, BW), F32),
                        pltpu.VMEM((TM1 + 2 * HALO, 3 * BW), F32)],
        compiler_params=_cparams(("arbitrary",), 48),
        name="local_ops",
    )(pool_in, pool_in, pool_in, sc_in, sc_in, sc_in, hy_in, hy_in, hy_in,
      pool_w_bf, pool_scale, hy_conv_w, hy_conv_b, sc_conv_w, _pool_bands())


def _dft_tables(n2):
    n = FFT_N1 * n2
    k1 = jnp.arange(FFT_K1, dtype=I32)
    n1 = jnp.arange(FFT_N1 // 2, dtype=I32)
    ok = (k1 <= FFT_N1 // 2)
    ang = (2.0 * math.pi / FFT_N1) * ((k1[:, None] * n1[None, :]) % FFT_N1).astype(F32)
    c1 = jnp.where(ok[:, None], jnp.cos(ang), 0.0)
    s1 = jnp.where(ok[:, None], jnp.sin(ang), 0.0)
    f1 = jnp.concatenate([c1, -s1], axis=0).astype(BF16)
    wt = jnp.where((k1 == 0) | (k1 == FFT_N1 // 2), 1.0, 2.0) / n
    f1i = jnp.concatenate([(c1 * wt[:, None]).T, (-s1 * wt[:, None]).T], axis=1).astype(BF16)
    if n2 == 1:
        return f1, f1i, None, None
    k2 = jnp.arange(n2, dtype=I32)
    m2 = jnp.arange(n2, dtype=I32)
    kk = k1[:, None, None] + FFT_N1 * k2[None, :, None]
    th = (2.0 * math.pi / n) * ((kk * m2[None, None, :]) % n).astype(F32)
    okb = ok[:, None, None]
    cc = jnp.where(okb, jnp.cos(th), 0.0)
    ss = jnp.where(okb, jnp.sin(th), 0.0)
    g = jnp.concatenate([jnp.concatenate([cc, ss], axis=2), jnp.concatenate([-ss, cc], axis=2)], axis=1)
    return f1, f1i, g.astype(BF16), jnp.swapaxes(g, 1, 2).astype(BF16)


def _filter_body(feat_ref, w1_ref, b1_ref, f1_ref, w2_ref, b2_ref, f2_ref, w3_ref, dl_ref, dft_ref,
                 af_ref, ss_ref, filt):
    i = pl.program_id(0)
    feat = feat_ref[...]
    h = jnp.sin(f1_ref[...] * (_dot_bf16x3(feat, w1_ref[...]) + b1_ref[...]))
    h = jnp.sin(f2_ref[...] * (_dot_bf16x3(h, w2_ref[...]) + b2_ref[...]))
    raw = jnp.dot(h.astype(BF16), w3_ref[...], preferred_element_type=F32)
    win = jnp.exp(-feat[:, 0:1] * dl_ref[...]) + HY_MOD_SHIFT
    row = lax.broadcasted_iota(I32, (TM, 1), 0) + i * TM

    @pl.when(i == 0)
    def _():
        ss_ref[...] = jnp.zeros_like(ss_ref)

    for j in range(4):
        f = raw[:, j * BW:(j + 1) * BW] * win
        if j >= 2:
            f = jnp.where(row == 0, 0.0, f)
        filt[:, j * BW:(j + 1) * BW] = f.astype(BF16)
        ss_ref[:, j * BW:(j + 1) * BW] += jnp.sum(f * f, axis=0, keepdims=True)
    af_ref[...] = jnp.dot(dft_ref[...], filt[...], preferred_element_type=F32).astype(BF16)


def _hy_filters(feat_perm, w1p, b1, fr1, w2, b2, fr2, w3p_bf, deltas, f1, layer):
    l = feat_perm.shape[0]
    n2 = l // TM
    lsel = lambda *blk: pl.BlockSpec((None,) + blk, lambda i: (layer,) + (0,) * len(blk))
    return pl.pallas_call(
        _filter_body,
        out_shape=[SDS((2 * FFT_K1, n2 * 4 * BW), BF16), SDS((1, 4 * BW), F32)],
        grid=(n2,),
        in_specs=[pl.BlockSpec((TM, LANE), lambda i: (i, 0)),
                  lsel(LANE, HY_HIDDEN), lsel(1, HY_HIDDEN), lsel(1, HY_HIDDEN),
                  lsel(HY_HIDDEN, HY_HIDDEN), lsel(1, HY_HIDDEN), lsel(1, HY_HIDDEN),
                  lsel(HY_HIDDEN, 4 * BW), pl.BlockSpec((1, BW), lambda i: (0, 0)),
                  pl.BlockSpec((2 * FFT_K1, FFT_N1 // 2), lambda i: (0, 0))],
        out_specs=[pl.BlockSpec((2 * FFT_K1, 4 * BW), lambda i: (0, i)), pl.BlockSpec((1, 4 * BW), lambda i: (0, 0))],
        scratch_shapes=[pltpu.VMEM((TM, 4 * BW), BF16)],
        compiler_params=_cparams(("arbitrary",), 32),
        name="hy_filters",
    )(feat_perm, w1p, b1, fr1, w2, b2, fr2, w3p_bf, deltas, f1)


def _dft_major_body(f_ref, x_ref, o_ref):
    o_ref[...] = jnp.dot(f_ref[...], x_ref[...], preferred_element_type=F32).astype(BF16)


def _dft_major(f1, x2d, row_blk, ncols):
    tn = min(FFT_TN, ncols)
    return pl.pallas_call(
        _dft_major_body,
        out_shape=SDS((2 * FFT_K1, ncols), BF16),
        grid=(ncols // tn,),
        in_specs=[pl.BlockSpec((2 * FFT_K1, FFT_N1 // 2), lambda j: (0, 0)),
                  pl.BlockSpec((FFT_N1 // 2, tn), lambda j: (row_blk, j))],
        out_specs=pl.BlockSpec((2 * FFT_K1, tn), lambda j: (0, j)),
        compiler_params=_cparams(("parallel",), 32),
        name="dft_major",
    )(f1, x2d)


def _filter_minor_body(a_ref, g_ref, ss_ref, h_ref, *, n2):
    ssv = ss_ref[...]
    scale = lax.rsqrt(ssv[:, 0:2 * BW] + ssv[:, 2 * BW:4 * BW] + LN_EPS)
    for j in range(a_ref.shape[1]):
        if n2 == 1:
            s = jnp.concatenate([a_ref[0, j], a_ref[1, j]], axis=0).astype(F32)
        else:
            a = jnp.concatenate([a_ref[0, j], a_ref[1, j]], axis=0)
            s = jnp.dot(g_ref[j], a, preferred_element_type=F32)
        sf, sb = s[:, 0:2 * BW], s[:, 2 * BW:4 * BW]
        hr = (sf[0:n2] + sb[0:n2]) * scale
        hi = (sf[n2:2 * n2] - sb[n2:2 * n2]) * scale
        h_ref[j] = jnp.concatenate([hr, hi], axis=0).astype(BF16)


def _conv_minor_body(a_ref, h_ref, g_ref, gt_ref, d_ref, *, n2):
    for j in range(a_ref.shape[1]):
        a = jnp.concatenate([a_ref[0, j], a_ref[1, j]], axis=0)
        if n2 == 1:
            x = a.astype(F32)
        else:
            x = jnp.dot(g_ref[j], a, preferred_element_type=F32)
        h = h_ref[j].astype(F32)
        xr, xi, hr, hi = x[0:n2], x[n2:2 * n2], h[0:n2], h[n2:2 * n2]
        y = jnp.concatenate([xr * hr - xi * hi, xr * hi + xi * hr], axis=0)
        if n2 == 1:
            dv = y
        else:
            dv = jnp.dot(gt_ref[j], y.astype(BF16), preferred_element_type=F32)
        d_ref[0, j] = dv[0:n2].astype(BF16)
        d_ref[1, j] = dv[n2:2 * n2].astype(BF16)


def _filter_minor(af, g, ss, n2):
    a4 = af.reshape(2, FFT_K1, n2, 4 * BW)
    kb = FFT_KB
    gspec = (pl.BlockSpec((kb, 2 * n2, 2 * n2), lambda i: (i, 0, 0)) if n2 > 1
             else pl.BlockSpec((SUB, LANE), lambda i: (0, 0)))
    return pl.pallas_call(
        functools.partial(_filter_minor_body, n2=n2),
        out_shape=SDS((FFT_K1, 2 * n2, 2 * BW), BF16),
        grid=(FFT_K1 // kb,),
        in_specs=[pl.BlockSpec((2, kb, n2, 4 * BW), lambda i: (0, i, 0, 0)), gspec,
                  pl.BlockSpec((1, 4 * BW), lambda i: (0, 0))],
        out_specs=pl.BlockSpec((kb, 2 * n2, 2 * BW), lambda i: (i, 0, 0)),
        compiler_params=_cparams(("parallel",), 40),
        name="hy_filter_minor",
    )(a4, g if n2 > 1 else jnp.zeros((SUB, LANE), BF16), ss)


def _conv_minor(a, hspec, g, gt, order, n2):
    a4 = a.reshape(2, FFT_K1, n2, BW)
    kb = FFT_KB
    dummy = jnp.zeros((SUB, LANE), BF16)
    gspec = (pl.BlockSpec((kb, 2 * n2, 2 * n2), lambda i: (i, 0, 0)) if n2 > 1
             else pl.BlockSpec((SUB, LANE), lambda i: (0, 0)))
    d4 = pl.pallas_call(
        functools.partial(_conv_minor_body, n2=n2),
        out_shape=SDS((2, FFT_K1, n2, BW), BF16),
        grid=(FFT_K1 // kb,),
        in_specs=[pl.BlockSpec((2, kb, n2, BW), lambda i: (0, i, 0, 0)),
                  pl.BlockSpec((kb, 2 * n2, BW), lambda i: (i, 0, order)), gspec, gspec],
        out_specs=pl.BlockSpec((2, kb, n2, BW), lambda i: (0, i, 0, 0)),
        compiler_params=_cparams(("parallel",), 32),
        name="hy_conv_minor",
    )(a4, hspec, g if n2 > 1 else dummy, gt if n2 > 1 else dummy)
    return d4.reshape(2 * FFT_K1, n2 * BW)


def _idft_gate_body(f_ref, d_ref, gate_ref, z_ref, b_ref, o_ref):
    y = jnp.dot(f_ref[...], d_ref[...], preferred_element_type=F32)
    z = z_ref[...].astype(F32)
    o_ref[...] = (gate_ref[...].astype(F32) * (y + b_ref[...] * z)).astype(BF16)


def _idft_gate(f1i, d, gate2d, gate_blk, z2d, z_blk, bias_t, ncols):
    tn = min(FFT_TN, ncols)
    return pl.pallas_call(
        _idft_gate_body,
        out_shape=SDS((FFT_N1 // 2, ncols), BF16),
        grid=(ncols // tn,),
        in_specs=[pl.BlockSpec((FFT_N1 // 2, 2 * FFT_K1), lambda j: (0, 0)),
                  pl.BlockSpec((2 * FFT_K1, tn), lambda j: (0, j)),
                  pl.BlockSpec((FFT_N1 // 2, tn), lambda j: (gate_blk, j)),
                  pl.BlockSpec((FFT_N1 // 2, tn), lambda j: (z_blk, j)),
                  pl.BlockSpec((1, tn), lambda j: (0, 0))],
        out_specs=pl.BlockSpec((FFT_N1 // 2, tn), lambda j: (0, j)),
        compiler_params=_cparams(("parallel",), 32),
        name="idft_gate",
    )(f1i, d, gate2d, z2d, bias_t)


def _hyena(x1, x2, v, feat, tabs, hyw, layer, seq_len, row0):
    n2 = seq_len // (FFT_N1 // 2)
    f1, f1i, g, gt = tabs
    w1p, b1, fr1, w2, b2, fr2, w3p_bf, deltas, hy_bias = hyw
    nt = x1.shape[0]
    af, ss = _hy_filters(feat, w1p, b1, fr1, w2, b2, fr2, w3p_bf, deltas, f1, layer)
    hspec = _filter_minor(af, g, ss, n2)
    view = lambda a: a.reshape(nt // n2, n2 * BW)
    blk = row0 // seq_len if n2 == 1 else 0
    ncols = n2 * BW
    tn = min(FFT_TN, ncols)
    z, zv, zblk = v, view(v), blk
    gates = (x1, x2)
    for o in range(2):
        a = _dft_major(f1, zv, zblk, ncols)
        d = _conv_minor(a, hspec, g, gt, o, n2)
        bias_t = jnp.tile(hy_bias[layer, o][None, :], (1, tn // BW))
        z2 = _idft_gate(f1i, d, view(gates[o]), blk, zv, zblk, bias_t, ncols)
        zv, zblk = z2, 0
    return zv.reshape(seq_len, BW)


def _store_spectrum_slab(ref, j, word):
    for q in range(word.shape[1] // LANE):
        ref[q, pl.ds(j, FFT_K1, stride=FFT_NB), :] = word[:, q * LANE:(q + 1) * LANE]


def _load_spectrum_slab(ref, j):
    return jnp.concatenate([ref[q, pl.ds(j, FFT_K1, stride=FFT_NB), :] for q in range(ref.shape[0])], axis=1)


def _load_spectrum_k1(ref, j):
    nq, nblk = ref.shape[0], ref.shape[1]
    return jnp.concatenate([jnp.concatenate([ref[q, b, j] for q in range(nq)], axis=1) for b in range(nblk)], axis=0)


def _store_spectrum_k1(ref, j, word):
    nq, nblk = ref.shape[0], ref.shape[1]
    for b in range(nblk):
        for q in range(nq):
            ref[q, b, j] = word[b * FFT_NB:(b + 1) * FFT_NB, q * LANE:(q + 1) * LANE]


def _filter_p_body(feat_ref, w1_ref, b1_ref, f1_ref, w2_ref, b2_ref, f2_ref, w3_ref, dl_ref, dft_ref,
                   af_ref, ss_ref, h2):
    nb = pl.program_id(0)
    cb = pl.program_id(1)
    nbk = FFT_NB

    @pl.when(jnp.logical_and(nb == 0, cb == 0))
    def _():
        ss_ref[...] = jnp.zeros_like(ss_ref)

    @pl.when(cb == 0)
    def _():
        feat = feat_ref[...]
        h = jnp.sin(f1_ref[...] * (_dot_bf16x3(feat, w1_ref[...]) + b1_ref[...]))
        h = jnp.sin(f2_ref[...] * (_dot_bf16x3(h, w2_ref[...]) + b2_ref[...]))
        h2[...] = h.astype(BF16)

    row = lax.broadcasted_iota(I32, (TM, 1), 0)
    lag0_bwd = jnp.logical_and(jnp.logical_and(nb == 0, cb >= 2), row == 0)
    ssum = jnp.zeros((1, BW), F32)
    for j in range(nbk):
        raw = jnp.dot(h2[j * TM:(j + 1) * TM, :], w3_ref[...], preferred_element_type=F32)
        win = jnp.exp(-feat_ref[j * TM:(j + 1) * TM, 0:1] * dl_ref[...]) + HY_MOD_SHIFT
        f = raw * win
        if j == 0:
            f = jnp.where(lag0_bwd, 0.0, f)
        ssum = ssum + jnp.sum(f * f, axis=0, keepdims=True)
        rr = jnp.dot(dft_ref[...], f.astype(BF16), preferred_element_type=F32)
        _store_spectrum_slab(af_ref, j, _pack2(rr[0:FFT_K1], rr[FFT_K1:2 * FFT_K1]))
    ss_ref[cb] = ss_ref[cb] + ssum


def _hy_filters_p(feat_perm, w1p, b1, fr1, w2, b2, fr2, w3p_bf, deltas, f1, layer):
    l = feat_perm.shape[0]
    n2 = l // TM
    nbk = FFT_NB
    lsel = lambda *blk: pl.BlockSpec((None,) + blk, lambda nb, cb: (layer,) + (0,) * len(blk))
    return pl.pallas_call(
        _filter_p_body,
        out_shape=[SDS((4 * BW // LANE, n2 // nbk, FFT_K1 * nbk, LANE), U32), SDS((4, 1, BW), F32)],
        grid=(n2 // nbk, 4),
        in_specs=[pl.BlockSpec((nbk * TM, LANE), lambda nb, cb: (nb, 0)),
                  lsel(LANE, HY_HIDDEN), lsel(1, HY_HIDDEN), lsel(1, HY_HIDDEN),
                  lsel(HY_HIDDEN, HY_HIDDEN), lsel(1, HY_HIDDEN), lsel(1, HY_HIDDEN),
                  pl.BlockSpec((None, HY_HIDDEN, BW), lambda nb, cb: (layer, 0, cb)),
                  pl.BlockSpec((1, BW), lambda nb, cb: (0, 0)),
                  pl.BlockSpec((2 * FFT_K1, FFT_N1 // 2), lambda nb, cb: (0, 0))],
        out_specs=[pl.BlockSpec((BW // LANE, None, FFT_K1 * nbk, LANE), lambda nb, cb: (cb, nb, 0, 0)),
                   pl.BlockSpec((4, 1, BW), lambda nb, cb: (0, 0, 0))],
        scratch_shapes=[pltpu.VMEM((nbk * TM, HY_HIDDEN), BF16)],
        compiler_params=_cparams(("arbitrary", "arbitrary"), 40),
        name="hy_filters_p",
    )(feat_perm, w1p, b1, fr1, w2, b2, fr2, w3p_bf, deltas, f1)


def _filter_minor_p_body(a_ref, g_ref, ss_ref, h_ref, *, n2):
    ssv = ss_ref[...]
    scale = lax.rsqrt(ssv[:, 0:2 * BW] + ssv[:, 2 * BW:4 * BW] + LN_EPS)
    for j in range(a_ref.shape[2]):
        re, im = _unpack2(_load_spectrum_k1(a_ref, j))
        a = jnp.concatenate([re, im], axis=0).astype(BF16)
        s = jnp.dot(g_ref[j], a, preferred_element_type=F32)
        sf, sb = s[:, 0:2 * BW], s[:, 2 * BW:4 * BW]
        hr = (sf[0:n2] + sb[0:n2]) * scale
        hi = (sf[n2:2 * n2] - sb[n2:2 * n2]) * scale
        h_ref[j] = jnp.concatenate([hr, hi], axis=0).astype(BF16)


def _filter_minor_p(af_p, g, ss, n2):
    kb = FFT_KB
    nq, nblk = af_p.shape[0], af_p.shape[1]
    af_p = af_p.reshape(nq, nblk, FFT_K1, FFT_NB, LANE)
    return pl.pallas_call(
        functools.partial(_filter_minor_p_body, n2=n2),
        out_shape=SDS((FFT_K1, 2 * n2, 2 * BW), BF16),
        grid=(FFT_K1 // kb,),
        in_specs=[pl.BlockSpec((nq, nblk, kb, FFT_NB, LANE), lambda i: (0, 0, i, 0, 0)),
                  pl.BlockSpec((kb, 2 * n2, 2 * n2), lambda i: (i, 0, 0)),
                  pl.BlockSpec((1, 4 * BW), lambda i: (0, 0))],
        out_specs=pl.BlockSpec((kb, 2 * n2, 2 * BW), lambda i: (i, 0, 0)),
        compiler_params=_cparams(("parallel",), 40),
        name="hy_filter_minor_p",
    )(af_p, g, ss)


def _unpack_slab(ref, j):
    nq, nblk = ref.shape[0], ref.shape[1]
    word = jnp.concatenate([jnp.concatenate([ref[q, b, j] for q in range(nq)], axis=1) for b in range(nblk)], axis=0)
    hi, lo = _unpack2(word)
    return jnp.concatenate([hi, lo], axis=1)


def _store_slab(ref, j, word):
    nq, nblk = ref.shape[0], ref.shape[1]
    for b in range(nblk):
        for q in range(nq):
            ref[q, b, j] = word[b * FFT_NB:(b + 1) * FFT_NB, q * LANE:(q + 1) * LANE]


def _slab_view(xp, n2):
    nq = xp.shape[0]
    x5 = xp.reshape(nq, xp.shape[1], n2, FFT_NB, LANE)
    spec = pl.BlockSpec((nq, FFT_N1 // 2 // FFT_NB, FFT_NB, FFT_NB, LANE), lambda i: (0, 0, i, 0, 0))
    return x5, spec


def _dft_major_p_body(f_ref, x_ref, a_ref):
    for j in range(x_ref.shape[2]):
        x = _unpack_slab(x_ref, j).astype(BF16)
        rr = jnp.dot(f_ref[...], x, preferred_element_type=F32)
        _store_spectrum_slab(a_ref, j, _pack2(rr[0:FFT_K1], rr[FFT_K1:2 * FFT_K1]))


def _dft_major_p(f1, xp, n2):
    nbk = FFT_NB
    xp, xspec = _slab_view(xp, n2)
    return pl.pallas_call(
        _dft_major_p_body,
        out_shape=SDS((BW // LANE, n2 // nbk, FFT_K1 * nbk, LANE), U32),
        grid=(n2 // nbk,),
        in_specs=[pl.BlockSpec((2 * FFT_K1, FFT_N1 // 2), lambda i: (0, 0)),
                  xspec],
        out_specs=pl.BlockSpec((BW // LANE, None, FFT_K1 * nbk, LANE), lambda i: (0, i, 0, 0)),
        compiler_params=_cparams(("parallel",), 32),
        name="dft_major_p",
    )(f1, xp)


def _conv_minor_p_body(a_ref, h_ref, g_ref, gt_ref, d_ref, *, n2):
    for j in range(a_ref.shape[2]):
        re, im = _unpack2(_load_spectrum_k1(a_ref, j))
        a = jnp.concatenate([re, im], axis=0).astype(BF16)
        x = jnp.dot(g_ref[j], a, preferred_element_type=F32)
        h = h_ref[j].astype(F32)
        xr, xi, hr, hi = x[0:n2], x[n2:2 * n2], h[0:n2], h[n2:2 * n2]
        y = jnp.concatenate([xr * hr - xi * hi, xr * hi + xi * hr], axis=0)
        dv = jnp.dot(gt_ref[j], y.astype(BF16), preferred_element_type=F32)
        _store_spectrum_k1(d_ref, j, _pack2(dv[0:n2], dv[n2:2 * n2]))


def _conv_minor_p(a_p, hspec, g, gt, order, n2):
    kb = FFT_KB
    nq, nblk = a_p.shape[0], a_p.shape[1]
    a_p = a_p.reshape(nq, nblk, FFT_K1, FFT_NB, LANE)
    gspec = pl.BlockSpec((kb, 2 * n2, 2 * n2), lambda i: (i, 0, 0))
    sspec = pl.BlockSpec((nq, nblk, kb, FFT_NB, LANE), lambda i: (0, 0, i, 0, 0))
    return pl.pallas_call(
        functools.partial(_conv_minor_p_body, n2=n2),
        out_shape=SDS((nq, nblk, FFT_K1, FFT_NB, LANE), U32),
        grid=(FFT_K1 // kb,),
        in_specs=[sspec, pl.BlockSpec((kb, 2 * n2, BW), lambda i: (i, 0, order)), gspec, gspec],
        out_specs=sspec,
        compiler_params=_cparams(("parallel",), 32),
        name="hy_conv_minor_p",
    )(a_p, hspec, g, gt)


def _idft_gate_p_body(f_ref, d_ref, gate_ref, z_ref, b_ref, o_ref):
    for j in range(o_ref.shape[2]):
        re, im = _unpack2(_load_spectrum_slab(d_ref, j))
        d = jnp.concatenate([re, im], axis=0).astype(BF16)
        y = jnp.dot(f_ref[...], d, preferred_element_type=F32)
        out = _unpack_slab(gate_ref, j) * (y + b_ref[...] * _unpack_slab(z_ref, j))
        _store_slab(o_ref, j, _pack2(out[:, 0:BW // 2], out[:, BW // 2:BW]))


def _idft_gate_p(f1i, d_p, gate_p, z_p, bias, n2):
    nbk = FFT_NB
    nq, nblk = d_p.shape[0], d_p.shape[1]
    d_p = d_p.reshape(nq, nblk, FFT_K1 * FFT_NB, LANE)
    gate_p, slab = _slab_view(gate_p, n2)
    z_p, _ = _slab_view(z_p, n2)
    nqt = gate_p.shape[0]
    return pl.pallas_call(
        _idft_gate_p_body,
        out_shape=SDS((nqt, FFT_N1 // 2 // FFT_NB, n2, FFT_NB, LANE), U32),
        grid=(n2 // nbk,),
        in_specs=[pl.BlockSpec((FFT_N1 // 2, 2 * FFT_K1), lambda i: (0, 0)),
                  pl.BlockSpec((nq, None, FFT_K1 * FFT_NB, LANE), lambda i: (0, i, 0, 0)),
                  slab, slab, pl.BlockSpec((1, BW), lambda i: (0, 0))],
        out_specs=slab,
        compiler_params=_cparams(("parallel",), 32),
        name="idft_gate_p",
    )(f1i, d_p, gate_p, z_p, bias)


def _hyena_latent(x1p, x2p, vp, feat_perm, tabs, hyw, layer, seq_len):
    n2 = seq_len // (FFT_N1 // 2)
    f1, f1i, g, gt = tabs
    w1p, b1, fr1, w2, b2, fr2, w3p_bf, deltas, hy_bias = hyw
    af_p, ss = _hy_filters_p(feat_perm, w1p, b1, fr1, w2, b2, fr2, w3p_bf, deltas, f1, layer)
    hspec = _filter_minor_p(af_p, g, ss.reshape(1, 4 * BW), n2)
    z = vp
    for o, gate in enumerate((x1p, x2p)):
        a_p = _dft_major_p(f1, z, n2)
        d_p = _conv_minor_p(a_p, hspec, g, gt, o, n2)
        z = _idft_gate_p(f1i, d_p, gate, z, hy_bias[layer, o][None, :], n2)
    return z.reshape(z.shape[0], z.shape[1], n2 * FFT_NB, LANE)


def _merge_body(att_ref, yp_ref, yhp_ref, yhc_ref, ys_ref, hmod_ref, wg_ref, x_ref, wb_ref, wo_ref,
                g1_ref, lg_ref, lb_ref, sh2_ref, sc2_ref, rw_ref, rb_ref,
                x1_ref, m_ref, rt_ref, cnt_ref, *, nlat, n_ctx, alpha):
    i = pl.program_id(0)
    n2c = yhp_ref.shape[2] // FFT_NB
    pieces = []
    for a in range(TM1 // n2c):
        word = jnp.concatenate([yhp_ref[q, a // FFT_NB, pl.ds(a % FFT_NB, n2c, stride=FFT_NB), :]
                                for q in range(yhp_ref.shape[0])], axis=1)
        hi, lo = _unpack2(word)
        pieces.append(jnp.concatenate([hi, lo], axis=1).astype(BF16))
    yh_lat = jnp.concatenate(pieces, axis=0)
    yh_ctx = jnp.concatenate([yhc_ref[...], jnp.zeros((TM1 - n_ctx, BW), BF16)], axis=0)
    yh = jnp.where(i >= nlat, yh_ctx, yh_lat)
    ys = (att_ref[...], yp_ref[...], yh, ys_ref[...])
    half = TM1 // 2
    logit_parts = []
    for hh in range(2):
        rows = slice(hh * half, (hh + 1) * half)
        hm = hmod_ref[rows, :]
        merged = jnp.zeros((half, D), F32)
        for n in range(4):
            br = jnp.dot(ys[n][rows], wb_ref[n], preferred_element_type=F32)
            gl = jnp.dot(hm, wg_ref[:, n * D:(n + 1) * D], preferred_element_type=F32)
            merged = merged + br * (0.5 * jnp.tanh(0.5 * gl) + 0.5)
        y = jnp.dot(merged.astype(BF16), wo_ref[...], preferred_element_type=F32)
        x1 = _layer_norm(alpha * x_ref[rows, :] + g1_ref[...] * y) * lg_ref[...] + lb_ref[...]
        x1_ref[rows, :] = x1
        m = _layer_norm(x1) * (1.0 + sc2_ref[...]) + sh2_ref[...]
        m_ref[rows, :] = m.astype(BF16)
        m_hi = m.astype(BF16)
        m_lo = (m - m_hi.astype(F32)).astype(BF16)
        logit_parts.append(jnp.dot(m_hi, rw_ref[0], preferred_element_type=F32)
                           + jnp.dot(m_lo, rw_ref[0], preferred_element_type=F32)
                           + jnp.dot(m_hi, rw_ref[1], preferred_element_type=F32) + rb_ref[...])
    logits = jnp.concatenate(logit_parts, axis=0)
    lt = logits.T
    le = lt[0:N_EXPERTS]
    lgp = lt[N_EXPERTS:N_EXPERTS + SUB]
    big = 1 << 20
    gi = lax.broadcasted_iota(I32, lgp.shape, 0)
    gmax = jnp.max(lgp, axis=0, keepdims=True)
    gsel = jnp.min(jnp.where(lgp == gmax, gi, big), axis=0, keepdims=True)
    gate_g = 1.0 / jnp.sum(jnp.exp(lgp - gmax), axis=0, keepdims=True)
    ei = lax.broadcasted_iota(I32, le.shape, 0)
    lem = jnp.where(lax.shift_right_logical(ei, 3) == gsel, le, -3.0e38)
    v1 = jnp.max(lem, axis=0, keepdims=True)
    i1 = jnp.min(jnp.where(lem == v1, ei, big), axis=0, keepdims=True)
    lem2 = jnp.where(ei == i1, -3.0e38, lem)
    v2 = jnp.max(lem2, axis=0, keepdims=True)
    i2 = jnp.min(jnp.where(lem2 == v2, ei, big), axis=0, keepdims=True)
    e2 = jnp.exp(v2 - v1)
    wa = gate_g / (1.0 + e2)
    wb = gate_g * e2 / (1.0 + e2)
    tok = lax.broadcasted_iota(I32, (1, TM1), 1)
    valid = jnp.logical_or(i < nlat, tok < n_ctx)
    i1 = jnp.where(valid, i1, -1)
    i2 = jnp.where(valid, i2, -1)
    ri = lax.broadcasted_iota(I32, (SUB, TM1), 0)
    rt = jnp.where(ri == 0, i1.astype(F32), jnp.where(ri == 1, i2.astype(F32),
                   jnp.where(ri == 2, wa, jnp.where(ri == 3, wb, 0.0))))
    rt_ref[...] = rt
    ci = lax.broadcasted_iota(I32, (LANE, TM1), 0)
    oh = jnp.logical_or(ci == i1, ci == i2).astype(BF16)
    cnt_ref[0] = lax.dot_general(jnp.ones((SUB, TM1), BF16), oh, (((1,), (1,)), ((), ())),
                                 preferred_element_type=F32)


def _merge(att, ypool, yhp, yhc, ysc, hmod, wg_bf, x, wb_bf, wo_bf, g1, ln_g, ln_b, sh2, sc2, rw, rb, layer, n_lat,
           n_ctx, alpha):
    nt = x.shape[0]
    nlat = n_lat // TM1
    ntl = nt // TM1
    n2c = yhp.shape[2] // FFT_NB
    sel = lambda i: (jnp.where(i >= nlat, 1, 0), 0, 0)
    row = lambda c: pl.BlockSpec((TM1, c), lambda i: (i, 0))
    lsel = lambda *blk, **kw: pl.BlockSpec((None,) + blk, lambda i: (layer,) + (0,) * len(blk), **kw)
    msel = pl.BlockSpec((None, 1, D), sel)
    return pl.pallas_call(
        functools.partial(_merge_body, nlat=nlat, n_ctx=n_ctx, alpha=alpha),
        out_shape=[SDS((nt, D), F32), SDS((nt, D), BF16), SDS((SUB, nt), F32), SDS((ntl, SUB, LANE), F32)],
        grid=(ntl,),
        in_specs=[row(BW), row(BW),
                  pl.BlockSpec((yhp.shape[0], TM1 // n2c // FFT_NB, n2c * FFT_NB, LANE),
                               lambda i: (0, jnp.minimum(i, nlat - 1), 0, 0)),
                  pl.BlockSpec((n_ctx, BW), lambda i: (0, 0)),
                  row(BW), row(D), lsel(D, 4 * D, pipeline_mode=pl.Buffered(1)), row(D),
                  lsel(4, BW, D, pipeline_mode=pl.Buffered(1)), lsel(D, D, pipeline_mode=pl.Buffered(1)),
                  msel, lsel(1, D), lsel(1, D), msel, msel,
                  lsel(2, D, LANE), lsel(1, LANE)],
        out_specs=[row(D), row(D), pl.BlockSpec((SUB, TM1), lambda i: (0, i)),
                   pl.BlockSpec((1, SUB, LANE), lambda i: (i, 0, 0))],
        compiler_params=_cparams(("parallel",), 56),
        name="merge_router",
    )(att, ypool, yhp, yhc, ysc, hmod, wg_bf, x, wb_bf, wo_bf, g1, ln_g, ln_b, sh2, sc2, rw, rb)


def _piece_loops(np_ref, so_ref, do_ref, j, fn):
    for e in range(N_EXPERTS):
        n = np_ref[j * N_EXPERTS + e]
        so = so_ref[j * N_EXPERTS + e]
        do = do_ref[j * N_EXPERTS + e]

        nbig = lax.shift_right_logical(n, 2)
        big = MOE_BIG * MOE_PIECE

        def body_big(p, carry, so=so, do=do):
            fn(pl.multiple_of(so + p * big, MOE_PIECE), pl.multiple_of(do + p * big, MOE_PIECE), big)
            return carry

        def body(p, carry, so=so, do=do, nbig=nbig):
            off = nbig * big + p * MOE_PIECE
            fn(pl.multiple_of(so + off, MOE_PIECE), pl.multiple_of(do + off, MOE_PIECE), MOE_PIECE)
            return carry

        lax.fori_loop(0, nbig, body_big, 0)
        lax.fori_loop(0, n - nbig * MOE_BIG, body, 0)


def _dispatch_body(np_ref, so_ref, do_ref, nv_ref, m_ref, rt_ref, u_ref, hs_hbm, pos_ref, hs_vmem, sem, *,
                   ntiles, nblk):
    j = pl.program_id(0)
    real = j < ntiles
    rt = rt_ref[...]
    e0 = jnp.where(real, rt[0:1].astype(I32), -1)
    e1 = jnp.where(real, rt[1:2].astype(I32), -1)
    ei = lax.broadcasted_iota(I32, (N_EXPERTS, TM1), 0)
    oh0 = (ei == e0).astype(F32)
    oh1 = (ei == e1).astype(F32)
    c0 = jnp.dot(oh0.astype(BF16), u_ref[...], preferred_element_type=F32)
    c1 = jnp.dot(oh1.astype(BF16), u_ref[...], preferred_element_type=F32)
    n0 = jnp.sum(oh0, axis=1, keepdims=True)
    ecol = lax.broadcasted_iota(I32, (N_EXPERTS, 1), 0)
    toff = jnp.zeros((N_EXPERTS, 1), F32)
    for e in range(N_EXPERTS):
        toff = jnp.where(ecol == e, so_ref[j * N_EXPERTS + e].astype(F32), toff)
    pos0 = jnp.sum(oh0 * (toff + c0), axis=0, keepdims=True)
    pos1 = jnp.sum(oh1 * (toff + n0 + c1), axis=0, keepdims=True)
    pos0 = jnp.where(e0 >= 0, pos0, -1.0)
    pos1 = jnp.where(e1 >= 0, pos1, -1.0)
    ri = lax.broadcasted_iota(I32, (SUB, TM1), 0)
    pos_ref[...] = jnp.where(ri == 0, pos0, jnp.where(ri == 1, pos1, 0.0))
    si = lax.broadcasted_iota(I32, (MOE_S, TM1), 0)
    perm = jnp.logical_or(si == pos0.astype(I32), si == pos1.astype(I32)).astype(BF16)
    hs = jnp.dot(perm, m_ref[...], preferred_element_type=F32)
    slot = lax.rem(j, 2)
    hs_vmem[slot] = _pack2(hs[:, 0:D // 2], hs[:, D // 2:D])

    def copy(s, so, do, n):
        return pltpu.make_async_copy(hs_vmem.at[s, pl.ds(so, n)], hs_hbm.at[pl.ds(do, n)], sem.at[s])

    _piece_loops(np_ref, so_ref, do_ref, j, lambda so, do, n: copy(slot, so, do, n).start())

    @pl.when(j > 0)
    def _():
        _piece_loops(np_ref, so_ref, do_ref, j - 1, lambda so, do, n: copy(1 - slot, so, do, n).wait())

    @pl.when(j == ntiles)
    def _():
        def blk_copy(b):
            return pltpu.make_async_copy(hs_vmem.at[slot, pl.ds(0, MOE_MB)],
                                         hs_hbm.at[pl.ds(pl.multiple_of(b * MOE_MB, MOE_MB), MOE_MB)], sem.at[slot])

        def start(b, carry):
            blk_copy(b).start()
            return carry

        def wait(b, carry):
            blk_copy(b).wait()
            return carry

        lax.fori_loop(nv_ref[0], nblk, start, 0)
        _piece_loops(np_ref, so_ref, do_ref, j, lambda so, do, n: copy(slot, so, do, n).wait())
        lax.fori_loop(nv_ref[0], nblk, wait, 0)


def _dispatch(npieces, soff, doff, nvb, m_bf, rt, upper, rmax):
    nt = m_bf.shape[0]
    ntiles = nt // TM1
    last = ntiles - 1
    return pl.pallas_call(
        functools.partial(_dispatch_body, ntiles=ntiles, nblk=rmax // MOE_MB),
        out_shape=[SDS((rmax, D // 2), U32), SDS((SUB, (ntiles + 1) * TM1), F32)],
        grid_spec=pltpu.PrefetchScalarGridSpec(
            num_scalar_prefetch=4,
            grid=(ntiles + 1,),
            in_specs=[pl.BlockSpec((TM1, D), lambda j, *_: (jnp.minimum(j, last), 0)),
                      pl.BlockSpec((SUB, TM1), lambda j, *_: (0, jnp.minimum(j, last))),
                      pl.BlockSpec((TM1, TM1), lambda j, *_: (0, 0))],
            out_specs=[pl.BlockSpec(memory_space=pl.ANY),
                       pl.BlockSpec((SUB, TM1), lambda j, *_: (0, j))],
            scratch_shapes=[pltpu.VMEM((2, MOE_S, D // 2), U32), pltpu.SemaphoreType.DMA((2,))]),
        compiler_params=_cparams(("arbitrary",), 40),
        name="moe_dispatch",
    )(npieces, soff, doff, nvb, m_bf, rt, upper)


def _expert_body(be_ref, nv_ref, x_ref, w1_ref, w3_ref, w2_ref, o_ref, w1b, w3b, w2b):
    b = pl.program_id(0)
    valid = b < nv_ref[0]
    prev = be_ref[jnp.maximum(b - 1, 0)]
    fresh = jnp.logical_or(b == 0, be_ref[b] != prev)

    @pl.when(jnp.logical_and(valid, fresh))
    def _():
        w1b[...] = w1_ref[...].astype(BF16)
        w3b[...] = w3_ref[...].astype(BF16)
        w2b[...] = w2_ref[...].astype(BF16)

    @pl.when(valid)
    def _():
        xh, xl = _unpack2(x_ref[...])
        xh, xl = xh.astype(BF16), xl.astype(BF16)
        half = D // 2
        h1 = (jnp.dot(xh, w1b[0:half, :], preferred_element_type=F32)
              + jnp.dot(xl, w1b[half:D, :], preferred_element_type=F32))
        h3 = (jnp.dot(xh, w3b[0:half, :], preferred_element_type=F32)
              + jnp.dot(xl, w3b[half:D, :], preferred_element_type=F32))
        hh = (h1 * (0.5 * jnp.tanh(0.5 * h1) + 0.5)) * h3
        y = jnp.dot(hh.astype(BF16), w2b[...], preferred_element_type=F32)
        o_ref[...] = _pack2(y[:, 0:half], y[:, half:D])

    @pl.when(b >= nv_ref[0])
    def _():
        o_ref[...] = jnp.zeros_like(o_ref)


def _experts(blk_e, nvb, hs, ex_w1, ex_w3, ex_w2, layer):
    rmax = hs.shape[0]
    nb = rmax // MOE_MB
    bi = lambda b, be, nv: jnp.maximum(jnp.minimum(b, nv[0] - 1), 0)
    return pl.pallas_call(
        _expert_body,
        out_shape=SDS((rmax, D // 2), U32),
        grid_spec=pltpu.PrefetchScalarGridSpec(
            num_scalar_prefetch=2,
            grid=(nb,),
            in_specs=[pl.BlockSpec((MOE_MB, D // 2), lambda b, be, nv: (bi(b, be, nv), 0)),
                      pl.BlockSpec((None, None, D, EXPERT_HIDDEN), lambda b, be, nv: (layer, be[bi(b, be, nv)], 0, 0)),
                      pl.BlockSpec((None, None, D, EXPERT_HIDDEN), lambda b, be, nv: (layer, be[bi(b, be, nv)], 0, 0)),
                      pl.BlockSpec((None, None, EXPERT_HIDDEN, D), lambda b, be, nv: (layer, be[bi(b, be, nv)], 0, 0))],
            out_specs=pl.BlockSpec((MOE_MB, D // 2), lambda b, be, nv: (b, 0)),
            scratch_shapes=[pltpu.VMEM((D, EXPERT_HIDDEN), BF16), pltpu.VMEM((D, EXPERT_HIDDEN), BF16),
                            pltpu.VMEM((EXPERT_HIDDEN, D), BF16)]),
        compiler_params=_cparams(("arbitrary",), 48),
        name="moe_experts",
    )(blk_e, nvb, hs, ex_w1, ex_w3, ex_w2)


def _combine_body(np_ref, so_ref, do_ref, ys_hbm, pos_ref, rt_ref, x1_ref, g2_ref, lg_ref, lb_ref,
                  o_ref, ys_vmem, sem, *, alpha, nsteps):
    j = pl.program_id(0)
    slot = lax.rem(j, 2)

    def copy(s, so, do, n):
        return pltpu.make_async_copy(ys_hbm.at[pl.ds(do, n)], ys_vmem.at[s, pl.ds(so, n)], sem.at[s])

    @pl.when(j == 0)
    def _():
        _piece_loops(np_ref, so_ref, do_ref, j, lambda so, do, n: copy(slot, so, do, n).start())

    @pl.when(j + 1 < nsteps)
    def _():
        _piece_loops(np_ref, so_ref, do_ref, j + 1, lambda so, do, n: copy(1 - slot, so, do, n).start())
    z = jnp.concatenate([pos_ref[...], rt_ref[...], jnp.zeros((LANE - 2 * SUB, TM1), F32)], axis=0)
    zt = z.T
    p0 = zt[:, 0:1].astype(I32)
    p1 = zt[:, 1:2].astype(I32)
    w0 = zt[:, SUB + 2:SUB + 3]
    w1 = zt[:, SUB + 3:SUB + 4]
    si = lax.broadcasted_iota(I32, (TM1, MOE_S), 1)
    wm = (jnp.where(si == p0, w0, 0.0) + jnp.where(si == p1, w1, 0.0)).astype(BF16)
    _piece_loops(np_ref, so_ref, do_ref, j, lambda so, do, n: copy(slot, so, do, n).wait())
    last = j * N_EXPERTS + N_EXPERTS - 1
    total = so_ref[last] + np_ref[last] * MOE_PIECE
    srow = lax.broadcasted_iota(I32, (MOE_S, 1), 0)
    yh, yl = _unpack2(jnp.where(srow < total, ys_vmem[slot], jnp.uint32(0)))
    f = jnp.concatenate([jnp.dot(wm, yh.astype(BF16), preferred_element_type=F32),
                         jnp.dot(wm, yl.astype(BF16), preferred_element_type=F32)], axis=1)
    o_ref[...] = _layer_norm(alpha * x1_ref[...] + g2_ref[...] * f) * lg_ref[...] + lb_ref[...]


def _combine(npieces, soff, doff, ys, pos, rt, x1, g2, ln_g, ln_b, layer, n_lat, n_out_tiles, n_out_rows, alpha):
    nlat = n_lat // TM1
    sel = lambda j, *_: (jnp.where(j >= nlat, 1, 0), 0, 0)
    lsel = lambda *blk: pl.BlockSpec((None,) + blk, lambda j, *_: (layer,) + (0,) * len(blk))
    return pl.pallas_call(
        functools.partial(_combine_body, alpha=alpha, nsteps=n_out_tiles),
        out_shape=SDS((n_out_rows, D), F32),
        grid_spec=pltpu.PrefetchScalarGridSpec(
            num_scalar_prefetch=3,
            grid=(n_out_tiles,),
            in_specs=[pl.BlockSpec(memory_space=pl.ANY),
                      pl.BlockSpec((SUB, TM1), lambda j, *_: (0, j)),
                      pl.BlockSpec((SUB, TM1), lambda j, *_: (0, j)),
                      pl.BlockSpec((TM1, D), lambda j, *_: (j, 0)),
                      pl.BlockSpec((None, 1, D), sel), lsel(1, D), lsel(1, D)],
            out_specs=pl.BlockSpec((TM1, D), lambda j, *_: (j, 0)),
            scratch_shapes=[pltpu.VMEM((2, MOE_S, D // 2), U32), pltpu.SemaphoreType.DMA((2,))]),
        compiler_params=_cparams(("arbitrary",), 40),
        name="moe_combine",
    )(npieces, soff, doff, ys, pos, rt, x1, g2, ln_g, ln_b)


def _moe_plan(cnt, ntiles):
    c = cnt[:, 0, :N_EXPERTS].astype(I32)
    pad8 = (c + MOE_PIECE - 1) // MOE_PIECE * MOE_PIECE
    toff = jnp.cumsum(pad8, axis=1) - pad8
    tot = pad8.sum(axis=0)
    totb = (tot + MOE_MB - 1) // MOE_MB * MOE_MB
    ends = jnp.cumsum(totb)
    base = ends - totb
    dest = base[None, :] + jnp.cumsum(pad8, axis=0) - pad8
    npieces = jnp.concatenate([pad8 // MOE_PIECE, ((totb - tot) // MOE_PIECE)[None, :]], axis=0)
    soff = jnp.concatenate([toff, jnp.zeros((1, N_EXPERTS), I32)], axis=0)
    doff = jnp.concatenate([dest, (base + tot)[None, :]], axis=0)
    return npieces.reshape(-1), soff.reshape(-1), doff.reshape(-1), ends


def _rope_tables(n_lat, nt):
    t = jnp.arange(n_lat)
    pos = jnp.stack([(t // GRID_W).astype(F32), (t % GRID_W).astype(F32)], axis=1)
    inv = ROPE_BASE ** (-jnp.arange(ROPE_PAIRS, dtype=F32) / ROPE_PAIRS)
    ang = pos[:, :, None] * inv[None, None, :]
    cos = jnp.repeat(jnp.cos(ang), 2, axis=1).reshape(n_lat, 2, 2, ROPE_PAIRS)
    sin = jnp.sin(ang)
    sin = jnp.stack([-sin, sin], axis=2)
    cos = jnp.tile(cos.reshape(n_lat, HEAD_DIM), (1, LANE // HEAD_DIM))
    sin = jnp.tile(sin.reshape(n_lat, HEAD_DIM), (1, LANE // HEAD_DIM))
    cos = jnp.concatenate([cos, jnp.ones((nt - n_lat, LANE), F32)], axis=0)
    sin = jnp.concatenate([sin, jnp.zeros((nt - n_lat, LANE), F32)], axis=0)
    return cos, sin


def _hy_features(l):
    t01 = jnp.linspace(0.0, 1.0, l, dtype=F32)
    fr = jnp.linspace(1e-4, HY_BANDS - 1, HY_BANDS, dtype=F32)
    ang = (2.0 * math.pi / l) * jnp.arange(l, dtype=F32)[:, None] * fr[None, :]
    feat = jnp.concatenate([t01[:, None], jnp.cos(ang), -jnp.sin(ang)], axis=-1)
    feat = jnp.pad(feat, ((0, 0), (0, LANE - HY_EMB)))
    return feat.reshape(FFT_N1 // 2, l // (FFT_N1 // 2), LANE).transpose(1, 0, 2).reshape(l, LANE)


def kernel(x, c, ctx, c_ctx, ada_w, ada_b, w_in, attn_sink, pool_w, pool_scale, hy_conv_w, hy_conv_b, hy_w1, hy_b1, hy_freq1, hy_w2, hy_b2, hy_freq2, hy_w3, hy_bias, sc_conv_w, w_branch, w_out, ln1_g, ln1_b, ln2_g, ln2_b, rg_w, rg_b, re_w, re_b, ex_w1, ex_w3, ex_w2):
    depth = w_in.shape[0]
    assert x.shape[0] == 1 and ctx.shape[0] == 1 and x.shape[2] == D
    n_lat = x.shape[1]
    n_ctx = ctx.shape[1]
    assert n_ctx == TM and n_lat % TM1 == 0 and n_lat >= 2 * TM1
    nt = n_lat + n_ctx + PADR
    ntiles1 = nt // TM1
    alpha = (2 * depth) ** 0.25

    cos_t, sin_t = _rope_tables(n_lat, nt)
    w_main_bf = w_in[:, :, :MAIN_COLS].astype(BF16)
    wg_bf = w_in[:, :, MAIN_COLS:].astype(BF16)
    wb_bf = w_branch.astype(BF16)
    wo_bf = w_out.astype(BF16)
    pool_w_bf = pool_w.astype(BF16)
    r3 = lambda a: a.reshape(depth, 1, a.shape[-1])
    tabs_l = _dft_tables(n_lat // (FFT_N1 // 2))
    tabs_c = _dft_tables(n_ctx // (FFT_N1 // 2))
    feat_l, feat_c = _hy_features(n_lat), _hy_features(n_ctx)
    deltas = jnp.abs(jnp.linspace(math.log(HY_TARGET) / HY_SLOW_PCT, math.log(HY_TARGET) / HY_FAST_PCT,
                                  BW, dtype=F32))[None, :]
    w1p = jnp.pad(hy_w1, ((0, 0), (0, LANE - HY_EMB), (0, 0)))
    w3p_bf = hy_w3.reshape(depth, HY_HIDDEN, 2, 2, BW).transpose(0, 1, 3, 2, 4).reshape(depth, HY_HIDDEN, 4 * BW).astype(BF16)
    hyw = (w1p, r3(hy_b1), r3(hy_freq1), hy_w2, r3(hy_b2), r3(hy_freq2), w3p_bf, deltas, hy_bias)
    rw = jnp.concatenate([re_w, rg_w, jnp.zeros((depth, D, LANE - N_EXPERTS - N_GROUPS), F32)], axis=2)
    rw_hi = rw.astype(BF16)
    rw = jnp.stack([rw_hi, (rw - rw_hi.astype(F32)).astype(BF16)], axis=1)
    rb = jnp.concatenate([re_b, rg_b, jnp.full((depth, LANE - N_EXPERTS - N_GROUPS), NEG_INF, F32)], axis=1)
    rb = rb.reshape(depth, 1, LANE)
    upper = (jnp.arange(TM1)[:, None] < jnp.arange(TM1)[None, :]).astype(BF16)
    rmax = -(-(2 * (n_lat + n_ctx) + ntiles1 * N_EXPERTS * (MOE_PIECE - 1) + N_EXPERTS * (MOE_MB - 1)) // MOE_MB) * MOE_MB
    nblk = rmax // MOE_MB

    s8 = jnp.concatenate([c, c_ctx[None, :], jnp.zeros((SUB - 2, D), F32)], axis=0)
    mod = _mod_all(s8, ada_w, ada_b.reshape(depth, 1, 6 * D))

    xs = jnp.concatenate([x[0], ctx[0], jnp.zeros((PADR, D), F32)], axis=0)
    for i in range(depth):
        last = i == depth - 1
        mp = lambda j: mod[i, 0:2, j * D:(j + 1) * D].reshape(2, 1, D)
        sh1, sc1, g1, sh2, sc2, g2 = (mp(j) for j in range(6))

        q, k, v, pool_in, hy_in, sc_in, hmod = _inproj(xs, sh1, sc1, cos_t, sin_t, w_main_bf, i, n_lat // TM1)
        att = _attention(attn_sink[i], q, k, v, n_lat)
        ypool, ysc, x1p, x2p, vp, x1c, x2c, vc = _local_ops(
            pool_in, sc_in, hy_in, pool_w_bf, pool_scale.reshape(depth, 1, BW),
            hy_conv_w, hy_conv_b.reshape(depth, 1, 3 * BW), sc_conv_w, i, n_lat, n_ctx)
        yhp = _hyena_latent(x1p, x2p, vp, feat_l, tabs_l, hyw, i, n_lat)
        if last:
            yhc = jnp.zeros((n_ctx, BW), BF16)
        else:
            yhc = _hyena(x1c, x2c, vc, feat_c, tabs_c, hyw, i, n_ctx, 0)
        x1, m_bf, rt, cnt = _merge(att, ypool, yhp, yhc, ysc, hmod, wg_bf, xs, wb_bf, wo_bf, g1, r3(ln1_g), r3(ln1_b),
                                   sh2, sc2, rw, rb, i, n_lat, n_ctx, alpha)

        npieces, soff, doff, ends = _moe_plan(cnt, ntiles1)
        nvb = (ends[-1] // MOE_MB).astype(I32).reshape(1)
        blk_start = jnp.arange(nblk, dtype=I32) * MOE_MB
        blk_e = jnp.minimum(jnp.sum((ends[None, :] <= blk_start[:, None]).astype(I32), axis=1), N_EXPERTS - 1)
        hs, pos = _dispatch(npieces, soff, doff, nvb, m_bf, rt, upper, rmax)
        ys = _experts(blk_e, nvb, hs, ex_w1, ex_w3, ex_w2, i)
        if last:
            xs = _combine(npieces, soff, doff, ys, pos, rt, x1, g2, r3(ln2_g), r3(ln2_b), i, n_lat,
                          n_lat // TM1, n_lat, alpha)
        else:
            xs = _combine(npieces, soff, doff, ys, pos, rt, x1, g2, r3(ln2_g), r3(ln2_b), i, n_lat, ntiles1, nt,
                          alpha)
    return xs[None]
```

```python
import functools
import math

import jax
import jax.numpy as jnp
from jax import lax
from jax.experimental import pallas as pl
from jax.experimental.pallas import tpu as pltpu

F32 = jnp.float32
BF16 = jnp.bfloat16
I32 = jnp.int32
U32 = jnp.uint32
SDS = jax.ShapeDtypeStruct
HIGHEST = lax.Precision.HIGHEST

D = 1024
GRID_W = 64
HEADS = 8
KV_HEADS = 2
HEAD_DIM = 64
WINDOW = 128
ATT_BLK = 128
ROPE_PAIRS = 16
ROPE_BASE = 10000.0
BW = 512
POOL_WINDOWS = (2, 4, 8, 16)
POOL_GW = 128
HY_EMB = 33
HY_BANDS = 16
HY_HIDDEN = 64
HY_TARGET = 1e-2
HY_FAST_PCT = 0.3
HY_SLOW_PCT = 1.5
HY_MOD_SHIFT = 0.05
Q_COLS = HEADS * HEAD_DIM
KV_COLS = KV_HEADS * HEAD_DIM
MAIN_COLS = Q_COLS + 2 * KV_COLS + BW + 3 * BW + 3 * BW
N_GROUPS = 4
EPG = 8
N_EXPERTS = 32
EXPERT_HIDDEN = 512
LN_EPS = 1e-6
NEG_INF = -1e30

LANE = 128
SUB = 8
BSUB = 16
TM = 256
TM1 = 512
PADR = 256
HALO = 16
FFT_N1 = 512
FFT_K1 = 264
FFT_KB = 24
FFT_TN = 2048
FFT_NB = 8
MOE_PIECE = 8
MOE_BIG = 4
MOE_MB = 512
MOE_S = 2 * TM1 + 256


def _cparams(sem, vmem_mb):
    return pltpu.CompilerParams(dimension_semantics=sem, vmem_limit_bytes=vmem_mb * 2 ** 20)


def _dot_bf16x3(a, b):
    a_hi = a.astype(BF16)
    a_lo = (a - a_hi.astype(F32)).astype(BF16)
    b_hi = b.astype(BF16)
    b_lo = (b - b_hi.astype(F32)).astype(BF16)
    return (jnp.dot(a_hi, b_hi, preferred_element_type=F32) + jnp.dot(a_lo, b_hi, preferred_element_type=F32)
            + jnp.dot(a_hi, b_lo, preferred_element_type=F32))


def _layer_norm(x):
    mu = jnp.mean(x, axis=-1, keepdims=True)
    xc = x - mu
    var = jnp.mean(xc * xc, axis=-1, keepdims=True)
    return xc * lax.rsqrt(var + LN_EPS)


def _mod_body(s_ref, w_ref, b_ref, o_ref):
    s = s_ref[...]
    s = s * (1.0 / (1.0 + jnp.exp(-s)))
    o_ref[...] = jnp.dot(s.astype(BF16), w_ref[...].astype(BF16), preferred_element_type=F32) + b_ref[...]


def _mod_all(s8, ada_w, ada_b3):
    depth = ada_w.shape[0]
    tn = 1024
    return pl.pallas_call(
        _mod_body,
        out_shape=SDS((depth, SUB, 6 * D), F32),
        grid=(depth, 6 * D // tn),
        in_specs=[pl.BlockSpec((SUB, D), lambda l, j: (0, 0)),
                  pl.BlockSpec((None, D, tn), lambda l, j: (l, 0, j)),
                  pl.BlockSpec((None, 1, tn), lambda l, j: (l, 0, j))],
        out_specs=pl.BlockSpec((None, SUB, tn), lambda l, j: (l, 0, j)),
        compiler_params=_cparams(("parallel", "parallel"), 32),
        name="ada_mod",
    )(s8, ada_w, ada_b3)


def _inproj_body(x_ref, sh_ref, sc_ref, cos_ref, sin_ref, w_ref,
                 q_ref, k_ref, v_ref, pool_ref, hy_ref, scv_ref, hmod_ref):
    h = _layer_norm(x_ref[...]) * (1.0 + sc_ref[...]) + sh_ref[...]
    hb = h.astype(BF16)

    def mm(a, b):
        return jnp.dot(hb, w_ref[:, a:b], preferred_element_type=F32)

    cs = cos_ref[...]
    sn = sin_ref[...]
    lane = lax.broadcasted_iota(I32, cs.shape, 1)
    first = (lane & ROPE_PAIRS) == 0

    def rope(t):
        sw = jnp.where(first, pltpu.roll(t, LANE - ROPE_PAIRS, 1), pltpu.roll(t, ROPE_PAIRS, 1))
        return t * cs + sw * sn

    for j in range(Q_COLS // LANE):
        q_ref[:, j * LANE:(j + 1) * LANE] = (rope(mm(j * LANE, (j + 1) * LANE)) * (HEAD_DIM ** -0.5)).astype(BF16)
    o = Q_COLS
    k_ref[...] = rope(mm(o, o + KV_COLS)).astype(BF16)
    o += KV_COLS
    v_ref[...] = mm(o, o + KV_COLS).astype(BF16)
    o += KV_COLS
    pool_ref[...] = mm(o, o + BW).astype(BF16)
    o += BW
    for j in range(3):
        hy_ref[:, j * BW:(j + 1) * BW] = mm(o + j * BW, o + (j + 1) * BW).astype(BF16)
    o += 3 * BW
    for j in range(3):
        scv_ref[:, j * BW:(j + 1) * BW] = mm(o + j * BW, o + (j + 1) * BW).astype(BF16)
    hmod_ref[...] = hb


def _inproj(x, sh, sc, cos_t, sin_t, w_in_bf, layer, n_lat_tiles):
    nt = x.shape[0]
    sel = lambda i: (jnp.where(i >= n_lat_tiles, 1, 0), 0, 0)
    row = lambda c: pl.BlockSpec((TM1, c), lambda i: (i, 0))
    outs = [SDS((nt, Q_COLS), BF16), SDS((nt, KV_COLS), BF16), SDS((nt, KV_COLS), BF16), SDS((nt, BW), BF16),
            SDS((nt, 3 * BW), BF16), SDS((nt, 3 * BW), BF16), SDS((nt, D), BF16)]
    return pl.pallas_call(
        _inproj_body,
        out_shape=outs,
        grid=(nt // TM1,),
        in_specs=[row(D),
                  pl.BlockSpec((None, 1, D), sel), pl.BlockSpec((None, 1, D), sel),
                  row(LANE), row(LANE),
                  pl.BlockSpec((None, D, MAIN_COLS), lambda i: (layer, 0, 0), pipeline_mode=pl.Buffered(1))],
        out_specs=[row(Q_COLS), row(KV_COLS), row(KV_COLS), row(BW), row(3 * BW), row(3 * BW), row(D)],
        compiler_params=_cparams(("parallel",), 58),
        name="inproj",
    )(x, sh, sc, cos_t, sin_t, w_in_bf)


def _attn_body(sink_ref, q_ref, kp_ref, km_ref, kn_ref, vp_ref, vm_ref, vn_ref, kx_ref, vx_ref, o_ref, *, nlt):
    i = pl.program_id(0)
    is_lat = i < nlt
    lat_i = jnp.where(is_lat, 1, 0)
    first_i = jnp.where(jnp.logical_and(is_lat, i > 0), 1, 0)
    last_i = jnp.where(jnp.logical_and(is_lat, i < nlt - 1), 1, 0)
    nsub = TM1 // ATT_BLK

    def variants(x):
        xf = x.astype(F32)
        lo = lax.broadcasted_iota(I32, xf.shape, 1) < HEAD_DIM
        xr = pltpu.roll(xf, HEAD_DIM, 1)
        zero = jnp.zeros_like(xf)
        out = [[jnp.where(lo, xf, zero), jnp.where(lo, zero, xr)], [jnp.where(lo, xr, zero), jnp.where(lo, zero, xf)]]
        return [[a.astype(BF16) for a in row] for row in out]

    k_ctx, v_ctx = variants(kx_ref[...]), variants(vx_ref[...])
    k_loc = variants(jnp.concatenate([kp_ref[...], km_ref[...], kn_ref[...]], axis=0))
    v_loc = variants(jnp.concatenate([vp_ref[...], vm_ref[...], vn_ref[...]], axis=0))

    r = lax.broadcasted_iota(I32, (ATT_BLK, 3 * ATT_BLK), 0)
    c = lax.broadcasted_iota(I32, (ATT_BLK, 3 * ATT_BLK), 1)
    band = jnp.logical_and(c >= r, c <= r + 2 * WINDOW)
    nt_dims = (((1,), (1,)), ((), ()))
    for s in range(nsub):
        prev_i = lat_i if s > 0 else first_i
        next_i = lat_i if s < nsub - 1 else last_i
        grp = jnp.where(c < ATT_BLK, prev_i, jnp.where(c < 2 * ATT_BLK, lat_i, next_i))
        mask = jnp.logical_and(band, grp > 0)
        win = slice(s * ATT_BLK, (s + 3) * ATT_BLK)
        rows = slice(s * ATT_BLK, (s + 1) * ATT_BLK)
        for p in range(HEADS // 2):
            qp = q_ref[rows, p * LANE:(p + 1) * LANE]
            kh = (2 * p) // (HEADS // KV_HEADS)
            acc = jnp.zeros((ATT_BLK, LANE), F32)
            for rr in range(2):
                sink = sink_ref[2 * p + rr]
                s_ctx = lax.dot_general(qp, k_ctx[kh][rr], nt_dims, preferred_element_type=F32)
                s_loc = lax.dot_general(qp, k_loc[kh][rr][win], nt_dims, preferred_element_type=F32)
                s_loc = jnp.where(mask, s_loc, NEG_INF)
                m = jnp.maximum(jnp.maximum(jnp.max(s_ctx, axis=1, keepdims=True),
                                            jnp.max(s_loc, axis=1, keepdims=True)), sink)
                e_ctx = jnp.exp(s_ctx - m)
                e_loc = jnp.exp(s_loc - m)
                den = (jnp.sum(e_ctx, axis=1, keepdims=True) + jnp.sum(e_loc, axis=1, keepdims=True)
                       + jnp.exp(sink - m))
                o = (jnp.dot(e_ctx.astype(BF16), v_ctx[kh][rr], preferred_element_type=F32)
                     + jnp.dot(e_loc.astype(BF16), v_loc[kh][rr][win], preferred_element_type=F32))
                acc = acc + o * (1.0 / den)
            o_ref[rows, p * LANE:(p + 1) * LANE] = acc.astype(BF16)


def _attention(sink, q, k, v, n_lat):
    nt = q.shape[0]
    nlt = n_lat // TM1
    nlb = n_lat // ATT_BLK
    per = TM1 // ATT_BLK
    cxb = n_lat // TM
    side = lambda f: pl.BlockSpec((ATT_BLK, KV_COLS), lambda i: (jnp.clip(f(i), 0, nlb - 1), 0))
    prev = side(lambda i: i * per - 1)
    nxt = side(lambda i: (i + 1) * per)
    main = pl.BlockSpec((TM1, KV_COLS), lambda i: (jnp.minimum(i, nlt - 1), 0))
    cx = pl.BlockSpec((TM, KV_COLS), lambda i: (cxb, 0))
    return pl.pallas_call(
        functools.partial(_attn_body, nlt=nlt),
        out_shape=SDS((nt, Q_COLS), BF16),
        grid=(nt // TM1,),
        in_specs=[pl.BlockSpec(memory_space=pltpu.SMEM),
                  pl.BlockSpec((TM1, Q_COLS), lambda i: (i, 0)),
                  prev, main, nxt, prev, main, nxt, cx, cx],
        out_specs=pl.BlockSpec((TM1, Q_COLS), lambda i: (i, 0)),
        compiler_params=_cparams(("parallel",), 40),
        name="attention",
    )(sink, q, k, k, k, v, v, v, k, v)


def _pack2(a, b):
    hi = lax.bitcast_convert_type(a.astype(BF16).astype(F32), U32)
    lo = lax.bitcast_convert_type(b.astype(BF16).astype(F32), U32)
    return hi | (lo >> 16)


def _unpack2(w):
    return (lax.bitcast_convert_type(w & jnp.uint32(0xFFFF0000), F32),
            lax.bitcast_convert_type(w << 16, F32))


def _local_body(pm_ref, pa_ref, pb_ref, sm_ref, sa_ref, sb_ref, hm_ref, ha_ref, hb_ref,
                pw_ref, ps_ref, hw_ref, hbias_ref, sw_ref, band_ref,
                ypool_ref, ysc_ref, x1p_ref, x2p_ref, vp_ref, x1c_ref, x2c_ref, vc_ref,
                pext, sext, hext, *, nlat, n_lat, n_ctx):
    i = pl.program_id(0)
    is_lat = i < nlat
    pf = jnp.where(jnp.logical_and(is_lat, i > 0), 1.0, 0.0)
    nf = jnp.where(jnp.logical_and(is_lat, i < nlat - 1), 1.0, 0.0)
    r = lax.broadcasted_iota(I32, (TM1, 1), 0)
    keep = jnp.where(jnp.logical_or(is_lat, r < n_ctx), 1.0, 0.0)

    def fill(ext, a_ref, m_ref, b_ref):
        ext[0:HALO, :] = a_ref[...].astype(F32) * pf
        ext[HALO:HALO + TM1, :] = m_ref[...].astype(F32) * keep
        ext[HALO + TM1:HALO + TM1 + HALO, :] = b_ref[...].astype(F32) * nf

    def sh(ext, d, c0=None, c1=None):
        if c0 is None:
            return ext[pl.ds(HALO + d, TM1), :]
        return ext[pl.ds(HALO + d, TM1), c0:c1]

    fill(pext, pa_ref, pm_ref, pb_ref)
    t = jnp.where(is_lat, i * TM1, 0) + r
    ln = jnp.where(is_lat, n_lat, n_ctx)
    for g, w in enumerate(POOL_WINDOWS):
        left = w // 2
        right = w - left - 1
        c0, c1 = g * POOL_GW, (g + 1) * POOL_GW
        acc = jnp.dot(band_ref[g], pext[:, c0:c1].astype(BF16), preferred_element_type=F32)
        cnt = jnp.maximum(jnp.minimum(t + right + 1, ln) - jnp.maximum(t - left, 0), 1).astype(F32)
        dpool = acc / cnt - sh(pext, 0, c0, c1)
        y = jnp.dot(dpool.astype(BF16), pw_ref[g], preferred_element_type=F32)
        ypool_ref[:, c0:c1] = (y * ps_ref[:, c0:c1]).astype(BF16)

    sext[0:HALO, :] = (sa_ref[:, BW:2 * BW].astype(F32) * sa_ref[:, 2 * BW:3 * BW].astype(F32)) * pf
    sext[HALO:HALO + TM1, :] = sm_ref[:, BW:2 * BW].astype(F32) * sm_ref[:, 2 * BW:3 * BW].astype(F32) * keep
    sext[HALO + TM1:HALO + TM1 + HALO, :] = (sb_ref[:, BW:2 * BW].astype(F32) * sb_ref[:, 2 * BW:3 * BW].astype(F32)) * nf
    conv = sh(sext, -1) * sw_ref[0:1, :] + sh(sext, 0) * sw_ref[1:2, :] + sh(sext, 1) * sw_ref[2:3, :]
    ysc_ref[...] = (sm_ref[:, 0:BW].astype(F32) * conv).astype(BF16)

    fill(hext, ha_ref, hm_ref, hb_ref)
    nat = (x1c_ref, x2c_ref, vc_ref)
    pk = (x1p_ref, x2p_ref, vp_ref)
    n2c = x1p_ref.shape[2] // FFT_NB
    for j in range(3):
        c0, c1 = j * BW, (j + 1) * BW
        uc = (sh(hext, -1, c0, c1) * hw_ref[0:1, c0:c1] + sh(hext, 0, c0, c1) * hw_ref[1:2, c0:c1]
              + sh(hext, 1, c0, c1) * hw_ref[2:3, c0:c1] + hbias_ref[:, c0:c1])
        nat[j][...] = uc.astype(BF16)
        word = _pack2(uc[:, 0:BW // 2], uc[:, BW // 2:BW])
        for a in range(TM1 // n2c):
            for q in range(BW // 2 // LANE):
                pk[j][q, a // FFT_NB, pl.ds(a % FFT_NB, n2c, stride=FFT_NB), :] = (
                    word[a * n2c:(a + 1) * n2c, q * LANE:(q + 1) * LANE])


def _pool_bands():
    r = jnp.arange(TM1)[:, None]
    c = jnp.arange(TM1 + 2 * HALO)[None, :] - HALO
    return jnp.stack([((c >= r - w // 2) & (c <= r + (w - w // 2 - 1))).astype(BF16) for w in POOL_WINDOWS])


def _local_ops(pool_in, sc_in, hy_in, pool_w_bf, pool_scale, hy_conv_w, hy_conv_b, sc_conv_w, layer, n_lat, n_ctx):
    nt = pool_in.shape[0]
    nlat = n_lat // TM1
    n2c = n_lat // (FFT_N1 // 2)
    per_tile = TM1 // n2c
    nhb = nt // HALO
    per = TM1 // HALO
    main = lambda c: pl.BlockSpec((TM1, c), lambda i: (i, 0))
    before = lambda c: pl.BlockSpec((HALO, c), lambda i: (jnp.maximum(i * per - 1, 0), 0))
    after = lambda c: pl.BlockSpec((HALO, c), lambda i: (jnp.minimum((i + 1) * per, nhb - 1), 0))
    lsel = lambda *blk: pl.BlockSpec((None,) + blk, lambda i: (layer,) + (0,) * len(blk))
    out = SDS((nt, BW), BF16)
    nq, g8 = BW // 2 // LANE, per_tile // FFT_NB
    packed = SDS((nq, (nlat + 1) * g8, n2c * FFT_NB, LANE), U32)
    ctx_out = SDS((TM1, BW), BF16)
    pspec = pl.BlockSpec((nq, g8, n2c * FFT_NB, LANE), lambda i: (0, i, 0, 0))
    cspec = pl.BlockSpec((TM1, BW), lambda i: (0, 0))
    return pl.pallas_call(
        functools.partial(_local_body, nlat=nlat, n_lat=n_lat, n_ctx=n_ctx),
        out_shape=[out, out, packed, packed, packed, ctx_out, ctx_out, ctx_out],
        grid=(nlat + 1,),
        in_specs=[main(BW), before(BW), after(BW),
                  main(3 * BW), before(3 * BW), after(3 * BW),
                  main(3 * BW), before(3 * BW), after(3 * BW),
                  lsel(4, POOL_GW, POOL_GW), lsel(1, BW), lsel(3, 3 * BW), lsel(1, 3 * BW), lsel(3, BW),
                  pl.BlockSpec((len(POOL_WINDOWS), TM1, TM1 + 2 * HALO), lambda i: (0, 0, 0))],
        out_specs=[main(BW), main(BW), pspec, pspec, pspec, cspec, cspec, cspec],
        scratch_shapes=[pltpu.VMEM((TM1 + 2 * HALO, BW), F32), pltpu.VMEM((TM1 + 2 * HALO, BW), F32),
                        pltpu.VMEM((TM1 + 2 * HALO, 3 * BW), F32)],
        compiler_params=_cparams(("arbitrary",), 48),
        name="local_ops",
    )(pool_in, pool_in, pool_in, sc_in, sc_in, sc_in, hy_in, hy_in, hy_in,
      pool_w_bf, pool_scale, hy_conv_w, hy_conv_b, sc_conv_w, _pool_bands())


def _dft_tables(n2):
    n = FFT_N1 * n2
    k1 = jnp.arange(FFT_K1, dtype=I32)
    n1 = jnp.arange(FFT_N1 // 2, dtype=I32)
    ok = (k1 <= FFT_N1 // 2)
    ang = (2.0 * math.pi / FFT_N1) * ((k1[:, None] * n1[None, :]) % FFT_N1).astype(F32)
    c1 = jnp.where(ok[:, None], jnp.cos(ang), 0.0)
    s1 = jnp.where(ok[:, None], jnp.sin(ang), 0.0)
    f1 = jnp.concatenate([c1, -s1], axis=0).astype(BF16)
    wt = jnp.where((k1 == 0) | (k1 == FFT_N1 // 2), 1.0, 2.0) / n
    f1i = jnp.concatenate([(c1 * wt[:, None]).T, (-s1 * wt[:, None]).T], axis=1).astype(BF16)
    if n2 == 1:
        return f1, f1i, None, None
    k2 = jnp.arange(n2, dtype=I32)
    m2 = jnp.arange(n2, dtype=I32)
    kk = k1[:, None, None] + FFT_N1 * k2[None, :, None]
    th = (2.0 * math.pi / n) * ((kk * m2[None, None, :]) % n).astype(F32)
    okb = ok[:, None, None]
    cc = jnp.where(okb, jnp.cos(th), 0.0)
    ss = jnp.where(okb, jnp.sin(th), 0.0)
    g = jnp.concatenate([jnp.concatenate([cc, ss], axis=2), jnp.concatenate([-ss, cc], axis=2)], axis=1)
    return f1, f1i, g.astype(BF16), jnp.swapaxes(g, 1, 2).astype(BF16)


def _filter_body(feat_ref, w1_ref, b1_ref, f1_ref, w2_ref, b2_ref, f2_ref, w3_ref, dl_ref, dft_ref,
                 af_ref, ss_ref, filt):
    i = pl.program_id(0)
    feat = feat_ref[...]
    h = jnp.sin(f1_ref[...] * (_dot_bf16x3(feat, w1_ref[...]) + b1_ref[...]))
    h = jnp.sin(f2_ref[...] * (_dot_bf16x3(h, w2_ref[...]) + b2_ref[...]))
    raw = jnp.dot(h.astype(BF16), w3_ref[...], preferred_element_type=F32)
    win = jnp.exp(-feat[:, 0:1] * dl_ref[...]) + HY_MOD_SHIFT
    row = lax.broadcasted_iota(I32, (TM, 1), 0) + i * TM

    @pl.when(i == 0)
    def _():
        ss_ref[...] = jnp.zeros_like(ss_ref)

    for j in range(4):
        f = raw[:, j * BW:(j + 1) * BW] * win
        if j >= 2:
            f = jnp.where(row == 0, 0.0, f)
        filt[:, j * BW:(j + 1) * BW] = f.astype(BF16)
        ss_ref[:, j * BW:(j + 1) * BW] += jnp.sum(f * f, axis=0, keepdims=True)
    af_ref[...] = jnp.dot(dft_ref[...], filt[...], preferred_element_type=F32).astype(BF16)


def _hy_filters(feat_perm, w1p, b1, fr1, w2, b2, fr2, w3p_bf, deltas, f1, layer):
    l = feat_perm.shape[0]
    n2 = l // TM
    lsel = lambda *blk: pl.BlockSpec((None,) + blk, lambda i: (layer,) + (0,) * len(blk))
    return pl.pallas_call(
        _filter_body,
        out_shape=[SDS((2 * FFT_K1, n2 * 4 * BW), BF16), SDS((1, 4 * BW), F32)],
        grid=(n2,),
        in_specs=[pl.BlockSpec((TM, LANE), lambda i: (i, 0)),
                  lsel(LANE, HY_HIDDEN), lsel(1, HY_HIDDEN), lsel(1, HY_HIDDEN),
                  lsel(HY_HIDDEN, HY_HIDDEN), lsel(1, HY_HIDDEN), lsel(1, HY_HIDDEN),
                  lsel(HY_HIDDEN, 4 * BW), pl.BlockSpec((1, BW), lambda i: (0, 0)),
                  pl.BlockSpec((2 * FFT_K1, FFT_N1 // 2), lambda i: (0, 0))],
        out_specs=[pl.BlockSpec((2 * FFT_K1, 4 * BW), lambda i: (0, i)), pl.BlockSpec((1, 4 * BW), lambda i: (0, 0))],
        scratch_shapes=[pltpu.VMEM((TM, 4 * BW), BF16)],
        compiler_params=_cparams(("arbitrary",), 32),
        name="hy_filters",
    )(feat_perm, w1p, b1, fr1, w2, b2, fr2, w3p_bf, deltas, f1)


def _dft_major_body(f_ref, x_ref, o_ref):
    o_ref[...] = jnp.dot(f_ref[...], x_ref[...], preferred_element_type=F32).astype(BF16)


def _dft_major(f1, x2d, row_blk, ncols):
    tn = min(FFT_TN, ncols)
    return pl.pallas_call(
        _dft_major_body,
        out_shape=SDS((2 * FFT_K1, ncols), BF16),
        grid=(ncols // tn,),
        in_specs=[pl.BlockSpec((2 * FFT_K1, FFT_N1 // 2), lambda j: (0, 0)),
                  pl.BlockSpec((FFT_N1 // 2, tn), lambda j: (row_blk, j))],
        out_specs=pl.BlockSpec((2 * FFT_K1, tn), lambda j: (0, j)),
        compiler_params=_cparams(("parallel",), 32),
        name="dft_major",
    )(f1, x2d)


def _filter_minor_body(a_ref, g_ref, ss_ref, h_ref, *, n2):
    ssv = ss_ref[...]
    scale = lax.rsqrt(ssv[:, 0:2 * BW] + ssv[:, 2 * BW:4 * BW] + LN_EPS)
    for j in range(a_ref.shape[1]):
        if n2 == 1:
            s = jnp.concatenate([a_ref[0, j], a_ref[1, j]], axis=0).astype(F32)
        else:
            a = jnp.concatenate([a_ref[0, j], a_ref[1, j]], axis=0)
            s = jnp.dot(g_ref[j], a, preferred_element_type=F32)
        sf, sb = s[:, 0:2 * BW], s[:, 2 * BW:4 * BW]
        hr = (sf[0:n2] + sb[0:n2]) * scale
        hi = (sf[n2:2 * n2] - sb[n2:2 * n2]) * scale
        h_ref[j] = jnp.concatenate([hr, hi], axis=0).astype(BF16)


def _conv_minor_body(a_ref, h_ref, g_ref, gt_ref, d_ref, *, n2):
    for j in range(a_ref.shape[1]):
        a = jnp.concatenate([a_ref[0, j], a_ref[1, j]], axis=0)
        if n2 == 1:
            x = a.astype(F32)
        else:
            x = jnp.dot(g_ref[j], a, preferred_element_type=F32)
        h = h_ref[j].astype(F32)
        xr, xi, hr, hi = x[0:n2], x[n2:2 * n2], h[0:n2], h[n2:2 * n2]
        y = jnp.concatenate([xr * hr - xi * hi, xr * hi + xi * hr], axis=0)
        if n2 == 1:
            dv = y
        else:
            dv = jnp.dot(gt_ref[j], y.astype(BF16), preferred_element_type=F32)
        d_ref[0, j] = dv[0:n2].astype(BF16)
        d_ref[1, j] = dv[n2:2 * n2].astype(BF16)


def _filter_minor(af, g, ss, n2):
    a4 = af.reshape(2, FFT_K1, n2, 4 * BW)
    kb = FFT_KB
    gspec = (pl.BlockSpec((kb, 2 * n2, 2 * n2), lambda i: (i, 0, 0)) if n2 > 1
             else pl.BlockSpec((SUB, LANE), lambda i: (0, 0)))
    return pl.pallas_call(
        functools.partial(_filter_minor_body, n2=n2),
        out_shape=SDS((FFT_K1, 2 * n2, 2 * BW), BF16),
        grid=(FFT_K1 // kb,),
        in_specs=[pl.BlockSpec((2, kb, n2, 4 * BW), lambda i: (0, i, 0, 0)), gspec,
                  pl.BlockSpec((1, 4 * BW), lambda i: (0, 0))],
        out_specs=pl.BlockSpec((kb, 2 * n2, 2 * BW), lambda i: (i, 0, 0)),
        compiler_params=_cparams(("parallel",), 40),
        name="hy_filter_minor",
    )(a4, g if n2 > 1 else jnp.zeros((SUB, LANE), BF16), ss)


def _conv_minor(a, hspec, g, gt, order, n2):
    a4 = a.reshape(2, FFT_K1, n2, BW)
    kb = FFT_KB
    dummy = jnp.zeros((SUB, LANE), BF16)
    gspec = (pl.BlockSpec((kb, 2 * n2, 2 * n2), lambda i: (i, 0, 0)) if n2 > 1
             else pl.BlockSpec((SUB, LANE), lambda i: (0, 0)))
    d4 = pl.pallas_call(
        functools.partial(_conv_minor_body, n2=n2),
        out_shape=SDS((2, FFT_K1, n2, BW), BF16),
        grid=(FFT_K1 // kb,),
        in_specs=[pl.BlockSpec((2, kb, n2, BW), lambda i: (0, i, 0, 0)),
                  pl.BlockSpec((kb, 2 * n2, BW), lambda i: (i, 0, order)), gspec, gspec],
        out_specs=pl.BlockSpec((2, kb, n2, BW), lambda i: (0, i, 0, 0)),
        compiler_params=_cparams(("parallel",), 32),
        name="hy_conv_minor",
    )(a4, hspec, g if n2 > 1 else dummy, gt if n2 > 1 else dummy)
    return d4.reshape(2 * FFT_K1, n2 * BW)


def _idft_gate_body(f_ref, d_ref, gate_ref, z_ref, b_ref, o_ref):
    y = jnp.dot(f_ref[...], d_ref[...], preferred_element_type=F32)
    z = z_ref[...].astype(F32)
    o_ref[...] = (gate_ref[...].astype(F32) * (y + b_ref[...] * z)).astype(BF16)


def _idft_gate(f1i, d, gate2d, gate_blk, z2d, z_blk, bias_t, ncols):
    tn = min(FFT_TN, ncols)
    return pl.pallas_call(
        _idft_gate_body,
        out_shape=SDS((FFT_N1 // 2, ncols), BF16),
        grid=(ncols // tn,),
        in_specs=[pl.BlockSpec((FFT_N1 // 2, 2 * FFT_K1), lambda j: (0, 0)),
                  pl.BlockSpec((2 * FFT_K1, tn), lambda j: (0, j)),
                  pl.BlockSpec((FFT_N1 // 2, tn), lambda j: (gate_blk, j)),
                  pl.BlockSpec((FFT_N1 // 2, tn), lambda j: (z_blk, j)),
                  pl.BlockSpec((1, tn), lambda j: (0, 0))],
        out_specs=pl.BlockSpec((FFT_N1 // 2, tn), lambda j: (0, j)),
        compiler_params=_cparams(("parallel",), 32),
        name="idft_gate",
    )(f1i, d, gate2d, z2d, bias_t)


def _hyena(x1, x2, v, feat, tabs, hyw, layer, seq_len, row0):
    n2 = seq_len // (FFT_N1 // 2)
    f1, f1i, g, gt = tabs
    w1p, b1, fr1, w2, b2, fr2, w3p_bf, deltas, hy_bias = hyw
    nt = x1.shape[0]
    af, ss = _hy_filters(feat, w1p, b1, fr1, w2, b2, fr2, w3p_bf, deltas, f1, layer)
    hspec = _filter_minor(af, g, ss, n2)
    view = lambda a: a.reshape(nt // n2, n2 * BW)
    blk = row0 // seq_len if n2 == 1 else 0
    ncols = n2 * BW
    tn = min(FFT_TN, ncols)
    z, zv, zblk = v, view(v), blk
    gates = (x1, x2)
    for o in range(2):
        a = _dft_major(f1, zv, zblk, ncols)
        d = _conv_minor(a, hspec, g, gt, o, n2)
        bias_t = jnp.tile(hy_bias[layer, o][None, :], (1, tn // BW))
        z2 = _idft_gate(f1i, d, view(gates[o]), blk, zv, zblk, bias_t, ncols)
        zv, zblk = z2, 0
    return zv.reshape(seq_len, BW)


def _store_spectrum_slab(ref, j, word):
    for q in range(word.shape[1] // LANE):
        ref[q, pl.ds(j, FFT_K1, stride=FFT_NB), :] = word[:, q * LANE:(q + 1) * LANE]


def _load_spectrum_slab(ref, j):
    return jnp.concatenate([ref[q, pl.ds(j, FFT_K1, stride=FFT_NB), :] for q in range(ref.shape[0])], axis=1)


def _load_spectrum_k1(ref, j):
    nq, nblk = ref.shape[0], ref.shape[1]
    return jnp.concatenate([jnp.concatenate([ref[q, b, j] for q in range(nq)], axis=1) for b in range(nblk)], axis=0)


def _store_spectrum_k1(ref, j, word):
    nq, nblk = ref.shape[0], ref.shape[1]
    for b in range(nblk):
        for q in range(nq):
            ref[q, b, j] = word[b * FFT_NB:(b + 1) * FFT_NB, q * LANE:(q + 1) * LANE]


def _filter_p_body(feat_ref, w1_ref, b1_ref, f1_ref, w2_ref, b2_ref, f2_ref, w3_ref, dl_ref, dft_ref,
                   af_ref, ss_ref, h2):
    nb = pl.program_id(0)
    cb = pl.program_id(1)
    nbk = FFT_NB

    @pl.when(jnp.logical_and(nb == 0, cb == 0))
    def _():
        ss_ref[...] = jnp.zeros_like(ss_ref)

    half = nbk * TM // 2

    @pl.when(cb == 0)
    def _():
        feat2 = jnp.concatenate([feat_ref[0:half, :], feat_ref[half:2 * half, :]], axis=1)
        h = jnp.sin(f1_ref[...] * (_dot_bf16x3(feat2, w1_ref[...]) + b1_ref[...]))
        h = jnp.sin(f2_ref[...] * (_dot_bf16x3(h, w2_ref[...]) + b2_ref[...]))
        h2[...] = h.astype(BF16)

    row = lax.broadcasted_iota(I32, (TM, 1), 0)
    lag0_bwd = jnp.logical_and(jnp.logical_and(nb == 0, cb >= 2), row == 0)
    ssum = jnp.zeros((1, BW), F32)
    for j in range(nbk):
        side, jj = divmod(j, nbk // 2)
        raw = jnp.dot(h2[jj * TM:(jj + 1) * TM, :], w3_ref[side], preferred_element_type=F32)
        win = jnp.exp(-feat_ref[j * TM:(j + 1) * TM, 0:1] * dl_ref[...]) + HY_MOD_SHIFT
        f = raw * win
        if j == 0:
            f = jnp.where(lag0_bwd, 0.0, f)
        ssum = ssum + jnp.sum(f * f, axis=0, keepdims=True)
        rr = jnp.dot(dft_ref[...], f.astype(BF16), preferred_element_type=F32)
        _store_spectrum_slab(af_ref, j, _pack2(rr[0:FFT_K1], rr[FFT_K1:2 * FFT_K1]))
    ss_ref[cb] = ss_ref[cb] + ssum


def _hy_filters_p(feat_perm, w1p, b1, fr1, w2, b2, fr2, w3p_bf, deltas, f1, layer):
    l = feat_perm.shape[0]
    n2 = l // TM
    nbk = FFT_NB
    lsel = lambda *blk: pl.BlockSpec((None,) + blk, lambda nb, cb: (layer,) + (0,) * len(blk))
    return pl.pallas_call(
        _filter_p_body,
        out_shape=[SDS((4 * BW // LANE, n2 // nbk, FFT_K1 * nbk, LANE), U32), SDS((4, 1, BW), F32)],
        grid=(n2 // nbk, 4),
        in_specs=[pl.BlockSpec((nbk * TM, LANE), lambda nb, cb: (nb, 0)),
                  lsel(2 * LANE, 2 * HY_HIDDEN), lsel(1, 2 * HY_HIDDEN), lsel(1, 2 * HY_HIDDEN),
                  lsel(2 * HY_HIDDEN, 2 * HY_HIDDEN), lsel(1, 2 * HY_HIDDEN), lsel(1, 2 * HY_HIDDEN),
                  pl.BlockSpec((None, 2, 2 * HY_HIDDEN, BW), lambda nb, cb: (layer, 0, 0, cb)),
                  pl.BlockSpec((1, BW), lambda nb, cb: (0, 0)),
                  pl.BlockSpec((2 * FFT_K1, FFT_N1 // 2), lambda nb, cb: (0, 0))],
        out_specs=[pl.BlockSpec((BW // LANE, None, FFT_K1 * nbk, LANE), lambda nb, cb: (cb, nb, 0, 0)),
                   pl.BlockSpec((4, 1, BW), lambda nb, cb: (0, 0, 0))],
        scratch_shapes=[pltpu.VMEM((nbk * TM // 2, 2 * HY_HIDDEN), BF16)],
        compiler_params=_cparams(("arbitrary", "arbitrary"), 40),
        name="hy_filters_p",
    )(feat_perm, w1p, b1, fr1, w2, b2, fr2, w3p_bf, deltas, f1)


def _filter_minor_p_body(a_ref, g_ref, ss_ref, h_ref, *, n2):
    ssv = ss_ref[...]
    scale = lax.rsqrt(ssv[:, 0:2 * BW] + ssv[:, 2 * BW:4 * BW] + LN_EPS)
    for j in range(a_ref.shape[2]):
        re, im = _unpack2(_load_spectrum_k1(a_ref, j))
        a = jnp.concatenate([re, im], axis=0).astype(BF16)
        s = jnp.dot(g_ref[j], a, preferred_element_type=F32)
        sf, sb = s[:, 0:2 * BW], s[:, 2 * BW:4 * BW]
        hr = (sf[0:n2] + sb[0:n2]) * scale
        hi = (sf[n2:2 * n2] - sb[n2:2 * n2]) * scale
        h_ref[j] = jnp.concatenate([hr, hi], axis=0).astype(BF16)


def _filter_minor_p(af_p, g, ss, n2):
    kb = FFT_KB
    nq, nblk = af_p.shape[0], af_p.shape[1]
    af_p = af_p.reshape(nq, nblk, FFT_K1, FFT_NB, LANE)
    return pl.pallas_call(
        functools.partial(_filter_minor_p_body, n2=n2),
        out_shape=SDS((FFT_K1, 2 * n2, 2 * BW), BF16),
        grid=(FFT_K1 // kb,),
        in_specs=[pl.BlockSpec((nq, nblk, kb, FFT_NB, LANE), lambda i: (0, 0, i, 0, 0)),
                  pl.BlockSpec((kb, 2 * n2, 2 * n2), lambda i: (i, 0, 0)),
                  pl.BlockSpec((1, 4 * BW), lambda i: (0, 0))],
        out_specs=pl.BlockSpec((kb, 2 * n2, 2 * BW), lambda i: (i, 0, 0)),
        compiler_params=_cparams(("parallel",), 40),
        name="hy_filter_minor_p",
    )(af_p, g, ss)


def _unpack_slab(ref, j):
    nq, nblk = ref.shape[0], ref.shape[1]
    word = jnp.concatenate([jnp.concatenate([ref[q, b, j] for q in range(nq)], axis=1) for b in range(nblk)], axis=0)
    hi, lo = _unpack2(word)
    return jnp.concatenate([hi, lo], axis=1)


def _store_slab(ref, j, word):
    nq, nblk = ref.shape[0], ref.shape[1]
    for b in range(nblk):
        for q in range(nq):
            ref[q, b, j] = word[b * FFT_NB:(b + 1) * FFT_NB, q * LANE:(q + 1) * LANE]


def _slab_view(xp, n2):
    nq = xp.shape[0]
    x5 = xp.reshape(nq, xp.shape[1], n2, FFT_NB, LANE)
    spec = pl.BlockSpec((nq, FFT_N1 // 2 // FFT_NB, FFT_NB, FFT_NB, LANE), lambda i: (0, 0, i, 0, 0))
    return x5, spec


def _dft_major_p_body(f_ref, x_ref, a_ref):
    for j in range(x_ref.shape[2]):
        x = _unpack_slab(x_ref, j).astype(BF16)
        rr = jnp.dot(f_ref[...], x, preferred_element_type=F32)
        _store_spectrum_slab(a_ref, j, _pack2(rr[0:FFT_K1], rr[FFT_K1:2 * FFT_K1]))


def _dft_major_p(f1, xp, n2):
    nbk = FFT_NB
    xp, xspec = _slab_view(xp, n2)
    return pl.pallas_call(
        _dft_major_p_body,
        out_shape=SDS((BW // LANE, n2 // nbk, FFT_K1 * nbk, LANE), U32),
        grid=(n2 // nbk,),
        in_specs=[pl.BlockSpec((2 * FFT_K1, FFT_N1 // 2), lambda i: (0, 0)),
                  xspec],
        out_specs=pl.BlockSpec((BW // LANE, None, FFT_K1 * nbk, LANE), lambda i: (0, i, 0, 0)),
        compiler_params=_cparams(("parallel",), 32),
        name="dft_major_p",
    )(f1, xp)


def _conv_minor_p_body(a_ref, h_ref, g_ref, gt_ref, d_ref, *, n2):
    for j in range(a_ref.shape[2]):
        re, im = _unpack2(_load_spectrum_k1(a_ref, j))
        a = jnp.concatenate([re, im], axis=0).astype(BF16)
        x = jnp.dot(g_ref[j], a, preferred_element_type=F32)
        h = h_ref[j].astype(F32)
        xr, xi, hr, hi = x[0:n2], x[n2:2 * n2], h[0:n2], h[n2:2 * n2]
        y = jnp.concatenate([xr * hr - xi * hi, xr * hi + xi * hr], axis=0)
        dv = jnp.dot(gt_ref[j], y.astype(BF16), preferred_element_type=F32)
        _store_spectrum_k1(d_ref, j, _pack2(dv[0:n2], dv[n2:2 * n2]))


def _conv_minor_p(a_p, hspec, g, gt, order, n2):
    kb = FFT_KB
    nq, nblk = a_p.shape[0], a_p.shape[1]
    a_p = a_p.reshape(nq, nblk, FFT_K1, FFT_NB, LANE)
    gspec = pl.BlockSpec((kb, 2 * n2, 2 * n2), lambda i: (i, 0, 0))
    sspec = pl.BlockSpec((nq, nblk, kb, FFT_NB, LANE), lambda i: (0, 0, i, 0, 0))
    return pl.pallas_call(
        functools.partial(_conv_minor_p_body, n2=n2),
        out_shape=SDS((nq, nblk, FFT_K1, FFT_NB, LANE), U32),
        grid=(FFT_K1 // kb,),
        in_specs=[sspec, pl.BlockSpec((kb, 2 * n2, BW), lambda i: (i, 0, order)), gspec, gspec],
        out_specs=sspec,
        compiler_params=_cparams(("parallel",), 32),
        name="hy_conv_minor_p",
    )(a_p, hspec, g, gt)


def _idft_gate_p_body(f_ref, d_ref, gate_ref, z_ref, b_ref, o_ref):
    for j in range(o_ref.shape[2]):
        re, im = _unpack2(_load_spectrum_slab(d_ref, j))
        d = jnp.concatenate([re, im], axis=0).astype(BF16)
        y = jnp.dot(f_ref[...], d, preferred_element_type=F32)
        out = _unpack_slab(gate_ref, j) * (y + b_ref[...] * _unpack_slab(z_ref, j))
        _store_slab(o_ref, j, _pack2(out[:, 0:BW // 2], out[:, BW // 2:BW]))


def _idft_gate_p(f1i, d_p, gate_p, z_p, bias, n2):
    nbk = FFT_NB
    nq, nblk = d_p.shape[0], d_p.shape[1]
    d_p = d_p.reshape(nq, nblk, FFT_K1 * FFT_NB, LANE)
    gate_p, slab = _slab_view(gate_p, n2)
    z_p, _ = _slab_view(z_p, n2)
    nqt = gate_p.shape[0]
    return pl.pallas_call(
        _idft_gate_p_body,
        out_shape=SDS((nqt, FFT_N1 // 2 // FFT_NB, n2, FFT_NB, LANE), U32),
        grid=(n2 // nbk,),
        in_specs=[pl.BlockSpec((FFT_N1 // 2, 2 * FFT_K1), lambda i: (0, 0)),
                  pl.BlockSpec((nq, None, FFT_K1 * FFT_NB, LANE), lambda i: (0, i, 0, 0)),
                  slab, slab, pl.BlockSpec((1, BW), lambda i: (0, 0))],
        out_specs=slab,
        compiler_params=_cparams(("parallel",), 32),
        name="idft_gate_p",
    )(f1i, d_p, gate_p, z_p, bias)


def _hyena_latent(x1p, x2p, vp, feat_perm, tabs, hyw, layer, seq_len):
    n2 = seq_len // (FFT_N1 // 2)
    f1, f1i, g, gt = tabs
    w1p, b1, fr1, w2, b2, fr2, w3p_bf, deltas, hy_bias = hyw
    af_p, ss = _hy_filters_p(feat_perm, w1p, b1, fr1, w2, b2, fr2, w3p_bf, deltas, f1, layer)
    hspec = _filter_minor_p(af_p, g, ss.reshape(1, 4 * BW), n2)
    z = vp
    for o, gate in enumerate((x1p, x2p)):
        a_p = _dft_major_p(f1, z, n2)
        d_p = _conv_minor_p(a_p, hspec, g, gt, o, n2)
        z = _idft_gate_p(f1i, d_p, gate, z, hy_bias[layer, o][None, :], n2)
    return z.reshape(z.shape[0], z.shape[1], n2 * FFT_NB, LANE)


def _merge_body(att_ref, yp_ref, yhp_ref, yhc_ref, ys_ref, hmod_ref, wg_ref, x_ref, wb_ref, wo_ref,
                g1_ref, lg_ref, lb_ref, sh2_ref, sc2_ref, rw_ref, rb_ref,
                x1_ref, m_ref, rt_ref, cnt_ref, *, nlat, n_ctx, alpha):
    i = pl.program_id(0)
    n2c = yhp_ref.shape[2] // FFT_NB
    pieces = []
    for a in range(TM1 // n2c):
        word = jnp.concatenate([yhp_ref[q, a // FFT_NB, pl.ds(a % FFT_NB, n2c, stride=FFT_NB), :]
                                for q in range(yhp_ref.shape[0])], axis=1)
        hi, lo = _unpack2(word)
        pieces.append(jnp.concatenate([hi, lo], axis=1).astype(BF16))
    yh_lat = jnp.concatenate(pieces, axis=0)
    yh_ctx = jnp.concatenate([yhc_ref[...], jnp.zeros((TM1 - n_ctx, BW), BF16)], axis=0)
    yh = jnp.where(i >= nlat, yh_ctx, yh_lat)
    ys = (att_ref[...], yp_ref[...], yh, ys_ref[...])
    half = TM1 // 2
    logit_parts = []
    for hh in range(2):
        rows = slice(hh * half, (hh + 1) * half)
        hm = hmod_ref[rows, :]
        merged = jnp.zeros((half, D), F32)
        for n in range(4):
            br = jnp.dot(ys[n][rows], wb_ref[n], preferred_element_type=F32)
            gl = jnp.dot(hm, wg_ref[:, n * D:(n + 1) * D], preferred_element_type=F32)
            merged = merged + br * (0.5 * jnp.tanh(0.5 * gl) + 0.5)
        y = jnp.dot(merged.astype(BF16), wo_ref[...], preferred_element_type=F32)
        x1 = _layer_norm(alpha * x_ref[rows, :] + g1_ref[...] * y) * lg_ref[...] + lb_ref[...]
        x1_ref[rows, :] = x1
        m = _layer_norm(x1) * (1.0 + sc2_ref[...]) + sh2_ref[...]
        m_ref[rows, :] = m.astype(BF16)
        m_hi = m.astype(BF16)
        m_lo = (m - m_hi.astype(F32)).astype(BF16)
        logit_parts.append(jnp.dot(m_hi, rw_ref[0], preferred_element_type=F32)
                           + jnp.dot(m_lo, rw_ref[0], preferred_element_type=F32)
                           + jnp.dot(m_hi, rw_ref[1], preferred_element_type=F32) + rb_ref[...])
    logits = jnp.concatenate(logit_parts, axis=0)
    lt = logits.T
    le = lt[0:N_EXPERTS]
    lgp = lt[N_EXPERTS:N_EXPERTS + SUB]
    big = 1 << 20
    gi = lax.broadcasted_iota(I32, lgp.shape, 0)
    gmax = jnp.max(lgp, axis=0, keepdims=True)
    gsel = jnp.min(jnp.where(lgp == gmax, gi, big), axis=0, keepdims=True)
    gate_g = 1.0 / jnp.sum(jnp.exp(lgp - gmax), axis=0, keepdims=True)
    ei = lax.broadcasted_iota(I32, le.shape, 0)
    lem = jnp.where(lax.shift_right_logical(ei, 3) == gsel, le, -3.0e38)
    v1 = jnp.max(lem, axis=0, keepdims=True)
    i1 = jnp.min(jnp.where(lem == v1, ei, big), axis=0, keepdims=True)
    lem2 = jnp.where(ei == i1, -3.0e38, lem)
    v2 = jnp.max(lem2, axis=0, keepdims=True)
    i2 = jnp.min(jnp.where(lem2 == v2, ei, big), axis=0, keepdims=True)
    e2 = jnp.exp(v2 - v1)
    wa = gate_g / (1.0 + e2)
    wb = gate_g * e2 / (1.0 + e2)
    tok = lax.broadcasted_iota(I32, (1, TM1), 1)
    valid = jnp.logical_or(i < nlat, tok < n_ctx)
    i1 = jnp.where(valid, i1, -1)
    i2 = jnp.where(valid, i2, -1)
    ri = lax.broadcasted_iota(I32, (SUB, TM1), 0)
    rt = jnp.where(ri == 0, i1.astype(F32), jnp.where(ri == 1, i2.astype(F32),
                   jnp.where(ri == 2, wa, jnp.where(ri == 3, wb, 0.0))))
    rt_ref[...] = rt
    ci = lax.broadcasted_iota(I32, (LANE, TM1), 0)
    oh = jnp.logical_or(ci == i1, ci == i2).astype(BF16)
    cnt_ref[0] = lax.dot_general(jnp.ones((SUB, TM1), BF16), oh, (((1,), (1,)), ((), ())),
                                 preferred_element_type=F32)


def _merge(att, ypool, yhp, yhc, ysc, hmod, wg_bf, x, wb_bf, wo_bf, g1, ln_g, ln_b, sh2, sc2, rw, rb, layer, n_lat,
           n_ctx, alpha):
    nt = x.shape[0]
    nlat = n_lat // TM1
    ntl = nt // TM1
    n2c = yhp.shape[2] // FFT_NB
    sel = lambda i: (jnp.where(i >= nlat, 1, 0), 0, 0)
    row = lambda c: pl.BlockSpec((TM1, c), lambda i: (i, 0))
    lsel = lambda *blk, **kw: pl.BlockSpec((None,) + blk, lambda i: (layer,) + (0,) * len(blk), **kw)
    msel = pl.BlockSpec((None, 1, D), sel)
    return pl.pallas_call(
        functools.partial(_merge_body, nlat=nlat, n_ctx=n_ctx, alpha=alpha),
        out_shape=[SDS((nt, D), F32), SDS((nt, D), BF16), SDS((SUB, nt), F32), SDS((ntl, SUB, LANE), F32)],
        grid=(ntl,),
        in_specs=[row(BW), row(BW),
                  pl.BlockSpec((yhp.shape[0], TM1 // n2c // FFT_NB, n2c * FFT_NB, LANE),
                               lambda i: (0, jnp.minimum(i, nlat - 1), 0, 0)),
                  pl.BlockSpec((n_ctx, BW), lambda i: (0, 0)),
                  row(BW), row(D), lsel(D, 4 * D, pipeline_mode=pl.Buffered(1)), row(D),
                  lsel(4, BW, D, pipeline_mode=pl.Buffered(1)), lsel(D, D, pipeline_mode=pl.Buffered(1)),
                  msel, lsel(1, D), lsel(1, D), msel, msel,
                  lsel(2, D, LANE), lsel(1, LANE)],
        out_specs=[row(D), row(D), pl.BlockSpec((SUB, TM1), lambda i: (0, i)),
                   pl.BlockSpec((1, SUB, LANE), lambda i: (i, 0, 0))],
        compiler_params=_cparams(("parallel",), 56),
        name="merge_router",
    )(att, ypool, yhp, yhc, ysc, hmod, wg_bf, x, wb_bf, wo_bf, g1, ln_g, ln_b, sh2, sc2, rw, rb)


def _piece_loops(np_ref, so_ref, do_ref, j, fn):
    for e in range(N_EXPERTS):
        n = np_ref[j * N_EXPERTS + e]
        so = so_ref[j * N_EXPERTS + e]
        do = do_ref[j * N_EXPERTS + e]

        nbig = lax.shift_right_logical(n, 2)
        big = MOE_BIG * MOE_PIECE

        def body_big(p, carry, so=so, do=do):
            fn(pl.multiple_of(so + p * big, MOE_PIECE), pl.multiple_of(do + p * big, MOE_PIECE), big)
            return carry

        def body(p, carry, so=so, do=do, nbig=nbig):
            off = nbig * big + p * MOE_PIECE
            fn(pl.multiple_of(so + off, MOE_PIECE), pl.multiple_of(do + off, MOE_PIECE), MOE_PIECE)
            return carry

        lax.fori_loop(0, nbig, body_big, 0)
        lax.fori_loop(0, n - nbig * MOE_BIG, body, 0)


def _dispatch_body(np_ref, so_ref, do_ref, nv_ref, m_ref, rt_ref, u_ref, hs_hbm, pos_ref, hs_vmem, sem, *,
                   ntiles, nblk):
    j = pl.program_id(0)
    real = j < ntiles
    rt = rt_ref[...]
    e0 = jnp.where(real, rt[0:1].astype(I32), -1)
    e1 = jnp.where(real, rt[1:2].astype(I32), -1)
    ei = lax.broadcasted_iota(I32, (N_EXPERTS, TM1), 0)
    oh0 = (ei == e0).astype(F32)
    oh1 = (ei == e1).astype(F32)
    c0 = jnp.dot(oh0.astype(BF16), u_ref[...], preferred_element_type=F32)
    c1 = jnp.dot(oh1.astype(BF16), u_ref[...], preferred_element_type=F32)
    n0 = jnp.sum(oh0, axis=1, keepdims=True)
    ecol = lax.broadcasted_iota(I32, (N_EXPERTS, 1), 0)
    toff = jnp.zeros((N_EXPERTS, 1), F32)
    for e in range(N_EXPERTS):
        toff = jnp.where(ecol == e, so_ref[j * N_EXPERTS + e].astype(F32), toff)
    pos0 = jnp.sum(oh0 * (toff + c0), axis=0, keepdims=True)
    pos1 = jnp.sum(oh1 * (toff + n0 + c1), axis=0, keepdims=True)
    pos0 = jnp.where(e0 >= 0, pos0, -1.0)
    pos1 = jnp.where(e1 >= 0, pos1, -1.0)
    ri = lax.broadcasted_iota(I32, (SUB, TM1), 0)
    pos_ref[...] = jnp.where(ri == 0, pos0, jnp.where(ri == 1, pos1, 0.0))
    si = lax.broadcasted_iota(I32, (MOE_S, TM1), 0)
    perm = jnp.logical_or(si == pos0.astype(I32), si == pos1.astype(I32)).astype(BF16)
    hs = jnp.dot(perm, m_ref[...], preferred_element_type=F32)
    slot = lax.rem(j, 2)
    hs_vmem[slot] = _pack2(hs[:, 0:D // 2], hs[:, D // 2:D])

    def copy(s, so, do, n):
        return pltpu.make_async_copy(hs_vmem.at[s, pl.ds(so, n)], hs_hbm.at[pl.ds(do, n)], sem.at[s])

    _piece_loops(np_ref, so_ref, do_ref, j, lambda so, do, n: copy(slot, so, do, n).start())

    @pl.when(j > 0)
    def _():
        _piece_loops(np_ref, so_ref, do_ref, j - 1, lambda so, do, n: copy(1 - slot, so, do, n).wait())

    @pl.when(j == ntiles)
    def _():
        def blk_copy(b):
            return pltpu.make_async_copy(hs_vmem.at[slot, pl.ds(0, MOE_MB)],
                                         hs_hbm.at[pl.ds(pl.multiple_of(b * MOE_MB, MOE_MB), MOE_MB)], sem.at[slot])

        def start(b, carry):
            blk_copy(b).start()
            return carry

        def wait(b, carry):
            blk_copy(b).wait()
            return carry

        lax.fori_loop(nv_ref[0], nblk, start, 0)
        _piece_loops(np_ref, so_ref, do_ref, j, lambda so, do, n: copy(slot, so, do, n).wait())
        lax.fori_loop(nv_ref[0], nblk, wait, 0)


def _dispatch(npieces, soff, doff, nvb, m_bf, rt, upper, rmax):
    nt = m_bf.shape[0]
    ntiles = nt // TM1
    last = ntiles - 1
    return pl.pallas_call(
        functools.partial(_dispatch_body, ntiles=ntiles, nblk=rmax // MOE_MB),
        out_shape=[SDS((rmax, D // 2), U32), SDS((SUB, (ntiles + 1) * TM1), F32)],
        grid_spec=pltpu.PrefetchScalarGridSpec(
            num_scalar_prefetch=4,
            grid=(ntiles + 1,),
            in_specs=[pl.BlockSpec((TM1, D), lambda j, *_: (jnp.minimum(j, last), 0)),
                      pl.BlockSpec((SUB, TM1), lambda j, *_: (0, jnp.minimum(j, last))),
                      pl.BlockSpec((TM1, TM1), lambda j, *_: (0, 0))],
            out_specs=[pl.BlockSpec(memory_space=pl.ANY),
                       pl.BlockSpec((SUB, TM1), lambda j, *_: (0, j))],
            scratch_shapes=[pltpu.VMEM((2, MOE_S, D // 2), U32), pltpu.SemaphoreType.DMA((2,))]),
        compiler_params=_cparams(("arbitrary",), 40),
        name="moe_dispatch",
    )(npieces, soff, doff, nvb, m_bf, rt, upper)


def _expert_body(be_ref, nv_ref, x_ref, w1_ref, w3_ref, w2_ref, o_ref, w1b, w3b, w2b):
    b = pl.program_id(0)
    valid = b < nv_ref[0]
    prev = be_ref[jnp.maximum(b - 1, 0)]
    fresh = jnp.logical_or(b == 0, be_ref[b] != prev)

    @pl.when(jnp.logical_and(valid, fresh))
    def _():
        w1b[...] = w1_ref[...].astype(BF16)
        w3b[...] = w3_ref[...].astype(BF16)
        w2b[...] = w2_ref[...].astype(BF16)

    @pl.when(valid)
    def _():
        xh, xl = _unpack2(x_ref[...])
        xh, xl = xh.astype(BF16), xl.astype(BF16)
        half = D // 2
        h1 = (jnp.dot(xh, w1b[0:half, :], preferred_element_type=F32)
              + jnp.dot(xl, w1b[half:D, :], preferred_element_type=F32))
        h3 = (jnp.dot(xh, w3b[0:half, :], preferred_element_type=F32)
              + jnp.dot(xl, w3b[half:D, :], preferred_element_type=F32))
        hh = (h1 * (0.5 * jnp.tanh(0.5 * h1) + 0.5)) * h3
        y = jnp.dot(hh.astype(BF16), w2b[...], preferred_element_type=F32)
        o_ref[...] = _pack2(y[:, 0:half], y[:, half:D])

    @pl.when(b >= nv_ref[0])
    def _():
        o_ref[...] = jnp.zeros_like(o_ref)


def _experts(blk_e, nvb, hs, ex_w1, ex_w3, ex_w2, layer):
    rmax = hs.shape[0]
    nb = rmax // MOE_MB
    bi = lambda b, be, nv: jnp.maximum(jnp.minimum(b, nv[0] - 1), 0)
    return pl.pallas_call(
        _expert_body,
        out_shape=SDS((rmax, D // 2), U32),
        grid_spec=pltpu.PrefetchScalarGridSpec(
            num_scalar_prefetch=2,
            grid=(nb,),
            in_specs=[pl.BlockSpec((MOE_MB, D // 2), lambda b, be, nv: (bi(b, be, nv), 0)),
                      pl.BlockSpec((None, None, D, EXPERT_HIDDEN), lambda b, be, nv: (layer, be[bi(b, be, nv)], 0, 0)),
                      pl.BlockSpec((None, None, D, EXPERT_HIDDEN), lambda b, be, nv: (layer, be[bi(b, be, nv)], 0, 0)),
                      pl.BlockSpec((None, None, EXPERT_HIDDEN, D), lambda b, be, nv: (layer, be[bi(b, be, nv)], 0, 0))],
            out_specs=pl.BlockSpec((MOE_MB, D // 2), lambda b, be, nv: (b, 0)),
            scratch_shapes=[pltpu.VMEM((D, EXPERT_HIDDEN), BF16), pltpu.VMEM((D, EXPERT_HIDDEN), BF16),
                            pltpu.VMEM((EXPERT_HIDDEN, D), BF16)]),
        compiler_params=_cparams(("arbitrary",), 48),
        name="moe_experts",
    )(blk_e, nvb, hs, ex_w1, ex_w3, ex_w2)


def _combine_body(np_ref, so_ref, do_ref, ys_hbm, pos_ref, rt_ref, x1_ref, g2_ref, lg_ref, lb_ref,
                  o_ref, ys_vmem, sem, *, alpha, nsteps):
    j = pl.program_id(0)
    slot = lax.rem(j, 2)

    def copy(s, so, do, n):
        return pltpu.make_async_copy(ys_hbm.at[pl.ds(do, n)], ys_vmem.at[s, pl.ds(so, n)], sem.at[s])

    @pl.when(j == 0)
    def _():
        _piece_loops(np_ref, so_ref, do_ref, j, lambda so, do, n: copy(slot, so, do, n).start())

    @pl.when(j + 1 < nsteps)
    def _():
        _piece_loops(np_ref, so_ref, do_ref, j + 1, lambda so, do, n: copy(1 - slot, so, do, n).start())
    z = jnp.concatenate([pos_ref[...], rt_ref[...], jnp.zeros((LANE - 2 * SUB, TM1), F32)], axis=0)
    zt = z.T
    p0 = zt[:, 0:1].astype(I32)
    p1 = zt[:, 1:2].astype(I32)
    w0 = zt[:, SUB + 2:SUB + 3]
    w1 = zt[:, SUB + 3:SUB + 4]
    si = lax.broadcasted_iota(I32, (TM1, MOE_S), 1)
    wm = (jnp.where(si == p0, w0, 0.0) + jnp.where(si == p1, w1, 0.0)).astype(BF16)
    _piece_loops(np_ref, so_ref, do_ref, j, lambda so, do, n: copy(slot, so, do, n).wait())
    last = j * N_EXPERTS + N_EXPERTS - 1
    total = so_ref[last] + np_ref[last] * MOE_PIECE
    srow = lax.broadcasted_iota(I32, (MOE_S, 1), 0)
    yh, yl = _unpack2(jnp.where(srow < total, ys_vmem[slot], jnp.uint32(0)))
    f = jnp.concatenate([jnp.dot(wm, yh.astype(BF16), preferred_element_type=F32),
                         jnp.dot(wm, yl.astype(BF16), preferred_element_type=F32)], axis=1)
    o_ref[...] = _layer_norm(alpha * x1_ref[...] + g2_ref[...] * f) * lg_ref[...] + lb_ref[...]


def _combine(npieces, soff, doff, ys, pos, rt, x1, g2, ln_g, ln_b, layer, n_lat, n_out_tiles, n_out_rows, alpha):
    nlat = n_lat // TM1
    sel = lambda j, *_: (jnp.where(j >= nlat, 1, 0), 0, 0)
    lsel = lambda *blk: pl.BlockSpec((None,) + blk, lambda j, *_: (layer,) + (0,) * len(blk))
    return pl.pallas_call(
        functools.partial(_combine_body, alpha=alpha, nsteps=n_out_tiles),
        out_shape=SDS((n_out_rows, D), F32),
        grid_spec=pltpu.PrefetchScalarGridSpec(
            num_scalar_prefetch=3,
            grid=(n_out_tiles,),
            in_specs=[pl.BlockSpec(memory_space=pl.ANY),
                      pl.BlockSpec((SUB, TM1), lambda j, *_: (0, j)),
                      pl.BlockSpec((SUB, TM1), lambda j, *_: (0, j)),
                      pl.BlockSpec((TM1, D), lambda j, *_: (j, 0)),
                      pl.BlockSpec((None, 1, D), sel), lsel(1, D), lsel(1, D)],
            out_specs=pl.BlockSpec((TM1, D), lambda j, *_: (j, 0)),
            scratch_shapes=[pltpu.VMEM((2, MOE_S, D // 2), U32), pltpu.SemaphoreType.DMA((2,))]),
        compiler_params=_cparams(("arbitrary",), 40),
        name="moe_combine",
    )(npieces, soff, doff, ys, pos, rt, x1, g2, ln_g, ln_b)


def _moe_plan(cnt, ntiles):
    c = cnt[:, 0, :N_EXPERTS].astype(I32)
    pad8 = (c + MOE_PIECE - 1) // MOE_PIECE * MOE_PIECE
    toff = jnp.cumsum(pad8, axis=1) - pad8
    tot = pad8.sum(axis=0)
    totb = (tot + MOE_MB - 1) // MOE_MB * MOE_MB
    ends = jnp.cumsum(totb)
    base = ends - totb
    dest = base[None, :] + jnp.cumsum(pad8, axis=0) - pad8
    npieces = jnp.concatenate([pad8 // MOE_PIECE, ((totb - tot) // MOE_PIECE)[None, :]], axis=0)
    soff = jnp.concatenate([toff, jnp.zeros((1, N_EXPERTS), I32)], axis=0)
    doff = jnp.concatenate([dest, (base + tot)[None, :]], axis=0)
    return npieces.reshape(-1), soff.reshape(-1), doff.reshape(-1), ends


def _rope_tables(n_lat, nt):
    t = jnp.arange(n_lat)
    pos = jnp.stack([(t // GRID_W).astype(F32), (t % GRID_W).astype(F32)], axis=1)
    inv = ROPE_BASE ** (-jnp.arange(ROPE_PAIRS, dtype=F32) / ROPE_PAIRS)
    ang = pos[:, :, None] * inv[None, None, :]
    cos = jnp.repeat(jnp.cos(ang), 2, axis=1).reshape(n_lat, 2, 2, ROPE_PAIRS)
    sin = jnp.sin(ang)
    sin = jnp.stack([-sin, sin], axis=2)
    cos = jnp.tile(cos.reshape(n_lat, HEAD_DIM), (1, LANE // HEAD_DIM))
    sin = jnp.tile(sin.reshape(n_lat, HEAD_DIM), (1, LANE // HEAD_DIM))
    cos = jnp.concatenate([cos, jnp.ones((nt - n_lat, LANE), F32)], axis=0)
    sin = jnp.concatenate([sin, jnp.zeros((nt - n_lat, LANE), F32)], axis=0)
    return cos, sin


def _hy_features(l):
    t01 = jnp.linspace(0.0, 1.0, l, dtype=F32)
    fr = jnp.linspace(1e-4, HY_BANDS - 1, HY_BANDS, dtype=F32)
    ang = (2.0 * math.pi / l) * jnp.arange(l, dtype=F32)[:, None] * fr[None, :]
    feat = jnp.concatenate([t01[:, None], jnp.cos(ang), -jnp.sin(ang)], axis=-1)
    feat = jnp.pad(feat, ((0, 0), (0, LANE - HY_EMB)))
    return feat.reshape(FFT_N1 // 2, l // (FFT_N1 // 2), LANE).transpose(1, 0, 2).reshape(l, LANE)


def kernel(x, c, ctx, c_ctx, ada_w, ada_b, w_in, attn_sink, pool_w, pool_scale, hy_conv_w, hy_conv_b, hy_w1, hy_b1, hy_freq1, hy_w2, hy_b2, hy_freq2, hy_w3, hy_bias, sc_conv_w, w_branch, w_out, ln1_g, ln1_b, ln2_g, ln2_b, rg_w, rg_b, re_w, re_b, ex_w1, ex_w3, ex_w2):
    depth = w_in.shape[0]
    assert x.shape[0] == 1 and ctx.shape[0] == 1 and x.shape[2] == D
    n_lat = x.shape[1]
    n_ctx = ctx.shape[1]
    assert n_ctx == TM and n_lat % TM1 == 0 and n_lat >= 2 * TM1
    nt = n_lat + n_ctx + PADR
    ntiles1 = nt // TM1
    alpha = (2 * depth) ** 0.25

    cos_t, sin_t = _rope_tables(n_lat, nt)
    w_main_bf = w_in[:, :, :MAIN_COLS].astype(BF16)
    wg_bf = w_in[:, :, MAIN_COLS:].astype(BF16)
    wb_bf = w_branch.astype(BF16)
    wo_bf = w_out.astype(BF16)
    pool_w_bf = pool_w.astype(BF16)
    r3 = lambda a: a.reshape(depth, 1, a.shape[-1])
    tabs_l = _dft_tables(n_lat // (FFT_N1 // 2))
    tabs_c = _dft_tables(n_ctx // (FFT_N1 // 2))
    feat_l, feat_c = _hy_features(n_lat), _hy_features(n_ctx)
    deltas = jnp.abs(jnp.linspace(math.log(HY_TARGET) / HY_SLOW_PCT, math.log(HY_TARGET) / HY_FAST_PCT,
                                  BW, dtype=F32))[None, :]
    w1p = jnp.pad(hy_w1, ((0, 0), (0, LANE - HY_EMB), (0, 0)))
    w3p_bf = hy_w3.reshape(depth, HY_HIDDEN, 2, 2, BW).transpose(0, 1, 3, 2, 4).reshape(depth, HY_HIDDEN, 4 * BW).astype(BF16)
    hyw = (w1p, r3(hy_b1), r3(hy_freq1), hy_w2, r3(hy_b2), r3(hy_freq2), w3p_bf, deltas, hy_bias)
    bdiag = lambda w: jnp.concatenate([jnp.concatenate([w, jnp.zeros_like(w)], axis=2),
                                       jnp.concatenate([jnp.zeros_like(w), w], axis=2)], axis=1)
    twice = lambda a: jnp.tile(r3(a), (1, 1, 2))
    zw3 = jnp.zeros_like(w3p_bf)
    w3ab = jnp.stack([jnp.concatenate([w3p_bf, zw3], axis=1), jnp.concatenate([zw3, w3p_bf], axis=1)], axis=1)
    hyw_lat = (bdiag(w1p), twice(hy_b1), twice(hy_freq1), bdiag(hy_w2), twice(hy_b2), twice(hy_freq2), w3ab,
               deltas, hy_bias)
    rw = jnp.concatenate([re_w, rg_w, jnp.zeros((depth, D, LANE - N_EXPERTS - N_GROUPS), F32)], axis=2)
    rw_hi = rw.astype(BF16)
    rw = jnp.stack([rw_hi, (rw - rw_hi.astype(F32)).astype(BF16)], axis=1)
    rb = jnp.concatenate([re_b, rg_b, jnp.full((depth, LANE - N_EXPERTS - N_GROUPS), NEG_INF, F32)], axis=1)
    rb = rb.reshape(depth, 1, LANE)
    upper = (jnp.arange(TM1)[:, None] < jnp.arange(TM1)[None, :]).astype(BF16)
    rmax = -(-(2 * (n_lat + n_ctx) + ntiles1 * N_EXPERTS * (MOE_PIECE - 1) + N_EXPERTS * (MOE_MB - 1)) // MOE_MB) * MOE_MB
    nblk = rmax // MOE_MB

    s8 = jnp.concatenate([c, c_ctx[None, :], jnp.zeros((SUB - 2, D), F32)], axis=0)
    mod = _mod_all(s8, ada_w, ada_b.reshape(depth, 1, 6 * D))

    xs = jnp.concatenate([x[0], ctx[0], jnp.zeros((PADR, D), F32)], axis=0)
    for i in range(depth):
        last = i == depth - 1
        mp = lambda j: mod[i, 0:2, j * D:(j + 1) * D].reshape(2, 1, D)
        sh1, sc1, g1, sh2, sc2, g2 = (mp(j) for j in range(6))

        q, k, v, pool_in, hy_in, sc_in, hmod = _inproj(xs, sh1, sc1, cos_t, sin_t, w_main_bf, i, n_lat // TM1)
        att = _attention(attn_sink[i], q, k, v, n_lat)
        ypool, ysc, x1p, x2p, vp, x1c, x2c, vc = _local_ops(
            pool_in, sc_in, hy_in, pool_w_bf, pool_scale.reshape(depth, 1, BW),
            hy_conv_w, hy_conv_b.reshape(depth, 1, 3 * BW), sc_conv_w, i, n_lat, n_ctx)
        yhp = _hyena_latent(x1p, x2p, vp, feat_l, tabs_l, hyw_lat, i, n_lat)
        if last:
            yhc = jnp.zeros((n_ctx, BW), BF16)
        else:
            yhc = _hyena(x1c, x2c, vc, feat_c, tabs_c, hyw, i, n_ctx, 0)
        x1, m_bf, rt, cnt = _merge(att, ypool, yhp, yhc, ysc, hmod, wg_bf, xs, wb_bf, wo_bf, g1, r3(ln1_g), r3(ln1_b),
                                   sh2, sc2, rw, rb, i, n_lat, n_ctx, alpha)

        npieces, soff, doff, ends = _moe_plan(cnt, ntiles1)
        nvb = (ends[-1] // MOE_MB).astype(I32).reshape(1)
        blk_start = jnp.arange(nblk, dtype=I32) * MOE_MB
        blk_e = jnp.minimum(jnp.sum((ends[None, :] <= blk_start[:, None]).astype(I32), axis=1), N_EXPERTS - 1)
        hs, pos = _dispatch(npieces, soff, doff, nvb, m_bf, rt, upper, rmax)
        ys = _experts(blk_e, nvb, hs, ex_w1, ex_w3, ex_w2, i)
        if last:
            xs = _combine(npieces, soff, doff, ys, pos, rt, x1, g2, r3(ln2_g), r3(ln2_b), i, n_lat,
                          n_lat // TM1, n_lat, alpha)
        else:
            xs = _combine(npieces, soff, doff, ys, pos, rt, x1, g2, r3(ln2_g), r3(ln2_b), i, n_lat, ntiles1, nt,
                          alpha)
    return xs[None]
```
